```python
import math
import jax, jax.numpy as jnp
from jax import lax
import numpy as np

D_MODEL = 2048
BATCH = 16
SEQ = 256
DEPTH = 4
DEC_BATCH = 2
DEC_SEQ = 1024
PAST_LEN = 512

GRID_W = 64
N_EVEN = (DEPTH + 1) // 2
N_ODD = DEPTH // 2
HEAD_DIM = 128
MIX_HEADS = D_MODEL // HEAD_DIM
Q_BLOCK = 128
ROPE_BASE = 10000.0
RMS_EPS = 1e-6
NEG_INF = -1e30

A_HEADS = MIX_HEADS // 2
A_QK = HEAD_DIM // 2
A_V = HEAD_DIM
B_HEADS = MIX_HEADS // 2
B_Q_RANK = D_MODEL // 4
B_KV_RANK = D_MODEL // 8
B_NOPE = 128
B_ROPE = 64
B_V = HEAD_DIM
C_HEADS = MIX_HEADS // 2
C_HD = HEAD_DIM
NA_KH = 8
NA_KW = 16
D_HS = 64
D_WIDTH = D_MODEL // 2
D_HEADS = D_WIDTH // D_HS
D_DECAY_LORA = 64
D_A_LORA = 64
D_GATE_LORA = 128
D_LN_EPS = 64e-5
PEER_HEADS = 8
PEER_NKEYS = 128
PEER_N = PEER_NKEYS * PEER_NKEYS
PEER_DQ = 256
PEER_HALF = PEER_DQ // 2
PEER_TOPK = 16
PEER_CHUNK = 128

A_WIDTH = A_HEADS * A_V
B_WIDTH = B_HEADS * B_V
C_WIDTH = C_HEADS * C_HD
EVEN_SIZES = (A_HEADS * 2 * A_QK, A_HEADS * 2 * A_QK, A_WIDTH, B_Q_RANK, B_KV_RANK, B_ROPE)
EVEN_IN = sum(EVEN_SIZES)
D_SIZES = (D_WIDTH, D_WIDTH, D_WIDTH, 2 * D_DECAY_LORA, 2 * D_A_LORA, D_GATE_LORA)
D_IN = sum(D_SIZES)
ODD_SIZES = (C_WIDTH, C_WIDTH, C_WIDTH, D_IN)
ODD_IN = sum(ODD_SIZES)

kernel_name = 'hybrid_diffusion_prefix_trunk_step'


def _split(z, sizes):
    out, o = [], 0
    for s in sizes:
        out.append(z[..., o:o + s])
        o += s
    return out


def rms_norm(x, g):
    xf = x.astype(jnp.float32)
    y = xf * lax.rsqrt(jnp.mean(xf * xf, axis=-1, keepdims=True) + RMS_EPS)
    return (y * g.astype(jnp.float32)).astype(x.dtype)


def _ada(cond, w, b):
    return jnp.split(jax.nn.silu(cond) @ w + b, 6, axis=-1)


def _modulate(x, g, shift, scale):
    return rms_norm(x, g) * (1.0 + scale) + shift


def axial_rope(x, rows, cols):
    half = x.shape[-1] // 2
    nf = half // 2
    inv = ROPE_BASE ** (-jnp.arange(nf, dtype=jnp.float32) / nf)
    bshape = (rows.shape[0],) + (1,) * (x.ndim - 3) + (nf,)

    def rot(xp, p):
        ang = (p.astype(jnp.float32)[:, None] * inv).reshape(bshape)
        cos, sin = jnp.cos(ang).astype(x.dtype), jnp.sin(ang).astype(x.dtype)
        x1, x2 = xp[..., :nf], xp[..., nf:]
        return jnp.concatenate([x1 * cos - x2 * sin, x1 * sin + x2 * cos], axis=-1)

    return jnp.concatenate([rot(x[..., :half], rows), rot(x[..., half:], cols)], axis=-1)


def _sweep_queries(fn, q):
    B_, T = q.shape[:2]
    nb = T // Q_BLOCK
    qb = jnp.moveaxis(q.reshape((B_, nb, Q_BLOCK) + q.shape[2:]), 1, 0)
    out = jnp.moveaxis(lax.map(fn, qb), 0, 1)
    return out.reshape((B_, T) + out.shape[3:])


def attention(q, k, v, scale):
    def blk(qb):
        s = jnp.einsum('bqhd,bkhd->bhqk', qb, k).astype(jnp.float32) * scale
        p = jax.nn.softmax(s, axis=-1)
        return jnp.einsum('bhqk,bkhd->bqhd', p.astype(v.dtype), v)
    return _sweep_queries(blk, q)


def diff_attention(q, k, v, lam, scale):
    def blk(qb):
        s = jnp.einsum('bqhmd,bkhmd->bhmqk', qb, k).astype(jnp.float32) * scale
        p = jax.nn.softmax(s, axis=-1)
        a = p[:, :, 0] - lam * p[:, :, 1]
        return jnp.einsum('bhqk,bkhd->bqhd', a.astype(v.dtype), v)
    return _sweep_queries(blk, q)


def neighbourhood_attention(q, k, v, k_ctx, v_ctx, rpb):
    B_, T, H, dh = q.shape
    rows = T // GRID_W
    kh = min(NA_KH, rows)
    kw = NA_KW
    qg = q.reshape(B_, rows, GRID_W, H, dh)
    kg = k.reshape(B_, rows, GRID_W, H, dh)
    vg = v.reshape(B_, rows, GRID_W, H, dh)
    cols = jnp.arange(GRID_W)
    cs = jnp.clip(cols - kw // 2, 0, GRID_W - kw)
    col_in = (cols[None, :] >= cs[:, None]) & (cols[None, :] < cs[:, None] + kw)
    dc_idx = jnp.clip(cols[None, :] - cols[:, None] + NA_KW - 1, 0, 2 * NA_KW - 2)
    mask = jnp.broadcast_to(col_in[:, None, :], (GRID_W, kh, GRID_W)).reshape(GRID_W, kh * GRID_W)
    scale = dh ** -0.5
    nwin = kh * GRID_W

    def row_fn(r):
        rs = jnp.clip(r - kh // 2, 0, rows - kh)
        kb = lax.dynamic_slice_in_dim(kg, rs, kh, axis=1).reshape(B_, nwin, H, dh)
        vb = lax.dynamic_slice_in_dim(vg, rs, kh, axis=1).reshape(B_, nwin, H, dh)
        qr = lax.dynamic_index_in_dim(qg, r, axis=1, keepdims=False)
        dr_idx = rs + jnp.arange(kh) - r + NA_KH - 1
        bias = rpb[:, dr_idx][:, :, dc_idx]
        bias = jnp.transpose(bias, (0, 2, 1, 3)).reshape(H, GRID_W, nwin).astype(jnp.float32)
        s_win = jnp.einsum('bqhd,bkhd->bhqk', qr, kb).astype(jnp.float32) * scale + bias
        s_win = jnp.where(mask, s_win, NEG_INF)
        s_ctx = jnp.einsum('bqhd,blhd->bhql', qr, k_ctx).astype(jnp.float32) * scale
        p = jax.nn.softmax(jnp.concatenate([s_win, s_ctx], axis=-1), axis=-1).astype(v.dtype)
        return (jnp.einsum('bhqk,bkhd->bqhd', p[..., :nwin], vb)
                + jnp.einsum('bhql,blhd->bqhd', p[..., nwin:], v_ctx))

    out = lax.map(row_fn, jnp.arange(rows))
    return jnp.moveaxis(out, 0, 1).reshape(B_, T, H, dh)


def short_conv3(x, w):
    xp = jnp.pad(x, ((0, 0), (1, 1), (0, 0)))
    return xp[:, :-2] * w[0] + xp[:, 1:-1] * w[1] + xp[:, 2:] * w[2]


def rwkv7_bidir(r, w, k, v, kk, a, s0):
    def both(x):
        return jnp.broadcast_to(x[:, :, None], x.shape[:2] + (2,) + x.shape[2:])

    def seq_dir(x):
        x = x.astype(jnp.float32)
        return jnp.moveaxis(jnp.stack([x[:, :, 0], jnp.flip(x[:, :, 1], axis=1)], axis=0), 2, 0)

    xs = tuple(seq_dir(z) for z in (both(r), w, k, both(v), both(kk), a))

    def step(S, inp):
        rt, wt, kt, vt, kkt, at = inp
        sa = jnp.einsum('dbhij,dbhj->dbhi', S, kkt)
        S = (S * wt[..., None, :] - sa[..., :, None] * (kkt * at)[..., None, :]
             + vt[..., :, None] * kt[..., None, :])
        return S, jnp.einsum('dbhij,dbhj->dbhi', S, rt)

    s_fin, ys = lax.scan(step, s0, xs)
    y = ys[:, 0] + jnp.flip(ys[:, 1], axis=0)
    return jnp.moveaxis(y, 0, 1), s_fin


def peer(x, wq, keys, u_tab, v_tab):
    B_, T, D = x.shape
    n = B_ * T
    xf = x.reshape(n, D)
    q = (xf @ wq).reshape(n, PEER_HEADS, 2, PEER_HALF)
    s = jnp.einsum('nhpd,hpkd->nhpk', q, keys).astype(jnp.float32)
    sv, si = lax.top_k(s, PEER_TOPK)
    cand = (sv[:, :, 0, :, None] + sv[:, :, 1, None, :]).reshape(n, PEER_HEADS, PEER_TOPK * PEER_TOPK)
    cidx = (si[:, :, 0, :, None] * PEER_NKEYS + si[:, :, 1, None, :]).reshape(n, PEER_HEADS, PEER_TOPK * PEER_TOPK)
    top_s, top_p = lax.top_k(cand, PEER_TOPK)
    eidx = jnp.take_along_axis(cidx, top_p, axis=-1).reshape(n, PEER_HEADS * PEER_TOPK)
    gate = jax.nn.softmax(top_s, axis=-1).reshape(n, PEER_HEADS * PEER_TOPK).astype(x.dtype)
    nc = n // PEER_CHUNK

    def chunk(args):
        xc, ic, gc = args
        hid = jax.nn.gelu(jnp.einsum('cd,ckd->ck', xc, u_tab[ic]), approximate=False)
        return jnp.einsum('ck,ckd->cd', gc * hid, v_tab[ic])

    out = lax.map(chunk, (xf.reshape(nc, PEER_CHUNK, D), eidx.reshape(nc, PEER_CHUNK, -1),
                          gate.reshape(nc, PEER_CHUNK, -1)))
    return out.reshape(B_, T, D)


def even_mixer(h, pos, ctx, w_in, lam, subln_g, q_norm_g, w_uq, kv_norm_g, w_ukv, lam_init):
    B_, T, _ = h.shape
    qa, ka, va, cq, ckv, kpe = _split(h @ w_in, EVEN_SIZES)
    qa = qa.reshape(B_, T, A_HEADS, 2, A_QK)
    ka = ka.reshape(B_, T, A_HEADS, 2, A_QK)
    va = va.reshape(B_, T, A_HEADS, A_V)
    ckv = rms_norm(ckv, kv_norm_g)
    qb = (rms_norm(cq, q_norm_g) @ w_uq).reshape(B_, T, B_HEADS, B_NOPE + B_ROPE)
    own = (ka.reshape(B_, T, A_HEADS, 2 * A_QK), va, ckv, kpe)
    if pos is not None:
        qa = axial_rope(qa, *pos)
        ka = axial_rope(ka, *pos)
        qb = jnp.concatenate([qb[..., :B_NOPE], axial_rope(qb[..., B_NOPE:], *pos)], axis=-1)
        kpe = axial_rope(kpe, *pos)
    if ctx is not None:
        ck, cv, cckv, ckpe = ctx
        L = ck.shape[1]
        ka = jnp.concatenate([ck.reshape(B_, L, A_HEADS, 2, A_QK), ka], axis=1)
        va = jnp.concatenate([cv, va], axis=1)
        ckv = jnp.concatenate([cckv, ckv], axis=1)
        kpe = jnp.concatenate([ckpe, kpe], axis=1)
    S = ka.shape[1]
    lamf = lam.astype(jnp.float32)
    lam_val = jnp.exp(jnp.sum(lamf[0] * lamf[1])) - jnp.exp(jnp.sum(lamf[2] * lamf[3])) + lam_init
    oa = diff_attention(qa, ka, va, lam_val, A_QK ** -0.5)
    oa = rms_norm(oa, subln_g) * (1.0 - lam_init)
    kv = (ckv @ w_ukv).reshape(B_, S, B_HEADS, B_NOPE + B_V)
    kb = jnp.concatenate([kv[..., :B_NOPE],
                          jnp.broadcast_to(kpe[:, :, None, :], (B_, S, B_HEADS, B_ROPE))], axis=-1)
    ob = attention(qb, kb, kv[..., B_NOPE:], (B_NOPE + B_ROPE) ** -0.5)
    out = jnp.concatenate([oa.reshape(B_, T, A_WIDTH), ob.reshape(B_, T, B_WIDTH)], axis=-1)
    return out, (own if ctx is None else None)


def odd_mixer(h, ctx, w_in, rpb, conv_w, w0, w2, a0, a2, g2, k_k, k_a, r_k, ln_g, ln_b):
    B_, T, _ = h.shape
    qc, kc, vc, zd = _split(h @ w_in, ODD_SIZES)
    qc = qc.reshape(B_, T, C_HEADS, C_HD)
    kc = kc.reshape(B_, T, C_HEADS, C_HD)
    vc = vc.reshape(B_, T, C_HEADS, C_HD)
    if ctx is None:
        oc = attention(qc, kc, vc, C_HD ** -0.5)
        s0 = jnp.zeros((2, B_, D_HEADS, D_HS, D_HS), jnp.float32)
    else:
        ck, cv, cst = ctx
        oc = neighbourhood_attention(qc, kc, vc, ck, cv, rpb)
        s0 = jnp.moveaxis(cst, 1, 0).astype(jnp.float32)
    r, k, v, xw, xa, xg = _split(short_conv3(zd, conv_w), D_SIZES)
    xw = xw.reshape(B_, T, 2, D_DECAY_LORA)
    xa = xa.reshape(B_, T, 2, D_A_LORA)
    wlog = -jax.nn.softplus(-(w0 + jnp.einsum('btdr,drc->btdc', jnp.tanh(xw), w2))) - 0.5
    decay = jnp.exp(-jnp.exp(wlog.astype(jnp.float32)))
    a = jax.nn.sigmoid(a0 + jnp.einsum('btdr,drc->btdc', xa, a2))
    g = jax.nn.sigmoid(xg) @ g2

    def hd(z):
        return z.reshape(z.shape[:-1] + (D_HEADS, D_HS))

    kkf = hd(k * k_k).astype(jnp.float32)
    kk = kkf / jnp.maximum(jnp.sqrt(jnp.sum(kkf * kkf, axis=-1, keepdims=True)), 1e-12)
    kd = k[:, :, None] * (1.0 + (a - 1.0) * k_a)
    rh, vh, kdh = hd(r), hd(v), hd(kd)
    y, s_fin = rwkv7_bidir(rh, hd(decay), kdh, vh, kk, hd(a), s0)
    mu = jnp.mean(y, axis=-1, keepdims=True)
    var = jnp.mean(jnp.square(y - mu), axis=-1, keepdims=True)
    yn = (y - mu) * lax.rsqrt(var + D_LN_EPS) * hd(ln_g).astype(jnp.float32) + hd(ln_b).astype(jnp.float32)
    bonus = jnp.sum(jnp.sum(rh[:, :, None] * kdh * r_k, axis=-1), axis=2)[..., None] * vh
    od = ((yn + bonus) * hd(g)).reshape(B_, T, D_WIDTH).astype(h.dtype)
    out = jnp.concatenate([oc.reshape(B_, T, C_WIDTH), od], axis=-1)
    return out, ((kc, vc, jnp.moveaxis(s_fin, 0, 1)) if ctx is None else None)


def setup_inputs(seed: int = 0) -> dict:
    key = jax.random.key(seed)
    ks = iter(jax.random.split(key, 48))
    D = D_MODEL

    def nrm(shape, s=1.0):
        return jax.random.normal(next(ks), shape, jnp.float32) * s

    def gain(shape):
        return 1.0 + nrm(shape, 0.02)

    conv_base = jnp.array([0.25, 0.5, 0.25], jnp.float32)[:, None]
    return {
        'x_prompt': nrm((BATCH, SEQ, D)),
        'x_sample': nrm((DEC_BATCH, DEC_SEQ, D)),
        'cache_a_k': nrm((DEC_BATCH, N_EVEN, PAST_LEN, A_HEADS, 2 * A_QK)),
        'cache_a_v': nrm((DEC_BATCH, N_EVEN, PAST_LEN, A_HEADS, A_V)),
        'cache_b_ckv': nrm((DEC_BATCH, N_EVEN, PAST_LEN, B_KV_RANK)),
        'cache_b_kpe': nrm((DEC_BATCH, N_EVEN, PAST_LEN, B_ROPE)),
        'cache_c_k': nrm((DEC_BATCH, N_ODD, PAST_LEN, C_HEADS, C_HD)),
        'cache_c_v': nrm((DEC_BATCH, N_ODD, PAST_LEN, C_HEADS, C_HD)),
        'state_d': nrm((DEC_BATCH, N_ODD, 2, D_HEADS, D_HS, D_HS), 0.3),
        'c': nrm((DEC_BATCH, D)),
        'c_ctx': nrm((D,)),
        'ada_w': nrm((DEPTH, D, 6 * D), 0.5 * D ** -0.5),
        'ada_b': nrm((DEPTH, 6 * D), 0.01),
        'norm1_g': gain((DEPTH, D)),
        'norm2_g': gain((DEPTH, D)),
        'w_out': nrm((DEPTH, D, D), D ** -0.5),
        'peer_wq': nrm((DEPTH, D, PEER_HEADS * PEER_DQ), D ** -0.5),
        'peer_keys': nrm((DEPTH, PEER_HEADS, 2, PEER_NKEYS, PEER_HALF), PEER_HALF ** -0.5),
        'peer_u': nrm((DEPTH, PEER_N, D), D ** -0.5),
        'peer_v': nrm((DEPTH, PEER_N, D), PEER_HEADS ** -0.5),
        'final_g': gain((D,)),
        'ab_w_in': nrm((N_EVEN, D, EVEN_IN), D ** -0.5),
        'a_lam': nrm((N_EVEN, 4, A_QK), 0.1),
        'a_subln_g': gain((N_EVEN, A_V)),
        'b_q_norm_g': gain((N_EVEN, B_Q_RANK)),
        'b_w_uq': nrm((N_EVEN, B_Q_RANK, B_HEADS * (B_NOPE + B_ROPE)), B_Q_RANK ** -0.5),
        'b_kv_norm_g': gain((N_EVEN, B_KV_RANK)),
        'b_w_ukv': nrm((N_EVEN, B_KV_RANK, B_HEADS * (B_NOPE + B_V)), B_KV_RANK ** -0.5),
        'cd_w_in': nrm((N_ODD, D, ODD_IN), D ** -0.5),
        'c_rpb': nrm((N_ODD, C_HEADS, 2 * NA_KH - 1, 2 * NA_KW - 1), 0.2),
        'd_conv': conv_base + nrm((N_ODD, 3, D_IN), 0.1),
        'd_w0': -1.5 + nrm((N_ODD, 2, D_WIDTH), 0.8),
        'd_w2': nrm((N_ODD, 2, D_DECAY_LORA, D_WIDTH), 0.5 * D_DECAY_LORA ** -0.5),
        'd_a0': nrm((N_ODD, 2, D_WIDTH), 0.5),
        'd_a2': nrm((N_ODD, 2, D_A_LORA, D_WIDTH), 0.5 * D_A_LORA ** -0.5),
        'd_g2': nrm((N_ODD, D_GATE_LORA, D_WIDTH), D_GATE_LORA ** -0.5),
        'd_k_k': 0.85 + nrm((N_ODD, D_WIDTH), 0.05),
        'd_k_a': 1.0 + nrm((N_ODD, D_WIDTH), 0.05),
        'd_r_k': nrm((N_ODD, D_HEADS, D_HS), 0.1),
        'd_ln_g': gain((N_ODD, D_WIDTH)),
        'd_ln_b': nrm((N_ODD, D_WIDTH), 0.01),
    }


def reference(x_prompt, x_sample, cache_a_k, cache_a_v, cache_b_ckv, cache_b_kpe, cache_c_k, cache_c_v, state_d, c, c_ctx, ada_w, ada_b, norm1_g, norm2_g, w_out, peer_wq, peer_keys, peer_u, peer_v, final_g, ab_w_in, a_lam, a_subln_g, b_q_norm_g, b_w_uq, b_kv_norm_g, b_w_ukv, cd_w_in, c_rpb, d_conv, d_w0, d_w2, d_a0, d_a2, d_g2, d_k_k, d_k_a, d_r_k, d_ln_g, d_ln_b):
    T = x_sample.shape[1]
    t = jnp.arange(T)
    pos = (t // GRID_W, t % GRID_W)
    xc, xl = x_prompt, x_sample
    new_ak, new_av, new_bc, new_bp, new_ck, new_cv, new_sd = [], [], [], [], [], [], []
    for l in range(DEPTH):
        i = l // 2
        sh1c, sc1c, g1c, sh2c, sc2c, g2c = _ada(c_ctx[None, None, :], ada_w[l], ada_b[l])
        sh1l, sc1l, g1l, sh2l, sc2l, g2l = _ada(c[:, None, :], ada_w[l], ada_b[l])
        hc = _modulate(xc, norm1_g[l], sh1c, sc1c)
        hl = _modulate(xl, norm1_g[l], sh1l, sc1l)
        if l % 2 == 0:
            prm = (ab_w_in[i], a_lam[i], a_subln_g[i], b_q_norm_g[i], b_w_uq[i], b_kv_norm_g[i], b_w_ukv[i],
                   0.8 - 0.6 * math.exp(-0.3 * l))
            oc, (ak, av, bc, bp) = even_mixer(hc, None, None, *prm)
            ol, _ = even_mixer(hl, pos, (cache_a_k[:, i], cache_a_v[:, i], cache_b_ckv[:, i], cache_b_kpe[:, i]), *prm)
            new_ak.append(ak)
            new_av.append(av)
            new_bc.append(bc)
            new_bp.append(bp)
        else:
            prm = (cd_w_in[i], c_rpb[i], d_conv[i], d_w0[i], d_w2[i], d_a0[i], d_a2[i], d_g2[i],
                   d_k_k[i], d_k_a[i], d_r_k[i], d_ln_g[i], d_ln_b[i])
            oc, (ck, cv, sd) = odd_mixer(hc, None, *prm)
            ol, _ = odd_mixer(hl, (cache_c_k[:, i], cache_c_v[:, i], state_d[:, i]), *prm)
            new_ck.append(ck)
            new_cv.append(cv)
            new_sd.append(sd)
        xc = xc + g1c * (oc @ w_out[l])
        xl = xl + g1l * (ol @ w_out[l])
        xc = xc + g2c * peer(_modulate(xc, norm2_g[l], sh2c, sc2c), peer_wq[l], peer_keys[l], peer_u[l], peer_v[l])
        xl = xl + g2l * peer(_modulate(xl, norm2_g[l], sh2l, sc2l), peer_wq[l], peer_keys[l], peer_u[l], peer_v[l])
    y_prompt = rms_norm(xc, final_g)
    y_sample = rms_norm(xl, final_g)
    return (y_prompt, y_sample, jnp.stack(new_ak, axis=1), jnp.stack(new_av, axis=1), jnp.stack(new_bc, axis=1), jnp.stack(new_bp, axis=1), jnp.stack(new_ck, axis=1), jnp.stack(new_cv, axis=1), jnp.stack(new_sd, axis=1))
```

```python
import functools
import math

import numpy as np
import jax
import jax.numpy as jnp
from jax import lax
from jax.experimental import pallas as pl
from jax.experimental.pallas import tpu as pltpu

F32 = jnp.float32
BF16 = jnp.bfloat16

D_MODEL = 2048
BATCH = 16
SEQ = 256
DEPTH = 4
DEC_BATCH = 2
DEC_SEQ = 1024
PAST_LEN = 512
GRID_W = 64
GRID_ROWS = DEC_SEQ // GRID_W
ROPE_BASE = 10000.0
RMS_EPS = 1e-6
NEG_INF = -1e30
N_CTX = BATCH * SEQ
N_SMP = DEC_BATCH * DEC_SEQ
N_TOK = N_CTX + N_SMP
N_COND = 1 + DEC_BATCH

HEADS = 8
HEAD_DIM = 128
A_QK = 64
B_Q_RANK = 512
B_KV_RANK = 256
B_NOPE = 128
B_ROPE = 64
NA_KH = 8
NA_KW = 16
NA_WIN = NA_KH * GRID_W
D_HS = 64
D_WIDTH = 1024
D_HEADS = 16
D_LORA = 64
D_GATE_LORA = 128
D_IN = 3 * D_WIDTH + 2 * D_LORA + 2 * D_LORA + D_GATE_LORA
D_LN_EPS = 64e-5
EVEN_IN = 3904
ODD_IN = 6528
PEER_HEADS = 8
PEER_NKEYS = 128
PEER_HALF = 128
PEER_TOPK = 16

LANES = 128
VMEM_LIMIT = 56 * 1024 * 1024


def _params(*sem):
    return pltpu.CompilerParams(dimension_semantics=sem, vmem_limit_bytes=VMEM_LIMIT)


def _dot_nt(a, b):
    return lax.dot_general(a, b, (((1,), (1,)), ((), ())), preferred_element_type=F32)


def _dot_tn(a, b):
    return lax.dot_general(a, b, (((0,), (0,)), ((), ())), preferred_element_type=F32)


def _cond_of_tile(i, tm):
    n_ctx_tiles = N_CTX // tm
    per_batch = DEC_SEQ // tm
    return jnp.where(i < n_ctx_tiles, 0, 1 + (i - n_ctx_tiles) // per_batch)


def _ada_kernel(c_ref, w_ref, b_ref, o_ref):
    c = c_ref[...]
    sc = c * (1.0 / (1.0 + jnp.exp(-c)))
    o_ref[0] = jnp.dot(sc.astype(BF16), w_ref[0].astype(BF16), preferred_element_type=F32) + b_ref[0]


def _ada_all(cond8, ada_w, ada_b):
    tn = 1024
    n_out = ada_w.shape[-1]
    return pl.pallas_call(
        _ada_kernel,
        grid=(DEPTH, n_out // tn),
        in_specs=[pl.BlockSpec((8, D_MODEL), lambda l, j: (0, 0)),
                  pl.BlockSpec((1, D_MODEL, tn), lambda l, j: (l, 0, j)),
                  pl.BlockSpec((1, 1, tn), lambda l, j: (l, 0, j))],
        out_specs=pl.BlockSpec((1, 8, tn), lambda l, j: (l, 0, j)),
        out_shape=jax.ShapeDtypeStruct((DEPTH, 8, n_out), F32),
        compiler_params=_params("parallel", "parallel"),
        name="ada",
    )(cond8, ada_w, ada_b.reshape(DEPTH, 1, n_out))


def _modulate_kernel(x_ref, g_ref, sh_ref, sc_ref, o_ref):
    x = x_ref[...]
    y = x * lax.rsqrt(jnp.mean(x * x, axis=-1, keepdims=True) + RMS_EPS) * g_ref[...]
    o_ref[...] = (y * (1.0 + sc_ref[0]) + sh_ref[0]).astype(o_ref.dtype)


def _modulate(x, g, shift, scale, out_dtype=BF16, tm=256):
    n, d = x.shape
    return pl.pallas_call(
        _modulate_kernel,
        grid=(n // tm,),
        in_specs=[pl.BlockSpec((tm, d), lambda i: (i, 0)),
                  pl.BlockSpec((1, d), lambda i: (0, 0)),
                  pl.BlockSpec((1, 1, d), lambda i: (_cond_of_tile(i, tm), 0, 0)),
                  pl.BlockSpec((1, 1, d), lambda i: (_cond_of_tile(i, tm), 0, 0))],
        out_specs=pl.BlockSpec((tm, d), lambda i: (i, 0)),
        out_shape=jax.ShapeDtypeStruct((n, d), out_dtype),
        compiler_params=_params("parallel"),
        name="modulate",
    )(x, g.reshape(1, d), shift, scale)


def _rmsnorm_kernel(x_ref, g_ref, o_ref):
    x = x_ref[...]
    y = x * lax.rsqrt(jnp.mean(x * x, axis=-1, keepdims=True) + RMS_EPS) * g_ref[...]
    o_ref[...] = y.astype(o_ref.dtype)


def _rmsnorm(x, g, out_dtype=F32, tm=512):
    n, d = x.shape
    return pl.pallas_call(
        _rmsnorm_kernel,
        grid=(n // tm,),
        in_specs=[pl.BlockSpec((tm, d), lambda i: (i, 0)),
                  pl.BlockSpec((1, d), lambda i: (0, 0))],
        out_specs=pl.BlockSpec((tm, d), lambda i: (i, 0)),
        out_shape=jax.ShapeDtypeStruct((n, d), out_dtype),
        compiler_params=_params("parallel"),
        name="rmsnorm",
    )(x, g.reshape(1, d))


def _mm_kernel(x_ref, w_ref, o_ref):
    o_ref[...] = jnp.dot(x_ref[...].astype(BF16), w_ref[...], preferred_element_type=F32).astype(o_ref.dtype)


def _matmul(x, w, out_dtype=F32, tm=512, tn=512):
    m, k = x.shape
    n = w.shape[1]
    tn = min(tn, n)
    assert m % tm == 0 and n % tn == 0, (m, n, tm, tn)
    return pl.pallas_call(
        _mm_kernel,
        grid=(m // tm, n // tn),
        in_specs=[pl.BlockSpec((tm, k), lambda i, j: (i, 0)),
                  pl.BlockSpec((k, tn), lambda i, j: (0, j))],
        out_specs=pl.BlockSpec((tm, tn), lambda i, j: (i, j)),
        out_shape=jax.ShapeDtypeStruct((m, n), out_dtype),
        compiler_params=_params("parallel", "parallel"),
        name="matmul",
    )(x, w)


def _mm_res_kernel(x_ref, w_ref, res_ref, gate_ref, o_ref):
    acc = jnp.dot(x_ref[...].astype(BF16), w_ref[...], preferred_element_type=F32)
    o_ref[...] = res_ref[...] + gate_ref[0] * acc


def _matmul_residual(x, w, res, gate, tm=512, tn=512):
    m, k = x.shape
    n = w.shape[1]
    return pl.pallas_call(
        _mm_res_kernel,
        grid=(m // tm, n // tn),
        in_specs=[pl.BlockSpec((tm, k), lambda i, j: (i, 0)),
                  pl.BlockSpec((k, tn), lambda i, j: (0, j)),
                  pl.BlockSpec((tm, tn), lambda i, j: (i, j)),
                  pl.BlockSpec((1, 1, tn), lambda i, j: (_cond_of_tile(i, tm), 0, j))],
        out_specs=pl.BlockSpec((tm, tn), lambda i, j: (i, j)),
        out_shape=jax.ShapeDtypeStruct((m, n), F32),
        compiler_params=_params("parallel", "parallel"),
        name="matmul_residual",
    )(x, w, res, gate)


def _rope_tables():
    nf = 16
    inv = ROPE_BASE ** (-np.arange(nf, dtype=np.float64) / nf)
    t = np.arange(DEC_SEQ)
    rows, cols = t // GRID_W, t % GRID_W
    lane = np.arange(64)
    pos = np.where(lane[None, :] < 32, rows[:, None], cols[:, None]).astype(np.float32)
    ang = (pos * inv[lane % nf][None, :].astype(np.float32)).astype(np.float32)
    first = (lane % 32) < nf
    cos = np.cos(ang.astype(np.float64))
    sin = np.sin(ang.astype(np.float64)) * np.where(first, -1.0, 1.0)[None, :]
    cos = np.tile(cos, (1, 2)).astype(np.float32)
    sin = np.tile(sin, (1, 2)).astype(np.float32)
    return jnp.asarray(cos), jnp.asarray(sin)


def _rope_kernel(x_ref, cos_ref, sin_ref, o_ref):
    cos = cos_ref[...]
    sin = sin_ref[...]
    lane = lax.broadcasted_iota(jnp.int32, cos.shape, 1)
    first = (lane % 32) < 16
    for c in range(x_ref.shape[1] // LANES):
        x = x_ref[:, c * LANES:(c + 1) * LANES].astype(F32)
        partner = jnp.where(first, pltpu.roll(x, LANES - 16, 1), pltpu.roll(x, 16, 1))
        o_ref[:, c * LANES:(c + 1) * LANES] = (x * cos + partner * sin).astype(o_ref.dtype)


def _rope(x, cos, sin, out_dtype=BF16, tm=256):
    n, w = x.shape
    per = DEC_SEQ // tm
    return pl.pallas_call(
        _rope_kernel,
        grid=(n // tm,),
        in_specs=[pl.BlockSpec((tm, w), lambda i: (i, 0)),
                  pl.BlockSpec((tm, LANES), lambda i: (i % per, 0)),
                  pl.BlockSpec((tm, LANES), lambda i: (i % per, 0))],
        out_specs=pl.BlockSpec((tm, w), lambda i: (i, 0)),
        out_shape=jax.ShapeDtypeStruct((n, w), out_dtype),
        compiler_params=_params("parallel"),
        name="rope",
    )(x, cos, sin)


def _softmax(s):
    p = jnp.exp(s - jnp.max(s, axis=-1, keepdims=True))
    return p / jnp.sum(p, axis=-1, keepdims=True)


def _diff_attn_kernel(lam_ref, g_ref, q_ref, k_ref, v_ref, o_ref, *, lam_init):
    lam = lam_ref[...]
    l1 = jnp.sum(jnp.sum(lam[0:1] * lam[1:2], axis=-1, keepdims=True), axis=0, keepdims=True)
    l2 = jnp.sum(jnp.sum(lam[2:3] * lam[3:4], axis=-1, keepdims=True), axis=0, keepdims=True)
    lam_val = jnp.exp(l1) - jnp.exp(l2) + lam_init
    scale = A_QK ** -0.5
    for h in range(HEADS):
        sl = slice(h * HEAD_DIM, (h + 1) * HEAD_DIM)
        q = q_ref[0, :, sl]
        k = k_ref[0, :, sl]
        p1 = _softmax(_dot_nt(q[:, :A_QK], k[:, :A_QK]) * scale)
        p2 = _softmax(_dot_nt(q[:, A_QK:], k[:, A_QK:]) * scale)
        a = p1 - lam_val * p2
        o = jnp.dot(a.astype(BF16), v_ref[0, :, sl], preferred_element_type=F32)
        o = o * lax.rsqrt(jnp.mean(o * o, axis=-1, keepdims=True) + RMS_EPS) * g_ref[...]
        o_ref[0, :, sl] = (o * (1.0 - lam_init)).astype(o_ref.dtype)


def _diff_attention(q, k, v, lam, subln_g, lam_init, tq=256):
    b, t, w = q.shape
    s = k.shape[1]
    return pl.pallas_call(
        functools.partial(_diff_attn_kernel, lam_init=lam_init),
        grid=(b, t // tq),
        in_specs=[pl.BlockSpec((4, A_QK), lambda i, j: (0, 0)),
                  pl.BlockSpec((1, HEAD_DIM), lambda i, j: (0, 0)),
                  pl.BlockSpec((1, tq, w), lambda i, j: (i, j, 0)),
                  pl.BlockSpec((1, s, w), lambda i, j: (i, 0, 0)),
                  pl.BlockSpec((1, s, w), lambda i, j: (i, 0, 0))],
        out_specs=pl.BlockSpec((1, tq, w), lambda i, j: (i, j, 0)),
        out_shape=jax.ShapeDtypeStruct((b, t, w), BF16),
        compiler_params=_params("parallel", "parallel"),
        name="diff_attention",
    )(lam, subln_g.reshape(1, HEAD_DIM), q, k, v)


def _mla_attn_kernel(qn_ref, qr_ref, kn_ref, kr_ref, v_ref, o_ref):
    scale = (B_NOPE + B_ROPE) ** -0.5
    kr = kr_ref[0]
    for h in range(HEADS):
        sl = slice(h * HEAD_DIM, (h + 1) * HEAD_DIM)
        s = _dot_nt(qn_ref[0, :, sl], kn_ref[0, :, sl])
        s = s + _dot_nt(qr_ref[0, :, h * B_ROPE:(h + 1) * B_ROPE], kr)
        p = _softmax(s * scale)
        o = jnp.dot(p.astype(BF16), v_ref[0, :, sl], preferred_element_type=F32)
        o_ref[0, :, sl] = o.astype(o_ref.dtype)


def _mla_attention(qn, qr, kn, kr, v, tq=256):
    b, t, w = qn.shape
    s = kn.shape[1]
    return pl.pallas_call(
        _mla_attn_kernel,
        grid=(b, t // tq),
        in_specs=[pl.BlockSpec((1, tq, w), lambda i, j: (i, j, 0)),
                  pl.BlockSpec((1, tq, HEADS * B_ROPE), lambda i, j: (i, j, 0)),
                  pl.BlockSpec((1, s, w), lambda i, j: (i, 0, 0)),
                  pl.BlockSpec((1, s, B_ROPE), lambda i, j: (i, 0, 0)),
                  pl.BlockSpec((1, s, w), lambda i, j: (i, 0, 0))],
        out_specs=pl.BlockSpec((1, tq, w), lambda i, j: (i, j, 0)),
        out_shape=jax.ShapeDtypeStruct((b, t, w), BF16),
        compiler_params=_params("parallel", "parallel"),
        name="mla_attention",
    )(qn, qr, kn, kr, v)


def _plain_attn_kernel(q_ref, k_ref, v_ref, o_ref):
    scale = HEAD_DIM ** -0.5
    for h in range(HEADS):
        sl = slice(h * HEAD_DIM, (h + 1) * HEAD_DIM)
        p = _softmax(_dot_nt(q_ref[0, :, sl], k_ref[0, :, sl]) * scale)
        o = jnp.dot(p.astype(BF16), v_ref[0, :, sl], preferred_element_type=F32)
        o_ref[0, :, sl] = o.astype(o_ref.dtype)


def _plain_attention(q, k, v, tq=256):
    b, t, w = q.shape
    s = k.shape[1]
    return pl.pallas_call(
        _plain_attn_kernel,
        grid=(b, t // tq),
        in_specs=[pl.BlockSpec((1, tq, w), lambda i, j: (i, j, 0)),
                  pl.BlockSpec((1, s, w), lambda i, j: (i, 0, 0)),
                  pl.BlockSpec((1, s, w), lambda i, j: (i, 0, 0))],
        out_specs=pl.BlockSpec((1, tq, w), lambda i, j: (i, j, 0)),
        out_shape=jax.ShapeDtypeStruct((b, t, w), BF16),
        compiler_params=_params("parallel", "parallel"),
        name="plain_attention",
    )(q, k, v)


def _na_window_start(r):
    return jnp.clip(r - NA_KH // 2, 0, GRID_ROWS - NA_KH)


def _na_attn_kernel(q_ref, k_ref, v_ref, kc_ref, vc_ref, bias_ref, o_ref):
    scale = HEAD_DIM ** -0.5
    r = pl.program_id(1)
    start = pl.multiple_of(_na_window_start(r) * GRID_W, GRID_W)
    for h in range(HEADS):
        sl = slice(h * HEAD_DIM, (h + 1) * HEAD_DIM)
        q = q_ref[0, :, sl]
        kw = k_ref[0, pl.ds(start, NA_WIN), sl]
        vw = v_ref[0, pl.ds(start, NA_WIN), sl]
        s_win = _dot_nt(q, kw) * scale + bias_ref[0, h]
        s_ctx = _dot_nt(q, kc_ref[0, :, sl]) * scale
        m = jnp.maximum(jnp.max(s_win, axis=-1, keepdims=True), jnp.max(s_ctx, axis=-1, keepdims=True))
        p_win = jnp.exp(s_win - m)
        p_ctx = jnp.exp(s_ctx - m)
        den = jnp.sum(p_win, axis=-1, keepdims=True) + jnp.sum(p_ctx, axis=-1, keepdims=True)
        o = (jnp.dot(p_win.astype(BF16), vw, preferred_element_type=F32)
             + jnp.dot(p_ctx.astype(BF16), vc_ref[0, :, sl], preferred_element_type=F32))
        o_ref[0, :, sl] = (o / den).astype(o_ref.dtype)


def _na_bias_tables(rpb):
    cols = np.arange(GRID_W)
    cs = np.clip(cols - NA_KW // 2, 0, GRID_W - NA_KW)
    col_in = (cols[None, :] >= cs[:, None]) & (cols[None, :] < cs[:, None] + NA_KW)
    dc_idx = np.clip(cols[None, :] - cols[:, None] + NA_KW - 1, 0, 2 * NA_KW - 2)
    mask = np.broadcast_to(col_in[:, None, :], (GRID_W, NA_KH, GRID_W)).reshape(GRID_W, NA_WIN)
    tables = []
    for delta in range(NA_KH):
        dr_idx = np.arange(NA_KH) - delta + NA_KH - 1
        b = rpb[:, dr_idx][:, :, dc_idx]
        b = jnp.transpose(b, (0, 2, 1, 3)).reshape(HEADS, GRID_W, NA_WIN)
        tables.append(jnp.where(mask[None], b, NEG_INF))
    return jnp.stack(tables).astype(F32)


def _na_attention(q, k, v, kc, vc, bias):
    b, t, w = q.shape
    l = kc.shape[1]
    return pl.pallas_call(
        _na_attn_kernel,
        grid=(b, GRID_ROWS),
        in_specs=[pl.BlockSpec((1, GRID_W, w), lambda i, r: (i, r, 0)),
                  pl.BlockSpec((1, t, w), lambda i, r: (i, 0, 0)),
                  pl.BlockSpec((1, t, w), lambda i, r: (i, 0, 0)),
                  pl.BlockSpec((1, l, w), lambda i, r: (i, 0, 0)),
                  pl.BlockSpec((1, l, w), lambda i, r: (i, 0, 0)),
                  pl.BlockSpec((1, HEADS, GRID_W, NA_WIN), lambda i, r: (r - _na_window_start(r), 0, 0, 0))],
        out_specs=pl.BlockSpec((1, GRID_W, w), lambda i, r: (i, r, 0)),
        out_shape=jax.ShapeDtypeStruct((b, t, w), BF16),
        compiler_params=_params("parallel", "parallel"),
        name="na_attention",
    )(q, k, v, kc, vc, bias)


CONV_CW = 384


def _conv3_kernel(x_ref, w_ref, o_ref):
    x = x_ref[...]
    t = x.shape[0]
    row = lax.broadcasted_iota(jnp.int32, x.shape, 0)
    prev = jnp.where(row == 0, 0.0, pltpu.roll(x, 1, 0))
    nxt = jnp.where(row == t - 1, 0.0, pltpu.roll(x, t - 1, 0))
    w = w_ref[...]
    o_ref[...] = prev * w[0:1] + x * w[1:2] + nxt * w[2:3]


def _conv3(z, conv_w, seq, row0, n_seq):
    col0 = (3 * D_WIDTH) // CONV_CW
    blk0 = row0 // seq
    return pl.pallas_call(
        _conv3_kernel,
        grid=(n_seq, D_IN // CONV_CW),
        in_specs=[pl.BlockSpec((seq, CONV_CW), lambda i, j: (blk0 + i, col0 + j)),
                  pl.BlockSpec((3, CONV_CW), lambda i, j: (0, j))],
        out_specs=pl.BlockSpec((seq, CONV_CW), lambda i, j: (i, j)),
        out_shape=jax.ShapeDtypeStruct((n_seq * seq, D_IN), F32),
        compiler_params=_params("parallel", "parallel"),
        name="conv3",
    )(z, conv_w)


def _dot_f32(a, b):
    return jnp.dot(a, b, preferred_element_type=F32, precision=lax.Precision.HIGHEST)


def _d_prep_kernel(x_ref, w0_ref, w2_ref, a0_ref, a2_ref, g2_ref, dec_ref, a_ref, g_ref):
    x = x_ref[...]
    xw = x[:, 0:2 * D_LORA]
    xa = x[:, 2 * D_LORA:4 * D_LORA]
    xg = x[:, 4 * D_LORA:]
    for d in range(2):
        u = w0_ref[d] + _dot_f32(jnp.tanh(xw[:, d * D_LORA:(d + 1) * D_LORA]), w2_ref[d])
        nu = -u
        softplus = jnp.maximum(nu, 0.0) + jnp.log1p(jnp.exp(-jnp.abs(nu)))
        wlog = -softplus - 0.5
        dec_ref[d] = jnp.exp(-jnp.exp(wlog))
        av = a0_ref[d] + _dot_f32(xa[:, d * D_LORA:(d + 1) * D_LORA], a2_ref[d])
        a_ref[d] = 1.0 / (1.0 + jnp.exp(-av))
    g_ref[...] = _dot_f32(1.0 / (1.0 + jnp.exp(-xg)), g2_ref[...])


def _d_prep(zc, w0, w2, a0, a2, g2, tm=256):
    n = zc.shape[0]
    cb = (3 * D_WIDTH) // CONV_CW
    return pl.pallas_call(
        _d_prep_kernel,
        grid=(n // tm,),
        in_specs=[pl.BlockSpec((tm, CONV_CW), lambda i: (i, cb)),
                  pl.BlockSpec((2, 1, D_WIDTH), lambda i: (0, 0, 0)),
                  pl.BlockSpec((2, D_LORA, D_WIDTH), lambda i: (0, 0, 0)),
                  pl.BlockSpec((2, 1, D_WIDTH), lambda i: (0, 0, 0)),
                  pl.BlockSpec((2, D_LORA, D_WIDTH), lambda i: (0, 0, 0)),
                  pl.BlockSpec((D_GATE_LORA, D_WIDTH), lambda i: (0, 0))],
        out_specs=[pl.BlockSpec((2, tm, D_WIDTH), lambda i: (0, i, 0)),
                   pl.BlockSpec((2, tm, D_WIDTH), lambda i: (0, i, 0)),
                   pl.BlockSpec((tm, D_WIDTH), lambda i: (i, 0))],
        out_shape=[jax.ShapeDtypeStruct((2, n, D_WIDTH), F32),
                   jax.ShapeDtypeStruct((2, n, D_WIDTH), F32),
                   jax.ShapeDtypeStruct((n, D_WIDTH), F32)],
        compiler_params=_params("parallel"),
        name="d_prep",
    )(zc, w0.reshape(2, 1, D_WIDTH), w2, a0.reshape(2, 1, D_WIDTH), a2, g2)


SCAN_TT = 32


def _rwkv_scan_kernel(r_ref, k_ref, v_ref, w_ref, a_ref, kkp_ref, kap_ref, rkp_ref, s0_ref,
                      y_ref, bon_ref, sfin_ref, s_ref, tmp_ref):
    tb = pl.program_id(1)

    @pl.when(tb == 0)
    def _():
        s_ref[...] = s0_ref[...]

    kkp = kkp_ref[...]
    kap = kap_ref[...]
    rkp = rkp_ref[...]

    def step(t, carry):
        r = r_ref[t]
        k = k_ref[t]
        v = v_ref[t]
        a = a_ref[t]
        kkf = k * kkp
        nrm = jnp.sqrt(jnp.sum(kkf * kkf, axis=0, keepdims=True))
        kk = kkf / jnp.maximum(nrm, 1e-12)
        kd = k * (1.0 + (a - 1.0) * kap)
        bon_ref[t] = jnp.sum(r * kd * rkp, axis=0, keepdims=True)
        tmp_ref[0] = kk
        tmp_ref[1] = kk * a
        tmp_ref[2] = kd
        sa = jnp.zeros((D_HS, LANES), F32)
        for j in range(D_HS):
            sa = sa + s_ref[j] * tmp_ref[0, pl.ds(j, 1), :]
        y = jnp.zeros((D_HS, LANES), F32)
        for j in range(D_HS):
            sn = (s_ref[j] * w_ref[t, pl.ds(j, 1), :] - sa * tmp_ref[1, pl.ds(j, 1), :]
                  + v * tmp_ref[2, pl.ds(j, 1), :])
            s_ref[j] = sn
            y = y + sn * r_ref[t, pl.ds(j, 1), :]
        y_ref[t] = y
        return carry

    lax.fori_loop(0, SCAN_TT, step, 0)

    @pl.when(tb == pl.num_programs(1) - 1)
    def _():
        sfin_ref[...] = s_ref[...]


def _rwkv_scan(r, k, v, w, a, kkp, kap, rkp, s0):
    t, _, c = r.shape
    seq_spec = pl.BlockSpec((SCAN_TT, D_HS, LANES), lambda ci, ti: (ti, 0, ci))
    par_spec = pl.BlockSpec((D_HS, LANES), lambda ci, ti: (0, ci))
    st_spec = pl.BlockSpec((D_HS, D_HS, LANES), lambda ci, ti: (0, 0, ci))
    return pl.pallas_call(
        _rwkv_scan_kernel,
        grid=(c // LANES, t // SCAN_TT),
        in_specs=[seq_spec] * 5 + [par_spec] * 3 + [st_spec],
        out_specs=[seq_spec,
                   pl.BlockSpec((SCAN_TT, 1, LANES), lambda ci, ti: (ti, 0, ci)),
                   st_spec],
        out_shape=[jax.ShapeDtypeStruct((t, D_HS, c), F32),
                   jax.ShapeDtypeStruct((t, 1, c), F32),
                   jax.ShapeDtypeStruct((D_HS, D_HS, c), F32)],
        scratch_shapes=[pltpu.VMEM((D_HS, D_HS, LANES), F32), pltpu.VMEM((3, D_HS, LANES), F32)],
        compiler_params=_params("parallel", "arbitrary"),
        name="rwkv_scan",
    )(r, k, v, w, a, kkp, kap, rkp, s0)


def _d_post_kernel(yf_ref, yb_ref, bf_ref, bb_ref, v_ref, g_ref, lng_ref, lnb_ref, o_ref):
    y = yf_ref[...] + yb_ref[...]
    mu = jnp.mean(y, axis=1, keepdims=True)
    var = jnp.mean(jnp.square(y - mu), axis=1, keepdims=True)
    yn = (y - mu) * lax.rsqrt(var + D_LN_EPS) * lng_ref[...] + lnb_ref[...]
    o_ref[...] = (yn + (bf_ref[...] + bb_ref[...]) * v_ref[...]) * g_ref[...]


def _d_post(yf, yb, bf, bb, v, g, lng, lnb, tt=32):
    t, _, c = yf.shape
    seq = pl.BlockSpec((tt, D_HS, LANES), lambda ci, ti: (ti, 0, ci))
    bon = pl.BlockSpec((tt, 1, LANES), lambda ci, ti: (ti, 0, ci))
    par = pl.BlockSpec((1, D_HS, LANES), lambda ci, ti: (0, 0, ci))
    return pl.pallas_call(
        _d_post_kernel,
        grid=(c // LANES, t // tt),
        in_specs=[seq, seq, bon, bon, seq, seq, par, par],
        out_specs=seq,
        out_shape=jax.ShapeDtypeStruct((t, D_HS, c), F32),
        compiler_params=_params("parallel", "parallel"),
        name="d_post",
    )(yf, yb, bf, bb, v, g, lng.reshape(1, D_HS, c), lnb.reshape(1, D_HS, c))


def _to_chain(x, b, t):
    return x.reshape(b, t, D_HEADS, D_HS).transpose(1, 3, 0, 2).reshape(t, D_HS, b * D_HEADS)


def _from_chain(x, b, t):
    return x.reshape(t, D_HS, b, D_HEADS).transpose(2, 0, 3, 1).reshape(b * t, D_WIDTH)


def _param_chain(p, n_chain):
    return jnp.tile(p.reshape(D_HEADS, D_HS).T, (1, n_chain // D_HEADS))


def _pad_lanes(x):
    c = x.shape[-1]
    pad = (-c) % LANES
    if pad == 0:
        return x
    return jnp.pad(x, [(0, 0)] * (x.ndim - 1) + [(0, pad)])


def _rwkv_branch(zc, dec, a, g, b, t, s0, k_k, k_a, r_k, ln_g, ln_b):
    nb = b * D_HEADS
    r = _to_chain(zc[:, 0:D_WIDTH], b, t)
    k = _to_chain(zc[:, D_WIDTH:2 * D_WIDTH], b, t)
    v = _to_chain(zc[:, 2 * D_WIDTH:3 * D_WIDTH], b, t)
    gch = _to_chain(g, b, t)

    def both(x):
        return _pad_lanes(jnp.concatenate([x, jnp.flip(x, axis=0)], axis=-1))

    def per_dir(x):
        return _pad_lanes(jnp.concatenate([_to_chain(x[0], b, t), jnp.flip(_to_chain(x[1], b, t), axis=0)], axis=-1))

    if s0 is None:
        s0c = jnp.zeros((D_HS, D_HS, 2 * nb), F32)
    else:
        s0c = s0.transpose(4, 3, 0, 1, 2).reshape(D_HS, D_HS, 2 * nb)
    y, bon, sfin = _rwkv_scan(both(r), both(k), both(v), per_dir(dec), per_dir(a),
                              _pad_lanes(_param_chain(k_k, 2 * nb)), _pad_lanes(_param_chain(k_a, 2 * nb)),
                              _pad_lanes(_param_chain(r_k.reshape(-1), 2 * nb)), _pad_lanes(s0c))
    yf = _pad_lanes(y[:, :, :nb])
    yb = _pad_lanes(jnp.flip(y[:, :, nb:2 * nb], axis=0))
    bf = _pad_lanes(bon[:, :, :nb])
    bb = _pad_lanes(jnp.flip(bon[:, :, nb:2 * nb], axis=0))
    od = _d_post(yf, yb, bf, bb, _pad_lanes(v), _pad_lanes(gch),
                 _pad_lanes(_param_chain(ln_g, nb)), _pad_lanes(_param_chain(ln_b, nb)))
    od = _from_chain(od[:, :, :nb], b, t)
    sfin = sfin[:, :, :2 * nb].reshape(D_HS, D_HS, 2, b, D_HEADS).transpose(3, 2, 4, 1, 0)
    return od, sfin


TOPK_TM = 256
BIG = 1e9


def _extract_topk(s, n_rows):
    shape = s.shape
    iota = lax.broadcasted_iota(jnp.int32, shape, 0).astype(F32)
    iota16 = lax.broadcasted_iota(jnp.int32, (PEER_TOPK, shape[1]), 0)
    rank = jnp.full(shape, BIG, F32)
    vals = jnp.zeros((PEER_TOPK, shape[1]), F32)
    for r in range(PEER_TOPK):
        m = jnp.max(s, axis=0, keepdims=True)
        idx = jnp.min(jnp.where(s == m, iota, BIG), axis=0, keepdims=True)
        hit = iota == idx
        rank = jnp.where(hit, float(r), rank)
        vals = jnp.where(iota16 == r, m, vals)
        s = jnp.where(hit, -jnp.inf, s)
    return rank, vals


def _peer_topk_kernel(q_ref, keys_ref, lim_ref, e1_ref, rb_ref, e2_ref):
    for c in range(TOPK_TM // LANES):
        cs = slice(c * LANES, (c + 1) * LANES)
        q1 = q_ref[cs, 0:PEER_HALF].astype(BF16)
        q2 = q_ref[cs, PEER_HALF:2 * PEER_HALF].astype(BF16)
        s1 = _dot_nt(keys_ref[0, 0], q1)
        s2 = _dot_nt(keys_ref[0, 1], q2)
        rank1, sv1 = _extract_topk(s1, PEER_NKEYS)
        rank2, sv2 = _extract_topk(s2, PEER_NKEYS)
        cand = jnp.concatenate([sv1[a:a + 1] + sv2 for a in range(PEER_TOPK)], axis=0)
        crank, cvals = _extract_topk(cand, PEER_TOPK * PEER_TOPK)
        z = jnp.sum(jnp.exp(cvals - cvals[0:1]), axis=0, keepdims=True)
        sel = jnp.where(crank < BIG, 1.0, 0.0)
        lim = jnp.zeros_like(s1)
        for a in range(PEER_TOPK):
            n_a = jnp.sum(sel[a * PEER_TOPK:(a + 1) * PEER_TOPK], axis=0, keepdims=True)
            lim = jnp.where(rank1 == float(a), n_a, lim)
        lim_ref[0, :, cs] = lim
        e1_ref[0, :, cs] = jnp.exp(s1 - sv1[0:1])
        rb_ref[0, :, cs] = rank2
        e2_ref[0, :, cs] = jnp.exp(s2 - sv2[0:1]) / z


def _peer_topk(q, keys):
    n = q.shape[0]
    out = jax.ShapeDtypeStruct((PEER_HEADS, PEER_NKEYS, n), F32)
    ospec = pl.BlockSpec((1, PEER_NKEYS, TOPK_TM), lambda i, h: (h, 0, i))
    return pl.pallas_call(
        _peer_topk_kernel,
        grid=(n // TOPK_TM, PEER_HEADS),
        in_specs=[pl.BlockSpec((TOPK_TM, 2 * PEER_HALF), lambda i, h: (i, h)),
                  pl.BlockSpec((1, 2, PEER_NKEYS, PEER_HALF), lambda i, h: (h, 0, 0, 0))],
        out_specs=[ospec] * 4,
        out_shape=[out] * 4,
        compiler_params=_params("parallel", "parallel"),
        name="peer_topk",
    )(q, keys)


PEER_TM = 512
PEER_TI = 8


def _gelu(x):
    return 0.5 * x * (1.0 + lax.erf(x * (2.0 ** -0.5)))


def _peer_dense_kernel(x_ref, u_ref, v_ref, lim_ref, e1_ref, rb_ref, e2_ref, res_ref, gate_ref, o_ref, w_ref):
    e = pl.program_id(1)

    @pl.when(e == 0)
    def _():
        o_ref[...] = jnp.zeros_like(o_ref)

    hid = _dot_nt(u_ref[...], x_ref[...])
    for ii in range(PEER_TI):
        for c in range(PEER_TM // LANES):
            cs = slice(c * LANES, (c + 1) * LANES)
            g = jnp.zeros((PEER_NKEYS, LANES), F32)
            for h in range(PEER_HEADS):
                lim = lim_ref[h, ii:ii + 1, cs]
                e1 = e1_ref[h, ii:ii + 1, cs]
                g = g + jnp.where(rb_ref[h, :, cs] < lim, e2_ref[h, :, cs], 0.0) * e1
            act = _gelu(hid[ii * PEER_NKEYS:(ii + 1) * PEER_NKEYS, cs])
            w_ref[ii * PEER_NKEYS:(ii + 1) * PEER_NKEYS, cs] = (g * act).astype(BF16)
    o_ref[...] += _dot_tn(w_ref[...], v_ref[...])

    @pl.when(e == pl.num_programs(1) - 1)
    def _():
        o_ref[...] = res_ref[...] + gate_ref[0] * o_ref[...]


def _peer_dense(xm, u, v, lim, e1, rb, e2, res, gate):
    n, d = xm.shape
    n_exp = u.shape[0]
    te = PEER_TI * PEER_NKEYS
    sel_spec = pl.BlockSpec((PEER_HEADS, PEER_NKEYS, PEER_TM), lambda i, e: (0, 0, i))
    row_spec = pl.BlockSpec((PEER_HEADS, PEER_TI, PEER_TM), lambda i, e: (0, e, i))
    return pl.pallas_call(
        _peer_dense_kernel,
        grid=(n // PEER_TM, n_exp // te),
        in_specs=[pl.BlockSpec((PEER_TM, d), lambda i, e: (i, 0)),
                  pl.BlockSpec((te, d), lambda i, e: (e, 0)),
                  pl.BlockSpec((te, d), lambda i, e: (e, 0)),
                  row_spec, row_spec, sel_spec, sel_spec,
                  pl.BlockSpec((PEER_TM, d), lambda i, e: (i, 0)),
                  pl.BlockSpec((1, 1, d), lambda i, e: (_cond_of_tile(i, PEER_TM), 0, 0))],
        out_specs=pl.BlockSpec((PEER_TM, d), lambda i, e: (i, 0)),
        out_shape=jax.ShapeDtypeStruct((n, d), F32),
        scratch_shapes=[pltpu.VMEM((te, PEER_TM), BF16)],
        compiler_params=_params("parallel", "arbitrary"),
        name="peer_dense",
    )(xm, u, v, lim, e1, rb, e2, res, gate)


def _peer_layer(x, norm_g, shift, scale, gate, wq, keys, u, v):
    xm = _modulate(x, norm_g, shift, scale)
    q = _matmul(xm, wq)
    lim, e1, rb, e2 = _peer_topk(q, keys)
    return _peer_dense(xm, u, v, lim, e1, rb, e2, x, gate)


def _pad_cols(w, mult):
    pad = (-w.shape[1]) % mult
    return jnp.pad(w, ((0, 0), (0, pad))) if pad else w


def _even_mixer(h, w_in, lam, subln_g, q_norm_g, w_uq, kv_norm_g, w_ukv, lam_init,
                cache_k, cache_v, cache_ckv, cache_kpe, rope_cos, rope_sin):
    z = _matmul(h, _pad_cols(w_in, 512).astype(BF16))
    qa, ka, va = z[:, 0:1024], z[:, 1024:2048], z[:, 2048:3072]
    cq, ckv, kpe = z[:, 3072:3584], z[:, 3584:3840], z[:, 3840:3904]
    ckv_n = _rmsnorm(ckv, kv_norm_g)
    cq_n = _rmsnorm(cq, q_norm_g, out_dtype=BF16)
    wq3 = w_uq.reshape(B_Q_RANK, HEADS, B_NOPE + B_ROPE)
    w_uq_r = jnp.concatenate([wq3[:, :, :B_NOPE].reshape(B_Q_RANK, -1), wq3[:, :, B_NOPE:].reshape(B_Q_RANK, -1)], axis=1)
    qb = _matmul(cq_n, w_uq_r.astype(BF16), tn=512)
    qn, qr = qb[:, :HEADS * B_NOPE], qb[:, HEADS * B_NOPE:]
    new = (ka[:N_CTX].reshape(BATCH, SEQ, HEADS, 2 * A_QK), va[:N_CTX].reshape(BATCH, SEQ, HEADS, HEAD_DIM),
           ckv_n[:N_CTX].reshape(BATCH, SEQ, B_KV_RANK), kpe[:N_CTX].reshape(BATCH, SEQ, B_ROPE))

    qa_s = _rope(qa[N_CTX:], rope_cos, rope_sin)
    ka_s = _rope(ka[N_CTX:], rope_cos, rope_sin)
    qr_s = _rope(qr[N_CTX:], rope_cos, rope_sin)
    kpe_s = _rope(z[N_CTX:, 3840:3968], rope_cos, rope_sin)[:, :B_ROPE]

    def with_ctx(cache, own, width):
        return jnp.concatenate([cache.reshape(DEC_BATCH, PAST_LEN, width).astype(own.dtype),
                                own.reshape(DEC_BATCH, DEC_SEQ, width)], axis=1)

    ka_all = with_ctx(cache_k, ka_s, 1024)
    va_all = with_ctx(cache_v, va[N_CTX:].astype(BF16), 1024)
    ckv_all = with_ctx(cache_ckv, ckv_n[N_CTX:], B_KV_RANK)
    kpe_all = with_ctx(cache_kpe, kpe_s, B_ROPE)
    s_all = PAST_LEN + DEC_SEQ

    wkv3 = w_ukv.reshape(B_KV_RANK, HEADS, B_NOPE + HEAD_DIM)
    w_ukv_r = jnp.concatenate([wkv3[:, :, :B_NOPE].reshape(B_KV_RANK, -1), wkv3[:, :, B_NOPE:].reshape(B_KV_RANK, -1)], axis=1)
    ckv_rows = jnp.concatenate([ckv_n[:N_CTX], ckv_all.reshape(DEC_BATCH * s_all, B_KV_RANK)], axis=0)
    kv = _matmul(ckv_rows, w_ukv_r.astype(BF16), out_dtype=BF16)
    kn_c, v_c = kv[:N_CTX, :1024], kv[:N_CTX, 1024:]
    kn_s, v_s = kv[N_CTX:, :1024], kv[N_CTX:, 1024:]

    def c3(x):
        return x.reshape(BATCH, SEQ, -1)

    def s3(x, t=DEC_SEQ):
        return x.reshape(DEC_BATCH, t, -1)

    qa_c = qa[:N_CTX].astype(BF16)
    ka_c = ka[:N_CTX].astype(BF16)
    va_c = va[:N_CTX].astype(BF16)
    oa_c = _diff_attention(c3(qa_c), c3(ka_c), c3(va_c), lam, subln_g, lam_init)
    oa_s = _diff_attention(s3(qa_s), ka_all, va_all, lam, subln_g, lam_init)
    ob_c = _mla_attention(c3(qn[:N_CTX].astype(BF16)), c3(qr[:N_CTX].astype(BF16)), c3(kn_c),
                          c3(kpe[:N_CTX].astype(BF16)), c3(v_c))
    ob_s = _mla_attention(s3(qn[N_CTX:].astype(BF16)), s3(qr_s), s3(kn_s, s_all), kpe_all, s3(v_s, s_all))
    out = jnp.concatenate([
        jnp.concatenate([oa_c.reshape(N_CTX, -1), oa_s.reshape(N_SMP, -1)], axis=0),
        jnp.concatenate([ob_c.reshape(N_CTX, -1), ob_s.reshape(N_SMP, -1)], axis=0)], axis=1)
    return out, new


def _odd_mixer(h, w_in, rpb, conv_w, w0, w2, a0, a2, g2, k_k, k_a, r_k, ln_g, ln_b,
               cache_k, cache_v, state):
    z = _matmul(h, _pad_cols(w_in, 512).astype(BF16))
    qc, kc, vc = z[:, 0:1024], z[:, 1024:2048], z[:, 2048:3072]
    new_k = kc[:N_CTX].reshape(BATCH, SEQ, HEADS, HEAD_DIM)
    new_v = vc[:N_CTX].reshape(BATCH, SEQ, HEADS, HEAD_DIM)
    qb, kb, vb = qc.astype(BF16), kc.astype(BF16), vc.astype(BF16)
    oc_c = _plain_attention(qb[:N_CTX].reshape(BATCH, SEQ, -1), kb[:N_CTX].reshape(BATCH, SEQ, -1),
                            vb[:N_CTX].reshape(BATCH, SEQ, -1))
    oc_s = _na_attention(qb[N_CTX:].reshape(DEC_BATCH, DEC_SEQ, -1), kb[N_CTX:].reshape(DEC_BATCH, DEC_SEQ, -1),
                         vb[N_CTX:].reshape(DEC_BATCH, DEC_SEQ, -1),
                         cache_k.reshape(DEC_BATCH, PAST_LEN, -1).astype(BF16),
                         cache_v.reshape(DEC_BATCH, PAST_LEN, -1).astype(BF16), _na_bias_tables(rpb))

    zc_c = _conv3(z, conv_w, SEQ, 0, BATCH)
    zc_s = _conv3(z, conv_w, DEC_SEQ, N_CTX, DEC_BATCH)
    dec_c, a_c, g_c = _d_prep(zc_c, w0, w2, a0, a2, g2)
    dec_s, a_s, g_s = _d_prep(zc_s, w0, w2, a0, a2, g2)
    od_c, sfin = _rwkv_branch(zc_c, dec_c, a_c, g_c, BATCH, SEQ, None, k_k, k_a, r_k, ln_g, ln_b)
    s0 = jnp.moveaxis(state, 1, 0)
    od_s, _ = _rwkv_branch(zc_s, dec_s, a_s, g_s, DEC_BATCH, DEC_SEQ, s0, k_k, k_a, r_k, ln_g, ln_b)
    out = jnp.concatenate([
        jnp.concatenate([oc_c.reshape(N_CTX, -1), oc_s.reshape(N_SMP, -1)], axis=0),
        jnp.concatenate([od_c, od_s], axis=0).astype(BF16)], axis=1)
    return out, (new_k, new_v, sfin)


def kernel(x_prompt, x_sample, cache_a_k, cache_a_v, cache_b_ckv, cache_b_kpe, cache_c_k, cache_c_v, state_d, c, c_ctx, ada_w, ada_b, norm1_g, norm2_g, w_out, peer_wq, peer_keys, peer_u, peer_v, final_g, ab_w_in, a_lam, a_subln_g, b_q_norm_g, b_w_uq, b_kv_norm_g, b_w_ukv, cd_w_in, c_rpb, d_conv, d_w0, d_w2, d_a0, d_a2, d_g2, d_k_k, d_k_a, d_r_k, d_ln_g, d_ln_b):
    x = jnp.concatenate([x_prompt.reshape(N_CTX, D_MODEL), x_sample.reshape(N_SMP, D_MODEL)], axis=0)
    cond8 = jnp.pad(jnp.concatenate([c_ctx[None, :], c], axis=0), ((0, 8 - N_COND), (0, 0)))
    ada = _ada_all(cond8, ada_w, ada_b)
    rope_cos, rope_sin = _rope_tables()
    new_ak, new_av, new_bc, new_bp, new_ck, new_cv, new_sd = [], [], [], [], [], [], []
    for l in range(DEPTH):
        i = l // 2
        mods = ada[l, :N_COND].reshape(N_COND, 6, 1, D_MODEL)
        sh1, sc1, g1, sh2, sc2, g2 = (mods[:, m] for m in range(6))
        h = _modulate(x, norm1_g[l], sh1, sc1)
        if l % 2 == 0:
            lam_init = 0.8 - 0.6 * math.exp(-0.3 * l)
            o, (ak, av, bc, bp) = _even_mixer(
                h, ab_w_in[i], a_lam[i], a_subln_g[i], b_q_norm_g[i], b_w_uq[i], b_kv_norm_g[i], b_w_ukv[i],
                lam_init, cache_a_k[:, i], cache_a_v[:, i], cache_b_ckv[:, i], cache_b_kpe[:, i], rope_cos, rope_sin)
            new_ak.append(ak)
            new_av.append(av)
            new_bc.append(bc)
            new_bp.append(bp)
        else:
            o, (ck, cv, sd) = _odd_mixer(
                h, cd_w_in[i], c_rpb[i], d_conv[i], d_w0[i], d_w2[i], d_a0[i], d_a2[i], d_g2[i],
                d_k_k[i], d_k_a[i], d_r_k[i], d_ln_g[i], d_ln_b[i], cache_c_k[:, i], cache_c_v[:, i], state_d[:, i])
            new_ck.append(ck)
            new_cv.append(cv)
            new_sd.append(sd)
        x = _matmul_residual(o, w_out[l].astype(BF16), x, g1)
        x = _peer_layer(x, norm2_g[l], sh2, sc2, g2, peer_wq[l].astype(BF16),
                        peer_keys[l].astype(BF16), peer_u[l].astype(BF16), peer_v[l].astype(BF16))
    y = _rmsnorm(x, final_g)
    y_prompt = y[:N_CTX].reshape(BATCH, SEQ, D_MODEL)
    y_sample = y[N_CTX:].reshape(DEC_BATCH, DEC_SEQ, D_MODEL)
    return (y_prompt, y_sample, jnp.stack(new_ak, axis=1), jnp.stack(new_av, axis=1), jnp.stack(new_bc, axis=1),
            jnp.stack(new_bp, axis=1), jnp.stack(new_ck, axis=1), jnp.stack(new_cv, axis=1), jnp.stack(new_sd, axis=1))
```

```python
import functools
import math

import numpy as np
import jax
import jax.numpy as jnp
from jax import lax
from jax.experimental import pallas as pl
from jax.experimental.pallas import tpu as pltpu

F32 = jnp.float32
BF16 = jnp.bfloat16

D_MODEL = 2048
BATCH = 16
SEQ = 256
DEPTH = 4
DEC_BATCH = 2
DEC_SEQ = 1024
PAST_LEN = 512
GRID_W = 64
GRID_ROWS = DEC_SEQ // GRID_W
ROPE_BASE = 10000.0
RMS_EPS = 1e-6
NEG_INF = -1e30
N_CTX = BATCH * SEQ
N_SMP = DEC_BATCH * DEC_SEQ
N_TOK = N_CTX + N_SMP
N_COND = 1 + DEC_BATCH

HEADS = 8
HEAD_DIM = 128
A_QK = 64
B_Q_RANK = 512
B_KV_RANK = 256
B_NOPE = 128
B_ROPE = 64
NA_KH = 8
NA_KW = 16
NA_WIN = NA_KH * GRID_W
D_HS = 64
D_WIDTH = 1024
D_HEADS = 16
D_LORA = 64
D_GATE_LORA = 128
D_IN = 3 * D_WIDTH + 2 * D_LORA + 2 * D_LORA + D_GATE_LORA
D_LN_EPS = 64e-5
EVEN_IN = 3904
ODD_IN = 6528
PEER_HEADS = 8
PEER_NKEYS = 128
PEER_HALF = 128
PEER_TOPK = 16

LANES = 128
VMEM_LIMIT = 56 * 1024 * 1024


def _params(*sem):
    return pltpu.CompilerParams(dimension_semantics=sem, vmem_limit_bytes=VMEM_LIMIT)


def _dot_nt(a, b):
    return lax.dot_general(a, b, (((1,), (1,)), ((), ())), preferred_element_type=F32)


def _dot_tn(a, b):
    return lax.dot_general(a, b, (((0,), (0,)), ((), ())), preferred_element_type=F32)


def _cond_of_tile(i, tm):
    n_ctx_tiles = N_CTX // tm
    per_batch = DEC_SEQ // tm
    return jnp.where(i < n_ctx_tiles, 0, 1 + (i - n_ctx_tiles) // per_batch)


def _ada_kernel(c_ref, w_ref, b_ref, o_ref):
    c = c_ref[...]
    sc = c * (1.0 / (1.0 + jnp.exp(-c)))
    o_ref[0] = jnp.dot(sc.astype(BF16), w_ref[0].astype(BF16), preferred_element_type=F32) + b_ref[0]


def _ada_all(cond8, ada_w, ada_b):
    tn = 1024
    n_out = ada_w.shape[-1]
    return pl.pallas_call(
        _ada_kernel,
        grid=(DEPTH, n_out // tn),
        in_specs=[pl.BlockSpec((8, D_MODEL), lambda l, j: (0, 0)),
                  pl.BlockSpec((1, D_MODEL, tn), lambda l, j: (l, 0, j)),
                  pl.BlockSpec((1, 1, tn), lambda l, j: (l, 0, j))],
        out_specs=pl.BlockSpec((1, 8, tn), lambda l, j: (l, 0, j)),
        out_shape=jax.ShapeDtypeStruct((DEPTH, 8, n_out), F32),
        compiler_params=_params("parallel", "parallel"),
        name="ada",
    )(cond8, ada_w, ada_b.reshape(DEPTH, 1, n_out))


def _modulate_kernel(x_ref, g_ref, sh_ref, sc_ref, o_ref):
    x = x_ref[...]
    y = x * lax.rsqrt(jnp.mean(x * x, axis=-1, keepdims=True) + RMS_EPS) * g_ref[...]
    o_ref[...] = (y * (1.0 + sc_ref[0]) + sh_ref[0]).astype(o_ref.dtype)


def _modulate(x, g, shift, scale, out_dtype=BF16, tm=256):
    n, d = x.shape
    return pl.pallas_call(
        _modulate_kernel,
        grid=(n // tm,),
        in_specs=[pl.BlockSpec((tm, d), lambda i: (i, 0)),
                  pl.BlockSpec((1, d), lambda i: (0, 0)),
                  pl.BlockSpec((1, 1, d), lambda i: (_cond_of_tile(i, tm), 0, 0)),
                  pl.BlockSpec((1, 1, d), lambda i: (_cond_of_tile(i, tm), 0, 0))],
        out_specs=pl.BlockSpec((tm, d), lambda i: (i, 0)),
        out_shape=jax.ShapeDtypeStruct((n, d), out_dtype),
        compiler_params=_params("parallel"),
        name="modulate",
    )(x, g.reshape(1, d), shift, scale)


def _rmsnorm_kernel(x_ref, g_ref, o_ref):
    x = x_ref[...]
    y = x * lax.rsqrt(jnp.mean(x * x, axis=-1, keepdims=True) + RMS_EPS) * g_ref[...]
    o_ref[...] = y.astype(o_ref.dtype)


def _rmsnorm(x, g, out_dtype=F32, tm=512, col=0):
    n = x.shape[0]
    d = g.shape[-1]
    return pl.pallas_call(
        _rmsnorm_kernel,
        grid=(n // tm,),
        in_specs=[pl.BlockSpec((tm, d), lambda i: (i, col)),
                  pl.BlockSpec((1, d), lambda i: (0, 0))],
        out_specs=pl.BlockSpec((tm, d), lambda i: (i, 0)),
        out_shape=jax.ShapeDtypeStruct((n, d), out_dtype),
        compiler_params=_params("parallel"),
        name="rmsnorm",
    )(x, g.reshape(1, d))


def _mm_kernel(x_ref, w_ref, o_ref):
    o_ref[...] = jnp.dot(x_ref[...].astype(BF16), w_ref[...], preferred_element_type=F32).astype(o_ref.dtype)


def _weight_spec(w, layer, tn):
    if w.ndim == 2:
        return pl.BlockSpec((w.shape[0], tn), lambda i, j: (0, j))
    return pl.BlockSpec((None, w.shape[1], tn), lambda i, j: (layer, 0, j))


def _matmul(x, w, layer=None, out_dtype=F32, tm=512, tn=512, x_col=0, x_width=None):
    m = x.shape[0]
    k = x_width or x.shape[1]
    n = w.shape[-1]
    tn = min(tn, n)
    assert m % tm == 0 and n % tn == 0, (m, n, tm, tn)
    return pl.pallas_call(
        _mm_kernel,
        grid=(m // tm, n // tn),
        in_specs=[pl.BlockSpec((tm, k), lambda i, j: (i, x_col)),
                  _weight_spec(w, layer, tn)],
        out_specs=pl.BlockSpec((tm, tn), lambda i, j: (i, j)),
        out_shape=jax.ShapeDtypeStruct((m, n), out_dtype),
        compiler_params=_params("parallel", "parallel"),
        name="matmul",
    )(x, w)


def _mm_res_kernel(lc_ref, ls_ref, rc_ref, rs_ref, w_ref, res_ref, gate_ref, o_ref, *, n_ctx_tiles):
    half = w_ref.shape[0] // 2

    def emit(l_ref, r_ref):
        acc = jnp.dot(l_ref[...].astype(BF16), w_ref[0:half, :], preferred_element_type=F32)
        acc = acc + jnp.dot(r_ref[...].astype(BF16), w_ref[half:, :], preferred_element_type=F32)
        o_ref[...] = res_ref[...] + gate_ref[0] * acc

    @pl.when(pl.program_id(0) < n_ctx_tiles)
    def _():
        emit(lc_ref, rc_ref)

    @pl.when(pl.program_id(0) >= n_ctx_tiles)
    def _():
        emit(ls_ref, rs_ref)


def _matmul_residual(left_c, left_s, right_c, right_s, w, layer, res, gate, tm=512, tn=512):
    m = res.shape[0]
    n = w.shape[-1]
    kh = left_c.shape[1]
    nct = left_c.shape[0] // tm
    ctx_spec = pl.BlockSpec((tm, kh), lambda i, j: (jnp.minimum(i, nct - 1), 0))
    smp_spec = pl.BlockSpec((tm, kh), lambda i, j: (jnp.maximum(i - nct, 0), 0))
    return pl.pallas_call(
        functools.partial(_mm_res_kernel, n_ctx_tiles=nct),
        grid=(m // tm, n // tn),
        in_specs=[ctx_spec, smp_spec, ctx_spec, smp_spec,
                  _weight_spec(w, layer, tn),
                  pl.BlockSpec((tm, tn), lambda i, j: (i, j)),
                  pl.BlockSpec((1, 1, tn), lambda i, j: (_cond_of_tile(i, tm), 0, j))],
        out_specs=pl.BlockSpec((tm, tn), lambda i, j: (i, j)),
        out_shape=jax.ShapeDtypeStruct((m, n), F32),
        compiler_params=_params("parallel", "parallel"),
        name="matmul_residual",
    )(left_c, left_s, right_c, right_s, w, res, gate)


def _rope_tables():
    nf = 16
    inv = ROPE_BASE ** (-np.arange(nf, dtype=np.float64) / nf)
    t = np.arange(DEC_SEQ)
    rows, cols = t // GRID_W, t % GRID_W
    lane = np.arange(64)
    pos = np.where(lane[None, :] < 32, rows[:, None], cols[:, None]).astype(np.float32)
    ang = (pos * inv[lane % nf][None, :].astype(np.float32)).astype(np.float32)
    first = (lane % 32) < nf
    cos = np.cos(ang.astype(np.float64))
    sin = np.sin(ang.astype(np.float64)) * np.where(first, -1.0, 1.0)[None, :]
    cos = np.tile(cos, (1, 2)).astype(np.float32)
    sin = np.tile(sin, (1, 2)).astype(np.float32)
    return jnp.asarray(cos), jnp.asarray(sin)


def _rope_kernel(x_ref, cos_ref, sin_ref, o_ref):
    cos = cos_ref[...]
    sin = sin_ref[...]
    lane = lax.broadcasted_iota(jnp.int32, cos.shape, 1)
    first = (lane % 32) < 16
    for c in range(x_ref.shape[1] // LANES):
        x = x_ref[:, c * LANES:(c + 1) * LANES].astype(F32)
        partner = jnp.where(first, pltpu.roll(x, LANES - 16, 1), pltpu.roll(x, 16, 1))
        o_ref[:, c * LANES:(c + 1) * LANES] = (x * cos + partner * sin).astype(o_ref.dtype)


def _rope(x, cos, sin, w, col=0, out_dtype=BF16, tm=256):
    n = N_SMP
    row0 = (x.shape[0] - N_SMP) // tm
    per = DEC_SEQ // tm
    return pl.pallas_call(
        _rope_kernel,
        grid=(n // tm,),
        in_specs=[pl.BlockSpec((tm, w), lambda i: (row0 + i, col)),
                  pl.BlockSpec((tm, LANES), lambda i: (i % per, 0)),
                  pl.BlockSpec((tm, LANES), lambda i: (i % per, 0))],
        out_specs=pl.BlockSpec((tm, w), lambda i: (i, 0)),
        out_shape=jax.ShapeDtypeStruct((n, w), out_dtype),
        compiler_params=_params("parallel"),
        name="rope",
    )(x, cos, sin)


def _softmax(s):
    p = jnp.exp(s - jnp.max(s, axis=-1, keepdims=True))
    return p / jnp.sum(p, axis=-1, keepdims=True)


def _diff_attn_kernel(lam_ref, g_ref, q_ref, k_ref, v_ref, o_ref, *, lam_init):
    lam = lam_ref[...]
    l1 = jnp.sum(jnp.sum(lam[0:1] * lam[1:2], axis=-1, keepdims=True), axis=0, keepdims=True)
    l2 = jnp.sum(jnp.sum(lam[2:3] * lam[3:4], axis=-1, keepdims=True), axis=0, keepdims=True)
    lam_val = jnp.exp(l1) - jnp.exp(l2) + lam_init
    scale = A_QK ** -0.5
    for h in range(HEADS):
        sl = slice(h * HEAD_DIM, (h + 1) * HEAD_DIM)
        q = q_ref[:, sl].astype(BF16)
        k = k_ref[:, sl].astype(BF16)
        p1 = _softmax(_dot_nt(q[:, :A_QK], k[:, :A_QK]) * scale)
        p2 = _softmax(_dot_nt(q[:, A_QK:], k[:, A_QK:]) * scale)
        a = p1 - lam_val * p2
        o = jnp.dot(a.astype(BF16), v_ref[:, sl].astype(BF16), preferred_element_type=F32)
        o = o * lax.rsqrt(jnp.mean(o * o, axis=-1, keepdims=True) + RMS_EPS) * g_ref[...]
        o_ref[:, sl] = (o * (1.0 - lam_init)).astype(o_ref.dtype)


ATT_TQ = 256
ATT_W = HEADS * HEAD_DIM


def _q_rows(t, width, row0=0, col=0):
    per, off = t // ATT_TQ, row0 // ATT_TQ
    return pl.BlockSpec((ATT_TQ, width), lambda b, j: (off + b * per + j, col))


def _kv_rows(s, width, row0=0, col=0):
    off = row0 // s
    return pl.BlockSpec((s, width), lambda b, j: (off + b, col))


def _attention_call(kernel, name, nb, t, operands, specs):
    return pl.pallas_call(
        kernel,
        grid=(nb, t // ATT_TQ),
        in_specs=specs,
        out_specs=_q_rows(t, ATT_W),
        out_shape=jax.ShapeDtypeStruct((nb * t, ATT_W), BF16),
        compiler_params=_params("parallel", "parallel"),
        name=name,
    )(*operands)


def _diff_attention(nb, t, q, k, v, lam, subln_g, lam_init):
    const = [pl.BlockSpec((4, A_QK), lambda i, j: (0, 0)), pl.BlockSpec((1, HEAD_DIM), lambda i, j: (0, 0))]
    return _attention_call(functools.partial(_diff_attn_kernel, lam_init=lam_init), "diff_attention", nb, t,
                           [lam, subln_g.reshape(1, HEAD_DIM), q[0], k[0], v[0]], const + [q[1], k[1], v[1]])


def _mla_attn_kernel(qn_ref, qr_ref, kn_ref, kr_ref, v_ref, o_ref):
    scale = (B_NOPE + B_ROPE) ** -0.5
    kr = kr_ref[:, 0:B_ROPE].astype(BF16)
    for h in range(HEADS):
        sl = slice(h * HEAD_DIM, (h + 1) * HEAD_DIM)
        s = _dot_nt(qn_ref[:, sl].astype(BF16), kn_ref[:, sl])
        s = s + _dot_nt(qr_ref[:, h * B_ROPE:(h + 1) * B_ROPE].astype(BF16), kr)
        p = _softmax(s * scale)
        o = jnp.dot(p.astype(BF16), v_ref[:, sl], preferred_element_type=F32)
        o_ref[:, sl] = o.astype(o_ref.dtype)


def _mla_attention(nb, t, qn, qr, kn, kr, v):
    ops = [qn, qr, kn, kr, v]
    return _attention_call(_mla_attn_kernel, "mla_attention", nb, t, [o[0] for o in ops], [o[1] for o in ops])


def _plain_attn_kernel(q_ref, k_ref, v_ref, o_ref):
    scale = HEAD_DIM ** -0.5
    for h in range(HEADS):
        sl = slice(h * HEAD_DIM, (h + 1) * HEAD_DIM)
        p = _softmax(_dot_nt(q_ref[:, sl].astype(BF16), k_ref[:, sl].astype(BF16)) * scale)
        o = jnp.dot(p.astype(BF16), v_ref[:, sl].astype(BF16), preferred_element_type=F32)
        o_ref[:, sl] = o.astype(o_ref.dtype)


def _plain_attention(nb, t, q, k, v):
    ops = [q, k, v]
    return _attention_call(_plain_attn_kernel, "plain_attention", nb, t, [o[0] for o in ops], [o[1] for o in ops])


def _na_window_start(r):
    return jnp.clip(r - NA_KH // 2, 0, GRID_ROWS - NA_KH)


def _na_attn_kernel(q_ref, k_ref, v_ref, kc_ref, vc_ref, bias_ref, o_ref):
    scale = HEAD_DIM ** -0.5
    r = pl.program_id(1)
    start = pl.multiple_of(_na_window_start(r) * GRID_W, GRID_W)
    for h in range(HEADS):
        sl = slice(h * HEAD_DIM, (h + 1) * HEAD_DIM)
        q = q_ref[:, sl].astype(BF16)
        kw = k_ref[pl.ds(start, NA_WIN), sl].astype(BF16)
        vw = v_ref[pl.ds(start, NA_WIN), sl].astype(BF16)
        s_win = _dot_nt(q, kw) * scale + bias_ref[0, h]
        s_ctx = _dot_nt(q, kc_ref[:, sl].astype(BF16)) * scale
        m = jnp.maximum(jnp.max(s_win, axis=-1, keepdims=True), jnp.max(s_ctx, axis=-1, keepdims=True))
        p_win = jnp.exp(s_win - m)
        p_ctx = jnp.exp(s_ctx - m)
        den = jnp.sum(p_win, axis=-1, keepdims=True) + jnp.sum(p_ctx, axis=-1, keepdims=True)
        o = (jnp.dot(p_win.astype(BF16), vw, preferred_element_type=F32)
             + jnp.dot(p_ctx.astype(BF16), vc_ref[:, sl].astype(BF16), preferred_element_type=F32))
        o_ref[:, sl] = (o / den).astype(o_ref.dtype)


NA_NDC = 2 * NA_KW


def _na_bias_kernel(r_ref, e_ref, o_ref):
    o_ref[0] = jnp.dot(r_ref[0], e_ref[...], preferred_element_type=F32, precision=lax.Precision.HIGHEST)


def _na_bias_tables(rpb):
    cols = np.arange(GRID_W)
    cs = np.clip(cols - NA_KW // 2, 0, GRID_W - NA_KW)
    col_in = (cols[None, :] >= cs[:, None]) & (cols[None, :] < cs[:, None] + NA_KW)
    dc_idx = np.clip(cols[None, :] - cols[:, None] + NA_KW - 1, 0, 2 * NA_KW - 2)
    mask = np.broadcast_to(col_in[:, None, :], (GRID_W, NA_KH, GRID_W)).reshape(GRID_W, NA_WIN)
    onehot = (np.arange(NA_NDC)[:, None] == dc_idx.reshape(1, -1)).astype(np.float32)
    rp = jnp.pad(rpb, ((0, 0), (0, 0), (0, NA_NDC - rpb.shape[2])))
    rows = jnp.stack([rp[:, NA_KH - 1 - d:2 * NA_KH - 1 - d, :] for d in range(NA_KH)])
    rows = rows.reshape(NA_KH, HEADS * NA_KH, NA_NDC)
    nqk = GRID_W * GRID_W
    b = pl.pallas_call(
        _na_bias_kernel,
        grid=(NA_KH,),
        in_specs=[pl.BlockSpec((1, HEADS * NA_KH, NA_NDC), lambda d: (d, 0, 0)),
                  pl.BlockSpec((NA_NDC, nqk), lambda d: (0, 0))],
        out_specs=pl.BlockSpec((1, HEADS * NA_KH, nqk), lambda d: (d, 0, 0)),
        out_shape=jax.ShapeDtypeStruct((NA_KH, HEADS * NA_KH, nqk), F32),
        compiler_params=_params("parallel"),
        name="na_bias",
    )(rows, jnp.asarray(onehot))
    b = b.reshape(NA_KH, HEADS, NA_KH, GRID_W, GRID_W).transpose(0, 1, 3, 2, 4).reshape(NA_KH, HEADS, GRID_W, NA_WIN)
    return jnp.where(mask[None, None], b, NEG_INF)


def _na_attention(z, kc, vc, bias):
    w = ATT_W
    row0 = z.shape[0] - N_SMP

    def all_rows(n, col, base):
        return pl.BlockSpec((n, w), lambda i, r: (base // n + i, col))

    return pl.pallas_call(
        _na_attn_kernel,
        grid=(DEC_BATCH, GRID_ROWS),
        in_specs=[pl.BlockSpec((GRID_W, w), lambda i, r: (row0 // GRID_W + i * GRID_ROWS + r, 0)),
                  all_rows(DEC_SEQ, 1, row0), all_rows(DEC_SEQ, 2, row0),
                  all_rows(PAST_LEN, 0, 0), all_rows(PAST_LEN, 0, 0),
                  pl.BlockSpec((1, HEADS, GRID_W, NA_WIN), lambda i, r: (r - _na_window_start(r), 0, 0, 0))],
        out_specs=pl.BlockSpec((GRID_W, w), lambda i, r: (i * GRID_ROWS + r, 0)),
        out_shape=jax.ShapeDtypeStruct((N_SMP, w), BF16),
        compiler_params=_params("parallel", "parallel"),
        name="na_attention",
    )(z, z, z, kc, vc, bias)


CONV_CW = 384


def _conv3_kernel(x_ref, w_ref, o_ref):
    x = x_ref[...]
    t = x.shape[0]
    row = lax.broadcasted_iota(jnp.int32, x.shape, 0)
    prev = jnp.where(row == 0, 0.0, pltpu.roll(x, 1, 0))
    nxt = jnp.where(row == t - 1, 0.0, pltpu.roll(x, t - 1, 0))
    w = w_ref[...]
    o_ref[...] = prev * w[0:1] + x * w[1:2] + nxt * w[2:3]


def _conv3(z, conv_w, seq, row0, n_seq):
    col0 = (3 * D_WIDTH) // CONV_CW
    blk0 = row0 // seq
    return pl.pallas_call(
        _conv3_kernel,
        grid=(n_seq, D_IN // CONV_CW),
        in_specs=[pl.BlockSpec((seq, CONV_CW), lambda i, j: (blk0 + i, col0 + j)),
                  pl.BlockSpec((3, CONV_CW), lambda i, j: (0, j))],
        out_specs=pl.BlockSpec((seq, CONV_CW), lambda i, j: (i, j)),
        out_shape=jax.ShapeDtypeStruct((n_seq * seq, D_IN), F32),
        compiler_params=_params("parallel", "parallel"),
        name="conv3",
    )(z, conv_w)


def _dot_f32(a, b):
    return jnp.dot(a, b, preferred_element_type=F32, precision=lax.Precision.HIGHEST)


def _d_prep_kernel(x_ref, w0_ref, w2_ref, a0_ref, a2_ref, g2_ref, dec_ref, a_ref, g_ref):
    x = x_ref[...]
    xw = x[:, 0:2 * D_LORA]
    xa = x[:, 2 * D_LORA:4 * D_LORA]
    xg = x[:, 4 * D_LORA:]
    for d in range(2):
        u = w0_ref[d] + _dot_f32(jnp.tanh(xw[:, d * D_LORA:(d + 1) * D_LORA]), w2_ref[d])
        nu = -u
        softplus = jnp.maximum(nu, 0.0) + jnp.log1p(jnp.exp(-jnp.abs(nu)))
        wlog = -softplus - 0.5
        dec_ref[d] = jnp.exp(-jnp.exp(wlog))
        av = a0_ref[d] + _dot_f32(xa[:, d * D_LORA:(d + 1) * D_LORA], a2_ref[d])
        a_ref[d] = 1.0 / (1.0 + jnp.exp(-av))
    g_ref[...] = _dot_f32(1.0 / (1.0 + jnp.exp(-xg)), g2_ref[...])


def _d_prep(zc, w0, w2, a0, a2, g2, tm=256):
    n = zc.shape[0]
    cb = (3 * D_WIDTH) // CONV_CW
    return pl.pallas_call(
        _d_prep_kernel,
        grid=(n // tm,),
        in_specs=[pl.BlockSpec((tm, CONV_CW), lambda i: (i, cb)),
                  pl.BlockSpec((2, 1, D_WIDTH), lambda i: (0, 0, 0)),
                  pl.BlockSpec((2, D_LORA, D_WIDTH), lambda i: (0, 0, 0)),
                  pl.BlockSpec((2, 1, D_WIDTH), lambda i: (0, 0, 0)),
                  pl.BlockSpec((2, D_LORA, D_WIDTH), lambda i: (0, 0, 0)),
                  pl.BlockSpec((D_GATE_LORA, D_WIDTH), lambda i: (0, 0))],
        out_specs=[pl.BlockSpec((2, tm, D_WIDTH), lambda i: (0, i, 0)),
                   pl.BlockSpec((2, tm, D_WIDTH), lambda i: (0, i, 0)),
                   pl.BlockSpec((tm, D_WIDTH), lambda i: (i, 0))],
        out_shape=[jax.ShapeDtypeStruct((2, n, D_WIDTH), F32),
                   jax.ShapeDtypeStruct((2, n, D_WIDTH), F32),
                   jax.ShapeDtypeStruct((n, D_WIDTH), F32)],
        compiler_params=_params("parallel"),
        name="d_prep",
    )(zc, w0.reshape(2, 1, D_WIDTH), w2, a0.reshape(2, 1, D_WIDTH), a2, g2)


SCAN_TT = 32


def _rwkv_scan_kernel(r_ref, k_ref, v_ref, w_ref, a_ref, kkp_ref, kap_ref, rkp_ref, s0_ref,
                      y_ref, bon_ref, sfin_ref, s_ref, tmp_ref, *, groups_per_dir):
    backward = pl.program_id(0) >= groups_per_dir
    tb = pl.program_id(1)

    @pl.when(tb == 0)
    def _():
        s_ref[...] = s0_ref[...]

    kkp = kkp_ref[...]
    kap = kap_ref[...]
    rkp = rkp_ref[...]
    iq = v_ref.shape[1]

    def step(n, carry):
        t = jnp.where(backward, SCAN_TT - 1 - n, n)
        r = r_ref[t]
        k = k_ref[t]
        v = v_ref[t]
        a = a_ref[t]
        kkf = k * kkp
        nrm = jnp.sqrt(jnp.sum(kkf * kkf, axis=0, keepdims=True))
        kk = kkf / jnp.maximum(nrm, 1e-12)
        kd = k * (1.0 + (a - 1.0) * kap)
        bon_ref[t] = jnp.sum(r * kd * rkp, axis=0, keepdims=True)
        tmp_ref[0] = kk
        tmp_ref[1] = kk * a
        tmp_ref[2] = kd
        sa = jnp.zeros((iq, LANES), F32)
        for j in range(D_HS):
            sa = sa + s_ref[j] * tmp_ref[0, pl.ds(j, 1), :]
        y = jnp.zeros((iq, LANES), F32)
        for j in range(D_HS):
            sn = (s_ref[j] * w_ref[t, pl.ds(j, 1), :] - sa * tmp_ref[1, pl.ds(j, 1), :]
                  + v * tmp_ref[2, pl.ds(j, 1), :])
            s_ref[j] = sn
            y = y + sn * r_ref[t, pl.ds(j, 1), :]
        y_ref[t] = y
        return carry

    lax.fori_loop(0, SCAN_TT, step, 0)

    @pl.when(tb == pl.num_programs(1) - 1)
    def _():
        sfin_ref[...] = s_ref[...]


def _rwkv_scan(r, k, v, w, a, kkp, kap, rkp, s0):
    t, _, lanes = r.shape
    iq = v.shape[1]
    gpd = lanes // LANES
    nt = t // SCAN_TT

    def tmap(g, ti):
        return jnp.where(g < gpd, ti, nt - 1 - ti)

    j_spec = pl.BlockSpec((SCAN_TT, D_HS, LANES), lambda g, ti: (tmap(g, ti), 0, g % gpd))
    i_spec = pl.BlockSpec((SCAN_TT, iq, LANES), lambda g, ti: (tmap(g, ti), 0, g % gpd))
    dj_spec = pl.BlockSpec((None, SCAN_TT, D_HS, LANES), lambda g, ti: (g // gpd, tmap(g, ti), 0, g % gpd))
    par_spec = pl.BlockSpec((D_HS, LANES), lambda g, ti: (0, g % gpd))
    st_spec = pl.BlockSpec((None, D_HS, iq, LANES), lambda g, ti: (g // gpd, 0, 0, g % gpd))
    return pl.pallas_call(
        functools.partial(_rwkv_scan_kernel, groups_per_dir=gpd),
        grid=(2 * gpd, nt),
        in_specs=[j_spec, j_spec, i_spec, dj_spec, dj_spec, par_spec, par_spec, par_spec, st_spec],
        out_specs=[pl.BlockSpec((None, SCAN_TT, iq, LANES), lambda g, ti: (g // gpd, tmap(g, ti), 0, g % gpd)),
                   pl.BlockSpec((None, SCAN_TT, 1, LANES), lambda g, ti: (g // gpd, tmap(g, ti), 0, g % gpd)),
                   st_spec],
        out_shape=[jax.ShapeDtypeStruct((2, t, iq, lanes), F32),
                   jax.ShapeDtypeStruct((2, t, 1, lanes), F32),
                   jax.ShapeDtypeStruct((2, D_HS, iq, lanes), F32)],
        scratch_shapes=[pltpu.VMEM((D_HS, iq, LANES), F32), pltpu.VMEM((3, D_HS, LANES), F32)],
        compiler_params=_params("parallel", "arbitrary"),
        name="rwkv_scan",
    )(r, k, v, w, a, kkp, kap, rkp, s0)


def _d_post_kernel(yf_ref, yb_ref, bf_ref, bb_ref, v_ref, g_ref, lng_ref, lnb_ref, o_ref):
    y = yf_ref[...] + yb_ref[...]
    mu = jnp.mean(y, axis=1, keepdims=True)
    var = jnp.mean(jnp.square(y - mu), axis=1, keepdims=True)
    yn = (y - mu) * lax.rsqrt(var + D_LN_EPS) * lng_ref[...] + lnb_ref[...]
    o_ref[...] = (yn + (bf_ref[...] + bb_ref[...]) * v_ref[...]) * g_ref[...]


def _d_post(y, bon, v, g, lng, lnb, tt=32):
    _, t, _, c = y.shape

    def dir_spec(d, rows):
        return pl.BlockSpec((None, tt, rows, LANES), lambda ci, ti: (d, ti, 0, ci))

    seq = pl.BlockSpec((tt, D_HS, LANES), lambda ci, ti: (ti, 0, ci))
    par = pl.BlockSpec((1, D_HS, LANES), lambda ci, ti: (0, 0, ci))
    return pl.pallas_call(
        _d_post_kernel,
        grid=(c // LANES, t // tt),
        in_specs=[dir_spec(0, D_HS), dir_spec(1, D_HS), dir_spec(0, 1), dir_spec(1, 1), seq, seq, par, par],
        out_specs=seq,
        out_shape=jax.ShapeDtypeStruct((t, D_HS, c), F32),
        compiler_params=_params("parallel", "parallel"),
        name="d_post",
    )(y, y, bon, bon, v, g, lng.reshape(1, D_HS, c), lnb.reshape(1, D_HS, c))


def _to_chain(x, b, t):
    return x.reshape(b, t, D_HEADS, D_HS).transpose(1, 3, 0, 2).reshape(t, D_HS, b * D_HEADS)


def _from_chain(x, b, t):
    return x.reshape(t, D_HS, b, D_HEADS).transpose(2, 0, 3, 1).reshape(b * t, D_WIDTH)


def _param_chain(p, n_chain):
    return jnp.tile(p.reshape(D_HEADS, D_HS).T, (1, n_chain // D_HEADS))


def _pad_lanes(x):
    c = x.shape[-1]
    pad = (-c) % LANES
    if pad == 0:
        return x
    return jnp.pad(x, [(0, 0)] * (x.ndim - 1) + [(0, pad)])


def _rwkv_branch(zc, dec, a, g, b, t, s0, k_k, k_a, r_k, ln_g, ln_b):
    nb = b * D_HEADS
    split = max(LANES // nb, 1)
    iq = D_HS // split
    lanes = nb * split

    def j_arr(x):
        xc = _to_chain(x, b, t)
        return jnp.tile(xc, (1, 1, split)) if split > 1 else xc

    def split_rows(x):
        if split == 1:
            return x
        lead = x.shape[:-2]
        n = len(lead)
        x = x.reshape(lead + (split, iq, nb))
        return jnp.swapaxes(x, n, n + 1).reshape(lead + (iq, lanes))

    def merge_rows(x):
        if split == 1:
            return x
        lead = x.shape[:-2]
        n = len(lead)
        x = x.reshape(lead + (iq, split, nb))
        return jnp.swapaxes(x, n, n + 1).reshape(lead + (D_HS, nb))

    r = j_arr(zc[:, 0:D_WIDTH])
    k = j_arr(zc[:, D_WIDTH:2 * D_WIDTH])
    v_chain = _to_chain(zc[:, 2 * D_WIDTH:3 * D_WIDTH], b, t)
    w2 = jnp.stack([j_arr(dec[0]), j_arr(dec[1])])
    a2 = jnp.stack([j_arr(a[0]), j_arr(a[1])])
    if s0 is None:
        s0c = jnp.zeros((2, D_HS, iq, lanes), F32)
    else:
        s0c = split_rows(s0.transpose(0, 4, 3, 1, 2).reshape(2, D_HS, D_HS, nb))
    y, bon, sfin = _rwkv_scan(r, k, split_rows(v_chain), w2, a2,
                              _param_chain(k_k, lanes), _param_chain(k_a, lanes),
                              _param_chain(r_k.reshape(-1), lanes), s0c)
    y = _pad_lanes(merge_rows(y))
    bon = _pad_lanes(bon[..., :nb])
    od = _d_post(y, bon, _pad_lanes(v_chain), _pad_lanes(_to_chain(g, b, t)),
                 _pad_lanes(_param_chain(ln_g, nb)), _pad_lanes(_param_chain(ln_b, nb)))
    od = _from_chain(od[:, :, :nb], b, t)
    sfin = merge_rows(sfin).reshape(2, D_HS, D_HS, b, D_HEADS).transpose(3, 0, 4, 2, 1)
    return od, sfin


TOPK_TM = 256
BIG = 1e9


def _extract_topk(s, labels):
    iota16 = lax.broadcasted_iota(jnp.int32, (PEER_TOPK, s.shape[1]), 0)
    rank = jnp.full(s.shape, BIG, F32)
    vals = jnp.zeros((PEER_TOPK, s.shape[1]), F32)
    for r in range(PEER_TOPK):
        m = jnp.max(s, axis=0, keepdims=True)
        idx = jnp.min(jnp.where(s == m, labels, BIG), axis=0, keepdims=True)
        hit = labels == idx
        rank = jnp.where(hit, float(r), rank)
        vals = jnp.where(iota16 == r, m, vals)
        s = jnp.where(hit, -jnp.inf, s)
    return rank, vals


_CAND_PAIRS = ([(0, b) for b in range(16)] + [(1, b) for b in range(8)] + [(2, b) for b in range(8)]
               + [(3, b) for b in range(8)] + [(4, b) for b in range(4)] + [(5, b) for b in range(4)]
               + [(6, b) for b in range(4)] + [(7, b) for b in range(4)] + [(a, 0) for a in range(8, 16)])
N_CAND = len(_CAND_PAIRS)


def _cand_labels():
    lab = np.array([a * PEER_TOPK + b for a, b in _CAND_PAIRS], np.float32)
    return jnp.asarray(np.broadcast_to(lab[:, None], (N_CAND, LANES)).copy())


def _peer_topk_kernel(q_ref, keys_ref, lab_ref, lim_ref, e1_ref, rb_ref, e2_ref):
    iota_k = lax.broadcasted_iota(jnp.int32, (PEER_NKEYS, LANES), 0).astype(F32)
    row8 = lax.broadcasted_iota(jnp.int32, (8, LANES), 0)
    labels = lab_ref[...]
    for c in range(TOPK_TM // LANES):
        cs = slice(c * LANES, (c + 1) * LANES)
        q1 = q_ref[cs, 0:PEER_HALF].astype(BF16)
        q2 = q_ref[cs, PEER_HALF:2 * PEER_HALF].astype(BF16)
        s1 = _dot_nt(keys_ref[0, 0], q1)
        s2 = _dot_nt(keys_ref[0, 1], q2)
        rank1, sv1 = _extract_topk(s1, iota_k)
        rank2, sv2 = _extract_topk(s2, iota_k)
        lo8 = sv2[0:8]
        lo4 = jnp.where(row8 < 4, lo8, pltpu.roll(lo8, 4, 0))
        cand = jnp.concatenate([
            sv1[0:1] + lo8, sv1[0:1] + sv2[8:16], sv1[1:2] + lo8, sv1[2:3] + lo8, sv1[3:4] + lo8,
            jnp.where(row8 < 4, sv1[4:5], sv1[5:6]) + lo4, jnp.where(row8 < 4, sv1[6:7], sv1[7:8]) + lo4,
            sv1[8:16] + sv2[0:1]], axis=0)
        crank, cvals = _extract_topk(cand, labels)
        z = jnp.sum(jnp.exp(cvals - cvals[0:1]), axis=0, keepdims=True)
        sel = jnp.where(crank < BIG, 1.0, 0.0)

        def count(lo, hi):
            return jnp.sum(sel[lo:hi], axis=0, keepdims=True)

        def count_half(lo, first):
            part = jnp.where((row8 < 4) if first else (row8 >= 4), sel[lo:lo + 8], 0.0)
            return jnp.sum(part, axis=0, keepdims=True)

        n_sel = [count(0, 16), count(16, 24), count(24, 32), count(32, 40),
                 count_half(40, True), count_half(40, False), count_half(48, True), count_half(48, False)]
        n_sel += [sel[56 + a:57 + a] for a in range(8)]
        lim = jnp.zeros_like(s1)
        for a in range(PEER_TOPK):
            lim = jnp.where(rank1 == float(a), n_sel[a], lim)
        lim_ref[0, :, cs] = lim
        e1_ref[0, :, cs] = jnp.exp(s1 - sv1[0:1])
        rb_ref[0, :, cs] = rank2
        e2_ref[0, :, cs] = jnp.exp(s2 - sv2[0:1]) / z


def _peer_topk(q, keys):
    n = q.shape[0]
    ospec = pl.BlockSpec((1, PEER_NKEYS, TOPK_TM), lambda i, h: (h, 0, i))
    return pl.pallas_call(
        _peer_topk_kernel,
        grid=(n // TOPK_TM, PEER_HEADS),
        in_specs=[pl.BlockSpec((TOPK_TM, 2 * PEER_HALF), lambda i, h: (i, h)),
                  pl.BlockSpec((1, 2, PEER_NKEYS, PEER_HALF), lambda i, h: (h, 0, 0, 0)),
                  pl.BlockSpec((N_CAND, LANES), lambda i, h: (0, 0))],
        out_specs=[ospec] * 4,
        out_shape=[jax.ShapeDtypeStruct((PEER_HEADS, PEER_NKEYS, n), F32)] * 4,
        compiler_params=_params("parallel", "parallel"),
        name="peer_topk",
    )(q, keys, _cand_labels())


PEER_TM = 512
PEER_TI = 8
PEER_RG = 4
PEER_JT = 32


def _gelu(x):
    return 0.5 * x * (1.0 + lax.erf(x * (2.0 ** -0.5)))


def _peer_dense_kernel(x_ref, u_ref, v_ref, lim_ref, e1_ref, rb_ref, e2_ref, res_ref, gate_ref, o_ref, w_ref):
    e = pl.program_id(1)

    @pl.when(e == 0)
    def _():
        o_ref[...] = jnp.zeros_like(o_ref)

    hid = _dot_nt(u_ref[...], x_ref[...])
    for c in range(PEER_TM // LANES):
        cs = slice(c * LANES, (c + 1) * LANES)
        for jt in range(PEER_NKEYS // PEER_JT):
            js = slice(jt * PEER_JT, (jt + 1) * PEER_JT)
            for rg in range(PEER_TI // PEER_RG):
                rows = range(rg * PEER_RG, (rg + 1) * PEER_RG)
                g = [jnp.zeros((PEER_JT, LANES), F32) for _ in rows]
                for h in range(PEER_HEADS):
                    rb = rb_ref[h, js, cs]
                    e2 = e2_ref[h, js, cs]
                    for k, ii in enumerate(rows):
                        g[k] = g[k] + jnp.where(rb < lim_ref[h, ii:ii + 1, cs], e2, 0.0) * e1_ref[h, ii:ii + 1, cs]
                for k, ii in enumerate(rows):
                    es = slice(ii * PEER_NKEYS + jt * PEER_JT, ii * PEER_NKEYS + (jt + 1) * PEER_JT)
                    w_ref[es, cs] = (g[k] * _gelu(hid[es, cs])).astype(BF16)
    o_ref[...] += _dot_tn(w_ref[...], v_ref[...])

    @pl.when(e == pl.num_programs(1) - 1)
    def _():
        o_ref[...] = res_ref[...] + gate_ref[0] * o_ref[...]


def _peer_dense(xm, u, v, layer, lim, e1, rb, e2, res, gate):
    n, d = xm.shape
    n_exp = u.shape[1]
    te = PEER_TI * PEER_NKEYS
    sel_spec = pl.BlockSpec((PEER_HEADS, PEER_NKEYS, PEER_TM), lambda i, e: (0, 0, i))
    row_spec = pl.BlockSpec((PEER_HEADS, PEER_TI, PEER_TM), lambda i, e: (0, e, i))
    return pl.pallas_call(
        _peer_dense_kernel,
        grid=(n // PEER_TM, n_exp // te),
        in_specs=[pl.BlockSpec((PEER_TM, d), lambda i, e: (i, 0)),
                  pl.BlockSpec((None, te, d), lambda i, e: (layer, e, 0)),
                  pl.BlockSpec((None, te, d), lambda i, e: (layer, e, 0)),
                  row_spec, row_spec, sel_spec, sel_spec,
                  pl.BlockSpec((PEER_TM, d), lambda i, e: (i, 0)),
                  pl.BlockSpec((1, 1, d), lambda i, e: (_cond_of_tile(i, PEER_TM), 0, 0))],
        out_specs=pl.BlockSpec((PEER_TM, d), lambda i, e: (i, 0)),
        out_shape=jax.ShapeDtypeStruct((n, d), F32),
        scratch_shapes=[pltpu.VMEM((te, PEER_TM), BF16)],
        compiler_params=_params("parallel", "arbitrary"),
        name="peer_dense",
    )(xm, u, v, lim, e1, rb, e2, res, gate)


def _peer_layer(x, layer, norm_g, shift, scale, gate, wq, keys, u, v):
    xm = _modulate(x, norm_g, shift, scale)
    q = _matmul(xm, wq, layer)
    lim, e1, rb, e2 = _peer_topk(q, keys)
    return _peer_dense(xm, u, v, layer, lim, e1, rb, e2, x, gate)


EVEN_PAD = 4096
ODD_PAD = 6656
S_ALL = PAST_LEN + DEC_SEQ


def _even_mixer(h, w_in, layer, lam, subln_g, q_norm_g, w_uq, kv_norm_g, w_ukv, lam_init,
                cache_k, cache_v, cache_ckv, cache_kpe, rope_cos, rope_sin):
    z = _matmul(h, w_in, layer)
    ckv_n = _rmsnorm(z, kv_norm_g, col=3584 // B_KV_RANK)
    cq_n = _rmsnorm(z, q_norm_g, out_dtype=BF16, col=3072 // B_Q_RANK)
    wq3 = w_uq.reshape(B_Q_RANK, HEADS, B_NOPE + B_ROPE)
    w_uq_r = jnp.concatenate([wq3[:, :, :B_NOPE].reshape(B_Q_RANK, -1), wq3[:, :, B_NOPE:].reshape(B_Q_RANK, -1)], axis=1)
    qb = _matmul(cq_n, w_uq_r.astype(BF16), tn=512)
    new = (z[:N_CTX, 1024:2048].reshape(BATCH, SEQ, HEADS, 2 * A_QK),
           z[:N_CTX, 2048:3072].reshape(BATCH, SEQ, HEADS, HEAD_DIM),
           ckv_n[:N_CTX].reshape(BATCH, SEQ, B_KV_RANK), z[:N_CTX, 3840:3904].reshape(BATCH, SEQ, B_ROPE))

    qa_s = _rope(z, rope_cos, rope_sin, ATT_W, col=0)
    ka_s = _rope(z, rope_cos, rope_sin, ATT_W, col=1)
    qr_s = _rope(qb, rope_cos, rope_sin, HEADS * B_ROPE, col=2)
    kpe_s = _rope(z, rope_cos, rope_sin, LANES, col=3840 // LANES)[:, :B_ROPE]

    def with_ctx(cache, own, width):
        both = jnp.concatenate([cache.reshape(DEC_BATCH, PAST_LEN, width).astype(own.dtype),
                                own.reshape(DEC_BATCH, DEC_SEQ, width)], axis=1)
        return both.reshape(DEC_BATCH * S_ALL, width)

    ka_all = with_ctx(cache_k, ka_s, ATT_W)
    va_all = with_ctx(cache_v, z[N_CTX:, 2048:3072].astype(BF16), ATT_W)
    ckv_all = with_ctx(cache_ckv, ckv_n[N_CTX:], B_KV_RANK)
    kpe_all = with_ctx(cache_kpe, kpe_s, B_ROPE)

    wkv3 = w_ukv.reshape(B_KV_RANK, HEADS, B_NOPE + HEAD_DIM)
    w_ukv_r = jnp.concatenate([wkv3[:, :, :B_NOPE].reshape(B_KV_RANK, -1), wkv3[:, :, B_NOPE:].reshape(B_KV_RANK, -1)], axis=1)
    n_skv = DEC_BATCH * S_ALL
    kv = _matmul(jnp.concatenate([ckv_all, ckv_n[:N_CTX]], axis=0), w_ukv_r.astype(BF16), out_dtype=BF16)

    oa_c = _diff_attention(BATCH, SEQ, (z, _q_rows(SEQ, ATT_W, 0, 0)), (z, _kv_rows(SEQ, ATT_W, 0, 1)),
                           (z, _kv_rows(SEQ, ATT_W, 0, 2)), lam, subln_g, lam_init)
    oa_s = _diff_attention(DEC_BATCH, DEC_SEQ, (qa_s, _q_rows(DEC_SEQ, ATT_W)), (ka_all, _kv_rows(S_ALL, ATT_W)),
                           (va_all, _kv_rows(S_ALL, ATT_W)), lam, subln_g, lam_init)
    ob_c = _mla_attention(BATCH, SEQ, (qb, _q_rows(SEQ, ATT_W, 0, 0)), (qb, _q_rows(SEQ, HEADS * B_ROPE, 0, 2)),
                          (kv, _kv_rows(SEQ, ATT_W, n_skv, 0)), (z, _kv_rows(SEQ, LANES, 0, 3840 // LANES)),
                          (kv, _kv_rows(SEQ, ATT_W, n_skv, 1)))
    ob_s = _mla_attention(DEC_BATCH, DEC_SEQ, (qb, _q_rows(DEC_SEQ, ATT_W, N_CTX, 0)),
                          (qr_s, _q_rows(DEC_SEQ, HEADS * B_ROPE)), (kv, _kv_rows(S_ALL, ATT_W, 0, 0)),
                          (kpe_all, _kv_rows(S_ALL, B_ROPE)), (kv, _kv_rows(S_ALL, ATT_W, 0, 1)))
    return (oa_c, oa_s, ob_c, ob_s), new


def _odd_mixer(h, w_in, layer, rpb, conv_w, w0, w2, a0, a2, g2, k_k, k_a, r_k, ln_g, ln_b,
               cache_k, cache_v, state):
    z = _matmul(h, w_in, layer)
    new_k = z[:N_CTX, 1024:2048].reshape(BATCH, SEQ, HEADS, HEAD_DIM)
    new_v = z[:N_CTX, 2048:3072].reshape(BATCH, SEQ, HEADS, HEAD_DIM)
    oc_c = _plain_attention(BATCH, SEQ, (z, _q_rows(SEQ, ATT_W, 0, 0)), (z, _kv_rows(SEQ, ATT_W, 0, 1)),
                            (z, _kv_rows(SEQ, ATT_W, 0, 2)))
    oc_s = _na_attention(z, cache_k.reshape(DEC_BATCH * PAST_LEN, ATT_W), cache_v.reshape(DEC_BATCH * PAST_LEN, ATT_W),
                         _na_bias_tables(rpb))

    zc_c = _conv3(z, conv_w, SEQ, 0, BATCH)
    zc_s = _conv3(z, conv_w, DEC_SEQ, N_CTX, DEC_BATCH)
    dec_c, a_c, g_c = _d_prep(zc_c, w0, w2, a0, a2, g2)
    dec_s, a_s, g_s = _d_prep(zc_s, w0, w2, a0, a2, g2)
    od_c, sfin = _rwkv_branch(zc_c, dec_c, a_c, g_c, BATCH, SEQ, None, k_k, k_a, r_k, ln_g, ln_b)
    s0 = jnp.moveaxis(state, 1, 0)
    od_s, _ = _rwkv_branch(zc_s, dec_s, a_s, g_s, DEC_BATCH, DEC_SEQ, s0, k_k, k_a, r_k, ln_g, ln_b)
    return (oc_c, oc_s, od_c, od_s), (new_k, new_v, sfin)


def kernel(x_prompt, x_sample, cache_a_k, cache_a_v, cache_b_ckv, cache_b_kpe, cache_c_k, cache_c_v, state_d, c, c_ctx, ada_w, ada_b, norm1_g, norm2_g, w_out, peer_wq, peer_keys, peer_u, peer_v, final_g, ab_w_in, a_lam, a_subln_g, b_q_norm_g, b_w_uq, b_kv_norm_g, b_w_ukv, cd_w_in, c_rpb, d_conv, d_w0, d_w2, d_a0, d_a2, d_g2, d_k_k, d_k_a, d_r_k, d_ln_g, d_ln_b):
    x = jnp.concatenate([x_prompt.reshape(N_CTX, D_MODEL), x_sample.reshape(N_SMP, D_MODEL)], axis=0)
    cond8 = jnp.pad(jnp.concatenate([c_ctx[None, :], c], axis=0), ((0, 8 - N_COND), (0, 0)))
    ada = _ada_all(cond8, ada_w, ada_b)
    rope_cos, rope_sin = _rope_tables()
    w_even = jnp.pad(ab_w_in, ((0, 0), (0, 0), (0, EVEN_PAD - EVEN_IN))).astype(BF16)
    w_odd = jnp.pad(cd_w_in, ((0, 0), (0, 0), (0, ODD_PAD - ODD_IN))).astype(BF16)
    w_out_b, wq_b, keys_b = w_out.astype(BF16), peer_wq.astype(BF16), peer_keys.astype(BF16)
    u_b, v_b = peer_u.astype(BF16), peer_v.astype(BF16)
    new_ak, new_av, new_bc, new_bp, new_ck, new_cv, new_sd = [], [], [], [], [], [], []
    for l in range(DEPTH):
        i = l // 2
        mods = ada[l, :N_COND].reshape(N_COND, 6, 1, D_MODEL)
        sh1, sc1, g1, sh2, sc2, g2 = (mods[:, m] for m in range(6))
        h = _modulate(x, norm1_g[l], sh1, sc1)
        if l % 2 == 0:
            lam_init = 0.8 - 0.6 * math.exp(-0.3 * l)
            o, (ak, av, bc, bp) = _even_mixer(
                h, w_even, i, a_lam[i], a_subln_g[i], b_q_norm_g[i], b_w_uq[i], b_kv_norm_g[i], b_w_ukv[i],
                lam_init, cache_a_k[:, i], cache_a_v[:, i], cache_b_ckv[:, i], cache_b_kpe[:, i], rope_cos, rope_sin)
            new_ak.append(ak)
            new_av.append(av)
            new_bc.append(bc)
            new_bp.append(bp)
        else:
            o, (ck, cv, sd) = _odd_mixer(
                h, w_odd, i, c_rpb[i], d_conv[i], d_w0[i], d_w2[i], d_a0[i], d_a2[i], d_g2[i],
                d_k_k[i], d_k_a[i], d_r_k[i], d_ln_g[i], d_ln_b[i], cache_c_k[:, i], cache_c_v[:, i], state_d[:, i])
            new_ck.append(ck)
            new_cv.append(cv)
            new_sd.append(sd)
        x = _matmul_residual(*o, w_out_b, l, x, g1)
        x = _peer_layer(x, l, norm2_g[l], sh2, sc2, g2, wq_b, keys_b[l], u_b, v_b)
    y = _rmsnorm(x, final_g)
    y_prompt = y[:N_CTX].reshape(BATCH, SEQ, D_MODEL)
    y_sample = y[N_CTX:].reshape(DEC_BATCH, DEC_SEQ, D_MODEL)
    return (y_prompt, y_sample, jnp.stack(new_ak, axis=1), jnp.stack(new_av, axis=1), jnp.stack(new_bc, axis=1),
            jnp.stack(new_bp, axis=1), jnp.stack(new_ck, axis=1), jnp.stack(new_cv, axis=1), jnp.stack(new_sd, axis=1))
```

```python
import functools
import math

import numpy as np
import jax
import jax.numpy as jnp
from jax import lax
from jax.experimental import pallas as pl
from jax.experimental.pallas import tpu as pltpu

F32 = jnp.float32
BF16 = jnp.bfloat16

D_MODEL = 2048
BATCH = 16
SEQ = 256
DEPTH = 4
DEC_BATCH = 2
DEC_SEQ = 1024
PAST_LEN = 512
GRID_W = 64
GRID_ROWS = DEC_SEQ // GRID_W
ROPE_BASE = 10000.0
RMS_EPS = 1e-6
NEG_INF = -1e30
N_CTX = BATCH * SEQ
N_SMP = DEC_BATCH * DEC_SEQ
N_TOK = N_CTX + N_SMP
N_COND = 1 + DEC_BATCH

HEADS = 8
HEAD_DIM = 128
A_QK = 64
B_Q_RANK = 512
B_KV_RANK = 256
B_NOPE = 128
B_ROPE = 64
NA_KH = 8
NA_KW = 16
NA_WIN = NA_KH * GRID_W
D_HS = 64
D_WIDTH = 1024
D_HEADS = 16
D_LORA = 64
D_GATE_LORA = 128
D_IN = 3 * D_WIDTH + 2 * D_LORA + 2 * D_LORA + D_GATE_LORA
D_LN_EPS = 64e-5
EVEN_IN = 3904
ODD_IN = 6528
PEER_HEADS = 8
PEER_NKEYS = 128
PEER_HALF = 128
PEER_TOPK = 16

LANES = 128
VMEM_LIMIT = 56 * 1024 * 1024


def _params(*sem):
    return pltpu.CompilerParams(dimension_semantics=sem, vmem_limit_bytes=VMEM_LIMIT)


def _dot_nt(a, b):
    return lax.dot_general(a, b, (((1,), (1,)), ((), ())), preferred_element_type=F32)


def _dot_tn(a, b):
    return lax.dot_general(a, b, (((0,), (0,)), ((), ())), preferred_element_type=F32)


def _cond_of_tile(i, tm):
    n_ctx_tiles = N_CTX // tm
    per_batch = DEC_SEQ // tm
    return jnp.where(i < n_ctx_tiles, 0, 1 + (i - n_ctx_tiles) // per_batch)


def _ada_kernel(c_ref, w_ref, b_ref, o_ref):
    c = c_ref[...]
    sc = c * (1.0 / (1.0 + jnp.exp(-c)))
    o_ref[0] = jnp.dot(sc.astype(BF16), w_ref[0].astype(BF16), preferred_element_type=F32) + b_ref[0]


def _ada_all(cond8, ada_w, ada_b):
    tn = 1024
    n_out = ada_w.shape[-1]
    return pl.pallas_call(
        _ada_kernel,
        grid=(DEPTH, n_out // tn),
        in_specs=[pl.BlockSpec((8, D_MODEL), lambda l, j: (0, 0)),
                  pl.BlockSpec((1, D_MODEL, tn), lambda l, j: (l, 0, j)),
                  pl.BlockSpec((1, 1, tn), lambda l, j: (l, 0, j))],
        out_specs=pl.BlockSpec((1, 8, tn), lambda l, j: (l, 0, j)),
        out_shape=jax.ShapeDtypeStruct((DEPTH, 8, n_out), F32),
        compiler_params=_params("parallel", "parallel"),
        name="ada",
    )(cond8, ada_w, ada_b.reshape(DEPTH, 1, n_out))


def _modulate_kernel(x_ref, g_ref, sh_ref, sc_ref, o_ref):
    x = x_ref[...]
    y = x * lax.rsqrt(jnp.mean(x * x, axis=-1, keepdims=True) + RMS_EPS) * g_ref[...]
    o_ref[...] = (y * (1.0 + sc_ref[0]) + sh_ref[0]).astype(o_ref.dtype)


def _modulate(x, g, shift, scale, out_dtype=BF16, tm=256):
    n, d = x.shape
    return pl.pallas_call(
        _modulate_kernel,
        grid=(n // tm,),
        in_specs=[pl.BlockSpec((tm, d), lambda i: (i, 0)),
                  pl.BlockSpec((1, d), lambda i: (0, 0)),
                  pl.BlockSpec((1, 1, d), lambda i: (_cond_of_tile(i, tm), 0, 0)),
                  pl.BlockSpec((1, 1, d), lambda i: (_cond_of_tile(i, tm), 0, 0))],
        out_specs=pl.BlockSpec((tm, d), lambda i: (i, 0)),
        out_shape=jax.ShapeDtypeStruct((n, d), out_dtype),
        compiler_params=_params("parallel"),
        name="modulate",
    )(x, g.reshape(1, d), shift, scale)


def _rmsnorm_kernel(x_ref, g_ref, o_ref):
    x = x_ref[...]
    y = x * lax.rsqrt(jnp.mean(x * x, axis=-1, keepdims=True) + RMS_EPS) * g_ref[...]
    o_ref[...] = y.astype(o_ref.dtype)


def _rmsnorm(x, g, out_dtype=F32, tm=512, col=0):
    n = x.shape[0]
    d = g.shape[-1]
    return pl.pallas_call(
        _rmsnorm_kernel,
        grid=(n // tm,),
        in_specs=[pl.BlockSpec((tm, d), lambda i: (i, col)),
                  pl.BlockSpec((1, d), lambda i: (0, 0))],
        out_specs=pl.BlockSpec((tm, d), lambda i: (i, 0)),
        out_shape=jax.ShapeDtypeStruct((n, d), out_dtype),
        compiler_params=_params("parallel"),
        name="rmsnorm",
    )(x, g.reshape(1, d))


def _mm_kernel(x_ref, w_ref, o_ref):
    o_ref[...] = jnp.dot(x_ref[...].astype(BF16), w_ref[...], preferred_element_type=F32).astype(o_ref.dtype)


def _weight_spec(w, layer, tn):
    if w.ndim == 2:
        return pl.BlockSpec((w.shape[0], tn), lambda i, j: (0, j))
    return pl.BlockSpec((None, w.shape[1], tn), lambda i, j: (layer, 0, j))


def _matmul(x, w, layer=None, out_dtype=F32, tm=512, tn=512, x_col=0, x_width=None):
    m = x.shape[0]
    k = x_width or x.shape[1]
    n = w.shape[-1]
    tn = min(tn, n)
    assert m % tm == 0 and n % tn == 0, (m, n, tm, tn)
    return pl.pallas_call(
        _mm_kernel,
        grid=(m // tm, n // tn),
        in_specs=[pl.BlockSpec((tm, k), lambda i, j: (i, x_col)),
                  _weight_spec(w, layer, tn)],
        out_specs=pl.BlockSpec((tm, tn), lambda i, j: (i, j)),
        out_shape=jax.ShapeDtypeStruct((m, n), out_dtype),
        compiler_params=_params("parallel", "parallel"),
        name="matmul",
    )(x, w)


def _mm_res_kernel(lc_ref, ls_ref, rc_ref, rs_ref, w_ref, res_ref, gate_ref, o_ref, *, n_ctx_tiles):
    half = w_ref.shape[0] // 2

    def emit(l_ref, r_ref):
        acc = jnp.dot(l_ref[...].astype(BF16), w_ref[0:half, :], preferred_element_type=F32)
        acc = acc + jnp.dot(r_ref[...].astype(BF16), w_ref[half:, :], preferred_element_type=F32)
        o_ref[...] = res_ref[...] + gate_ref[0] * acc

    @pl.when(pl.program_id(0) < n_ctx_tiles)
    def _():
        emit(lc_ref, rc_ref)

    @pl.when(pl.program_id(0) >= n_ctx_tiles)
    def _():
        emit(ls_ref, rs_ref)


def _matmul_residual(left_c, left_s, right_c, right_s, w, layer, res, gate, tm=512, tn=512):
    m = res.shape[0]
    n = w.shape[-1]
    kh = left_c.shape[1]
    nct = left_c.shape[0] // tm
    ctx_spec = pl.BlockSpec((tm, kh), lambda i, j: (jnp.minimum(i, nct - 1), 0))
    smp_spec = pl.BlockSpec((tm, kh), lambda i, j: (jnp.maximum(i - nct, 0), 0))
    return pl.pallas_call(
        functools.partial(_mm_res_kernel, n_ctx_tiles=nct),
        grid=(m // tm, n // tn),
        in_specs=[ctx_spec, smp_spec, ctx_spec, smp_spec,
                  _weight_spec(w, layer, tn),
                  pl.BlockSpec((tm, tn), lambda i, j: (i, j)),
                  pl.BlockSpec((1, 1, tn), lambda i, j: (_cond_of_tile(i, tm), 0, j))],
        out_specs=pl.BlockSpec((tm, tn), lambda i, j: (i, j)),
        out_shape=jax.ShapeDtypeStruct((m, n), F32),
        compiler_params=_params("parallel", "parallel"),
        name="matmul_residual",
    )(left_c, left_s, right_c, right_s, w, res, gate)


def _rope_tables():
    nf = 16
    inv = ROPE_BASE ** (-np.arange(nf, dtype=np.float64) / nf)
    t = np.arange(DEC_SEQ)
    rows, cols = t // GRID_W, t % GRID_W
    lane = np.arange(64)
    pos = np.where(lane[None, :] < 32, rows[:, None], cols[:, None]).astype(np.float32)
    ang = (pos * inv[lane % nf][None, :].astype(np.float32)).astype(np.float32)
    first = (lane % 32) < nf
    cos = np.cos(ang.astype(np.float64))
    sin = np.sin(ang.astype(np.float64)) * np.where(first, -1.0, 1.0)[None, :]
    cos = np.tile(cos, (1, 2)).astype(np.float32)
    sin = np.tile(sin, (1, 2)).astype(np.float32)
    return jnp.asarray(cos), jnp.asarray(sin)


def _rope_kernel(x_ref, cos_ref, sin_ref, o_ref):
    cos = cos_ref[...]
    sin = sin_ref[...]
    lane = lax.broadcasted_iota(jnp.int32, cos.shape, 1)
    first = (lane % 32) < 16
    for c in range(x_ref.shape[1] // LANES):
        x = x_ref[:, c * LANES:(c + 1) * LANES].astype(F32)
        partner = jnp.where(first, pltpu.roll(x, LANES - 16, 1), pltpu.roll(x, 16, 1))
        o_ref[:, c * LANES:(c + 1) * LANES] = (x * cos + partner * sin).astype(o_ref.dtype)


def _rope(x, cos, sin, w, col=0, out_dtype=BF16, tm=256):
    n = N_SMP
    row0 = (x.shape[0] - N_SMP) // tm
    per = DEC_SEQ // tm
    return pl.pallas_call(
        _rope_kernel,
        grid=(n // tm,),
        in_specs=[pl.BlockSpec((tm, w), lambda i: (row0 + i, col)),
                  pl.BlockSpec((tm, LANES), lambda i: (i % per, 0)),
                  pl.BlockSpec((tm, LANES), lambda i: (i % per, 0))],
        out_specs=pl.BlockSpec((tm, w), lambda i: (i, 0)),
        out_shape=jax.ShapeDtypeStruct((n, w), out_dtype),
        compiler_params=_params("parallel"),
        name="rope",
    )(x, cos, sin)


def _softmax(s):
    p = jnp.exp(s - jnp.max(s, axis=-1, keepdims=True))
    return p / jnp.sum(p, axis=-1, keepdims=True)


def _diff_attn_kernel(lam_ref, g_ref, q_ref, k_ref, v_ref, o_ref, *, lam_init):
    lam = lam_ref[...]
    l1 = jnp.sum(jnp.sum(lam[0:1] * lam[1:2], axis=-1, keepdims=True), axis=0, keepdims=True)
    l2 = jnp.sum(jnp.sum(lam[2:3] * lam[3:4], axis=-1, keepdims=True), axis=0, keepdims=True)
    lam_val = jnp.exp(l1) - jnp.exp(l2) + lam_init
    scale = A_QK ** -0.5
    for h in range(HEADS):
        sl = slice(h * HEAD_DIM, (h + 1) * HEAD_DIM)
        q = q_ref[:, sl].astype(BF16)
        k = k_ref[:, sl].astype(BF16)
        p1 = _softmax(_dot_nt(q[:, :A_QK], k[:, :A_QK]) * scale)
        p2 = _softmax(_dot_nt(q[:, A_QK:], k[:, A_QK:]) * scale)
        a = p1 - lam_val * p2
        o = jnp.dot(a.astype(BF16), v_ref[:, sl].astype(BF16), preferred_element_type=F32)
        o = o * lax.rsqrt(jnp.mean(o * o, axis=-1, keepdims=True) + RMS_EPS) * g_ref[...]
        o_ref[:, sl] = (o * (1.0 - lam_init)).astype(o_ref.dtype)


ATT_TQ = 256
ATT_W = HEADS * HEAD_DIM


def _q_rows(t, width, row0=0, col=0):
    per, off = t // ATT_TQ, row0 // ATT_TQ
    return pl.BlockSpec((ATT_TQ, width), lambda b, j: (off + b * per + j, col))


def _kv_rows(s, width, row0=0, col=0):
    off = row0 // s
    return pl.BlockSpec((s, width), lambda b, j: (off + b, col))


def _attention_call(kernel, name, nb, t, operands, specs):
    return pl.pallas_call(
        kernel,
        grid=(nb, t // ATT_TQ),
        in_specs=specs,
        out_specs=_q_rows(t, ATT_W),
        out_shape=jax.ShapeDtypeStruct((nb * t, ATT_W), BF16),
        compiler_params=_params("parallel", "parallel"),
        name=name,
    )(*operands)


def _diff_attention(nb, t, q, k, v, lam, subln_g, lam_init):
    const = [pl.BlockSpec((4, A_QK), lambda i, j: (0, 0)), pl.BlockSpec((1, HEAD_DIM), lambda i, j: (0, 0))]
    return _attention_call(functools.partial(_diff_attn_kernel, lam_init=lam_init), "diff_attention", nb, t,
                           [lam, subln_g.reshape(1, HEAD_DIM), q[0], k[0], v[0]], const + [q[1], k[1], v[1]])


def _mla_attn_kernel(qn_ref, qr_ref, kn_ref, kr_ref, v_ref, o_ref):
    scale = (B_NOPE + B_ROPE) ** -0.5
    kr = kr_ref[:, 0:B_ROPE].astype(BF16)
    for h in range(HEADS):
        sl = slice(h * HEAD_DIM, (h + 1) * HEAD_DIM)
        s = _dot_nt(qn_ref[:, sl].astype(BF16), kn_ref[:, sl])
        s = s + _dot_nt(qr_ref[:, h * B_ROPE:(h + 1) * B_ROPE].astype(BF16), kr)
        p = _softmax(s * scale)
        o = jnp.dot(p.astype(BF16), v_ref[:, sl], preferred_element_type=F32)
        o_ref[:, sl] = o.astype(o_ref.dtype)


def _mla_attention(nb, t, qn, qr, kn, kr, v):
    ops = [qn, qr, kn, kr, v]
    return _attention_call(_mla_attn_kernel, "mla_attention", nb, t, [o[0] for o in ops], [o[1] for o in ops])


def _plain_attn_kernel(q_ref, k_ref, v_ref, o_ref):
    scale = HEAD_DIM ** -0.5
    for h in range(HEADS):
        sl = slice(h * HEAD_DIM, (h + 1) * HEAD_DIM)
        p = _softmax(_dot_nt(q_ref[:, sl].astype(BF16), k_ref[:, sl].astype(BF16)) * scale)
        o = jnp.dot(p.astype(BF16), v_ref[:, sl].astype(BF16), preferred_element_type=F32)
        o_ref[:, sl] = o.astype(o_ref.dtype)


def _plain_attention(nb, t, q, k, v):
    ops = [q, k, v]
    return _attention_call(_plain_attn_kernel, "plain_attention", nb, t, [o[0] for o in ops], [o[1] for o in ops])


def _na_window_start(r):
    return jnp.clip(r - NA_KH // 2, 0, GRID_ROWS - NA_KH)


def _na_attn_kernel(q_ref, k_ref, v_ref, kc_ref, vc_ref, bias_ref, o_ref):
    scale = HEAD_DIM ** -0.5
    r = pl.program_id(1)
    start = pl.multiple_of(_na_window_start(r) * GRID_W, GRID_W)
    for h in range(HEADS):
        sl = slice(h * HEAD_DIM, (h + 1) * HEAD_DIM)
        q = q_ref[:, sl].astype(BF16)
        kw = k_ref[pl.ds(start, NA_WIN), sl].astype(BF16)
        vw = v_ref[pl.ds(start, NA_WIN), sl].astype(BF16)
        s_win = _dot_nt(q, kw) * scale + bias_ref[0, h]
        s_ctx = _dot_nt(q, kc_ref[:, sl].astype(BF16)) * scale
        m = jnp.maximum(jnp.max(s_win, axis=-1, keepdims=True), jnp.max(s_ctx, axis=-1, keepdims=True))
        p_win = jnp.exp(s_win - m)
        p_ctx = jnp.exp(s_ctx - m)
        den = jnp.sum(p_win, axis=-1, keepdims=True) + jnp.sum(p_ctx, axis=-1, keepdims=True)
        o = (jnp.dot(p_win.astype(BF16), vw, preferred_element_type=F32)
             + jnp.dot(p_ctx.astype(BF16), vc_ref[:, sl].astype(BF16), preferred_element_type=F32))
        o_ref[:, sl] = (o / den).astype(o_ref.dtype)


NA_NDC = 2 * NA_KW


def _na_bias_kernel(r_ref, e_ref, o_ref):
    o_ref[0] = jnp.dot(r_ref[0], e_ref[...], preferred_element_type=F32, precision=lax.Precision.HIGHEST)


def _na_bias_tables(rpb):
    cols = np.arange(GRID_W)
    cs = np.clip(cols - NA_KW // 2, 0, GRID_W - NA_KW)
    col_in = (cols[None, :] >= cs[:, None]) & (cols[None, :] < cs[:, None] + NA_KW)
    dc_idx = np.clip(cols[None, :] - cols[:, None] + NA_KW - 1, 0, 2 * NA_KW - 2)
    mask = np.broadcast_to(col_in[:, None, :], (GRID_W, NA_KH, GRID_W)).reshape(GRID_W, NA_WIN)
    onehot = (np.arange(NA_NDC)[:, None] == dc_idx.reshape(1, -1)).astype(np.float32)
    rp = jnp.pad(rpb, ((0, 0), (0, 0), (0, NA_NDC - rpb.shape[2])))
    rows = jnp.stack([rp[:, NA_KH - 1 - d:2 * NA_KH - 1 - d, :] for d in range(NA_KH)])
    rows = rows.reshape(NA_KH, HEADS * NA_KH, NA_NDC)
    nqk = GRID_W * GRID_W
    b = pl.pallas_call(
        _na_bias_kernel,
        grid=(NA_KH,),
        in_specs=[pl.BlockSpec((1, HEADS * NA_KH, NA_NDC), lambda d: (d, 0, 0)),
                  pl.BlockSpec((NA_NDC, nqk), lambda d: (0, 0))],
        out_specs=pl.BlockSpec((1, HEADS * NA_KH, nqk), lambda d: (d, 0, 0)),
        out_shape=jax.ShapeDtypeStruct((NA_KH, HEADS * NA_KH, nqk), F32),
        compiler_params=_params("parallel"),
        name="na_bias",
    )(rows, jnp.asarray(onehot))
    b = b.reshape(NA_KH, HEADS, NA_KH, GRID_W, GRID_W).transpose(0, 1, 3, 2, 4).reshape(NA_KH, HEADS, GRID_W, NA_WIN)
    return jnp.where(mask[None, None], b, NEG_INF)


def _na_attention(z, kc, vc, bias):
    w = ATT_W
    row0 = z.shape[0] - N_SMP

    def all_rows(n, col, base):
        return pl.BlockSpec((n, w), lambda i, r: (base // n + i, col))

    return pl.pallas_call(
        _na_attn_kernel,
        grid=(DEC_BATCH, GRID_ROWS),
        in_specs=[pl.BlockSpec((GRID_W, w), lambda i, r: (row0 // GRID_W + i * GRID_ROWS + r, 0)),
                  all_rows(DEC_SEQ, 1, row0), all_rows(DEC_SEQ, 2, row0),
                  all_rows(PAST_LEN, 0, 0), all_rows(PAST_LEN, 0, 0),
                  pl.BlockSpec((1, HEADS, GRID_W, NA_WIN), lambda i, r: (r - _na_window_start(r), 0, 0, 0))],
        out_specs=pl.BlockSpec((GRID_W, w), lambda i, r: (i * GRID_ROWS + r, 0)),
        out_shape=jax.ShapeDtypeStruct((N_SMP, w), BF16),
        compiler_params=_params("parallel", "parallel"),
        name="na_attention",
    )(z, z, z, kc, vc, bias)


CONV_CW = 384


def _conv3_kernel(x_ref, w_ref, o_ref):
    x = x_ref[...]
    t = x.shape[0]
    row = lax.broadcasted_iota(jnp.int32, x.shape, 0)
    prev = jnp.where(row == 0, 0.0, pltpu.roll(x, 1, 0))
    nxt = jnp.where(row == t - 1, 0.0, pltpu.roll(x, t - 1, 0))
    w = w_ref[...]
    o_ref[...] = prev * w[0:1] + x * w[1:2] + nxt * w[2:3]


def _conv3(z, conv_w, seq, row0, n_seq):
    col0 = (3 * D_WIDTH) // CONV_CW
    blk0 = row0 // seq
    return pl.pallas_call(
        _conv3_kernel,
        grid=(n_seq, D_IN // CONV_CW),
        in_specs=[pl.BlockSpec((seq, CONV_CW), lambda i, j: (blk0 + i, col0 + j)),
                  pl.BlockSpec((3, CONV_CW), lambda i, j: (0, j))],
        out_specs=pl.BlockSpec((seq, CONV_CW), lambda i, j: (i, j)),
        out_shape=jax.ShapeDtypeStruct((n_seq * seq, D_IN), F32),
        compiler_params=_params("parallel", "parallel"),
        name="conv3",
    )(z, conv_w)


def _dot_f32(a, b):
    return jnp.dot(a, b, preferred_element_type=F32, precision=lax.Precision.HIGHEST)


def _d_prep_kernel(x_ref, w0_ref, w2_ref, a0_ref, a2_ref, g2_ref, dec_ref, a_ref, g_ref):
    x = x_ref[...]
    xw = x[:, 0:2 * D_LORA]
    xa = x[:, 2 * D_LORA:4 * D_LORA]
    xg = x[:, 4 * D_LORA:]
    for d in range(2):
        u = w0_ref[d] + _dot_f32(jnp.tanh(xw[:, d * D_LORA:(d + 1) * D_LORA]), w2_ref[d])
        nu = -u
        softplus = jnp.maximum(nu, 0.0) + jnp.log1p(jnp.exp(-jnp.abs(nu)))
        wlog = -softplus - 0.5
        dec_ref[d] = jnp.exp(-jnp.exp(wlog))
        av = a0_ref[d] + _dot_f32(xa[:, d * D_LORA:(d + 1) * D_LORA], a2_ref[d])
        a_ref[d] = 1.0 / (1.0 + jnp.exp(-av))
    g_ref[...] = _dot_f32(1.0 / (1.0 + jnp.exp(-xg)), g2_ref[...])


def _d_prep(zc, w0, w2, a0, a2, g2, tm=256):
    n = zc.shape[0]
    cb = (3 * D_WIDTH) // CONV_CW
    return pl.pallas_call(
        _d_prep_kernel,
        grid=(n // tm,),
        in_specs=[pl.BlockSpec((tm, CONV_CW), lambda i: (i, cb)),
                  pl.BlockSpec((2, 1, D_WIDTH), lambda i: (0, 0, 0)),
                  pl.BlockSpec((2, D_LORA, D_WIDTH), lambda i: (0, 0, 0)),
                  pl.BlockSpec((2, 1, D_WIDTH), lambda i: (0, 0, 0)),
                  pl.BlockSpec((2, D_LORA, D_WIDTH), lambda i: (0, 0, 0)),
                  pl.BlockSpec((D_GATE_LORA, D_WIDTH), lambda i: (0, 0))],
        out_specs=[pl.BlockSpec((2, tm, D_WIDTH), lambda i: (0, i, 0)),
                   pl.BlockSpec((2, tm, D_WIDTH), lambda i: (0, i, 0)),
                   pl.BlockSpec((tm, D_WIDTH), lambda i: (i, 0))],
        out_shape=[jax.ShapeDtypeStruct((2, n, D_WIDTH), F32),
                   jax.ShapeDtypeStruct((2, n, D_WIDTH), F32),
                   jax.ShapeDtypeStruct((n, D_WIDTH), F32)],
        compiler_params=_params("parallel"),
        name="d_prep",
    )(zc, w0.reshape(2, 1, D_WIDTH), w2, a0.reshape(2, 1, D_WIDTH), a2, g2)


SCAN_TT = 32


def _rwkv_scan_kernel(r_ref, k_ref, v_ref, w_ref, a_ref, kkp_ref, kap_ref, rkp_ref, s0_ref,
                      y_ref, bon_ref, sfin_ref, s_ref, tmp_ref, *, groups_per_dir):
    backward = pl.program_id(0) >= groups_per_dir
    tb = pl.program_id(1)

    @pl.when(tb == 0)
    def _():
        s_ref[...] = s0_ref[...]

    kkp = kkp_ref[...]
    kap = kap_ref[...]
    rkp = rkp_ref[...]
    iq = v_ref.shape[1]

    def step(n, carry):
        t = jnp.where(backward, SCAN_TT - 1 - n, n)
        r = r_ref[t]
        k = k_ref[t]
        v = v_ref[t]
        a = a_ref[t]
        kkf = k * kkp
        nrm = jnp.sqrt(jnp.sum(kkf * kkf, axis=0, keepdims=True))
        kk = kkf / jnp.maximum(nrm, 1e-12)
        kd = k * (1.0 + (a - 1.0) * kap)
        bon_ref[t] = jnp.sum(r * kd * rkp, axis=0, keepdims=True)
        tmp_ref[0] = kk
        tmp_ref[1] = kk * a
        tmp_ref[2] = kd
        sa = jnp.zeros((iq, LANES), F32)
        for j in range(D_HS):
            sa = sa + s_ref[j] * tmp_ref[0, pl.ds(j, 1), :]
        y = jnp.zeros((iq, LANES), F32)
        for j in range(D_HS):
            sn = (s_ref[j] * w_ref[t, pl.ds(j, 1), :] - sa * tmp_ref[1, pl.ds(j, 1), :]
                  + v * tmp_ref[2, pl.ds(j, 1), :])
            s_ref[j] = sn
            y = y + sn * r_ref[t, pl.ds(j, 1), :]
        y_ref[t] = y
        return carry

    lax.fori_loop(0, SCAN_TT, step, 0)

    @pl.when(tb == pl.num_programs(1) - 1)
    def _():
        sfin_ref[...] = s_ref[...]


def _rwkv_scan(r, k, v, w, a, kkp, kap, rkp, s0):
    t, _, lanes = r.shape
    iq = v.shape[1]
    gpd = lanes // LANES
    nt = t // SCAN_TT

    def tmap(g, ti):
        return jnp.where(g < gpd, ti, nt - 1 - ti)

    j_spec = pl.BlockSpec((SCAN_TT, D_HS, LANES), lambda g, ti: (tmap(g, ti), 0, g % gpd))
    i_spec = pl.BlockSpec((SCAN_TT, iq, LANES), lambda g, ti: (tmap(g, ti), 0, g % gpd))
    dj_spec = pl.BlockSpec((None, SCAN_TT, D_HS, LANES), lambda g, ti: (g // gpd, tmap(g, ti), 0, g % gpd))
    par_spec = pl.BlockSpec((D_HS, LANES), lambda g, ti: (0, g % gpd))
    st_spec = pl.BlockSpec((None, D_HS, iq, LANES), lambda g, ti: (g // gpd, 0, 0, g % gpd))
    return pl.pallas_call(
        functools.partial(_rwkv_scan_kernel, groups_per_dir=gpd),
        grid=(2 * gpd, nt),
        in_specs=[j_spec, j_spec, i_spec, dj_spec, dj_spec, par_spec, par_spec, par_spec, st_spec],
        out_specs=[pl.BlockSpec((None, SCAN_TT, iq, LANES), lambda g, ti: (g // gpd, tmap(g, ti), 0, g % gpd)),
                   pl.BlockSpec((None, SCAN_TT, 1, LANES), lambda g, ti: (g // gpd, tmap(g, ti), 0, g % gpd)),
                   st_spec],
        out_shape=[jax.ShapeDtypeStruct((2, t, iq, lanes), F32),
                   jax.ShapeDtypeStruct((2, t, 1, lanes), F32),
                   jax.ShapeDtypeStruct((2, D_HS, iq, lanes), F32)],
        scratch_shapes=[pltpu.VMEM((D_HS, iq, LANES), F32), pltpu.VMEM((3, D_HS, LANES), F32)],
        compiler_params=_params("parallel", "arbitrary"),
        name="rwkv_scan",
    )(r, k, v, w, a, kkp, kap, rkp, s0)


def _d_post_kernel(yf_ref, yb_ref, bf_ref, bb_ref, v_ref, g_ref, lng_ref, lnb_ref, o_ref):
    y = yf_ref[...] + yb_ref[...]
    mu = jnp.mean(y, axis=1, keepdims=True)
    var = jnp.mean(jnp.square(y - mu), axis=1, keepdims=True)
    yn = (y - mu) * lax.rsqrt(var + D_LN_EPS) * lng_ref[...] + lnb_ref[...]
    o_ref[...] = (yn + (bf_ref[...] + bb_ref[...]) * v_ref[...]) * g_ref[...]


def _d_post(y, bon, v, g, lng, lnb, tt=32):
    _, t, _, c = y.shape

    def dir_spec(d, rows):
        return pl.BlockSpec((None, tt, rows, LANES), lambda ci, ti: (d, ti, 0, ci))

    seq = pl.BlockSpec((tt, D_HS, LANES), lambda ci, ti: (ti, 0, ci))
    par = pl.BlockSpec((1, D_HS, LANES), lambda ci, ti: (0, 0, ci))
    return pl.pallas_call(
        _d_post_kernel,
        grid=(c // LANES, t // tt),
        in_specs=[dir_spec(0, D_HS), dir_spec(1, D_HS), dir_spec(0, 1), dir_spec(1, 1), seq, seq, par, par],
        out_specs=seq,
        out_shape=jax.ShapeDtypeStruct((t, D_HS, c), F32),
        compiler_params=_params("parallel", "parallel"),
        name="d_post",
    )(y, y, bon, bon, v, g, lng.reshape(1, D_HS, c), lnb.reshape(1, D_HS, c))


def _to_chain(x, b, t):
    return x.reshape(b, t, D_HEADS, D_HS).transpose(1, 3, 0, 2).reshape(t, D_HS, b * D_HEADS)


def _from_chain(x, b, t):
    return x.reshape(t, D_HS, b, D_HEADS).transpose(2, 0, 3, 1).reshape(b * t, D_WIDTH)


def _param_chain(p, n_chain):
    return jnp.tile(p.reshape(D_HEADS, D_HS).T, (1, n_chain // D_HEADS))


def _pad_lanes(x):
    c = x.shape[-1]
    pad = (-c) % LANES
    if pad == 0:
        return x
    return jnp.pad(x, [(0, 0)] * (x.ndim - 1) + [(0, pad)])


def _rwkv_branch(zc, dec, a, g, b, t, s0, k_k, k_a, r_k, ln_g, ln_b):
    nb = b * D_HEADS
    split = max(LANES // nb, 1)
    iq = D_HS // split
    lanes = nb * split

    def j_arr(x):
        xc = _to_chain(x, b, t)
        return jnp.tile(xc, (1, 1, split)) if split > 1 else xc

    def split_rows(x):
        if split == 1:
            return x
        lead = x.shape[:-2]
        n = len(lead)
        x = x.reshape(lead + (split, iq, nb))
        return jnp.swapaxes(x, n, n + 1).reshape(lead + (iq, lanes))

    def merge_rows(x):
        if split == 1:
            return x
        lead = x.shape[:-2]
        n = len(lead)
        x = x.reshape(lead + (iq, split, nb))
        return jnp.swapaxes(x, n, n + 1).reshape(lead + (D_HS, nb))

    r = j_arr(zc[:, 0:D_WIDTH])
    k = j_arr(zc[:, D_WIDTH:2 * D_WIDTH])
    v_chain = _to_chain(zc[:, 2 * D_WIDTH:3 * D_WIDTH], b, t)
    w2 = jnp.stack([j_arr(dec[0]), j_arr(dec[1])])
    a2 = jnp.stack([j_arr(a[0]), j_arr(a[1])])
    if s0 is None:
        s0c = jnp.zeros((2, D_HS, iq, lanes), F32)
    else:
        s0c = split_rows(s0.transpose(0, 4, 3, 1, 2).reshape(2, D_HS, D_HS, nb))
    y, bon, sfin = _rwkv_scan(r, k, split_rows(v_chain), w2, a2,
                              _param_chain(k_k, lanes), _param_chain(k_a, lanes),
                              _param_chain(r_k.reshape(-1), lanes), s0c)
    y = _pad_lanes(merge_rows(y))
    bon = _pad_lanes(bon[..., :nb])
    od = _d_post(y, bon, _pad_lanes(v_chain), _pad_lanes(_to_chain(g, b, t)),
                 _pad_lanes(_param_chain(ln_g, nb)), _pad_lanes(_param_chain(ln_b, nb)))
    od = _from_chain(od[:, :, :nb], b, t)
    sfin = merge_rows(sfin).reshape(2, D_HS, D_HS, b, D_HEADS).transpose(3, 0, 4, 2, 1)
    return od, sfin


TOPK_TM = 256
BIG = 1e9


def _extract_topk(s, labels):
    iota16 = lax.broadcasted_iota(jnp.int32, (PEER_TOPK, s.shape[1]), 0)
    rank = jnp.full(s.shape, BIG, F32)
    vals = jnp.zeros((PEER_TOPK, s.shape[1]), F32)
    for r in range(PEER_TOPK):
        m = jnp.max(s, axis=0, keepdims=True)
        idx = jnp.min(jnp.where(s == m, labels, BIG), axis=0, keepdims=True)
        hit = labels == idx
        rank = jnp.where(hit, float(r), rank)
        vals = jnp.where(iota16 == r, m, vals)
        s = jnp.where(hit, -jnp.inf, s)
    return rank, vals


_CAND_PAIRS = ([(0, b) for b in range(16)] + [(1, b) for b in range(8)] + [(2, b) for b in range(8)]
               + [(3, b) for b in range(8)] + [(4, b) for b in range(4)] + [(5, b) for b in range(4)]
               + [(6, b) for b in range(4)] + [(7, b) for b in range(4)] + [(a, 0) for a in range(8, 16)])
N_CAND = len(_CAND_PAIRS)


def _cand_labels():
    lab = np.array([a * PEER_TOPK + b for a, b in _CAND_PAIRS], np.float32)
    return jnp.asarray(np.broadcast_to(lab[:, None], (N_CAND, LANES)).copy())


def _peer_topk_kernel(q_ref, keys_ref, lab_ref, lim_ref, e1_ref, rb_ref, e2_ref):
    iota_k = lax.broadcasted_iota(jnp.int32, (PEER_NKEYS, LANES), 0).astype(F32)
    row8 = lax.broadcasted_iota(jnp.int32, (8, LANES), 0)
    labels = lab_ref[...]
    for c in range(TOPK_TM // LANES):
        cs = slice(c * LANES, (c + 1) * LANES)
        q1 = q_ref[cs, 0:PEER_HALF].astype(BF16)
        q2 = q_ref[cs, PEER_HALF:2 * PEER_HALF].astype(BF16)
        s1 = _dot_nt(keys_ref[0, 0], q1)
        s2 = _dot_nt(keys_ref[0, 1], q2)
        rank1, sv1 = _extract_topk(s1, iota_k)
        rank2, sv2 = _extract_topk(s2, iota_k)
        lo8 = sv2[0:8]
        lo4 = jnp.where(row8 < 4, lo8, pltpu.roll(lo8, 4, 0))
        cand = jnp.concatenate([
            sv1[0:1] + lo8, sv1[0:1] + sv2[8:16], sv1[1:2] + lo8, sv1[2:3] + lo8, sv1[3:4] + lo8,
            jnp.where(row8 < 4, sv1[4:5], sv1[5:6]) + lo4, jnp.where(row8 < 4, sv1[6:7], sv1[7:8]) + lo4,
            sv1[8:16] + sv2[0:1]], axis=0)
        crank, cvals = _extract_topk(cand, labels)
        z = jnp.sum(jnp.exp(cvals - cvals[0:1]), axis=0, keepdims=True)
        sel = jnp.where(crank < BIG, 1.0, 0.0)

        def count(lo, hi):
            return jnp.sum(sel[lo:hi], axis=0, keepdims=True)

        def count_half(lo, first):
            part = jnp.where((row8 < 4) if first else (row8 >= 4), sel[lo:lo + 8], 0.0)
            return jnp.sum(part, axis=0, keepdims=True)

        n_sel = [count(0, 16), count(16, 24), count(24, 32), count(32, 40),
                 count_half(40, True), count_half(40, False), count_half(48, True), count_half(48, False)]
        n_sel += [sel[56 + a:57 + a] for a in range(8)]
        lim = jnp.zeros_like(s1)
        for a in range(PEER_TOPK):
            lim = jnp.where(rank1 == float(a), n_sel[a], lim)
        lim_ref[0, :, cs] = lim
        e1_ref[0, :, cs] = jnp.exp(s1 - sv1[0:1])
        rb_ref[0, :, cs] = pltpu.bitcast(rank2.astype(BF16), jnp.uint32)
        e2_ref[0, :, cs] = pltpu.bitcast((jnp.exp(s2 - sv2[0:1]) / z).astype(BF16), jnp.uint32)


def _peer_topk(q, keys):
    n = q.shape[0]
    ospec = pl.BlockSpec((1, PEER_NKEYS, TOPK_TM), lambda i, h: (h, 0, i))
    pspec = pl.BlockSpec((1, PEER_NKEYS // 2, TOPK_TM), lambda i, h: (h, 0, i))
    return pl.pallas_call(
        _peer_topk_kernel,
        grid=(n // TOPK_TM, PEER_HEADS),
        in_specs=[pl.BlockSpec((TOPK_TM, 2 * PEER_HALF), lambda i, h: (i, h)),
                  pl.BlockSpec((1, 2, PEER_NKEYS, PEER_HALF), lambda i, h: (h, 0, 0, 0)),
                  pl.BlockSpec((N_CAND, LANES), lambda i, h: (0, 0))],
        out_specs=[ospec, ospec, pspec, pspec],
        out_shape=[jax.ShapeDtypeStruct((PEER_HEADS, PEER_NKEYS, n), F32)] * 2
        + [jax.ShapeDtypeStruct((PEER_HEADS, PEER_NKEYS // 2, n), jnp.uint32)] * 2,
        compiler_params=_params("parallel", "parallel"),
        name="peer_topk",
    )(q, keys, _cand_labels())


PEER_TM = 512
PEER_TI = 8
PEER_RG = 4
PEER_JT = 16


def _gelu(x):
    return 0.5 * x * (1.0 + lax.erf(x * (2.0 ** -0.5)))


def _peer_dense_kernel(x_ref, u_ref, v_ref, lim_ref, e1_ref, rb_ref, e2_ref, res_ref, gate_ref, o_ref, w_ref):
    e = pl.program_id(1)

    @pl.when(e == 0)
    def _():
        o_ref[...] = jnp.zeros_like(o_ref)

    hid = _dot_nt(u_ref[...], x_ref[...])
    zero = jnp.zeros((), BF16)
    n_jt = PEER_NKEYS // PEER_JT
    for c in range(PEER_TM // LANES):
        cs = slice(c * LANES, (c + 1) * LANES)
        for rg in range(PEER_TI // PEER_RG):
            rows = range(rg * PEER_RG, (rg + 1) * PEER_RG)
            g = [[jnp.zeros((PEER_JT, LANES), BF16) for _ in range(n_jt)] for _ in rows]
            for h in range(PEER_HEADS):
                lim = [jnp.broadcast_to(lim_ref[h, ii:ii + 1, cs], (PEER_JT, LANES)).astype(BF16) for ii in rows]
                e1 = [jnp.broadcast_to(e1_ref[h, ii:ii + 1, cs], (PEER_JT, LANES)).astype(BF16) for ii in rows]
                for jt in range(n_jt):
                    ws = slice(jt * PEER_JT // 2, (jt + 1) * PEER_JT // 2)
                    rb = pltpu.bitcast(rb_ref[h, ws, cs], BF16)
                    e2 = pltpu.bitcast(e2_ref[h, ws, cs], BF16)
                    for k in range(PEER_RG):
                        g[k][jt] = g[k][jt] + jnp.where(rb < lim[k], e2, zero) * e1[k]
            for k, ii in enumerate(rows):
                for jt in range(n_jt):
                    e0 = ii * PEER_NKEYS + jt * PEER_JT
                    w = g[k][jt] * _gelu(hid[e0:e0 + PEER_JT, cs].astype(BF16))
                    w_ref[e0 // 2:(e0 + PEER_JT) // 2, cs] = pltpu.bitcast(w, jnp.uint32)
    o_ref[...] += _dot_tn(pltpu.bitcast(w_ref[...], BF16), v_ref[...])

    @pl.when(e == pl.num_programs(1) - 1)
    def _():
        o_ref[...] = res_ref[...] + gate_ref[0] * o_ref[...]


def _peer_dense(xm, u, v, layer, lim, e1, rb, e2, res, gate):
    n, d = xm.shape
    n_exp = u.shape[1]
    te = PEER_TI * PEER_NKEYS
    sel_spec = pl.BlockSpec((PEER_HEADS, PEER_NKEYS // 2, PEER_TM), lambda i, e: (0, 0, i))
    row_spec = pl.BlockSpec((PEER_HEADS, PEER_TI, PEER_TM), lambda i, e: (0, e, i))
    return pl.pallas_call(
        _peer_dense_kernel,
        grid=(n // PEER_TM, n_exp // te),
        in_specs=[pl.BlockSpec((PEER_TM, d), lambda i, e: (i, 0)),
                  pl.BlockSpec((None, te, d), lambda i, e: (layer, e, 0)),
                  pl.BlockSpec((None, te, d), lambda i, e: (layer, e, 0)),
                  row_spec, row_spec, sel_spec, sel_spec,
                  pl.BlockSpec((PEER_TM, d), lambda i, e: (i, 0)),
                  pl.BlockSpec((1, 1, d), lambda i, e: (_cond_of_tile(i, PEER_TM), 0, 0))],
        out_specs=pl.BlockSpec((PEER_TM, d), lambda i, e: (i, 0)),
        out_shape=jax.ShapeDtypeStruct((n, d), F32),
        scratch_shapes=[pltpu.VMEM((te // 2, PEER_TM), jnp.uint32)],
        compiler_params=_params("parallel", "arbitrary"),
        name="peer_dense",
    )(xm, u, v, lim, e1, rb, e2, res, gate)


def _peer_layer(x, layer, norm_g, shift, scale, gate, wq, keys, u, v):
    xm = _modulate(x, norm_g, shift, scale)
    q = _matmul(xm, wq, layer)
    lim, e1, rb, e2 = _peer_topk(q, keys)
    return _peer_dense(xm, u, v, layer, lim, e1, rb, e2, x, gate)


EVEN_PAD = 4096
ODD_PAD = 6656
S_ALL = PAST_LEN + DEC_SEQ


def _even_mixer(h, w_in, layer, lam, subln_g, q_norm_g, w_uq, kv_norm_g, w_ukv, lam_init,
                cache_k, cache_v, cache_ckv, cache_kpe, rope_cos, rope_sin):
    z = _matmul(h, w_in, layer)
    ckv_n = _rmsnorm(z, kv_norm_g, col=3584 // B_KV_RANK)
    cq_n = _rmsnorm(z, q_norm_g, out_dtype=BF16, col=3072 // B_Q_RANK)
    wq3 = w_uq.reshape(B_Q_RANK, HEADS, B_NOPE + B_ROPE)
    w_uq_r = jnp.concatenate([wq3[:, :, :B_NOPE].reshape(B_Q_RANK, -1), wq3[:, :, B_NOPE:].reshape(B_Q_RANK, -1)], axis=1)
    qb = _matmul(cq_n, w_uq_r.astype(BF16), tn=512)
    new = (z[:N_CTX, 1024:2048].reshape(BATCH, SEQ, HEADS, 2 * A_QK),
           z[:N_CTX, 2048:3072].reshape(BATCH, SEQ, HEADS, HEAD_DIM),
           ckv_n[:N_CTX].reshape(BATCH, SEQ, B_KV_RANK), z[:N_CTX, 3840:3904].reshape(BATCH, SEQ, B_ROPE))

    qa_s = _rope(z, rope_cos, rope_sin, ATT_W, col=0)
    ka_s = _rope(z, rope_cos, rope_sin, ATT_W, col=1)
    qr_s = _rope(qb, rope_cos, rope_sin, HEADS * B_ROPE, col=2)
    kpe_s = _rope(z, rope_cos, rope_sin, LANES, col=3840 // LANES)[:, :B_ROPE]

    def with_ctx(cache, own, width):
        both = jnp.concatenate([cache.reshape(DEC_BATCH, PAST_LEN, width).astype(own.dtype),
                                own.reshape(DEC_BATCH, DEC_SEQ, width)], axis=1)
        return both.reshape(DEC_BATCH * S_ALL, width)

    ka_all = with_ctx(cache_k, ka_s, ATT_W)
    va_all = with_ctx(cache_v, z[N_CTX:, 2048:3072].astype(BF16), ATT_W)
    ckv_all = with_ctx(cache_ckv, ckv_n[N_CTX:], B_KV_RANK)
    kpe_all = with_ctx(cache_kpe, kpe_s, B_ROPE)

    wkv3 = w_ukv.reshape(B_KV_RANK, HEADS, B_NOPE + HEAD_DIM)
    w_ukv_r = jnp.concatenate([wkv3[:, :, :B_NOPE].reshape(B_KV_RANK, -1), wkv3[:, :, B_NOPE:].reshape(B_KV_RANK, -1)], axis=1)
    n_skv = DEC_BATCH * S_ALL
    kv = _matmul(jnp.concatenate([ckv_all, ckv_n[:N_CTX]], axis=0), w_ukv_r.astype(BF16), out_dtype=BF16)

    oa_c = _diff_attention(BATCH, SEQ, (z, _q_rows(SEQ, ATT_W, 0, 0)), (z, _kv_rows(SEQ, ATT_W, 0, 1)),
                           (z, _kv_rows(SEQ, ATT_W, 0, 2)), lam, subln_g, lam_init)
    oa_s = _diff_attention(DEC_BATCH, DEC_SEQ, (qa_s, _q_rows(DEC_SEQ, ATT_W)), (ka_all, _kv_rows(S_ALL, ATT_W)),
                           (va_all, _kv_rows(S_ALL, ATT_W)), lam, subln_g, lam_init)
    ob_c = _mla_attention(BATCH, SEQ, (qb, _q_rows(SEQ, ATT_W, 0, 0)), (qb, _q_rows(SEQ, HEADS * B_ROPE, 0, 2)),
                          (kv, _kv_rows(SEQ, ATT_W, n_skv, 0)), (z, _kv_rows(SEQ, LANES, 0, 3840 // LANES)),
                          (kv, _kv_rows(SEQ, ATT_W, n_skv, 1)))
    ob_s = _mla_attention(DEC_BATCH, DEC_SEQ, (qb, _q_rows(DEC_SEQ, ATT_W, N_CTX, 0)),
                          (qr_s, _q_rows(DEC_SEQ, HEADS * B_ROPE)), (kv, _kv_rows(S_ALL, ATT_W, 0, 0)),
                          (kpe_all, _kv_rows(S_ALL, B_ROPE)), (kv, _kv_rows(S_ALL, ATT_W, 0, 1)))
    return (oa_c, oa_s, ob_c, ob_s), new


def _odd_mixer(h, w_in, layer, rpb, conv_w, w0, w2, a0, a2, g2, k_k, k_a, r_k, ln_g, ln_b,
               cache_k, cache_v, state):
    z = _matmul(h, w_in, layer)
    new_k = z[:N_CTX, 1024:2048].reshape(BATCH, SEQ, HEADS, HEAD_DIM)
    new_v = z[:N_CTX, 2048:3072].reshape(BATCH, SEQ, HEADS, HEAD_DIM)
    oc_c = _plain_attention(BATCH, SEQ, (z, _q_rows(SEQ, ATT_W, 0, 0)), (z, _kv_rows(SEQ, ATT_W, 0, 1)),
                            (z, _kv_rows(SEQ, ATT_W, 0, 2)))
    oc_s = _na_attention(z, cache_k.reshape(DEC_BATCH * PAST_LEN, ATT_W), cache_v.reshape(DEC_BATCH * PAST_LEN, ATT_W),
                         _na_bias_tables(rpb))

    zc_c = _conv3(z, conv_w, SEQ, 0, BATCH)
    zc_s = _conv3(z, conv_w, DEC_SEQ, N_CTX, DEC_BATCH)
    dec_c, a_c, g_c = _d_prep(zc_c, w0, w2, a0, a2, g2)
    dec_s, a_s, g_s = _d_prep(zc_s, w0, w2, a0, a2, g2)
    od_c, sfin = _rwkv_branch(zc_c, dec_c, a_c, g_c, BATCH, SEQ, None, k_k, k_a, r_k, ln_g, ln_b)
    s0 = jnp.moveaxis(state, 1, 0)
    od_s, _ = _rwkv_branch(zc_s, dec_s, a_s, g_s, DEC_BATCH, DEC_SEQ, s0, k_k, k_a, r_k, ln_g, ln_b)
    return (oc_c, oc_s, od_c, od_s), (new_k, new_v, sfin)


def kernel(x_prompt, x_sample, cache_a_k, cache_a_v, cache_b_ckv, cache_b_kpe, cache_c_k, cache_c_v, state_d, c, c_ctx, ada_w, ada_b, norm1_g, norm2_g, w_out, peer_wq, peer_keys, peer_u, peer_v, final_g, ab_w_in, a_lam, a_subln_g, b_q_norm_g, b_w_uq, b_kv_norm_g, b_w_ukv, cd_w_in, c_rpb, d_conv, d_w0, d_w2, d_a0, d_a2, d_g2, d_k_k, d_k_a, d_r_k, d_ln_g, d_ln_b):
    x = jnp.concatenate([x_prompt.reshape(N_CTX, D_MODEL), x_sample.reshape(N_SMP, D_MODEL)], axis=0)
    cond8 = jnp.pad(jnp.concatenate([c_ctx[None, :], c], axis=0), ((0, 8 - N_COND), (0, 0)))
    ada = _ada_all(cond8, ada_w, ada_b)
    rope_cos, rope_sin = _rope_tables()
    w_even = jnp.pad(ab_w_in, ((0, 0), (0, 0), (0, EVEN_PAD - EVEN_IN))).astype(BF16)
    w_odd = jnp.pad(cd_w_in, ((0, 0), (0, 0), (0, ODD_PAD - ODD_IN))).astype(BF16)
    w_out_b, wq_b, keys_b = w_out.astype(BF16), peer_wq.astype(BF16), peer_keys.astype(BF16)
    u_b, v_b = peer_u.astype(BF16), peer_v.astype(BF16)
    new_ak, new_av, new_bc, new_bp, new_ck, new_cv, new_sd = [], [], [], [], [], [], []
    for l in range(DEPTH):
        i = l // 2
        mods = ada[l, :N_COND].reshape(N_COND, 6, 1, D_MODEL)
        sh1, sc1, g1, sh2, sc2, g2 = (mods[:, m] for m in range(6))
        h = _modulate(x, norm1_g[l], sh1, sc1)
        if l % 2 == 0:
            lam_init = 0.8 - 0.6 * math.exp(-0.3 * l)
            o, (ak, av, bc, bp) = _even_mixer(
                h, w_even, i, a_lam[i], a_subln_g[i], b_q_norm_g[i], b_w_uq[i], b_kv_norm_g[i], b_w_ukv[i],
                lam_init, cache_a_k[:, i], cache_a_v[:, i], cache_b_ckv[:, i], cache_b_kpe[:, i], rope_cos, rope_sin)
            new_ak.append(ak)
            new_av.append(av)
            new_bc.append(bc)
            new_bp.append(bp)
        else:
            o, (ck, cv, sd) = _odd_mixer(
                h, w_odd, i, c_rpb[i], d_conv[i], d_w0[i], d_w2[i], d_a0[i], d_a2[i], d_g2[i],
                d_k_k[i], d_k_a[i], d_r_k[i], d_ln_g[i], d_ln_b[i], cache_c_k[:, i], cache_c_v[:, i], state_d[:, i])
            new_ck.append(ck)
            new_cv.append(cv)
            new_sd.append(sd)
        x = _matmul_residual(*o, w_out_b, l, x, g1)
        x = _peer_layer(x, l, norm2_g[l], sh2, sc2, g2, wq_b, keys_b[l], u_b, v_b)
    y = _rmsnorm(x, final_g)
    y_prompt = y[:N_CTX].reshape(BATCH, SEQ, D_MODEL)
    y_sample = y[N_CTX:].reshape(DEC_BATCH, DEC_SEQ, D_MODEL)
    return (y_prompt, y_sample, jnp.stack(new_ak, axis=1), jnp.stack(new_av, axis=1), jnp.stack(new_bc, axis=1),
            jnp.stack(new_bp, axis=1), jnp.stack(new_ck, axis=1), jnp.stack(new_cv, axis=1), jnp.stack(new_sd, axis=1))
```

```python
import functools
import math

import numpy as np
import jax
import jax.numpy as jnp
from jax import lax
from jax.experimental import pallas as pl
from jax.experimental.pallas import tpu as pltpu

F32 = jnp.float32
BF16 = jnp.bfloat16

D_MODEL = 2048
BATCH = 16
SEQ = 256
DEPTH = 4
DEC_BATCH = 2
DEC_SEQ = 1024
PAST_LEN = 512
GRID_W = 64
GRID_ROWS = DEC_SEQ // GRID_W
ROPE_BASE = 10000.0
RMS_EPS = 1e-6
NEG_INF = -1e30
N_CTX = BATCH * SEQ
N_SMP = DEC_BATCH * DEC_SEQ
N_TOK = N_CTX + N_SMP
N_COND = 1 + DEC_BATCH

HEADS = 8
HEAD_DIM = 128
A_QK = 64
B_Q_RANK = 512
B_KV_RANK = 256
B_NOPE = 128
B_ROPE = 64
NA_KH = 8
NA_KW = 16
NA_WIN = NA_KH * GRID_W
D_HS = 64
D_WIDTH = 1024
D_HEADS = 16
D_LORA = 64
D_GATE_LORA = 128
D_IN = 3 * D_WIDTH + 2 * D_LORA + 2 * D_LORA + D_GATE_LORA
D_LN_EPS = 64e-5
EVEN_IN = 3904
ODD_IN = 6528
PEER_HEADS = 8
PEER_NKEYS = 128
PEER_HALF = 128
PEER_TOPK = 16

LANES = 128
VMEM_LIMIT = 56 * 1024 * 1024


def _params(*sem):
    return pltpu.CompilerParams(dimension_semantics=sem, vmem_limit_bytes=VMEM_LIMIT)


def _dot_nt(a, b):
    return lax.dot_general(a, b, (((1,), (1,)), ((), ())), preferred_element_type=F32)


def _dot_tn(a, b):
    return lax.dot_general(a, b, (((0,), (0,)), ((), ())), preferred_element_type=F32)


def _cond_of_tile(i, tm):
    n_ctx_tiles = N_CTX // tm
    per_batch = DEC_SEQ // tm
    return jnp.where(i < n_ctx_tiles, 0, 1 + (i - n_ctx_tiles) // per_batch)


def _ada_kernel(c_ref, w_ref, b_ref, o_ref):
    c = c_ref[...]
    sc = c * (1.0 / (1.0 + jnp.exp(-c)))
    o_ref[0] = jnp.dot(sc.astype(BF16), w_ref[0].astype(BF16), preferred_element_type=F32) + b_ref[0]


def _ada_all(cond8, ada_w, ada_b):
    tn = 1024
    n_out = ada_w.shape[-1]
    return pl.pallas_call(
        _ada_kernel,
        grid=(DEPTH, n_out // tn),
        in_specs=[pl.BlockSpec((8, D_MODEL), lambda l, j: (0, 0)),
                  pl.BlockSpec((1, D_MODEL, tn), lambda l, j: (l, 0, j)),
                  pl.BlockSpec((1, 1, tn), lambda l, j: (l, 0, j))],
        out_specs=pl.BlockSpec((1, 8, tn), lambda l, j: (l, 0, j)),
        out_shape=jax.ShapeDtypeStruct((DEPTH, 8, n_out), F32),
        compiler_params=_params("parallel", "parallel"),
        name="ada",
    )(cond8, ada_w, ada_b.reshape(DEPTH, 1, n_out))


def _modulate_kernel(x_ref, g_ref, sh_ref, sc_ref, o_ref):
    x = x_ref[...]
    y = x * lax.rsqrt(jnp.mean(x * x, axis=-1, keepdims=True) + RMS_EPS) * g_ref[...]
    o_ref[...] = (y * (1.0 + sc_ref[0]) + sh_ref[0]).astype(o_ref.dtype)


def _modulate(x, g, shift, scale, out_dtype=BF16, tm=256):
    n, d = x.shape
    return pl.pallas_call(
        _modulate_kernel,
        grid=(n // tm,),
        in_specs=[pl.BlockSpec((tm, d), lambda i: (i, 0)),
                  pl.BlockSpec((1, d), lambda i: (0, 0)),
                  pl.BlockSpec((1, 1, d), lambda i: (_cond_of_tile(i, tm), 0, 0)),
                  pl.BlockSpec((1, 1, d), lambda i: (_cond_of_tile(i, tm), 0, 0))],
        out_specs=pl.BlockSpec((tm, d), lambda i: (i, 0)),
        out_shape=jax.ShapeDtypeStruct((n, d), out_dtype),
        compiler_params=_params("parallel"),
        name="modulate",
    )(x, g.reshape(1, d), shift, scale)


def _rmsnorm_kernel(x_ref, g_ref, o_ref):
    x = x_ref[...]
    y = x * lax.rsqrt(jnp.mean(x * x, axis=-1, keepdims=True) + RMS_EPS) * g_ref[...]
    o_ref[...] = y.astype(o_ref.dtype)


def _rmsnorm(x, g, out_dtype=F32, tm=512, col=0):
    n = x.shape[0]
    d = g.shape[-1]
    return pl.pallas_call(
        _rmsnorm_kernel,
        grid=(n // tm,),
        in_specs=[pl.BlockSpec((tm, d), lambda i: (i, col)),
                  pl.BlockSpec((1, d), lambda i: (0, 0))],
        out_specs=pl.BlockSpec((tm, d), lambda i: (i, 0)),
        out_shape=jax.ShapeDtypeStruct((n, d), out_dtype),
        compiler_params=_params("parallel"),
        name="rmsnorm",
    )(x, g.reshape(1, d))


def _mm_kernel(x_ref, w_ref, o_ref):
    o_ref[...] = jnp.dot(x_ref[...].astype(BF16), w_ref[...], preferred_element_type=F32).astype(o_ref.dtype)


def _weight_spec(w, layer, tn):
    if w.ndim == 2:
        return pl.BlockSpec((w.shape[0], tn), lambda i, j: (0, j))
    return pl.BlockSpec((None, w.shape[1], tn), lambda i, j: (layer, 0, j))


MM_TM = 1024
MM_TN = 1024


def _matmul(x, w, layer=None, out_dtype=F32, tm=MM_TM, tn=None, x_col=0, x_width=None):
    m = x.shape[0]
    k = x_width or x.shape[1]
    n = w.shape[-1]
    if tn is None:
        tn = MM_TN if n % MM_TN == 0 else MM_TN // 2
    tn = min(tn, n)
    assert m % tm == 0 and n % tn == 0, (m, n, tm, tn)
    return pl.pallas_call(
        _mm_kernel,
        grid=(m // tm, n // tn),
        in_specs=[pl.BlockSpec((tm, k), lambda i, j: (i, x_col)),
                  _weight_spec(w, layer, tn)],
        out_specs=pl.BlockSpec((tm, tn), lambda i, j: (i, j)),
        out_shape=jax.ShapeDtypeStruct((m, n), out_dtype),
        compiler_params=_params("parallel", "parallel"),
        name="matmul",
    )(x, w)


def _mm_res_kernel(lc_ref, ls_ref, rc_ref, rs_ref, w_ref, res_ref, gate_ref, o_ref, *, n_ctx_tiles):
    half = w_ref.shape[0] // 2

    def emit(l_ref, r_ref):
        acc = jnp.dot(l_ref[...].astype(BF16), w_ref[0:half, :], preferred_element_type=F32)
        acc = acc + jnp.dot(r_ref[...].astype(BF16), w_ref[half:, :], preferred_element_type=F32)
        o_ref[...] = res_ref[...] + gate_ref[0] * acc

    @pl.when(pl.program_id(0) < n_ctx_tiles)
    def _():
        emit(lc_ref, rc_ref)

    @pl.when(pl.program_id(0) >= n_ctx_tiles)
    def _():
        emit(ls_ref, rs_ref)


def _matmul_residual(left_c, left_s, right_c, right_s, w, layer, res, gate, tm=MM_TM, tn=MM_TN // 2):
    m = res.shape[0]
    n = w.shape[-1]
    kh = left_c.shape[1]
    nct = left_c.shape[0] // tm
    ctx_spec = pl.BlockSpec((tm, kh), lambda i, j: (jnp.minimum(i, nct - 1), 0))
    smp_spec = pl.BlockSpec((tm, kh), lambda i, j: (jnp.maximum(i - nct, 0), 0))
    return pl.pallas_call(
        functools.partial(_mm_res_kernel, n_ctx_tiles=nct),
        grid=(m // tm, n // tn),
        in_specs=[ctx_spec, smp_spec, ctx_spec, smp_spec,
                  _weight_spec(w, layer, tn),
                  pl.BlockSpec((tm, tn), lambda i, j: (i, j)),
                  pl.BlockSpec((1, 1, tn), lambda i, j: (_cond_of_tile(i, tm), 0, j))],
        out_specs=pl.BlockSpec((tm, tn), lambda i, j: (i, j)),
        out_shape=jax.ShapeDtypeStruct((m, n), F32),
        compiler_params=_params("parallel", "parallel"),
        name="matmul_residual",
    )(left_c, left_s, right_c, right_s, w, res, gate)


def _rope_tables():
    nf = 16
    inv = ROPE_BASE ** (-np.arange(nf, dtype=np.float64) / nf)
    t = np.arange(DEC_SEQ)
    rows, cols = t // GRID_W, t % GRID_W
    lane = np.arange(64)
    pos = np.where(lane[None, :] < 32, rows[:, None], cols[:, None]).astype(np.float32)
    ang = (pos * inv[lane % nf][None, :].astype(np.float32)).astype(np.float32)
    first = (lane % 32) < nf
    cos = np.cos(ang.astype(np.float64))
    sin = np.sin(ang.astype(np.float64)) * np.where(first, -1.0, 1.0)[None, :]
    cos = np.tile(cos, (1, 2)).astype(np.float32)
    sin = np.tile(sin, (1, 2)).astype(np.float32)
    return jnp.asarray(cos), jnp.asarray(sin)


def _rope_kernel(x_ref, cos_ref, sin_ref, o_ref):
    cos = cos_ref[...]
    sin = sin_ref[...]
    lane = lax.broadcasted_iota(jnp.int32, cos.shape, 1)
    first = (lane % 32) < 16
    for c in range(x_ref.shape[1] // LANES):
        x = x_ref[:, c * LANES:(c + 1) * LANES].astype(F32)
        partner = jnp.where(first, pltpu.roll(x, LANES - 16, 1), pltpu.roll(x, 16, 1))
        o_ref[:, c * LANES:(c + 1) * LANES] = (x * cos + partner * sin).astype(o_ref.dtype)


def _rope(x, cos, sin, w, col=0, out_dtype=BF16, tm=256):
    n = N_SMP
    row0 = (x.shape[0] - N_SMP) // tm
    per = DEC_SEQ // tm
    return pl.pallas_call(
        _rope_kernel,
        grid=(n // tm,),
        in_specs=[pl.BlockSpec((tm, w), lambda i: (row0 + i, col)),
                  pl.BlockSpec((tm, LANES), lambda i: (i % per, 0)),
                  pl.BlockSpec((tm, LANES), lambda i: (i % per, 0))],
        out_specs=pl.BlockSpec((tm, w), lambda i: (i, 0)),
        out_shape=jax.ShapeDtypeStruct((n, w), out_dtype),
        compiler_params=_params("parallel"),
        name="rope",
    )(x, cos, sin)


def _softmax(s):
    p = jnp.exp(s - jnp.max(s, axis=-1, keepdims=True))
    return p / jnp.sum(p, axis=-1, keepdims=True)


def _diff_attn_kernel(lam_ref, g_ref, q_ref, k_ref, v_ref, o_ref, *, lam_init):
    lam = lam_ref[...]
    l1 = jnp.sum(jnp.sum(lam[0:1] * lam[1:2], axis=-1, keepdims=True), axis=0, keepdims=True)
    l2 = jnp.sum(jnp.sum(lam[2:3] * lam[3:4], axis=-1, keepdims=True), axis=0, keepdims=True)
    lam_val = jnp.exp(l1) - jnp.exp(l2) + lam_init
    scale = A_QK ** -0.5
    for h in range(HEADS):
        sl = slice(h * HEAD_DIM, (h + 1) * HEAD_DIM)
        q = q_ref[:, sl].astype(BF16)
        k = k_ref[:, sl].astype(BF16)
        p1 = _softmax(_dot_nt(q[:, :A_QK], k[:, :A_QK]) * scale)
        p2 = _softmax(_dot_nt(q[:, A_QK:], k[:, A_QK:]) * scale)
        a = p1 - lam_val * p2
        o = jnp.dot(a.astype(BF16), v_ref[:, sl].astype(BF16), preferred_element_type=F32)
        o = o * lax.rsqrt(jnp.mean(o * o, axis=-1, keepdims=True) + RMS_EPS) * g_ref[...]
        o_ref[:, sl] = (o * (1.0 - lam_init)).astype(o_ref.dtype)


ATT_TQ = 256
ATT_W = HEADS * HEAD_DIM


def _q_rows(t, width, row0=0, col=0):
    per, off = t // ATT_TQ, row0 // ATT_TQ
    return pl.BlockSpec((ATT_TQ, width), lambda b, j: (off + b * per + j, col))


def _kv_rows(s, width, row0=0, col=0):
    off = row0 // s
    return pl.BlockSpec((s, width), lambda b, j: (off + b, col))


def _attention_call(kernel, name, nb, t, operands, specs):
    return pl.pallas_call(
        kernel,
        grid=(nb, t // ATT_TQ),
        in_specs=specs,
        out_specs=_q_rows(t, ATT_W),
        out_shape=jax.ShapeDtypeStruct((nb * t, ATT_W), BF16),
        compiler_params=_params("parallel", "parallel"),
        name=name,
    )(*operands)


def _diff_attention(nb, t, q, k, v, lam, subln_g, lam_init):
    const = [pl.BlockSpec((4, A_QK), lambda i, j: (0, 0)), pl.BlockSpec((1, HEAD_DIM), lambda i, j: (0, 0))]
    return _attention_call(functools.partial(_diff_attn_kernel, lam_init=lam_init), "diff_attention", nb, t,
                           [lam, subln_g.reshape(1, HEAD_DIM), q[0], k[0], v[0]], const + [q[1], k[1], v[1]])


def _mla_attn_kernel(qn_ref, qr_ref, kn_ref, kr_ref, v_ref, o_ref):
    scale = (B_NOPE + B_ROPE) ** -0.5
    kr = kr_ref[:, 0:B_ROPE].astype(BF16)
    for h in range(HEADS):
        sl = slice(h * HEAD_DIM, (h + 1) * HEAD_DIM)
        s = _dot_nt(qn_ref[:, sl].astype(BF16), kn_ref[:, sl])
        s = s + _dot_nt(qr_ref[:, h * B_ROPE:(h + 1) * B_ROPE].astype(BF16), kr)
        p = _softmax(s * scale)
        o = jnp.dot(p.astype(BF16), v_ref[:, sl], preferred_element_type=F32)
        o_ref[:, sl] = o.astype(o_ref.dtype)


def _mla_attention(nb, t, qn, qr, kn, kr, v):
    ops = [qn, qr, kn, kr, v]
    return _attention_call(_mla_attn_kernel, "mla_attention", nb, t, [o[0] for o in ops], [o[1] for o in ops])


def _plain_attn_kernel(q_ref, k_ref, v_ref, o_ref):
    scale = HEAD_DIM ** -0.5
    for h in range(HEADS):
        sl = slice(h * HEAD_DIM, (h + 1) * HEAD_DIM)
        p = _softmax(_dot_nt(q_ref[:, sl].astype(BF16), k_ref[:, sl].astype(BF16)) * scale)
        o = jnp.dot(p.astype(BF16), v_ref[:, sl].astype(BF16), preferred_element_type=F32)
        o_ref[:, sl] = o.astype(o_ref.dtype)


def _plain_attention(nb, t, q, k, v):
    ops = [q, k, v]
    return _attention_call(_plain_attn_kernel, "plain_attention", nb, t, [o[0] for o in ops], [o[1] for o in ops])


def _na_window_start(r):
    return jnp.clip(r - NA_KH // 2, 0, GRID_ROWS - NA_KH)


def _na_attn_kernel(q_ref, k_ref, v_ref, kc_ref, vc_ref, bias_ref, o_ref):
    scale = HEAD_DIM ** -0.5
    r = pl.program_id(1)
    start = pl.multiple_of(_na_window_start(r) * GRID_W, GRID_W)
    for h in range(HEADS):
        sl = slice(h * HEAD_DIM, (h + 1) * HEAD_DIM)
        q = q_ref[:, sl].astype(BF16)
        kw = k_ref[pl.ds(start, NA_WIN), sl].astype(BF16)
        vw = v_ref[pl.ds(start, NA_WIN), sl].astype(BF16)
        s_win = _dot_nt(q, kw) * scale + bias_ref[0, h]
        s_ctx = _dot_nt(q, kc_ref[:, sl].astype(BF16)) * scale
        m = jnp.maximum(jnp.max(s_win, axis=-1, keepdims=True), jnp.max(s_ctx, axis=-1, keepdims=True))
        p_win = jnp.exp(s_win - m)
        p_ctx = jnp.exp(s_ctx - m)
        den = jnp.sum(p_win, axis=-1, keepdims=True) + jnp.sum(p_ctx, axis=-1, keepdims=True)
        o = (jnp.dot(p_win.astype(BF16), vw, preferred_element_type=F32)
             + jnp.dot(p_ctx.astype(BF16), vc_ref[:, sl].astype(BF16), preferred_element_type=F32))
        o_ref[:, sl] = (o / den).astype(o_ref.dtype)


NA_NDC = 2 * NA_KW


def _na_bias_kernel(r_ref, e_ref, o_ref):
    o_ref[0] = jnp.dot(r_ref[0], e_ref[...], preferred_element_type=F32, precision=lax.Precision.HIGHEST)


def _na_bias_tables(rpb):
    cols = np.arange(GRID_W)
    cs = np.clip(cols - NA_KW // 2, 0, GRID_W - NA_KW)
    col_in = (cols[None, :] >= cs[:, None]) & (cols[None, :] < cs[:, None] + NA_KW)
    dc_idx = np.clip(cols[None, :] - cols[:, None] + NA_KW - 1, 0, 2 * NA_KW - 2)
    mask = np.broadcast_to(col_in[:, None, :], (GRID_W, NA_KH, GRID_W)).reshape(GRID_W, NA_WIN)
    onehot = (np.arange(NA_NDC)[:, None] == dc_idx.reshape(1, -1)).astype(np.float32)
    rp = jnp.pad(rpb, ((0, 0), (0, 0), (0, NA_NDC - rpb.shape[2])))
    rows = jnp.stack([rp[:, NA_KH - 1 - d:2 * NA_KH - 1 - d, :] for d in range(NA_KH)])
    rows = rows.reshape(NA_KH, HEADS * NA_KH, NA_NDC)
    nqk = GRID_W * GRID_W
    b = pl.pallas_call(
        _na_bias_kernel,
        grid=(NA_KH,),
        in_specs=[pl.BlockSpec((1, HEADS * NA_KH, NA_NDC), lambda d: (d, 0, 0)),
                  pl.BlockSpec((NA_NDC, nqk), lambda d: (0, 0))],
        out_specs=pl.BlockSpec((1, HEADS * NA_KH, nqk), lambda d: (d, 0, 0)),
        out_shape=jax.ShapeDtypeStruct((NA_KH, HEADS * NA_KH, nqk), F32),
        compiler_params=_params("parallel"),
        name="na_bias",
    )(rows, jnp.asarray(onehot))
    b = b.reshape(NA_KH, HEADS, NA_KH, GRID_W, GRID_W).transpose(0, 1, 3, 2, 4).reshape(NA_KH, HEADS, GRID_W, NA_WIN)
    return jnp.where(mask[None, None], b, NEG_INF)


def _na_attention(z, kc, vc, bias):
    w = ATT_W
    row0 = z.shape[0] - N_SMP

    def all_rows(n, col, base):
        return pl.BlockSpec((n, w), lambda i, r: (base // n + i, col))

    return pl.pallas_call(
        _na_attn_kernel,
        grid=(DEC_BATCH, GRID_ROWS),
        in_specs=[pl.BlockSpec((GRID_W, w), lambda i, r: (row0 // GRID_W + i * GRID_ROWS + r, 0)),
                  all_rows(DEC_SEQ, 1, row0), all_rows(DEC_SEQ, 2, row0),
                  all_rows(PAST_LEN, 0, 0), all_rows(PAST_LEN, 0, 0),
                  pl.BlockSpec((1, HEADS, GRID_W, NA_WIN), lambda i, r: (r - _na_window_start(r), 0, 0, 0))],
        out_specs=pl.BlockSpec((GRID_W, w), lambda i, r: (i * GRID_ROWS + r, 0)),
        out_shape=jax.ShapeDtypeStruct((N_SMP, w), BF16),
        compiler_params=_params("parallel", "parallel"),
        name="na_attention",
    )(z, z, z, kc, vc, bias)


CONV_CW = 384


def _conv3_kernel(x_ref, w_ref, o_ref):
    x = x_ref[...]
    t = x.shape[0]
    row = lax.broadcasted_iota(jnp.int32, x.shape, 0)
    prev = jnp.where(row == 0, 0.0, pltpu.roll(x, 1, 0))
    nxt = jnp.where(row == t - 1, 0.0, pltpu.roll(x, t - 1, 0))
    w = w_ref[...]
    o_ref[...] = prev * w[0:1] + x * w[1:2] + nxt * w[2:3]


def _conv3(z, conv_w, seq, row0, n_seq):
    col0 = (3 * D_WIDTH) // CONV_CW
    blk0 = row0 // seq
    return pl.pallas_call(
        _conv3_kernel,
        grid=(n_seq, D_IN // CONV_CW),
        in_specs=[pl.BlockSpec((seq, CONV_CW), lambda i, j: (blk0 + i, col0 + j)),
                  pl.BlockSpec((3, CONV_CW), lambda i, j: (0, j))],
        out_specs=pl.BlockSpec((seq, CONV_CW), lambda i, j: (i, j)),
        out_shape=jax.ShapeDtypeStruct((n_seq * seq, D_IN), F32),
        compiler_params=_params("parallel", "parallel"),
        name="conv3",
    )(z, conv_w)


def _dot_f32(a, b):
    return jnp.dot(a, b, preferred_element_type=F32, precision=lax.Precision.HIGHEST)


def _d_prep_kernel(x_ref, w0_ref, w2_ref, a0_ref, a2_ref, g2_ref, dec_ref, a_ref, g_ref):
    x = x_ref[...]
    xw = x[:, 0:2 * D_LORA]
    xa = x[:, 2 * D_LORA:4 * D_LORA]
    xg = x[:, 4 * D_LORA:]
    for d in range(2):
        u = w0_ref[d] + _dot_f32(jnp.tanh(xw[:, d * D_LORA:(d + 1) * D_LORA]), w2_ref[d])
        nu = -u
        softplus = jnp.maximum(nu, 0.0) + jnp.log1p(jnp.exp(-jnp.abs(nu)))
        wlog = -softplus - 0.5
        dec_ref[d] = -jnp.exp(wlog)
        av = a0_ref[d] + _dot_f32(xa[:, d * D_LORA:(d + 1) * D_LORA], a2_ref[d])
        a_ref[d] = 1.0 / (1.0 + jnp.exp(-av))
    g_ref[...] = _dot_f32(1.0 / (1.0 + jnp.exp(-xg)), g2_ref[...])


def _d_prep(zc, w0, w2, a0, a2, g2, tm=256):
    n = zc.shape[0]
    cb = (3 * D_WIDTH) // CONV_CW
    return pl.pallas_call(
        _d_prep_kernel,
        grid=(n // tm,),
        in_specs=[pl.BlockSpec((tm, CONV_CW), lambda i: (i, cb)),
                  pl.BlockSpec((2, 1, D_WIDTH), lambda i: (0, 0, 0)),
                  pl.BlockSpec((2, D_LORA, D_WIDTH), lambda i: (0, 0, 0)),
                  pl.BlockSpec((2, 1, D_WIDTH), lambda i: (0, 0, 0)),
                  pl.BlockSpec((2, D_LORA, D_WIDTH), lambda i: (0, 0, 0)),
                  pl.BlockSpec((D_GATE_LORA, D_WIDTH), lambda i: (0, 0))],
        out_specs=[pl.BlockSpec((2, tm, D_WIDTH), lambda i: (0, i, 0)),
                   pl.BlockSpec((2, tm, D_WIDTH), lambda i: (0, i, 0)),
                   pl.BlockSpec((tm, D_WIDTH), lambda i: (i, 0))],
        out_shape=[jax.ShapeDtypeStruct((2, n, D_WIDTH), F32),
                   jax.ShapeDtypeStruct((2, n, D_WIDTH), F32),
                   jax.ShapeDtypeStruct((n, D_WIDTH), F32)],
        compiler_params=_params("parallel"),
        name="d_prep",
    )(zc, w0.reshape(2, 1, D_WIDTH), w2, a0.reshape(2, 1, D_WIDTH), a2, g2)


RW_C = 64
RW_PP = 8


def _bdot(a, b):
    ah, al = _split_hi_lo(a)
    bh, bl = _split_hi_lo(b)
    dot = functools.partial(jnp.dot, preferred_element_type=F32)
    return dot(ah, bh) + dot(ah, bl) + dot(al, bh)


def _bdot_nt(a, b):
    ah, al = _split_hi_lo(a)
    bh, bl = _split_hi_lo(b)
    return _dot_nt(ah, bh) + _dot_nt(ah, bl) + _dot_nt(al, bh)


def _bdot_tn(a, b):
    return _dot_tn(a.astype(BF16), b.astype(BF16))


def _split_hi_lo(x):
    hi = x.astype(BF16)
    return hi, (x - hi.astype(F32)).astype(BF16)


def _head_ones():
    r = lax.broadcasted_iota(jnp.int32, (LANES, LANES), 0) // D_HS
    c = lax.broadcasted_iota(jnp.int32, (LANES, LANES), 1) // D_HS
    return jnp.where(r == c, 1.0, 0.0).astype(BF16)


def _rwkv_chunk_kernel(r_ref, k_ref, v_ref, lw_ref, a_ref, kkp_ref, kap_ref, s0_ref, y_ref, sfin_ref, s_ref):
    c = pl.program_id(3)
    two_c = 2 * RW_C

    @pl.when(c == 0)
    def _():
        s_ref[...] = s0_ref[...]

    sgn = jnp.where(pl.program_id(1) == 1, -1, 1)
    row = lax.broadcasted_iota(jnp.int32, (two_c, two_c), 0)
    col = lax.broadcasted_iota(jnp.int32, (two_c, two_c), 1)
    same = (row // RW_C) == (col // RW_C)
    tt, ss = row % RW_C, col % RW_C
    before = (ss - tt) * sgn < 0
    strict = same & before
    incl = same & (before | (ss == tt))
    eye = jnp.where(row == col, 1.0, 0.0)
    r64 = lax.broadcasted_iota(jnp.int32, (RW_C, RW_C), 0)
    c64 = lax.broadcasted_iota(jnp.int32, (RW_C, RW_C), 1)
    ltri = jnp.where((c64 - r64) * sgn <= 0, 1.0, 0.0)
    head0 = lax.broadcasted_iota(jnp.int32, (RW_C, LANES), 1) < D_HS
    ones_blk = _head_ones().astype(F32)

    def stack(x):
        return jnp.concatenate([jnp.where(head0, x, 0.0), jnp.where(head0, 0.0, x)], axis=0)

    pairs = range(RW_PP)
    lss = [slice(p * LANES, (p + 1) * LANES) for p in pairs]
    r = [r_ref[:, ls] for ls in lss]
    k = [k_ref[:, ls] for ls in lss]
    lw = [lw_ref[:, ls] for ls in lss]
    a = [a_ref[:, ls] for ls in lss]
    kkf = [k[p] * kkp_ref[:, lss[p]] for p in pairs]
    n2 = [_dot_f32(kkf[p] * kkf[p], ones_blk) for p in pairs]
    kk = [kkf[p] / jnp.maximum(jnp.sqrt(n2[p]), 1e-12) for p in pairs]
    cum = [_dot_f32(ltri, lw[p]) for p in pairs]
    mid = [cum[p][RW_C // 2:RW_C // 2 + 1] for p in pairs]
    cumc = [cum[p] - mid[p] for p in pairs]
    pmid = [jnp.exp(mid[p]) for p in pairs]
    pend = [jnp.exp(jnp.sum(lw[p], axis=0, keepdims=True) - mid[p]) for p in pairs]
    pinv = [jnp.exp(-cumc[p]) for p in pairs]
    ks = [stack(kk[p] * jnp.exp(cumc[p] - lw[p])) for p in pairs]
    rs = [stack(r[p] * jnp.exp(cumc[p])) for p in pairs]
    khs = [stack(k[p] * (1.0 + (a[p] - 1.0) * kap_ref[:, lss[p]]) * pinv[p]) for p in pairs]
    bhs = [stack(kk[p] * a[p] * pinv[p]) for p in pairs]
    vs = [stack(v_ref[:, ls]) for ls in lss]
    big = [_bdot_nt(jnp.concatenate([ks[p], rs[p]], axis=0), jnp.concatenate([bhs[p], khs[p]], axis=0)) for p in pairs]
    a_b = [jnp.where(strict, big[p][:two_c, :two_c], 0.0) for p in pairs]
    a_k = [jnp.where(strict, big[p][:two_c, two_c:], 0.0) for p in pairs]
    l_b = [jnp.where(incl, big[p][two_c:, :two_c], 0.0) for p in pairs]
    l_k = [jnp.where(incl, big[p][two_c:, two_c:], 0.0) for p in pairs]
    npow = [-a_b[p] for p in pairs]
    tinv = [eye + npow[p] for p in pairs]
    for _ in range(5):
        npow = [_bdot(npow[p], npow[p]) for p in pairs]
        tinv = [tinv[p] + _bdot(tinv[p], npow[p]) for p in pairs]
    w1 = [_bdot(a_k[p], vs[p]) for p in pairs]
    tku = [_bdot(tinv[p], jnp.concatenate([ks[p], w1[p]], axis=1)) for p in pairs]
    lku = [_bdot(l_b[p], tku[p]) for p in pairs]
    rtil = [rs[p] - lku[p][:, :LANES] for p in pairs]
    yv = [_bdot(l_k[p], vs[p]) - lku[p][:, LANES:] for p in pairs]
    gt = [_bdot(tku[p][:, :LANES].T, bhs[p]) for p in pairs]
    ht = [_bdot(vs[p].T, khs[p]) - _bdot(tku[p][:, LANES:].T, bhs[p]) for p in pairs]
    s = [s_ref[p] * pmid[p] for p in pairs]
    y = [_bdot_nt(rtil[p], s[p]) + yv[p] for p in pairs]
    sg = [_bdot(s[p], gt[p]) for p in pairs]
    for p in pairs:
        s_ref[p] = (s[p] - sg[p] + ht[p]) * pend[p]
        y_ref[:, lss[p]] = y[p][:RW_C] + y[p][RW_C:]

    @pl.when(c == pl.num_programs(3) - 1)
    def _():
        sfin_ref[...] = s_ref[...]


def _pair_states(s):
    zero = jnp.zeros_like(s[:, :, 0::2])
    top = jnp.concatenate([s[:, :, 0::2], zero], axis=-1)
    bot = jnp.concatenate([zero, s[:, :, 1::2]], axis=-1)
    return jnp.concatenate([top, bot], axis=-2)


def _head_states(sp):
    b = sp.shape[0]
    both = jnp.stack([sp[:, :, :, :D_HS, :D_HS], sp[:, :, :, D_HS:, D_HS:]], axis=3)
    return both.reshape(b, 2, D_HEADS, D_HS, D_HS)


def _rwkv_chunked(zc, lw, a, k_k, k_a, s0, b, t):
    nc = t // RW_C
    gw = RW_PP * LANES
    npg = D_WIDTH // gw

    def cmap(d, c):
        return jnp.where(d == 0, c, nc - 1 - c)

    def z_spec(colblk):
        return pl.BlockSpec((RW_C, gw), lambda i, d, g, c: (i * nc + cmap(d, c), colblk * npg + g))

    dir_spec = pl.BlockSpec((None, RW_C, gw), lambda i, d, g, c: (d, i * nc + cmap(d, c), g))
    par_spec = pl.BlockSpec((1, gw), lambda i, d, g, c: (0, g))
    st_spec = pl.BlockSpec((None, None, RW_PP, LANES, LANES), lambda i, d, g, c: (i, d, g, 0, 0))
    return pl.pallas_call(
        _rwkv_chunk_kernel,
        grid=(b, 2, npg, nc),
        in_specs=[z_spec(0), z_spec(1), z_spec(2), dir_spec, dir_spec, par_spec, par_spec, st_spec],
        out_specs=[dir_spec, st_spec],
        out_shape=[jax.ShapeDtypeStruct((2, b * t, D_WIDTH), F32),
                   jax.ShapeDtypeStruct((b, 2, D_WIDTH // LANES, LANES, LANES), F32)],
        scratch_shapes=[pltpu.VMEM((RW_PP, LANES, LANES), F32)],
        compiler_params=_params("parallel", "parallel", "parallel", "arbitrary"),
        name="rwkv_chunk",
    )(zc, zc, zc, lw, a, k_k.reshape(1, D_WIDTH), k_a.reshape(1, D_WIDTH), s0)


def _d_out_kernel(y_ref, r_ref, k_ref, v_ref, a_ref, g_ref, kap_ref, rkp_ref, lng_ref, lnb_ref, o_ref):
    ones_blk = _head_ones()

    def seg_sum(x):
        hi, lo = _split_hi_lo(x)
        return jnp.dot(hi, ones_blk, preferred_element_type=F32) + jnp.dot(lo, ones_blk, preferred_element_type=F32)

    for cb in range(D_WIDTH // LANES):
        ls = slice(cb * LANES, (cb + 1) * LANES)
        y = y_ref[0, :, ls] + y_ref[1, :, ls]
        mu = seg_sum(y) * (1.0 / D_HS)
        yc = y - mu
        var = seg_sum(yc * yc) * (1.0 / D_HS)
        yn = yc * lax.rsqrt(var + D_LN_EPS) * lng_ref[:, ls] + lnb_ref[:, ls]
        asum = a_ref[0, :, ls] + a_ref[1, :, ls]
        bonus = seg_sum(r_ref[:, ls] * k_ref[:, ls] * rkp_ref[:, ls] * (2.0 + (asum - 2.0) * kap_ref[:, ls]))
        o_ref[:, ls] = (yn + bonus * v_ref[:, ls]) * g_ref[:, ls]


def _d_out(y, zc, a, g, k_a, r_k, ln_g, ln_b, tm=256):
    n = g.shape[0]

    def z_spec(colblk):
        return pl.BlockSpec((tm, D_WIDTH), lambda i: (i, colblk))

    dspec = pl.BlockSpec((2, tm, D_WIDTH), lambda i: (0, i, 0))
    pspec = pl.BlockSpec((1, D_WIDTH), lambda i: (0, 0))
    return pl.pallas_call(
        _d_out_kernel,
        grid=(n // tm,),
        in_specs=[dspec, z_spec(0), z_spec(1), z_spec(2), dspec, pl.BlockSpec((tm, D_WIDTH), lambda i: (i, 0)),
                  pspec, pspec, pspec, pspec],
        out_specs=pl.BlockSpec((tm, D_WIDTH), lambda i: (i, 0)),
        out_shape=jax.ShapeDtypeStruct((n, D_WIDTH), F32),
        compiler_params=_params("parallel"),
        name="d_out",
    )(y, zc, zc, zc, a, g, k_a.reshape(1, D_WIDTH), r_k.reshape(1, D_WIDTH), ln_g.reshape(1, D_WIDTH),
      ln_b.reshape(1, D_WIDTH))


TOPK_TM = 256
BIG = 1e9


def _extract_topk(s, labels):
    iota16 = lax.broadcasted_iota(jnp.int32, (PEER_TOPK, s.shape[1]), 0)
    rank = jnp.full(s.shape, BIG, F32)
    vals = jnp.zeros((PEER_TOPK, s.shape[1]), F32)
    for r in range(PEER_TOPK):
        m = jnp.max(s, axis=0, keepdims=True)
        idx = jnp.min(jnp.where(s == m, labels, BIG), axis=0, keepdims=True)
        hit = labels == idx
        rank = jnp.where(hit, float(r), rank)
        vals = jnp.where(iota16 == r, m, vals)
        s = jnp.where(hit, -jnp.inf, s)
    return rank, vals


_CAND_PAIRS = ([(0, b) for b in range(16)] + [(1, b) for b in range(8)] + [(2, b) for b in range(8)]
               + [(3, b) for b in range(8)] + [(4, b) for b in range(4)] + [(5, b) for b in range(4)]
               + [(6, b) for b in range(4)] + [(7, b) for b in range(4)] + [(a, 0) for a in range(8, 16)])
N_CAND = len(_CAND_PAIRS)


def _cand_labels():
    lab = np.array([a * PEER_TOPK + b for a, b in _CAND_PAIRS], np.float32)
    return jnp.asarray(np.broadcast_to(lab[:, None], (N_CAND, LANES)).copy())


def _peer_topk_kernel(q_ref, keys_ref, lab_ref, lim_ref, e1_ref, rb_ref, e2_ref):
    iota_k = lax.broadcasted_iota(jnp.int32, (PEER_NKEYS, LANES), 0).astype(F32)
    row8 = lax.broadcasted_iota(jnp.int32, (8, LANES), 0)
    labels = lab_ref[...]
    for c in range(TOPK_TM // LANES):
        cs = slice(c * LANES, (c + 1) * LANES)
        q1 = q_ref[cs, 0:PEER_HALF].astype(BF16)
        q2 = q_ref[cs, PEER_HALF:2 * PEER_HALF].astype(BF16)
        s1 = _dot_nt(keys_ref[0, 0], q1)
        s2 = _dot_nt(keys_ref[0, 1], q2)
        rank1, sv1 = _extract_topk(s1, iota_k)
        rank2, sv2 = _extract_topk(s2, iota_k)
        lo8 = sv2[0:8]
        lo4 = jnp.where(row8 < 4, lo8, pltpu.roll(lo8, 4, 0))
        cand = jnp.concatenate([
            sv1[0:1] + lo8, sv1[0:1] + sv2[8:16], sv1[1:2] + lo8, sv1[2:3] + lo8, sv1[3:4] + lo8,
            jnp.where(row8 < 4, sv1[4:5], sv1[5:6]) + lo4, jnp.where(row8 < 4, sv1[6:7], sv1[7:8]) + lo4,
            sv1[8:16] + sv2[0:1]], axis=0)
        crank, cvals = _extract_topk(cand, labels)
        z = jnp.sum(jnp.exp(cvals - cvals[0:1]), axis=0, keepdims=True)
        sel = jnp.where(crank < BIG, 1.0, 0.0)

        def count(lo, hi):
            return jnp.sum(sel[lo:hi], axis=0, keepdims=True)

        def count_half(lo, first):
            part = jnp.where((row8 < 4) if first else (row8 >= 4), sel[lo:lo + 8], 0.0)
            return jnp.sum(part, axis=0, keepdims=True)

        n_sel = [count(0, 16), count(16, 24), count(24, 32), count(32, 40),
                 count_half(40, True), count_half(40, False), count_half(48, True), count_half(48, False)]
        n_sel += [sel[56 + a:57 + a] for a in range(8)]
        lim = jnp.zeros_like(s1)
        for a in range(PEER_TOPK):
            lim = jnp.where(rank1 == float(a), n_sel[a], lim)
        lim_ref[0, :, cs] = lim
        e1_ref[0, :, cs] = jnp.exp(s1 - sv1[0:1])
        rb_ref[0, :, cs] = pltpu.bitcast(rank2.astype(BF16), jnp.uint32)
        e2_ref[0, :, cs] = pltpu.bitcast((jnp.exp(s2 - sv2[0:1]) / z).astype(BF16), jnp.uint32)


def _peer_topk(q, keys):
    n = q.shape[0]
    ospec = pl.BlockSpec((1, PEER_NKEYS, TOPK_TM), lambda i, h: (h, 0, i))
    pspec = pl.BlockSpec((1, PEER_NKEYS // 2, TOPK_TM), lambda i, h: (h, 0, i))
    return pl.pallas_call(
        _peer_topk_kernel,
        grid=(n // TOPK_TM, PEER_HEADS),
        in_specs=[pl.BlockSpec((TOPK_TM, 2 * PEER_HALF), lambda i, h: (i, h)),
                  pl.BlockSpec((1, 2, PEER_NKEYS, PEER_HALF), lambda i, h: (h, 0, 0, 0)),
                  pl.BlockSpec((N_CAND, LANES), lambda i, h: (0, 0))],
        out_specs=[ospec, ospec, pspec, pspec],
        out_shape=[jax.ShapeDtypeStruct((PEER_HEADS, PEER_NKEYS, n), F32)] * 2
        + [jax.ShapeDtypeStruct((PEER_HEADS, PEER_NKEYS // 2, n), jnp.uint32)] * 2,
        compiler_params=_params("parallel", "parallel"),
        name="peer_topk",
    )(q, keys, _cand_labels())


PEER_TM = 512
PEER_TI = 8
PEER_RG = 4
PEER_JT = 16


def _gelu(x):
    return 0.5 * x * (1.0 + lax.erf(x * (2.0 ** -0.5)))


def _peer_dense_kernel(x_ref, u_ref, v_ref, lim_ref, e1_ref, rb_ref, e2_ref, res_ref, gate_ref, o_ref, w_ref):
    e = pl.program_id(1)

    @pl.when(e == 0)
    def _():
        o_ref[...] = jnp.zeros_like(o_ref)

    hid = _dot_nt(u_ref[...], x_ref[...])
    zero = jnp.zeros((), BF16)
    n_jt = PEER_NKEYS // PEER_JT
    for c in range(PEER_TM // LANES):
        cs = slice(c * LANES, (c + 1) * LANES)
        for rg in range(PEER_TI // PEER_RG):
            rows = range(rg * PEER_RG, (rg + 1) * PEER_RG)
            g = [[jnp.zeros((PEER_JT, LANES), BF16) for _ in range(n_jt)] for _ in rows]
            for h in range(PEER_HEADS):
                lim = [jnp.broadcast_to(lim_ref[h, ii:ii + 1, cs], (PEER_JT, LANES)).astype(BF16) for ii in rows]
                e1 = [jnp.broadcast_to(e1_ref[h, ii:ii + 1, cs], (PEER_JT, LANES)).astype(BF16) for ii in rows]
                for jt in range(n_jt):
                    ws = slice(jt * PEER_JT // 2, (jt + 1) * PEER_JT // 2)
                    rb = pltpu.bitcast(rb_ref[h, ws, cs], BF16)
                    e2 = pltpu.bitcast(e2_ref[h, ws, cs], BF16)
                    for k in range(PEER_RG):
                        g[k][jt] = g[k][jt] + jnp.where(rb < lim[k], e2, zero) * e1[k]
            for k, ii in enumerate(rows):
                for jt in range(n_jt):
                    e0 = ii * PEER_NKEYS + jt * PEER_JT
                    w = g[k][jt] * _gelu(hid[e0:e0 + PEER_JT, cs].astype(BF16))
                    w_ref[e0 // 2:(e0 + PEER_JT) // 2, cs] = pltpu.bitcast(w, jnp.uint32)
    o_ref[...] += _dot_tn(pltpu.bitcast(w_ref[...], BF16), v_ref[...])

    @pl.when(e == pl.num_programs(1) - 1)
    def _():
        o_ref[...] = res_ref[...] + gate_ref[0] * o_ref[...]


def _peer_dense(xm, u, v, layer, lim, e1, rb, e2, res, gate):
    n, d = xm.shape
    n_exp = u.shape[1]
    te = PEER_TI * PEER_NKEYS
    sel_spec = pl.BlockSpec((PEER_HEADS, PEER_NKEYS // 2, PEER_TM), lambda i, e: (0, 0, i))
    row_spec = pl.BlockSpec((PEER_HEADS, PEER_TI, PEER_TM), lambda i, e: (0, e, i))
    return pl.pallas_call(
        _peer_dense_kernel,
        grid=(n // PEER_TM, n_exp // te),
        in_specs=[pl.BlockSpec((PEER_TM, d), lambda i, e: (i, 0)),
                  pl.BlockSpec((None, te, d), lambda i, e: (layer, e, 0)),
                  pl.BlockSpec((None, te, d), lambda i, e: (layer, e, 0)),
                  row_spec, row_spec, sel_spec, sel_spec,
                  pl.BlockSpec((PEER_TM, d), lambda i, e: (i, 0)),
                  pl.BlockSpec((1, 1, d), lambda i, e: (_cond_of_tile(i, PEER_TM), 0, 0))],
        out_specs=pl.BlockSpec((PEER_TM, d), lambda i, e: (i, 0)),
        out_shape=jax.ShapeDtypeStruct((n, d), F32),
        scratch_shapes=[pltpu.VMEM((te // 2, PEER_TM), jnp.uint32)],
        compiler_params=_params("parallel", "arbitrary"),
        name="peer_dense",
    )(xm, u, v, lim, e1, rb, e2, res, gate)


def _peer_layer(x, layer, norm_g, shift, scale, gate, wq, keys, u, v):
    xm = _modulate(x, norm_g, shift, scale)
    q = _matmul(xm, wq, layer)
    lim, e1, rb, e2 = _peer_topk(q, keys)
    return _peer_dense(xm, u, v, layer, lim, e1, rb, e2, x, gate)


EVEN_PAD = 4096
ODD_PAD = 6656
S_ALL = PAST_LEN + DEC_SEQ


def _even_mixer(h, w_in, layer, lam, subln_g, q_norm_g, w_uq, kv_norm_g, w_ukv, lam_init,
                cache_k, cache_v, cache_ckv, cache_kpe, rope_cos, rope_sin):
    z = _matmul(h, w_in, layer)
    ckv_n = _rmsnorm(z, kv_norm_g, col=3584 // B_KV_RANK)
    cq_n = _rmsnorm(z, q_norm_g, out_dtype=BF16, col=3072 // B_Q_RANK)
    wq3 = w_uq.reshape(B_Q_RANK, HEADS, B_NOPE + B_ROPE)
    w_uq_r = jnp.concatenate([wq3[:, :, :B_NOPE].reshape(B_Q_RANK, -1), wq3[:, :, B_NOPE:].reshape(B_Q_RANK, -1)], axis=1)
    qb = _matmul(cq_n, w_uq_r.astype(BF16), tn=512)
    new = (z[:N_CTX, 1024:2048].reshape(BATCH, SEQ, HEADS, 2 * A_QK),
           z[:N_CTX, 2048:3072].reshape(BATCH, SEQ, HEADS, HEAD_DIM),
           ckv_n[:N_CTX].reshape(BATCH, SEQ, B_KV_RANK), z[:N_CTX, 3840:3904].reshape(BATCH, SEQ, B_ROPE))

    qa_s = _rope(z, rope_cos, rope_sin, ATT_W, col=0)
    ka_s = _rope(z, rope_cos, rope_sin, ATT_W, col=1)
    qr_s = _rope(qb, rope_cos, rope_sin, HEADS * B_ROPE, col=2)
    kpe_s = _rope(z, rope_cos, rope_sin, LANES, col=3840 // LANES)[:, :B_ROPE]

    def with_ctx(cache, own, width):
        both = jnp.concatenate([cache.reshape(DEC_BATCH, PAST_LEN, width).astype(own.dtype),
                                own.reshape(DEC_BATCH, DEC_SEQ, width)], axis=1)
        return both.reshape(DEC_BATCH * S_ALL, width)

    ka_all = with_ctx(cache_k, ka_s, ATT_W)
    va_all = with_ctx(cache_v, z[N_CTX:, 2048:3072].astype(BF16), ATT_W)
    ckv_all = with_ctx(cache_ckv, ckv_n[N_CTX:], B_KV_RANK)
    kpe_all = with_ctx(cache_kpe, kpe_s, B_ROPE)

    wkv3 = w_ukv.reshape(B_KV_RANK, HEADS, B_NOPE + HEAD_DIM)
    w_ukv_r = jnp.concatenate([wkv3[:, :, :B_NOPE].reshape(B_KV_RANK, -1), wkv3[:, :, B_NOPE:].reshape(B_KV_RANK, -1)], axis=1)
    n_skv = DEC_BATCH * S_ALL
    kv = _matmul(jnp.concatenate([ckv_all, ckv_n[:N_CTX]], axis=0), w_ukv_r.astype(BF16), out_dtype=BF16)

    oa_c = _diff_attention(BATCH, SEQ, (z, _q_rows(SEQ, ATT_W, 0, 0)), (z, _kv_rows(SEQ, ATT_W, 0, 1)),
                           (z, _kv_rows(SEQ, ATT_W, 0, 2)), lam, subln_g, lam_init)
    oa_s = _diff_attention(DEC_BATCH, DEC_SEQ, (qa_s, _q_rows(DEC_SEQ, ATT_W)), (ka_all, _kv_rows(S_ALL, ATT_W)),
                           (va_all, _kv_rows(S_ALL, ATT_W)), lam, subln_g, lam_init)
    ob_c = _mla_attention(BATCH, SEQ, (qb, _q_rows(SEQ, ATT_W, 0, 0)), (qb, _q_rows(SEQ, HEADS * B_ROPE, 0, 2)),
                          (kv, _kv_rows(SEQ, ATT_W, n_skv, 0)), (z, _kv_rows(SEQ, LANES, 0, 3840 // LANES)),
                          (kv, _kv_rows(SEQ, ATT_W, n_skv, 1)))
    ob_s = _mla_attention(DEC_BATCH, DEC_SEQ, (qb, _q_rows(DEC_SEQ, ATT_W, N_CTX, 0)),
                          (qr_s, _q_rows(DEC_SEQ, HEADS * B_ROPE)), (kv, _kv_rows(S_ALL, ATT_W, 0, 0)),
                          (kpe_all, _kv_rows(S_ALL, B_ROPE)), (kv, _kv_rows(S_ALL, ATT_W, 0, 1)))
    return (oa_c, oa_s, ob_c, ob_s), new


def _odd_mixer(h, w_in, layer, rpb, conv_w, w0, w2, a0, a2, g2, k_k, k_a, r_k, ln_g, ln_b,
               cache_k, cache_v, state):
    z = _matmul(h, w_in, layer)
    new_k = z[:N_CTX, 1024:2048].reshape(BATCH, SEQ, HEADS, HEAD_DIM)
    new_v = z[:N_CTX, 2048:3072].reshape(BATCH, SEQ, HEADS, HEAD_DIM)
    oc_c = _plain_attention(BATCH, SEQ, (z, _q_rows(SEQ, ATT_W, 0, 0)), (z, _kv_rows(SEQ, ATT_W, 0, 1)),
                            (z, _kv_rows(SEQ, ATT_W, 0, 2)))
    oc_s = _na_attention(z, cache_k.reshape(DEC_BATCH * PAST_LEN, ATT_W), cache_v.reshape(DEC_BATCH * PAST_LEN, ATT_W),
                         _na_bias_tables(rpb))

    zc_c = _conv3(z, conv_w, SEQ, 0, BATCH)
    zc_s = _conv3(z, conv_w, DEC_SEQ, N_CTX, DEC_BATCH)
    lw_c, a_c, g_c = _d_prep(zc_c, w0, w2, a0, a2, g2)
    lw_s, a_s, g_s = _d_prep(zc_s, w0, w2, a0, a2, g2)
    rk = r_k.reshape(-1)
    y_c, sfin = _rwkv_chunked(zc_c, lw_c, a_c, k_k, k_a,
                              jnp.zeros((BATCH, 2, D_WIDTH // LANES, LANES, LANES), F32), BATCH, SEQ)
    sfin = _head_states(sfin)
    y_s, _ = _rwkv_chunked(zc_s, lw_s, a_s, k_k, k_a, _pair_states(state), DEC_BATCH, DEC_SEQ)
    od_c = _d_out(y_c, zc_c, a_c, g_c, k_a, rk, ln_g, ln_b)
    od_s = _d_out(y_s, zc_s, a_s, g_s, k_a, rk, ln_g, ln_b)
    return (oc_c, oc_s, od_c, od_s), (new_k, new_v, sfin)


def kernel(x_prompt, x_sample, cache_a_k, cache_a_v, cache_b_ckv, cache_b_kpe, cache_c_k, cache_c_v, state_d, c, c_ctx, ada_w, ada_b, norm1_g, norm2_g, w_out, peer_wq, peer_keys, peer_u, peer_v, final_g, ab_w_in, a_lam, a_subln_g, b_q_norm_g, b_w_uq, b_kv_norm_g, b_w_ukv, cd_w_in, c_rpb, d_conv, d_w0, d_w2, d_a0, d_a2, d_g2, d_k_k, d_k_a, d_r_k, d_ln_g, d_ln_b):
    x = jnp.concatenate([x_prompt.reshape(N_CTX, D_MODEL), x_sample.reshape(N_SMP, D_MODEL)], axis=0)
    cond8 = jnp.pad(jnp.concatenate([c_ctx[None, :], c], axis=0), ((0, 8 - N_COND), (0, 0)))
    ada = _ada_all(cond8, ada_w, ada_b)
    rope_cos, rope_sin = _rope_tables()
    w_even = jnp.pad(ab_w_in, ((0, 0), (0, 0), (0, EVEN_PAD - EVEN_IN))).astype(BF16)
    w_odd = jnp.pad(cd_w_in, ((0, 0), (0, 0), (0, ODD_PAD - ODD_IN))).astype(BF16)
    w_out_b, wq_b, keys_b = w_out.astype(BF16), peer_wq.astype(BF16), peer_keys.astype(BF16)
    u_b, v_b = peer_u.astype(BF16), peer_v.astype(BF16)
    new_ak, new_av, new_bc, new_bp, new_ck, new_cv, new_sd = [], [], [], [], [], [], []
    for l in range(DEPTH):
        i = l // 2
        mods = ada[l, :N_COND].reshape(N_COND, 6, 1, D_MODEL)
        sh1, sc1, g1, sh2, sc2, g2 = (mods[:, m] for m in range(6))
        h = _modulate(x, norm1_g[l], sh1, sc1)
        if l % 2 == 0:
            lam_init = 0.8 - 0.6 * math.exp(-0.3 * l)
            o, (ak, av, bc, bp) = _even_mixer(
                h, w_even, i, a_lam[i], a_subln_g[i], b_q_norm_g[i], b_w_uq[i], b_kv_norm_g[i], b_w_ukv[i],
                lam_init, cache_a_k[:, i], cache_a_v[:, i], cache_b_ckv[:, i], cache_b_kpe[:, i], rope_cos, rope_sin)
            new_ak.append(ak)
            new_av.append(av)
            new_bc.append(bc)
            new_bp.append(bp)
        else:
            o, (ck, cv, sd) = _odd_mixer(
                h, w_odd, i, c_rpb[i], d_conv[i], d_w0[i], d_w2[i], d_a0[i], d_a2[i], d_g2[i],
                d_k_k[i], d_k_a[i], d_r_k[i], d_ln_g[i], d_ln_b[i], cache_c_k[:, i], cache_c_v[:, i], state_d[:, i])
            new_ck.append(ck)
            new_cv.append(cv)
            new_sd.append(sd)
        x = _matmul_residual(*o, w_out_b, l, x, g1)
        x = _peer_layer(x, l, norm2_g[l], sh2, sc2, g2, wq_b, keys_b[l], u_b, v_b)
    y = _rmsnorm(x, final_g)
    y_prompt = y[:N_CTX].reshape(BATCH, SEQ, D_MODEL)
    y_sample = y[N_CTX:].reshape(DEC_BATCH, DEC_SEQ, D_MODEL)
    return (y_prompt, y_sample, jnp.stack(new_ak, axis=1), jnp.stack(new_av, axis=1), jnp.stack(new_bc, axis=1),
            jnp.stack(new_bp, axis=1), jnp.stack(new_ck, axis=1), jnp.stack(new_cv, axis=1), jnp.stack(new_sd, axis=1))
```

```python
import functools
import math

import numpy as np
import jax
import jax.numpy as jnp
from jax import lax
from jax.experimental import pallas as pl
from jax.experimental.pallas import tpu as pltpu

F32 = jnp.float32
BF16 = jnp.bfloat16

D_MODEL = 2048
BATCH = 16
SEQ = 256
DEPTH = 4
DEC_BATCH = 2
DEC_SEQ = 1024
PAST_LEN = 512
GRID_W = 64
GRID_ROWS = DEC_SEQ // GRID_W
ROPE_BASE = 10000.0
RMS_EPS = 1e-6
NEG_INF = -1e30
N_CTX = BATCH * SEQ
N_SMP = DEC_BATCH * DEC_SEQ
N_TOK = N_CTX + N_SMP
N_COND = 1 + DEC_BATCH

HEADS = 8
HEAD_DIM = 128
A_QK = 64
B_Q_RANK = 512
B_KV_RANK = 256
B_NOPE = 128
B_ROPE = 64
NA_KH = 8
NA_KW = 16
NA_WIN = NA_KH * GRID_W
D_HS = 64
D_WIDTH = 1024
D_HEADS = 16
D_LORA = 64
D_GATE_LORA = 128
D_IN = 3 * D_WIDTH + 2 * D_LORA + 2 * D_LORA + D_GATE_LORA
D_LN_EPS = 64e-5
EVEN_IN = 3904
ODD_IN = 6528
PEER_HEADS = 8
PEER_NKEYS = 128
PEER_HALF = 128
PEER_TOPK = 16

LANES = 128
VMEM_LIMIT = 56 * 1024 * 1024


def _params(*sem):
    return pltpu.CompilerParams(dimension_semantics=sem, vmem_limit_bytes=VMEM_LIMIT)


def _dot_nt(a, b):
    return lax.dot_general(a, b, (((1,), (1,)), ((), ())), preferred_element_type=F32)


def _dot_tn(a, b):
    return lax.dot_general(a, b, (((0,), (0,)), ((), ())), preferred_element_type=F32)


def _cond_of_tile(i, tm):
    n_ctx_tiles = N_CTX // tm
    per_batch = DEC_SEQ // tm
    return jnp.where(i < n_ctx_tiles, 0, 1 + (i - n_ctx_tiles) // per_batch)


def _ada_kernel(c_ref, w_ref, b_ref, o_ref):
    c = c_ref[...]
    sc = c * (1.0 / (1.0 + jnp.exp(-c)))
    o_ref[0] = jnp.dot(sc.astype(BF16), w_ref[0].astype(BF16), preferred_element_type=F32) + b_ref[0]


def _ada_all(cond8, ada_w, ada_b):
    tn = 1024
    n_out = ada_w.shape[-1]
    return pl.pallas_call(
        _ada_kernel,
        grid=(DEPTH, n_out // tn),
        in_specs=[pl.BlockSpec((8, D_MODEL), lambda l, j: (0, 0)),
                  pl.BlockSpec((1, D_MODEL, tn), lambda l, j: (l, 0, j)),
                  pl.BlockSpec((1, 1, tn), lambda l, j: (l, 0, j))],
        out_specs=pl.BlockSpec((1, 8, tn), lambda l, j: (l, 0, j)),
        out_shape=jax.ShapeDtypeStruct((DEPTH, 8, n_out), F32),
        compiler_params=_params("parallel", "parallel"),
        name="ada",
    )(cond8, ada_w, ada_b.reshape(DEPTH, 1, n_out))


def _modulate_kernel(x_ref, g_ref, sh_ref, sc_ref, o_ref):
    x = x_ref[...]
    y = x * lax.rsqrt(jnp.mean(x * x, axis=-1, keepdims=True) + RMS_EPS) * g_ref[...]
    o_ref[...] = (y * (1.0 + sc_ref[0]) + sh_ref[0]).astype(o_ref.dtype)


def _modulate(x, g, shift, scale, out_dtype=BF16, tm=512):
    n, d = x.shape
    return pl.pallas_call(
        _modulate_kernel,
        grid=(n // tm,),
        in_specs=[pl.BlockSpec((tm, d), lambda i: (i, 0)),
                  pl.BlockSpec((1, d), lambda i: (0, 0)),
                  pl.BlockSpec((1, 1, d), lambda i: (_cond_of_tile(i, tm), 0, 0)),
                  pl.BlockSpec((1, 1, d), lambda i: (_cond_of_tile(i, tm), 0, 0))],
        out_specs=pl.BlockSpec((tm, d), lambda i: (i, 0)),
        out_shape=jax.ShapeDtypeStruct((n, d), out_dtype),
        compiler_params=_params("parallel"),
        name="modulate",
    )(x, g.reshape(1, d), shift, scale)


def _rmsnorm_kernel(x_ref, g_ref, o_ref):
    x = x_ref[...]
    y = x * lax.rsqrt(jnp.mean(x * x, axis=-1, keepdims=True) + RMS_EPS) * g_ref[...]
    o_ref[...] = y.astype(o_ref.dtype)


def _rmsnorm(x, g, out_dtype=F32, tm=512, col=0):
    n = x.shape[0]
    d = g.shape[-1]
    return pl.pallas_call(
        _rmsnorm_kernel,
        grid=(n // tm,),
        in_specs=[pl.BlockSpec((tm, d), lambda i: (i, col)),
                  pl.BlockSpec((1, d), lambda i: (0, 0))],
        out_specs=pl.BlockSpec((tm, d), lambda i: (i, 0)),
        out_shape=jax.ShapeDtypeStruct((n, d), out_dtype),
        compiler_params=_params("parallel"),
        name="rmsnorm",
    )(x, g.reshape(1, d))


def _mm_kernel(x_ref, w_ref, o_ref):
    o_ref[...] = jnp.dot(x_ref[...].astype(BF16), w_ref[...], preferred_element_type=F32).astype(o_ref.dtype)


def _weight_spec(w, layer, tn):
    if w.ndim == 2:
        return pl.BlockSpec((w.shape[0], tn), lambda i, j: (0, j))
    return pl.BlockSpec((None, w.shape[1], tn), lambda i, j: (layer, 0, j))


MM_TM = 1024
MM_TN = 1024


def _matmul(x, w, layer=None, out_dtype=F32, tm=MM_TM, tn=None, x_col=0, x_width=None):
    m = x.shape[0]
    k = x_width or x.shape[1]
    n = w.shape[-1]
    if tn is None:
        tn = MM_TN if n % MM_TN == 0 else MM_TN // 2
    tn = min(tn, n)
    assert m % tm == 0 and n % tn == 0, (m, n, tm, tn)
    return pl.pallas_call(
        _mm_kernel,
        grid=(m // tm, n // tn),
        in_specs=[pl.BlockSpec((tm, k), lambda i, j: (i, x_col)),
                  _weight_spec(w, layer, tn)],
        out_specs=pl.BlockSpec((tm, tn), lambda i, j: (i, j)),
        out_shape=jax.ShapeDtypeStruct((m, n), out_dtype),
        compiler_params=_params("parallel", "parallel"),
        name="matmul",
    )(x, w)


def _mm_res_kernel(lc_ref, ls_ref, rc_ref, rs_ref, w_ref, res_ref, gate_ref, o_ref, *, n_ctx_tiles):
    half = w_ref.shape[0] // 2

    def emit(l_ref, r_ref):
        acc = jnp.dot(l_ref[...].astype(BF16), w_ref[0:half, :], preferred_element_type=F32)
        acc = acc + jnp.dot(r_ref[...].astype(BF16), w_ref[half:, :], preferred_element_type=F32)
        o_ref[...] = res_ref[...] + gate_ref[0] * acc

    @pl.when(pl.program_id(0) < n_ctx_tiles)
    def _():
        emit(lc_ref, rc_ref)

    @pl.when(pl.program_id(0) >= n_ctx_tiles)
    def _():
        emit(ls_ref, rs_ref)


def _matmul_residual(left_c, left_s, right_c, right_s, w, layer, res, gate, tm=MM_TM, tn=MM_TN // 2):
    m = res.shape[0]
    n = w.shape[-1]
    kh = left_c.shape[1]
    nct = left_c.shape[0] // tm
    ctx_spec = pl.BlockSpec((tm, kh), lambda i, j: (jnp.minimum(i, nct - 1), 0))
    smp_spec = pl.BlockSpec((tm, kh), lambda i, j: (jnp.maximum(i - nct, 0), 0))
    return pl.pallas_call(
        functools.partial(_mm_res_kernel, n_ctx_tiles=nct),
        grid=(m // tm, n // tn),
        in_specs=[ctx_spec, smp_spec, ctx_spec, smp_spec,
                  _weight_spec(w, layer, tn),
                  pl.BlockSpec((tm, tn), lambda i, j: (i, j)),
                  pl.BlockSpec((1, 1, tn), lambda i, j: (_cond_of_tile(i, tm), 0, j))],
        out_specs=pl.BlockSpec((tm, tn), lambda i, j: (i, j)),
        out_shape=jax.ShapeDtypeStruct((m, n), F32),
        compiler_params=_params("parallel", "parallel"),
        name="matmul_residual",
    )(left_c, left_s, right_c, right_s, w, res, gate)


def _rope_tables():
    nf = 16
    inv = ROPE_BASE ** (-np.arange(nf, dtype=np.float64) / nf)
    t = np.arange(DEC_SEQ)
    rows, cols = t // GRID_W, t % GRID_W
    lane = np.arange(64)
    pos = np.where(lane[None, :] < 32, rows[:, None], cols[:, None]).astype(np.float32)
    ang = (pos * inv[lane % nf][None, :].astype(np.float32)).astype(np.float32)
    first = (lane % 32) < nf
    cos = np.cos(ang.astype(np.float64))
    sin = np.sin(ang.astype(np.float64)) * np.where(first, -1.0, 1.0)[None, :]
    cos = np.tile(cos, (1, 2)).astype(np.float32)
    sin = np.tile(sin, (1, 2)).astype(np.float32)
    return jnp.asarray(cos), jnp.asarray(sin)


def _rope_kernel(x_ref, cos_ref, sin_ref, o_ref):
    cos = cos_ref[...]
    sin = sin_ref[...]
    lane = lax.broadcasted_iota(jnp.int32, cos.shape, 1)
    first = (lane % 32) < 16
    for c in range(x_ref.shape[1] // LANES):
        x = x_ref[:, c * LANES:(c + 1) * LANES].astype(F32)
        partner = jnp.where(first, pltpu.roll(x, LANES - 16, 1), pltpu.roll(x, 16, 1))
        o_ref[:, c * LANES:(c + 1) * LANES] = (x * cos + partner * sin).astype(o_ref.dtype)


def _rope(x, cos, sin, w, col=0, out_dtype=BF16, tm=256):
    n = N_SMP
    row0 = (x.shape[0] - N_SMP) // tm
    per = DEC_SEQ // tm
    return pl.pallas_call(
        _rope_kernel,
        grid=(n // tm,),
        in_specs=[pl.BlockSpec((tm, w), lambda i: (row0 + i, col)),
                  pl.BlockSpec((tm, LANES), lambda i: (i % per, 0)),
                  pl.BlockSpec((tm, LANES), lambda i: (i % per, 0))],
        out_specs=pl.BlockSpec((tm, w), lambda i: (i, 0)),
        out_shape=jax.ShapeDtypeStruct((n, w), out_dtype),
        compiler_params=_params("parallel"),
        name="rope",
    )(x, cos, sin)


def _softmax(s):
    p = jnp.exp(s - jnp.max(s, axis=-1, keepdims=True))
    return p / jnp.sum(p, axis=-1, keepdims=True)


def _diff_attn_kernel(lam_ref, g_ref, q_ref, k_ref, v_ref, o_ref, *, lam_init):
    lam = lam_ref[...]
    l1 = jnp.sum(jnp.sum(lam[0:1] * lam[1:2], axis=-1, keepdims=True), axis=0, keepdims=True)
    l2 = jnp.sum(jnp.sum(lam[2:3] * lam[3:4], axis=-1, keepdims=True), axis=0, keepdims=True)
    lam_val = jnp.exp(l1) - jnp.exp(l2) + lam_init
    scale = A_QK ** -0.5
    for h in range(HEADS):
        sl = slice(h * HEAD_DIM, (h + 1) * HEAD_DIM)
        q = q_ref[:, sl].astype(BF16)
        k = k_ref[:, sl].astype(BF16)
        p1 = _softmax(_dot_nt(q[:, :A_QK], k[:, :A_QK]) * scale)
        p2 = _softmax(_dot_nt(q[:, A_QK:], k[:, A_QK:]) * scale)
        a = p1 - lam_val * p2
        o = jnp.dot(a.astype(BF16), v_ref[:, sl].astype(BF16), preferred_element_type=F32)
        o = o * lax.rsqrt(jnp.mean(o * o, axis=-1, keepdims=True) + RMS_EPS) * g_ref[...]
        o_ref[:, sl] = (o * (1.0 - lam_init)).astype(o_ref.dtype)


ATT_TQ = 256
ATT_W = HEADS * HEAD_DIM


def _q_rows(t, width, row0=0, col=0):
    per, off = t // ATT_TQ, row0 // ATT_TQ
    return pl.BlockSpec((ATT_TQ, width), lambda b, j: (off + b * per + j, col))


def _kv_rows(s, width, row0=0, col=0):
    off = row0 // s
    return pl.BlockSpec((s, width), lambda b, j: (off + b, col))


def _attention_call(kernel, name, nb, t, operands, specs):
    return pl.pallas_call(
        kernel,
        grid=(nb, t // ATT_TQ),
        in_specs=specs,
        out_specs=_q_rows(t, ATT_W),
        out_shape=jax.ShapeDtypeStruct((nb * t, ATT_W), BF16),
        compiler_params=_params("parallel", "parallel"),
        name=name,
    )(*operands)


def _diff_attention(nb, t, q, k, v, lam, subln_g, lam_init):
    const = [pl.BlockSpec((4, A_QK), lambda i, j: (0, 0)), pl.BlockSpec((1, HEAD_DIM), lambda i, j: (0, 0))]
    return _attention_call(functools.partial(_diff_attn_kernel, lam_init=lam_init), "diff_attention", nb, t,
                           [lam, subln_g.reshape(1, HEAD_DIM), q[0], k[0], v[0]], const + [q[1], k[1], v[1]])


def _mla_attn_kernel(qn_ref, qr_ref, kn_ref, kr_ref, v_ref, o_ref):
    scale = (B_NOPE + B_ROPE) ** -0.5
    kr = kr_ref[:, 0:B_ROPE].astype(BF16)
    for h in range(HEADS):
        sl = slice(h * HEAD_DIM, (h + 1) * HEAD_DIM)
        s = _dot_nt(qn_ref[:, sl].astype(BF16), kn_ref[:, sl])
        s = s + _dot_nt(qr_ref[:, h * B_ROPE:(h + 1) * B_ROPE].astype(BF16), kr)
        p = _softmax(s * scale)
        o = jnp.dot(p.astype(BF16), v_ref[:, sl], preferred_element_type=F32)
        o_ref[:, sl] = o.astype(o_ref.dtype)


def _mla_attention(nb, t, qn, qr, kn, kr, v):
    ops = [qn, qr, kn, kr, v]
    return _attention_call(_mla_attn_kernel, "mla_attention", nb, t, [o[0] for o in ops], [o[1] for o in ops])


def _plain_attn_kernel(q_ref, k_ref, v_ref, o_ref):
    scale = HEAD_DIM ** -0.5
    for h in range(HEADS):
        sl = slice(h * HEAD_DIM, (h + 1) * HEAD_DIM)
        p = _softmax(_dot_nt(q_ref[:, sl].astype(BF16), k_ref[:, sl].astype(BF16)) * scale)
        o = jnp.dot(p.astype(BF16), v_ref[:, sl].astype(BF16), preferred_element_type=F32)
        o_ref[:, sl] = o.astype(o_ref.dtype)


def _plain_attention(nb, t, q, k, v):
    ops = [q, k, v]
    return _attention_call(_plain_attn_kernel, "plain_attention", nb, t, [o[0] for o in ops], [o[1] for o in ops])


def _na_window_start(r):
    return jnp.clip(r - NA_KH // 2, 0, GRID_ROWS - NA_KH)


def _na_attn_kernel(q_ref, k_ref, v_ref, kc_ref, vc_ref, bias_ref, o_ref):
    scale = HEAD_DIM ** -0.5
    r = pl.program_id(1)
    start = pl.multiple_of(_na_window_start(r) * GRID_W, GRID_W)
    for h in range(HEADS):
        sl = slice(h * HEAD_DIM, (h + 1) * HEAD_DIM)
        q = q_ref[:, sl].astype(BF16)
        kw = k_ref[pl.ds(start, NA_WIN), sl].astype(BF16)
        vw = v_ref[pl.ds(start, NA_WIN), sl].astype(BF16)
        s_win = _dot_nt(q, kw) * scale + bias_ref[0, h]
        s_ctx = _dot_nt(q, kc_ref[:, sl].astype(BF16)) * scale
        m = jnp.maximum(jnp.max(s_win, axis=-1, keepdims=True), jnp.max(s_ctx, axis=-1, keepdims=True))
        p_win = jnp.exp(s_win - m)
        p_ctx = jnp.exp(s_ctx - m)
        den = jnp.sum(p_win, axis=-1, keepdims=True) + jnp.sum(p_ctx, axis=-1, keepdims=True)
        o = (jnp.dot(p_win.astype(BF16), vw, preferred_element_type=F32)
             + jnp.dot(p_ctx.astype(BF16), vc_ref[:, sl].astype(BF16), preferred_element_type=F32))
        o_ref[:, sl] = (o / den).astype(o_ref.dtype)


NA_NDC = 2 * NA_KW


def _na_bias_kernel(r_ref, e_ref, o_ref):
    o_ref[0] = jnp.dot(r_ref[0], e_ref[...], preferred_element_type=F32, precision=lax.Precision.HIGHEST)


def _na_bias_tables(rpb):
    cols = np.arange(GRID_W)
    cs = np.clip(cols - NA_KW // 2, 0, GRID_W - NA_KW)
    col_in = (cols[None, :] >= cs[:, None]) & (cols[None, :] < cs[:, None] + NA_KW)
    dc_idx = np.clip(cols[None, :] - cols[:, None] + NA_KW - 1, 0, 2 * NA_KW - 2)
    mask = np.broadcast_to(col_in[:, None, :], (GRID_W, NA_KH, GRID_W)).reshape(GRID_W, NA_WIN)
    onehot = (np.arange(NA_NDC)[:, None] == dc_idx.reshape(1, -1)).astype(np.float32)
    rp = jnp.pad(rpb, ((0, 0), (0, 0), (0, NA_NDC - rpb.shape[2])))
    rows = jnp.stack([rp[:, NA_KH - 1 - d:2 * NA_KH - 1 - d, :] for d in range(NA_KH)])
    rows = rows.reshape(NA_KH, HEADS * NA_KH, NA_NDC)
    nqk = GRID_W * GRID_W
    b = pl.pallas_call(
        _na_bias_kernel,
        grid=(NA_KH,),
        in_specs=[pl.BlockSpec((1, HEADS * NA_KH, NA_NDC), lambda d: (d, 0, 0)),
                  pl.BlockSpec((NA_NDC, nqk), lambda d: (0, 0))],
        out_specs=pl.BlockSpec((1, HEADS * NA_KH, nqk), lambda d: (d, 0, 0)),
        out_shape=jax.ShapeDtypeStruct((NA_KH, HEADS * NA_KH, nqk), F32),
        compiler_params=_params("parallel"),
        name="na_bias",
    )(rows, jnp.asarray(onehot))
    b = b.reshape(NA_KH, HEADS, NA_KH, GRID_W, GRID_W).transpose(0, 1, 3, 2, 4).reshape(NA_KH, HEADS, GRID_W, NA_WIN)
    return jnp.where(mask[None, None], b, NEG_INF)


def _na_attention(z, kc, vc, bias):
    w = ATT_W
    row0 = z.shape[0] - N_SMP

    def all_rows(n, col, base):
        return pl.BlockSpec((n, w), lambda i, r: (base // n + i, col))

    return pl.pallas_call(
        _na_attn_kernel,
        grid=(DEC_BATCH, GRID_ROWS),
        in_specs=[pl.BlockSpec((GRID_W, w), lambda i, r: (row0 // GRID_W + i * GRID_ROWS + r, 0)),
                  all_rows(DEC_SEQ, 1, row0), all_rows(DEC_SEQ, 2, row0),
                  all_rows(PAST_LEN, 0, 0), all_rows(PAST_LEN, 0, 0),
                  pl.BlockSpec((1, HEADS, GRID_W, NA_WIN), lambda i, r: (r - _na_window_start(r), 0, 0, 0))],
        out_specs=pl.BlockSpec((GRID_W, w), lambda i, r: (i * GRID_ROWS + r, 0)),
        out_shape=jax.ShapeDtypeStruct((N_SMP, w), BF16),
        compiler_params=_params("parallel", "parallel"),
        name="na_attention",
    )(z, z, z, kc, vc, bias)


CONV_CW = 384


def _conv3_kernel(x_ref, w_ref, o_ref):
    x = x_ref[...]
    t = x.shape[0]
    row = lax.broadcasted_iota(jnp.int32, x.shape, 0)
    prev = jnp.where(row == 0, 0.0, pltpu.roll(x, 1, 0))
    nxt = jnp.where(row == t - 1, 0.0, pltpu.roll(x, t - 1, 0))
    w = w_ref[...]
    o_ref[...] = prev * w[0:1] + x * w[1:2] + nxt * w[2:3]


def _conv3(z, conv_w, seq, row0, n_seq):
    col0 = (3 * D_WIDTH) // CONV_CW
    blk0 = row0 // seq
    return pl.pallas_call(
        _conv3_kernel,
        grid=(n_seq, D_IN // CONV_CW),
        in_specs=[pl.BlockSpec((seq, CONV_CW), lambda i, j: (blk0 + i, col0 + j)),
                  pl.BlockSpec((3, CONV_CW), lambda i, j: (0, j))],
        out_specs=pl.BlockSpec((seq, CONV_CW), lambda i, j: (i, j)),
        out_shape=jax.ShapeDtypeStruct((n_seq * seq, D_IN), F32),
        compiler_params=_params("parallel", "parallel"),
        name="conv3",
    )(z, conv_w)


def _dot_f32(a, b):
    return jnp.dot(a, b, preferred_element_type=F32, precision=lax.Precision.HIGHEST)


def _d_prep_kernel(x_ref, w0_ref, w2_ref, a0_ref, a2_ref, g2_ref, dec_ref, a_ref, g_ref):
    x = x_ref[...]
    xw = x[:, 0:2 * D_LORA]
    xa = x[:, 2 * D_LORA:4 * D_LORA]
    xg = x[:, 4 * D_LORA:]
    for d in range(2):
        u = w0_ref[d] + _bdot(jnp.tanh(xw[:, d * D_LORA:(d + 1) * D_LORA]), w2_ref[d])
        nu = -u
        softplus = jnp.maximum(nu, 0.0) + jnp.log1p(jnp.exp(-jnp.abs(nu)))
        wlog = -softplus - 0.5
        dec_ref[d] = -jnp.exp(wlog)
        av = a0_ref[d] + _bdot(xa[:, d * D_LORA:(d + 1) * D_LORA], a2_ref[d])
        a_ref[d] = 1.0 / (1.0 + jnp.exp(-av))
    g_ref[...] = _bdot(1.0 / (1.0 + jnp.exp(-xg)), g2_ref[...])


def _d_prep(zc, w0, w2, a0, a2, g2, tm=512):
    n = zc.shape[0]
    cb = (3 * D_WIDTH) // CONV_CW
    return pl.pallas_call(
        _d_prep_kernel,
        grid=(n // tm,),
        in_specs=[pl.BlockSpec((tm, CONV_CW), lambda i: (i, cb)),
                  pl.BlockSpec((2, 1, D_WIDTH), lambda i: (0, 0, 0)),
                  pl.BlockSpec((2, D_LORA, D_WIDTH), lambda i: (0, 0, 0)),
                  pl.BlockSpec((2, 1, D_WIDTH), lambda i: (0, 0, 0)),
                  pl.BlockSpec((2, D_LORA, D_WIDTH), lambda i: (0, 0, 0)),
                  pl.BlockSpec((D_GATE_LORA, D_WIDTH), lambda i: (0, 0))],
        out_specs=[pl.BlockSpec((2, tm, D_WIDTH), lambda i: (0, i, 0)),
                   pl.BlockSpec((2, tm, D_WIDTH), lambda i: (0, i, 0)),
                   pl.BlockSpec((tm, D_WIDTH), lambda i: (i, 0))],
        out_shape=[jax.ShapeDtypeStruct((2, n, D_WIDTH), F32),
                   jax.ShapeDtypeStruct((2, n, D_WIDTH), F32),
                   jax.ShapeDtypeStruct((n, D_WIDTH), F32)],
        compiler_params=_params("parallel"),
        name="d_prep",
    )(zc, w0.reshape(2, 1, D_WIDTH), w2, a0.reshape(2, 1, D_WIDTH), a2, g2)


RW_C = 64
RW_PP = 8


def _bdot(a, b):
    ah, al = _split_hi_lo(a)
    bh, bl = _split_hi_lo(b)
    dot = functools.partial(jnp.dot, preferred_element_type=F32)
    return dot(ah, bh) + dot(ah, bl) + dot(al, bh)


def _bdot_nt(a, b):
    ah, al = _split_hi_lo(a)
    bh, bl = _split_hi_lo(b)
    return _dot_nt(ah, bh) + _dot_nt(ah, bl) + _dot_nt(al, bh)


def _bdot_tn(a, b):
    return _dot_tn(a.astype(BF16), b.astype(BF16))


def _split_hi_lo(x):
    hi = x.astype(BF16)
    return hi, (x - hi.astype(F32)).astype(BF16)


def _head_ones():
    r = lax.broadcasted_iota(jnp.int32, (LANES, LANES), 0) // D_HS
    c = lax.broadcasted_iota(jnp.int32, (LANES, LANES), 1) // D_HS
    return jnp.where(r == c, 1.0, 0.0).astype(BF16)


def _rwkv_chunk_kernel(r_ref, k_ref, v_ref, lw_ref, a_ref, kkp_ref, kap_ref, s0_ref, y_ref, sfin_ref, s_ref):
    c = pl.program_id(3)
    two_c = 2 * RW_C

    @pl.when(c == 0)
    def _():
        s_ref[...] = s0_ref[...]

    sgn = jnp.where(pl.program_id(1) == 1, -1, 1)
    row = lax.broadcasted_iota(jnp.int32, (two_c, two_c), 0)
    col = lax.broadcasted_iota(jnp.int32, (two_c, two_c), 1)
    same = (row // RW_C) == (col // RW_C)
    tt, ss = row % RW_C, col % RW_C
    before = (ss - tt) * sgn < 0
    strict = same & before
    incl = same & (before | (ss == tt))
    eye = jnp.where(row == col, 1.0, 0.0)
    r64 = lax.broadcasted_iota(jnp.int32, (RW_C, RW_C), 0)
    c64 = lax.broadcasted_iota(jnp.int32, (RW_C, RW_C), 1)
    ltri = jnp.where((c64 - r64) * sgn <= 0, 1.0, 0.0)
    head0 = lax.broadcasted_iota(jnp.int32, (RW_C, LANES), 1) < D_HS
    ones_blk = _head_ones().astype(F32)

    def stack(x):
        return jnp.concatenate([jnp.where(head0, x, 0.0), jnp.where(head0, 0.0, x)], axis=0)

    pairs = range(RW_PP)
    lss = [slice(p * LANES, (p + 1) * LANES) for p in pairs]
    r = [r_ref[:, ls] for ls in lss]
    k = [k_ref[:, ls] for ls in lss]
    lw = [lw_ref[:, ls] for ls in lss]
    a = [a_ref[:, ls] for ls in lss]
    kkf = [k[p] * kkp_ref[:, lss[p]] for p in pairs]
    n2 = [_dot_f32(kkf[p] * kkf[p], ones_blk) for p in pairs]
    kk = [kkf[p] / jnp.maximum(jnp.sqrt(n2[p]), 1e-12) for p in pairs]
    cum = [_dot_f32(ltri, lw[p]) for p in pairs]
    mid = [cum[p][RW_C // 2:RW_C // 2 + 1] for p in pairs]
    cumc = [cum[p] - mid[p] for p in pairs]
    pmid = [jnp.exp(mid[p]) for p in pairs]
    pend = [jnp.exp(jnp.sum(lw[p], axis=0, keepdims=True) - mid[p]) for p in pairs]
    pinv = [jnp.exp(-cumc[p]) for p in pairs]
    ks = [stack(kk[p] * jnp.exp(cumc[p] - lw[p])) for p in pairs]
    rs = [stack(r[p] * jnp.exp(cumc[p])) for p in pairs]
    khs = [stack(k[p] * (1.0 + (a[p] - 1.0) * kap_ref[:, lss[p]]) * pinv[p]) for p in pairs]
    bhs = [stack(kk[p] * a[p] * pinv[p]) for p in pairs]
    vs = [stack(v_ref[:, ls]) for ls in lss]
    big = [_bdot_nt(jnp.concatenate([ks[p], rs[p]], axis=0), jnp.concatenate([bhs[p], khs[p]], axis=0)) for p in pairs]
    a_b = [jnp.where(strict, big[p][:two_c, :two_c], 0.0) for p in pairs]
    a_k = [jnp.where(strict, big[p][:two_c, two_c:], 0.0) for p in pairs]
    l_b = [jnp.where(incl, big[p][two_c:, :two_c], 0.0) for p in pairs]
    l_k = [jnp.where(incl, big[p][two_c:, two_c:], 0.0) for p in pairs]
    npow = [-a_b[p] for p in pairs]
    tinv = [eye + npow[p] for p in pairs]
    for _ in range(5):
        npow = [_bdot(npow[p], npow[p]) for p in pairs]
        tinv = [tinv[p] + _bdot(tinv[p], npow[p]) for p in pairs]
    w1 = [_bdot(a_k[p], vs[p]) for p in pairs]
    tku = [_bdot(tinv[p], jnp.concatenate([ks[p], w1[p]], axis=1)) for p in pairs]
    lku = [_bdot(l_b[p], tku[p]) for p in pairs]
    rtil = [rs[p] - lku[p][:, :LANES] for p in pairs]
    yv = [_bdot(l_k[p], vs[p]) - lku[p][:, LANES:] for p in pairs]
    gt = [_bdot(tku[p][:, :LANES].T, bhs[p]) for p in pairs]
    ht = [_bdot(vs[p].T, khs[p]) - _bdot(tku[p][:, LANES:].T, bhs[p]) for p in pairs]
    s = [s_ref[p] * pmid[p] for p in pairs]
    y = [_bdot_nt(rtil[p], s[p]) + yv[p] for p in pairs]
    sg = [_bdot(s[p], gt[p]) for p in pairs]
    for p in pairs:
        s_ref[p] = (s[p] - sg[p] + ht[p]) * pend[p]
        y_ref[:, lss[p]] = y[p][:RW_C] + y[p][RW_C:]

    @pl.when(c == pl.num_programs(3) - 1)
    def _():
        sfin_ref[...] = s_ref[...]


def _pair_states(s):
    zero = jnp.zeros_like(s[:, :, 0::2])
    top = jnp.concatenate([s[:, :, 0::2], zero], axis=-1)
    bot = jnp.concatenate([zero, s[:, :, 1::2]], axis=-1)
    return jnp.concatenate([top, bot], axis=-2)


def _head_states(sp):
    b = sp.shape[0]
    both = jnp.stack([sp[:, :, :, :D_HS, :D_HS], sp[:, :, :, D_HS:, D_HS:]], axis=3)
    return both.reshape(b, 2, D_HEADS, D_HS, D_HS)


def _rwkv_chunked(zc, lw, a, k_k, k_a, s0, b, t):
    nc = t // RW_C
    gw = RW_PP * LANES
    npg = D_WIDTH // gw

    def cmap(d, c):
        return jnp.where(d == 0, c, nc - 1 - c)

    def z_spec(colblk):
        return pl.BlockSpec((RW_C, gw), lambda i, d, g, c: (i * nc + cmap(d, c), colblk * npg + g))

    dir_spec = pl.BlockSpec((None, RW_C, gw), lambda i, d, g, c: (d, i * nc + cmap(d, c), g))
    par_spec = pl.BlockSpec((1, gw), lambda i, d, g, c: (0, g))
    st_spec = pl.BlockSpec((None, None, RW_PP, LANES, LANES), lambda i, d, g, c: (i, d, g, 0, 0))
    return pl.pallas_call(
        _rwkv_chunk_kernel,
        grid=(b, 2, npg, nc),
        in_specs=[z_spec(0), z_spec(1), z_spec(2), dir_spec, dir_spec, par_spec, par_spec, st_spec],
        out_specs=[dir_spec, st_spec],
        out_shape=[jax.ShapeDtypeStruct((2, b * t, D_WIDTH), F32),
                   jax.ShapeDtypeStruct((b, 2, D_WIDTH // LANES, LANES, LANES), F32)],
        scratch_shapes=[pltpu.VMEM((RW_PP, LANES, LANES), F32)],
        compiler_params=_params("parallel", "parallel", "parallel", "arbitrary"),
        name="rwkv_chunk",
    )(zc, zc, zc, lw, a, k_k.reshape(1, D_WIDTH), k_a.reshape(1, D_WIDTH), s0)


def _d_out_kernel(y_ref, r_ref, k_ref, v_ref, a_ref, g_ref, kap_ref, rkp_ref, lng_ref, lnb_ref, o_ref):
    ones_blk = _head_ones()

    def seg_sum(x):
        hi, lo = _split_hi_lo(x)
        return jnp.dot(hi, ones_blk, preferred_element_type=F32) + jnp.dot(lo, ones_blk, preferred_element_type=F32)

    for cb in range(D_WIDTH // LANES):
        ls = slice(cb * LANES, (cb + 1) * LANES)
        y = y_ref[0, :, ls] + y_ref[1, :, ls]
        mu = seg_sum(y) * (1.0 / D_HS)
        yc = y - mu
        var = seg_sum(yc * yc) * (1.0 / D_HS)
        yn = yc * lax.rsqrt(var + D_LN_EPS) * lng_ref[:, ls] + lnb_ref[:, ls]
        asum = a_ref[0, :, ls] + a_ref[1, :, ls]
        bonus = seg_sum(r_ref[:, ls] * k_ref[:, ls] * rkp_ref[:, ls] * (2.0 + (asum - 2.0) * kap_ref[:, ls]))
        o_ref[:, ls] = (yn + bonus * v_ref[:, ls]) * g_ref[:, ls]


def _d_out(y, zc, a, g, k_a, r_k, ln_g, ln_b, tm=256):
    n = g.shape[0]

    def z_spec(colblk):
        return pl.BlockSpec((tm, D_WIDTH), lambda i: (i, colblk))

    dspec = pl.BlockSpec((2, tm, D_WIDTH), lambda i: (0, i, 0))
    pspec = pl.BlockSpec((1, D_WIDTH), lambda i: (0, 0))
    return pl.pallas_call(
        _d_out_kernel,
        grid=(n // tm,),
        in_specs=[dspec, z_spec(0), z_spec(1), z_spec(2), dspec, pl.BlockSpec((tm, D_WIDTH), lambda i: (i, 0)),
                  pspec, pspec, pspec, pspec],
        out_specs=pl.BlockSpec((tm, D_WIDTH), lambda i: (i, 0)),
        out_shape=jax.ShapeDtypeStruct((n, D_WIDTH), F32),
        compiler_params=_params("parallel"),
        name="d_out",
    )(y, zc, zc, zc, a, g, k_a.reshape(1, D_WIDTH), r_k.reshape(1, D_WIDTH), ln_g.reshape(1, D_WIDTH),
      ln_b.reshape(1, D_WIDTH))


TOPK_TM = 256
BIG = float(2 ** 30)


def _extract_topk(s, labels):
    iota16 = lax.broadcasted_iota(jnp.int32, (PEER_TOPK, s.shape[1]), 0)
    rank = jnp.full(s.shape, BIG, F32)
    vals = jnp.zeros((PEER_TOPK, s.shape[1]), F32)
    for r in range(PEER_TOPK):
        m = jnp.max(s, axis=0, keepdims=True)
        idx = jnp.min(jnp.where(s == m, labels, BIG), axis=0, keepdims=True)
        hit = labels == idx
        rank = jnp.where(hit, float(r), rank)
        vals = jnp.where(iota16 == r, m, vals)
        s = jnp.where(hit, -jnp.inf, s)
    return rank, vals


_CAND_PAIRS = ([(0, b) for b in range(16)] + [(1, b) for b in range(8)] + [(2, b) for b in range(8)]
               + [(3, b) for b in range(8)] + [(4, b) for b in range(4)] + [(5, b) for b in range(4)]
               + [(6, b) for b in range(4)] + [(7, b) for b in range(4)] + [(a, 0) for a in range(8, 16)])
N_CAND = len(_CAND_PAIRS)


def _cand_labels():
    lab = np.array([a * PEER_TOPK + b for a, b in _CAND_PAIRS], np.float32)
    return jnp.asarray(np.broadcast_to(lab[:, None], (N_CAND, LANES)).copy())


def _peer_topk_kernel(q_ref, keys_ref, lab_ref, lim_ref, e1_ref, rb_ref, e2_ref):
    iota_k = lax.broadcasted_iota(jnp.int32, (PEER_NKEYS, LANES), 0).astype(F32)
    row8 = lax.broadcasted_iota(jnp.int32, (8, LANES), 0)
    labels = lab_ref[...]
    for c in range(TOPK_TM // LANES):
        cs = slice(c * LANES, (c + 1) * LANES)
        q1 = q_ref[cs, 0:PEER_HALF].astype(BF16)
        q2 = q_ref[cs, PEER_HALF:2 * PEER_HALF].astype(BF16)
        s1 = _dot_nt(keys_ref[0, 0], q1)
        s2 = _dot_nt(keys_ref[0, 1], q2)
        rank1, sv1 = _extract_topk(s1, iota_k)
        rank2, sv2 = _extract_topk(s2, iota_k)
        lo8 = sv2[0:8]
        lo4 = jnp.where(row8 < 4, lo8, pltpu.roll(lo8, 4, 0))
        cand = jnp.concatenate([
            sv1[0:1] + lo8, sv1[0:1] + sv2[8:16], sv1[1:2] + lo8, sv1[2:3] + lo8, sv1[3:4] + lo8,
            jnp.where(row8 < 4, sv1[4:5], sv1[5:6]) + lo4, jnp.where(row8 < 4, sv1[6:7], sv1[7:8]) + lo4,
            sv1[8:16] + sv2[0:1]], axis=0)
        crank, cvals = _extract_topk(cand, labels)
        z = jnp.sum(jnp.exp(cvals - cvals[0:1]), axis=0, keepdims=True)
        sel = jnp.where(crank < BIG, 1.0, 0.0)

        def count(lo, hi):
            return jnp.sum(sel[lo:hi], axis=0, keepdims=True)

        def count_half(lo, first):
            part = jnp.where((row8 < 4) if first else (row8 >= 4), sel[lo:lo + 8], 0.0)
            return jnp.sum(part, axis=0, keepdims=True)

        n_sel = [count(0, 16), count(16, 24), count(24, 32), count(32, 40),
                 count_half(40, True), count_half(40, False), count_half(48, True), count_half(48, False)]
        n_sel += [sel[56 + a:57 + a] for a in range(8)]
        lim = jnp.zeros_like(s1)
        for a in range(PEER_TOPK):
            lim = jnp.where(rank1 == float(a), n_sel[a], lim)
        lim_ref[0, :, cs] = lim
        e1_ref[0, :, cs] = jnp.exp(s1 - sv1[0:1])
        rb_ref[0, :, cs] = pltpu.bitcast(rank2.astype(BF16), jnp.uint32)
        e2_ref[0, :, cs] = jnp.exp(s2 - sv2[0:1]) / z


def _peer_topk(q, keys):
    n = q.shape[0]
    ospec = pl.BlockSpec((1, PEER_NKEYS, TOPK_TM), lambda i, h: (h, 0, i))
    pspec = pl.BlockSpec((1, PEER_NKEYS // 2, TOPK_TM), lambda i, h: (h, 0, i))
    return pl.pallas_call(
        _peer_topk_kernel,
        grid=(n // TOPK_TM, PEER_HEADS),
        in_specs=[pl.BlockSpec((TOPK_TM, 2 * PEER_HALF), lambda i, h: (i, h)),
                  pl.BlockSpec((1, 2, PEER_NKEYS, PEER_HALF), lambda i, h: (h, 0, 0, 0)),
                  pl.BlockSpec((N_CAND, LANES), lambda i, h: (0, 0))],
        out_specs=[ospec, ospec, pspec, ospec],
        out_shape=[jax.ShapeDtypeStruct((PEER_HEADS, PEER_NKEYS, n), F32)] * 2
        + [jax.ShapeDtypeStruct((PEER_HEADS, PEER_NKEYS // 2, n), jnp.uint32),
           jax.ShapeDtypeStruct((PEER_HEADS, PEER_NKEYS, n), F32)],
        compiler_params=_params("parallel", "parallel"),
        name="peer_topk",
    )(q, keys, _cand_labels())


PEER_TM = 512
PEER_TI = 8
PEER_RG = 4
PEER_JT = 16


def _gelu(x):
    return 0.5 * x * (1.0 + lax.erf(x * (2.0 ** -0.5)))


def _peer_dense_kernel(x_ref, u_ref, v_ref, lim_ref, e1_ref, rb_ref, e2_ref, res_ref, gate_ref, o_ref, w_ref):
    e = pl.program_id(1)

    @pl.when(e == 0)
    def _():
        o_ref[...] = jnp.zeros_like(o_ref)

    hid = _dot_nt(u_ref[...], x_ref[...])
    zero = jnp.zeros((), BF16)
    n_jt = PEER_NKEYS // PEER_JT
    for c in range(PEER_TM // LANES):
        cs = slice(c * LANES, (c + 1) * LANES)
        for rg in range(PEER_TI // PEER_RG):
            rows = range(rg * PEER_RG, (rg + 1) * PEER_RG)
            g = [[jnp.zeros((PEER_JT, LANES), BF16) for _ in range(n_jt)] for _ in rows]
            for h in range(PEER_HEADS):
                lim = [jnp.broadcast_to(lim_ref[h, ii:ii + 1, cs], (PEER_JT, LANES)).astype(BF16) for ii in rows]
                e1 = [jnp.broadcast_to(e1_ref[h, ii:ii + 1, cs], (PEER_JT, LANES)).astype(BF16) for ii in rows]
                for jt in range(n_jt):
                    ws = slice(jt * PEER_JT // 2, (jt + 1) * PEER_JT // 2)
                    rb = pltpu.bitcast(rb_ref[h, ws, cs], BF16)
                    e2 = e2_ref[h, jt * PEER_JT:(jt + 1) * PEER_JT, cs].astype(BF16)
                    for k in range(PEER_RG):
                        g[k][jt] = g[k][jt] + jnp.where(rb < lim[k], e2, zero) * e1[k]
            for k, ii in enumerate(rows):
                for jt in range(n_jt):
                    e0 = ii * PEER_NKEYS + jt * PEER_JT
                    w_ref[e0:e0 + PEER_JT, cs] = g[k][jt] * _gelu(hid[e0:e0 + PEER_JT, cs].astype(BF16))
    o_ref[...] += _dot_tn(w_ref[...], v_ref[...])

    @pl.when(e == pl.num_programs(1) - 1)
    def _():
        o_ref[...] = res_ref[...] + gate_ref[0] * o_ref[...]


def _peer_dense(xm, u, v, layer, lim, e1, rb, e2, res, gate):
    n, d = xm.shape
    n_exp = u.shape[1]
    te = PEER_TI * PEER_NKEYS
    rank_spec = pl.BlockSpec((PEER_HEADS, PEER_NKEYS // 2, PEER_TM), lambda i, e: (0, 0, i))
    sel_spec = pl.BlockSpec((PEER_HEADS, PEER_NKEYS, PEER_TM), lambda i, e: (0, 0, i))
    row_spec = pl.BlockSpec((PEER_HEADS, PEER_TI, PEER_TM), lambda i, e: (0, e, i))
    return pl.pallas_call(
        _peer_dense_kernel,
        grid=(n // PEER_TM, n_exp // te),
        in_specs=[pl.BlockSpec((PEER_TM, d), lambda i, e: (i, 0)),
                  pl.BlockSpec((None, te, d), lambda i, e: (layer, e, 0)),
                  pl.BlockSpec((None, te, d), lambda i, e: (layer, e, 0)),
                  row_spec, row_spec, rank_spec, sel_spec,
                  pl.BlockSpec((PEER_TM, d), lambda i, e: (i, 0)),
                  pl.BlockSpec((1, 1, d), lambda i, e: (_cond_of_tile(i, PEER_TM), 0, 0))],
        out_specs=pl.BlockSpec((PEER_TM, d), lambda i, e: (i, 0)),
        out_shape=jax.ShapeDtypeStruct((n, d), F32),
        scratch_shapes=[pltpu.VMEM((te, PEER_TM), BF16)],
        compiler_params=_params("parallel", "arbitrary"),
        name="peer_dense",
    )(xm, u, v, lim, e1, rb, e2, res, gate)


def _peer_layer(x, layer, norm_g, shift, scale, gate, wq, keys, u, v):
    xm = _modulate(x, norm_g, shift, scale)
    q = _matmul(xm, wq, layer)
    lim, e1, rb, e2 = _peer_topk(q, keys)
    return _peer_dense(xm, u, v, layer, lim, e1, rb, e2, x, gate)


EVEN_PAD = 4096
ODD_PAD = 6656
S_ALL = PAST_LEN + DEC_SEQ


def _even_mixer(h, w_in, layer, lam, subln_g, q_norm_g, w_uq, kv_norm_g, w_ukv, lam_init,
                cache_k, cache_v, cache_ckv, cache_kpe, rope_cos, rope_sin):
    z = _matmul(h, w_in, layer)
    ckv_n = _rmsnorm(z, kv_norm_g, col=3584 // B_KV_RANK)
    cq_n = _rmsnorm(z, q_norm_g, out_dtype=BF16, col=3072 // B_Q_RANK)
    wq3 = w_uq.reshape(B_Q_RANK, HEADS, B_NOPE + B_ROPE)
    w_uq_r = jnp.concatenate([wq3[:, :, :B_NOPE].reshape(B_Q_RANK, -1), wq3[:, :, B_NOPE:].reshape(B_Q_RANK, -1)], axis=1)
    qb = _matmul(cq_n, w_uq_r.astype(BF16), tn=512)
    new = (z[:N_CTX, 1024:2048].reshape(BATCH, SEQ, HEADS, 2 * A_QK),
           z[:N_CTX, 2048:3072].reshape(BATCH, SEQ, HEADS, HEAD_DIM),
           ckv_n[:N_CTX].reshape(BATCH, SEQ, B_KV_RANK), z[:N_CTX, 3840:3904].reshape(BATCH, SEQ, B_ROPE))

    qa_s = _rope(z, rope_cos, rope_sin, ATT_W, col=0)
    ka_s = _rope(z, rope_cos, rope_sin, ATT_W, col=1)
    qr_s = _rope(qb, rope_cos, rope_sin, HEADS * B_ROPE, col=2)
    kpe_s = _rope(z, rope_cos, rope_sin, LANES, col=3840 // LANES)[:, :B_ROPE]

    def with_ctx(cache, own, width):
        both = jnp.concatenate([cache.reshape(DEC_BATCH, PAST_LEN, width).astype(own.dtype),
                                own.reshape(DEC_BATCH, DEC_SEQ, width)], axis=1)
        return both.reshape(DEC_BATCH * S_ALL, width)

    ka_all = with_ctx(cache_k, ka_s, ATT_W)
    va_all = with_ctx(cache_v, z[N_CTX:, 2048:3072].astype(BF16), ATT_W)
    ckv_all = with_ctx(cache_ckv, ckv_n[N_CTX:], B_KV_RANK)
    kpe_all = with_ctx(cache_kpe, kpe_s, B_ROPE)

    wkv3 = w_ukv.reshape(B_KV_RANK, HEADS, B_NOPE + HEAD_DIM)
    w_ukv_r = jnp.concatenate([wkv3[:, :, :B_NOPE].reshape(B_KV_RANK, -1), wkv3[:, :, B_NOPE:].reshape(B_KV_RANK, -1)], axis=1)
    n_skv = DEC_BATCH * S_ALL
    kv = _matmul(jnp.concatenate([ckv_all, ckv_n[:N_CTX]], axis=0), w_ukv_r.astype(BF16), out_dtype=BF16)

    oa_c = _diff_attention(BATCH, SEQ, (z, _q_rows(SEQ, ATT_W, 0, 0)), (z, _kv_rows(SEQ, ATT_W, 0, 1)),
                           (z, _kv_rows(SEQ, ATT_W, 0, 2)), lam, subln_g, lam_init)
    oa_s = _diff_attention(DEC_BATCH, DEC_SEQ, (qa_s, _q_rows(DEC_SEQ, ATT_W)), (ka_all, _kv_rows(S_ALL, ATT_W)),
                           (va_all, _kv_rows(S_ALL, ATT_W)), lam, subln_g, lam_init)
    ob_c = _mla_attention(BATCH, SEQ, (qb, _q_rows(SEQ, ATT_W, 0, 0)), (qb, _q_rows(SEQ, HEADS * B_ROPE, 0, 2)),
                          (kv, _kv_rows(SEQ, ATT_W, n_skv, 0)), (z, _kv_rows(SEQ, LANES, 0, 3840 // LANES)),
                          (kv, _kv_rows(SEQ, ATT_W, n_skv, 1)))
    ob_s = _mla_attention(DEC_BATCH, DEC_SEQ, (qb, _q_rows(DEC_SEQ, ATT_W, N_CTX, 0)),
                          (qr_s, _q_rows(DEC_SEQ, HEADS * B_ROPE)), (kv, _kv_rows(S_ALL, ATT_W, 0, 0)),
                          (kpe_all, _kv_rows(S_ALL, B_ROPE)), (kv, _kv_rows(S_ALL, ATT_W, 0, 1)))
    return (oa_c, oa_s, ob_c, ob_s), new


def _odd_mixer(h, w_in, layer, rpb, conv_w, w0, w2, a0, a2, g2, k_k, k_a, r_k, ln_g, ln_b,
               cache_k, cache_v, state):
    z = _matmul(h, w_in, layer)
    new_k = z[:N_CTX, 1024:2048].reshape(BATCH, SEQ, HEADS, HEAD_DIM)
    new_v = z[:N_CTX, 2048:3072].reshape(BATCH, SEQ, HEADS, HEAD_DIM)
    oc_c = _plain_attention(BATCH, SEQ, (z, _q_rows(SEQ, ATT_W, 0, 0)), (z, _kv_rows(SEQ, ATT_W, 0, 1)),
                            (z, _kv_rows(SEQ, ATT_W, 0, 2)))
    oc_s = _na_attention(z, cache_k.reshape(DEC_BATCH * PAST_LEN, ATT_W), cache_v.reshape(DEC_BATCH * PAST_LEN, ATT_W),
                         _na_bias_tables(rpb))

    zc_c = _conv3(z, conv_w, SEQ, 0, BATCH)
    zc_s = _conv3(z, conv_w, DEC_SEQ, N_CTX, DEC_BATCH)
    lw_c, a_c, g_c = _d_prep(zc_c, w0, w2, a0, a2, g2)
    lw_s, a_s, g_s = _d_prep(zc_s, w0, w2, a0, a2, g2)
    rk = r_k.reshape(-1)
    y_c, sfin = _rwkv_chunked(zc_c, lw_c, a_c, k_k, k_a,
                              jnp.zeros((BATCH, 2, D_WIDTH // LANES, LANES, LANES), F32), BATCH, SEQ)
    sfin = _head_states(sfin)
    y_s, _ = _rwkv_chunked(zc_s, lw_s, a_s, k_k, k_a, _pair_states(state), DEC_BATCH, DEC_SEQ)
    od_c = _d_out(y_c, zc_c, a_c, g_c, k_a, rk, ln_g, ln_b)
    od_s = _d_out(y_s, zc_s, a_s, g_s, k_a, rk, ln_g, ln_b)
    return (oc_c, oc_s, od_c, od_s), (new_k, new_v, sfin)


def kernel(x_prompt, x_sample, cache_a_k, cache_a_v, cache_b_ckv, cache_b_kpe, cache_c_k, cache_c_v, state_d, c, c_ctx, ada_w, ada_b, norm1_g, norm2_g, w_out, peer_wq, peer_keys, peer_u, peer_v, final_g, ab_w_in, a_lam, a_subln_g, b_q_norm_g, b_w_uq, b_kv_norm_g, b_w_ukv, cd_w_in, c_rpb, d_conv, d_w0, d_w2, d_a0, d_a2, d_g2, d_k_k, d_k_a, d_r_k, d_ln_g, d_ln_b):
    x = jnp.concatenate([x_prompt.reshape(N_CTX, D_MODEL), x_sample.reshape(N_SMP, D_MODEL)], axis=0)
    cond8 = jnp.pad(jnp.concatenate([c_ctx[None, :], c], axis=0), ((0, 8 - N_COND), (0, 0)))
    ada = _ada_all(cond8, ada_w, ada_b)
    rope_cos, rope_sin = _rope_tables()
    w_even = jnp.pad(ab_w_in, ((0, 0), (0, 0), (0, EVEN_PAD - EVEN_IN))).astype(BF16)
    w_odd = jnp.pad(cd_w_in, ((0, 0), (0, 0), (0, ODD_PAD - ODD_IN))).astype(BF16)
    w_out_b, wq_b, keys_b = w_out.astype(BF16), peer_wq.astype(BF16), peer_keys.astype(BF16)
    u_b, v_b = peer_u.astype(BF16), peer_v.astype(BF16)
    new_ak, new_av, new_bc, new_bp, new_ck, new_cv, new_sd = [], [], [], [], [], [], []
    for l in range(DEPTH):
        i = l // 2
        mods = ada[l, :N_COND].reshape(N_COND, 6, 1, D_MODEL)
        sh1, sc1, g1, sh2, sc2, g2 = (mods[:, m] for m in range(6))
        h = _modulate(x, norm1_g[l], sh1, sc1)
        if l % 2 == 0:
            lam_init = 0.8 - 0.6 * math.exp(-0.3 * l)
            o, (ak, av, bc, bp) = _even_mixer(
                h, w_even, i, a_lam[i], a_subln_g[i], b_q_norm_g[i], b_w_uq[i], b_kv_norm_g[i], b_w_ukv[i],
                lam_init, cache_a_k[:, i], cache_a_v[:, i], cache_b_ckv[:, i], cache_b_kpe[:, i], rope_cos, rope_sin)
            new_ak.append(ak)
            new_av.append(av)
            new_bc.append(bc)
            new_bp.append(bp)
        else:
            o, (ck, cv, sd) = _odd_mixer(
                h, w_odd, i, c_rpb[i], d_conv[i], d_w0[i], d_w2[i], d_a0[i], d_a2[i], d_g2[i],
                d_k_k[i], d_k_a[i], d_r_k[i], d_ln_g[i], d_ln_b[i], cache_c_k[:, i], cache_c_v[:, i], state_d[:, i])
            new_ck.append(ck)
            new_cv.append(cv)
            new_sd.append(sd)
        x = _matmul_residual(*o, w_out_b, l, x, g1)
        x = _peer_layer(x, l, norm2_g[l], sh2, sc2, g2, wq_b, keys_b[l], u_b, v_b)
    y = _rmsnorm(x, final_g)
    y_prompt = y[:N_CTX].reshape(BATCH, SEQ, D_MODEL)
    y_sample = y[N_CTX:].reshape(DEC_BATCH, DEC_SEQ, D_MODEL)
    return (y_prompt, y_sample, jnp.stack(new_ak, axis=1), jnp.stack(new_av, axis=1), jnp.stack(new_bc, axis=1),
            jnp.stack(new_bp, axis=1), jnp.stack(new_ck, axis=1), jnp.stack(new_cv, axis=1), jnp.stack(new_sd, axis=1))
```

```python
import functools
import math

import numpy as np
import jax
import jax.numpy as jnp
from jax import lax
from jax.experimental import pallas as pl
from jax.experimental.pallas import tpu as pltpu

F32 = jnp.float32
BF16 = jnp.bfloat16

D_MODEL = 2048
BATCH = 16
SEQ = 256
DEPTH = 4
DEC_BATCH = 2
DEC_SEQ = 1024
PAST_LEN = 512
GRID_W = 64
GRID_ROWS = DEC_SEQ // GRID_W
ROPE_BASE = 10000.0
RMS_EPS = 1e-6
NEG_INF = -1e30
N_CTX = BATCH * SEQ
N_SMP = DEC_BATCH * DEC_SEQ
N_TOK = N_CTX + N_SMP
N_COND = 1 + DEC_BATCH

HEADS = 8
HEAD_DIM = 128
A_QK = 64
B_Q_RANK = 512
B_KV_RANK = 256
B_NOPE = 128
B_ROPE = 64
NA_KH = 8
NA_KW = 16
NA_WIN = NA_KH * GRID_W
D_HS = 64
D_WIDTH = 1024
D_HEADS = 16
D_LORA = 64
D_GATE_LORA = 128
D_IN = 3 * D_WIDTH + 2 * D_LORA + 2 * D_LORA + D_GATE_LORA
D_LN_EPS = 64e-5
EVEN_IN = 3904
ODD_IN = 6528
PEER_HEADS = 8
PEER_NKEYS = 128
PEER_HALF = 128
PEER_TOPK = 16

LANES = 128
VMEM_LIMIT = 56 * 1024 * 1024


def _params(*sem):
    return pltpu.CompilerParams(dimension_semantics=sem, vmem_limit_bytes=VMEM_LIMIT)


def _dot_nt(a, b):
    return lax.dot_general(a, b, (((1,), (1,)), ((), ())), preferred_element_type=F32)


def _dot_tn(a, b):
    return lax.dot_general(a, b, (((0,), (0,)), ((), ())), preferred_element_type=F32)


def _cond_of_tile(i, tm):
    n_ctx_tiles = N_CTX // tm
    per_batch = DEC_SEQ // tm
    return jnp.where(i < n_ctx_tiles, 0, 1 + (i - n_ctx_tiles) // per_batch)


def _ada_kernel(c_ref, w_ref, b_ref, o_ref):
    c = c_ref[...]
    sc = c * (1.0 / (1.0 + jnp.exp(-c)))
    o_ref[0] = jnp.dot(sc.astype(BF16), w_ref[0].astype(BF16), preferred_element_type=F32) + b_ref[0]


def _ada_all(cond8, ada_w, ada_b):
    tn = 1024
    n_out = ada_w.shape[-1]
    return pl.pallas_call(
        _ada_kernel,
        grid=(DEPTH, n_out // tn),
        in_specs=[pl.BlockSpec((8, D_MODEL), lambda l, j: (0, 0)),
                  pl.BlockSpec((1, D_MODEL, tn), lambda l, j: (l, 0, j)),
                  pl.BlockSpec((1, 1, tn), lambda l, j: (l, 0, j))],
        out_specs=pl.BlockSpec((1, 8, tn), lambda l, j: (l, 0, j)),
        out_shape=jax.ShapeDtypeStruct((DEPTH, 8, n_out), F32),
        compiler_params=_params("parallel", "parallel"),
        name="ada",
    )(cond8, ada_w, ada_b.reshape(DEPTH, 1, n_out))


def _modulate_kernel(x_ref, g_ref, sh_ref, sc_ref, o_ref):
    x = x_ref[...]
    y = x * lax.rsqrt(jnp.mean(x * x, axis=-1, keepdims=True) + RMS_EPS) * g_ref[...]
    o_ref[...] = (y * (1.0 + sc_ref[0]) + sh_ref[0]).astype(o_ref.dtype)


def _modulate(x, g, shift, scale, out_dtype=BF16, tm=512):
    n, d = x.shape
    return pl.pallas_call(
        _modulate_kernel,
        grid=(n // tm,),
        in_specs=[pl.BlockSpec((tm, d), lambda i: (i, 0)),
                  pl.BlockSpec((1, d), lambda i: (0, 0)),
                  pl.BlockSpec((1, 1, d), lambda i: (_cond_of_tile(i, tm), 0, 0)),
                  pl.BlockSpec((1, 1, d), lambda i: (_cond_of_tile(i, tm), 0, 0))],
        out_specs=pl.BlockSpec((tm, d), lambda i: (i, 0)),
        out_shape=jax.ShapeDtypeStruct((n, d), out_dtype),
        compiler_params=_params("parallel"),
        name="modulate",
    )(x, g.reshape(1, d), shift, scale)


def _rmsnorm_kernel(x_ref, g_ref, o_ref):
    x = x_ref[...]
    y = x * lax.rsqrt(jnp.mean(x * x, axis=-1, keepdims=True) + RMS_EPS) * g_ref[...]
    o_ref[...] = y.astype(o_ref.dtype)


def _rmsnorm(x, g, out_dtype=F32, tm=512, col=0):
    n = x.shape[0]
    d = g.shape[-1]
    return pl.pallas_call(
        _rmsnorm_kernel,
        grid=(n // tm,),
        in_specs=[pl.BlockSpec((tm, d), lambda i: (i, col)),
                  pl.BlockSpec((1, d), lambda i: (0, 0))],
        out_specs=pl.BlockSpec((tm, d), lambda i: (i, 0)),
        out_shape=jax.ShapeDtypeStruct((n, d), out_dtype),
        compiler_params=_params("parallel"),
        name="rmsnorm",
    )(x, g.reshape(1, d))


def _mm_kernel(x_ref, w_ref, o_ref):
    o_ref[...] = jnp.dot(x_ref[...].astype(BF16), w_ref[...], preferred_element_type=F32).astype(o_ref.dtype)


def _weight_spec(w, layer, tn):
    if w.ndim == 2:
        return pl.BlockSpec((w.shape[0], tn), lambda i, j: (0, j))
    return pl.BlockSpec((None, w.shape[1], tn), lambda i, j: (layer, 0, j))


MM_TM = 1024
MM_TN = 1024


def _matmul(x, w, layer=None, out_dtype=F32, tm=MM_TM, tn=None, x_col=0, x_width=None):
    m = x.shape[0]
    k = x_width or x.shape[1]
    n = w.shape[-1]
    if tn is None:
        tn = MM_TN if n % MM_TN == 0 else MM_TN // 2
    tn = min(tn, n)
    assert m % tm == 0 and n % tn == 0, (m, n, tm, tn)
    return pl.pallas_call(
        _mm_kernel,
        grid=(m // tm, n // tn),
        in_specs=[pl.BlockSpec((tm, k), lambda i, j: (i, x_col)),
                  _weight_spec(w, layer, tn)],
        out_specs=pl.BlockSpec((tm, tn), lambda i, j: (i, j)),
        out_shape=jax.ShapeDtypeStruct((m, n), out_dtype),
        compiler_params=_params("parallel", "parallel"),
        name="matmul",
    )(x, w)


def _mm_res_kernel(lc_ref, ls_ref, rc_ref, rs_ref, w_ref, res_ref, gate_ref, o_ref, *, n_ctx_tiles):
    half = w_ref.shape[0] // 2

    def emit(l_ref, r_ref):
        acc = jnp.dot(l_ref[...].astype(BF16), w_ref[0:half, :], preferred_element_type=F32)
        acc = acc + jnp.dot(r_ref[...].astype(BF16), w_ref[half:, :], preferred_element_type=F32)
        o_ref[...] = res_ref[...] + gate_ref[0] * acc

    @pl.when(pl.program_id(0) < n_ctx_tiles)
    def _():
        emit(lc_ref, rc_ref)

    @pl.when(pl.program_id(0) >= n_ctx_tiles)
    def _():
        emit(ls_ref, rs_ref)


def _matmul_residual(left_c, left_s, right_c, right_s, w, layer, res, gate, tm=MM_TM, tn=MM_TN // 2):
    m = res.shape[0]
    n = w.shape[-1]
    kh = left_c.shape[1]
    nct = left_c.shape[0] // tm
    ctx_spec = pl.BlockSpec((tm, kh), lambda i, j: (jnp.minimum(i, nct - 1), 0))
    smp_spec = pl.BlockSpec((tm, kh), lambda i, j: (jnp.maximum(i - nct, 0), 0))
    return pl.pallas_call(
        functools.partial(_mm_res_kernel, n_ctx_tiles=nct),
        grid=(m // tm, n // tn),
        in_specs=[ctx_spec, smp_spec, ctx_spec, smp_spec,
                  _weight_spec(w, layer, tn),
                  pl.BlockSpec((tm, tn), lambda i, j: (i, j)),
                  pl.BlockSpec((1, 1, tn), lambda i, j: (_cond_of_tile(i, tm), 0, j))],
        out_specs=pl.BlockSpec((tm, tn), lambda i, j: (i, j)),
        out_shape=jax.ShapeDtypeStruct((m, n), F32),
        compiler_params=_params("parallel", "parallel"),
        name="matmul_residual",
    )(left_c, left_s, right_c, right_s, w, res, gate)


def _rope_tables():
    nf = 16
    inv = ROPE_BASE ** (-np.arange(nf, dtype=np.float64) / nf)
    t = np.arange(DEC_SEQ)
    rows, cols = t // GRID_W, t % GRID_W
    lane = np.arange(64)
    pos = np.where(lane[None, :] < 32, rows[:, None], cols[:, None]).astype(np.float32)
    ang = (pos * inv[lane % nf][None, :].astype(np.float32)).astype(np.float32)
    first = (lane % 32) < nf
    cos = np.cos(ang.astype(np.float64))
    sin = np.sin(ang.astype(np.float64)) * np.where(first, -1.0, 1.0)[None, :]
    cos = np.tile(cos, (1, 2)).astype(np.float32)
    sin = np.tile(sin, (1, 2)).astype(np.float32)
    return jnp.asarray(cos), jnp.asarray(sin)


def _rope_kernel(x_ref, cos_ref, sin_ref, o_ref):
    cos = cos_ref[...]
    sin = sin_ref[...]
    lane = lax.broadcasted_iota(jnp.int32, cos.shape, 1)
    first = (lane % 32) < 16
    for c in range(x_ref.shape[1] // LANES):
        x = x_ref[:, c * LANES:(c + 1) * LANES].astype(F32)
        partner = jnp.where(first, pltpu.roll(x, LANES - 16, 1), pltpu.roll(x, 16, 1))
        o_ref[:, c * LANES:(c + 1) * LANES] = (x * cos + partner * sin).astype(o_ref.dtype)


def _rope(x, cos, sin, w, col=0, out_dtype=BF16, tm=256):
    n = N_SMP
    row0 = (x.shape[0] - N_SMP) // tm
    per = DEC_SEQ // tm
    return pl.pallas_call(
        _rope_kernel,
        grid=(n // tm,),
        in_specs=[pl.BlockSpec((tm, w), lambda i: (row0 + i, col)),
                  pl.BlockSpec((tm, LANES), lambda i: (i % per, 0)),
                  pl.BlockSpec((tm, LANES), lambda i: (i % per, 0))],
        out_specs=pl.BlockSpec((tm, w), lambda i: (i, 0)),
        out_shape=jax.ShapeDtypeStruct((n, w), out_dtype),
        compiler_params=_params("parallel"),
        name="rope",
    )(x, cos, sin)


def _softmax(s):
    p = jnp.exp(s - jnp.max(s, axis=-1, keepdims=True))
    return p / jnp.sum(p, axis=-1, keepdims=True)


def _diff_attn_kernel(lam_ref, g_ref, q_ref, k_ref, v_ref, o_ref, *, lam_init):
    lam = lam_ref[...]
    l1 = jnp.sum(jnp.sum(lam[0:1] * lam[1:2], axis=-1, keepdims=True), axis=0, keepdims=True)
    l2 = jnp.sum(jnp.sum(lam[2:3] * lam[3:4], axis=-1, keepdims=True), axis=0, keepdims=True)
    lam_val = jnp.exp(l1) - jnp.exp(l2) + lam_init
    scale = A_QK ** -0.5
    for h in range(HEADS):
        sl = slice(h * HEAD_DIM, (h + 1) * HEAD_DIM)
        q = q_ref[:, sl].astype(BF16)
        k = k_ref[:, sl].astype(BF16)
        p1 = _softmax(_dot_nt(q[:, :A_QK], k[:, :A_QK]) * scale)
        p2 = _softmax(_dot_nt(q[:, A_QK:], k[:, A_QK:]) * scale)
        a = p1 - lam_val * p2
        o = jnp.dot(a.astype(BF16), v_ref[:, sl].astype(BF16), preferred_element_type=F32)
        o = o * lax.rsqrt(jnp.mean(o * o, axis=-1, keepdims=True) + RMS_EPS) * g_ref[...]
        o_ref[:, sl] = (o * (1.0 - lam_init)).astype(o_ref.dtype)


ATT_TQ = 256
ATT_W = HEADS * HEAD_DIM


def _q_rows(t, width, row0=0, col=0):
    per, off = t // ATT_TQ, row0 // ATT_TQ
    return pl.BlockSpec((ATT_TQ, width), lambda b, j: (off + b * per + j, col))


def _kv_rows(s, width, row0=0, col=0):
    off = row0 // s
    return pl.BlockSpec((s, width), lambda b, j: (off + b, col))


def _attention_call(kernel, name, nb, t, operands, specs):
    return pl.pallas_call(
        kernel,
        grid=(nb, t // ATT_TQ),
        in_specs=specs,
        out_specs=_q_rows(t, ATT_W),
        out_shape=jax.ShapeDtypeStruct((nb * t, ATT_W), BF16),
        compiler_params=_params("parallel", "parallel"),
        name=name,
    )(*operands)


def _diff_attention(nb, t, q, k, v, lam, subln_g, lam_init):
    const = [pl.BlockSpec((4, A_QK), lambda i, j: (0, 0)), pl.BlockSpec((1, HEAD_DIM), lambda i, j: (0, 0))]
    return _attention_call(functools.partial(_diff_attn_kernel, lam_init=lam_init), "diff_attention", nb, t,
                           [lam, subln_g.reshape(1, HEAD_DIM), q[0], k[0], v[0]], const + [q[1], k[1], v[1]])


def _mla_attn_kernel(qn_ref, qr_ref, kn_ref, kr_ref, v_ref, o_ref):
    scale = (B_NOPE + B_ROPE) ** -0.5
    kr = kr_ref[:, 0:B_ROPE].astype(BF16)
    for h in range(HEADS):
        sl = slice(h * HEAD_DIM, (h + 1) * HEAD_DIM)
        s = _dot_nt(qn_ref[:, sl].astype(BF16), kn_ref[:, sl])
        s = s + _dot_nt(qr_ref[:, h * B_ROPE:(h + 1) * B_ROPE].astype(BF16), kr)
        p = _softmax(s * scale)
        o = jnp.dot(p.astype(BF16), v_ref[:, sl], preferred_element_type=F32)
        o_ref[:, sl] = o.astype(o_ref.dtype)


def _mla_attention(nb, t, qn, qr, kn, kr, v):
    ops = [qn, qr, kn, kr, v]
    return _attention_call(_mla_attn_kernel, "mla_attention", nb, t, [o[0] for o in ops], [o[1] for o in ops])


def _plain_attn_kernel(q_ref, k_ref, v_ref, o_ref):
    scale = HEAD_DIM ** -0.5
    for h in range(HEADS):
        sl = slice(h * HEAD_DIM, (h + 1) * HEAD_DIM)
        p = _softmax(_dot_nt(q_ref[:, sl].astype(BF16), k_ref[:, sl].astype(BF16)) * scale)
        o = jnp.dot(p.astype(BF16), v_ref[:, sl].astype(BF16), preferred_element_type=F32)
        o_ref[:, sl] = o.astype(o_ref.dtype)


def _plain_attention(nb, t, q, k, v):
    ops = [q, k, v]
    return _attention_call(_plain_attn_kernel, "plain_attention", nb, t, [o[0] for o in ops], [o[1] for o in ops])


def _na_window_start(r):
    return jnp.clip(r - NA_KH // 2, 0, GRID_ROWS - NA_KH)


def _na_attn_kernel(q_ref, k_ref, v_ref, kc_ref, vc_ref, bias_ref, o_ref):
    scale = HEAD_DIM ** -0.5
    r = pl.program_id(1)
    start = pl.multiple_of(_na_window_start(r) * GRID_W, GRID_W)
    for h in range(HEADS):
        sl = slice(h * HEAD_DIM, (h + 1) * HEAD_DIM)
        q = q_ref[:, sl].astype(BF16)
        kw = k_ref[pl.ds(start, NA_WIN), sl].astype(BF16)
        vw = v_ref[pl.ds(start, NA_WIN), sl].astype(BF16)
        s_win = _dot_nt(q, kw) * scale + bias_ref[0, h]
        s_ctx = _dot_nt(q, kc_ref[:, sl].astype(BF16)) * scale
        m = jnp.maximum(jnp.max(s_win, axis=-1, keepdims=True), jnp.max(s_ctx, axis=-1, keepdims=True))
        p_win = jnp.exp(s_win - m)
        p_ctx = jnp.exp(s_ctx - m)
        den = jnp.sum(p_win, axis=-1, keepdims=True) + jnp.sum(p_ctx, axis=-1, keepdims=True)
        o = (jnp.dot(p_win.astype(BF16), vw, preferred_element_type=F32)
             + jnp.dot(p_ctx.astype(BF16), vc_ref[:, sl].astype(BF16), preferred_element_type=F32))
        o_ref[:, sl] = (o / den).astype(o_ref.dtype)


NA_NDC = 2 * NA_KW


def _na_bias_kernel(r_ref, e_ref, o_ref):
    o_ref[0] = jnp.dot(r_ref[0], e_ref[...], preferred_element_type=F32, precision=lax.Precision.HIGHEST)


def _na_bias_tables(rpb):
    cols = np.arange(GRID_W)
    cs = np.clip(cols - NA_KW // 2, 0, GRID_W - NA_KW)
    col_in = (cols[None, :] >= cs[:, None]) & (cols[None, :] < cs[:, None] + NA_KW)
    dc_idx = np.clip(cols[None, :] - cols[:, None] + NA_KW - 1, 0, 2 * NA_KW - 2)
    mask = np.broadcast_to(col_in[:, None, :], (GRID_W, NA_KH, GRID_W)).reshape(GRID_W, NA_WIN)
    onehot = (np.arange(NA_NDC)[:, None] == dc_idx.reshape(1, -1)).astype(np.float32)
    rp = jnp.pad(rpb, ((0, 0), (0, 0), (0, NA_NDC - rpb.shape[2])))
    rows = jnp.stack([rp[:, NA_KH - 1 - d:2 * NA_KH - 1 - d, :] for d in range(NA_KH)])
    rows = rows.reshape(NA_KH, HEADS * NA_KH, NA_NDC)
    nqk = GRID_W * GRID_W
    b = pl.pallas_call(
        _na_bias_kernel,
        grid=(NA_KH,),
        in_specs=[pl.BlockSpec((1, HEADS * NA_KH, NA_NDC), lambda d: (d, 0, 0)),
                  pl.BlockSpec((NA_NDC, nqk), lambda d: (0, 0))],
        out_specs=pl.BlockSpec((1, HEADS * NA_KH, nqk), lambda d: (d, 0, 0)),
        out_shape=jax.ShapeDtypeStruct((NA_KH, HEADS * NA_KH, nqk), F32),
        compiler_params=_params("parallel"),
        name="na_bias",
    )(rows, jnp.asarray(onehot))
    b = b.reshape(NA_KH, HEADS, NA_KH, GRID_W, GRID_W).transpose(0, 1, 3, 2, 4).reshape(NA_KH, HEADS, GRID_W, NA_WIN)
    return jnp.where(mask[None, None], b, NEG_INF)


def _na_attention(z, kc, vc, bias):
    w = ATT_W
    row0 = z.shape[0] - N_SMP

    def all_rows(n, col, base):
        return pl.BlockSpec((n, w), lambda i, r: (base // n + i, col))

    return pl.pallas_call(
        _na_attn_kernel,
        grid=(DEC_BATCH, GRID_ROWS),
        in_specs=[pl.BlockSpec((GRID_W, w), lambda i, r: (row0 // GRID_W + i * GRID_ROWS + r, 0)),
                  all_rows(DEC_SEQ, 1, row0), all_rows(DEC_SEQ, 2, row0),
                  all_rows(PAST_LEN, 0, 0), all_rows(PAST_LEN, 0, 0),
                  pl.BlockSpec((1, HEADS, GRID_W, NA_WIN), lambda i, r: (r - _na_window_start(r), 0, 0, 0))],
        out_specs=pl.BlockSpec((GRID_W, w), lambda i, r: (i * GRID_ROWS + r, 0)),
        out_shape=jax.ShapeDtypeStruct((N_SMP, w), BF16),
        compiler_params=_params("parallel", "parallel"),
        name="na_attention",
    )(z, z, z, kc, vc, bias)


CONV_CW = 384


CONV_ROWS = 1024


def _conv3_kernel(x_ref, w_ref, o_ref, *, seq):
    x = x_ref[...]
    t = x.shape[0]
    pos = lax.broadcasted_iota(jnp.int32, x.shape, 0) % seq
    prev = jnp.where(pos == 0, 0.0, pltpu.roll(x, 1, 0))
    nxt = jnp.where(pos == seq - 1, 0.0, pltpu.roll(x, t - 1, 0))
    w = w_ref[...]
    o_ref[...] = prev * w[0:1] + x * w[1:2] + nxt * w[2:3]


def _conv3(z, conv_w, seq, row0, n_seq):
    col0 = (3 * D_WIDTH) // CONV_CW
    blk0 = row0 // CONV_ROWS
    n_rows = n_seq * seq
    return pl.pallas_call(
        functools.partial(_conv3_kernel, seq=seq),
        grid=(n_rows // CONV_ROWS, D_IN // CONV_CW),
        in_specs=[pl.BlockSpec((CONV_ROWS, CONV_CW), lambda i, j: (blk0 + i, col0 + j)),
                  pl.BlockSpec((3, CONV_CW), lambda i, j: (0, j))],
        out_specs=pl.BlockSpec((CONV_ROWS, CONV_CW), lambda i, j: (i, j)),
        out_shape=jax.ShapeDtypeStruct((n_rows, D_IN), F32),
        compiler_params=_params("parallel", "parallel"),
        name="conv3",
    )(z, conv_w)


def _dot_f32(a, b):
    return jnp.dot(a, b, preferred_element_type=F32, precision=lax.Precision.HIGHEST)


def _d_prep_kernel(x_ref, w0_ref, w2_ref, a0_ref, a2_ref, g2_ref, dec_ref, a_ref, g_ref):
    x = x_ref[...]
    xw = x[:, 0:2 * D_LORA]
    xa = x[:, 2 * D_LORA:4 * D_LORA]
    xg = x[:, 4 * D_LORA:]
    for d in range(2):
        u = w0_ref[d] + _bdot(jnp.tanh(xw[:, d * D_LORA:(d + 1) * D_LORA]), w2_ref[d])
        nu = -u
        softplus = jnp.maximum(nu, 0.0) + jnp.log1p(jnp.exp(-jnp.abs(nu)))
        wlog = -softplus - 0.5
        dec_ref[d] = -jnp.exp(wlog)
        av = a0_ref[d] + _bdot(xa[:, d * D_LORA:(d + 1) * D_LORA], a2_ref[d])
        a_ref[d] = 1.0 / (1.0 + jnp.exp(-av))
    g_ref[...] = _bdot(1.0 / (1.0 + jnp.exp(-xg)), g2_ref[...])


def _d_prep(zc, w0, w2, a0, a2, g2, tm=512):
    n = zc.shape[0]
    cb = (3 * D_WIDTH) // CONV_CW
    return pl.pallas_call(
        _d_prep_kernel,
        grid=(n // tm,),
        in_specs=[pl.BlockSpec((tm, CONV_CW), lambda i: (i, cb)),
                  pl.BlockSpec((2, 1, D_WIDTH), lambda i: (0, 0, 0)),
                  pl.BlockSpec((2, D_LORA, D_WIDTH), lambda i: (0, 0, 0)),
                  pl.BlockSpec((2, 1, D_WIDTH), lambda i: (0, 0, 0)),
                  pl.BlockSpec((2, D_LORA, D_WIDTH), lambda i: (0, 0, 0)),
                  pl.BlockSpec((D_GATE_LORA, D_WIDTH), lambda i: (0, 0))],
        out_specs=[pl.BlockSpec((2, tm, D_WIDTH), lambda i: (0, i, 0)),
                   pl.BlockSpec((2, tm, D_WIDTH), lambda i: (0, i, 0)),
                   pl.BlockSpec((tm, D_WIDTH), lambda i: (i, 0))],
        out_shape=[jax.ShapeDtypeStruct((2, n, D_WIDTH), F32),
                   jax.ShapeDtypeStruct((2, n, D_WIDTH), F32),
                   jax.ShapeDtypeStruct((n, D_WIDTH), F32)],
        compiler_params=_params("parallel"),
        name="d_prep",
    )(zc, w0.reshape(2, 1, D_WIDTH), w2, a0.reshape(2, 1, D_WIDTH), a2, g2)


RW_C = 64
RW_PP = 8


def _bdot(a, b):
    ah, al = _split_hi_lo(a)
    bh, bl = _split_hi_lo(b)
    dot = functools.partial(jnp.dot, preferred_element_type=F32)
    return dot(ah, bh) + dot(ah, bl) + dot(al, bh)


def _bdot_nt(a, b):
    ah, al = _split_hi_lo(a)
    bh, bl = _split_hi_lo(b)
    return _dot_nt(ah, bh) + _dot_nt(ah, bl) + _dot_nt(al, bh)


def _bdot_tn(a, b):
    return _dot_tn(a.astype(BF16), b.astype(BF16))


def _split_hi_lo(x):
    hi = x.astype(BF16)
    return hi, (x - hi.astype(F32)).astype(BF16)


def _head_ones():
    r = lax.broadcasted_iota(jnp.int32, (LANES, LANES), 0) // D_HS
    c = lax.broadcasted_iota(jnp.int32, (LANES, LANES), 1) // D_HS
    return jnp.where(r == c, 1.0, 0.0).astype(BF16)


def _rwkv_chunk_kernel(r_ref, k_ref, v_ref, lw_ref, a_ref, kkp_ref, kap_ref, s0_ref, y_ref, sfin_ref, s_ref):
    c = pl.program_id(3)
    two_c = 2 * RW_C

    @pl.when(c == 0)
    def _():
        s_ref[...] = s0_ref[...]

    sgn = jnp.where(pl.program_id(1) == 1, -1, 1)
    row = lax.broadcasted_iota(jnp.int32, (two_c, two_c), 0)
    col = lax.broadcasted_iota(jnp.int32, (two_c, two_c), 1)
    same = (row // RW_C) == (col // RW_C)
    tt, ss = row % RW_C, col % RW_C
    before = (ss - tt) * sgn < 0
    strict = same & before
    incl = same & (before | (ss == tt))
    eye = jnp.where(row == col, 1.0, 0.0)
    r64 = lax.broadcasted_iota(jnp.int32, (RW_C, RW_C), 0)
    c64 = lax.broadcasted_iota(jnp.int32, (RW_C, RW_C), 1)
    ltri = jnp.where((c64 - r64) * sgn <= 0, 1.0, 0.0)
    head0 = lax.broadcasted_iota(jnp.int32, (RW_C, LANES), 1) < D_HS
    ones_blk = _head_ones().astype(F32)

    def stack(x):
        return jnp.concatenate([jnp.where(head0, x, 0.0), jnp.where(head0, 0.0, x)], axis=0)

    pairs = range(RW_PP)
    lss = [slice(p * LANES, (p + 1) * LANES) for p in pairs]
    r = [r_ref[:, ls] for ls in lss]
    k = [k_ref[:, ls] for ls in lss]
    lw = [lw_ref[:, ls] for ls in lss]
    a = [a_ref[:, ls] for ls in lss]
    kkf = [k[p] * kkp_ref[:, lss[p]] for p in pairs]
    n2 = [_dot_f32(kkf[p] * kkf[p], ones_blk) for p in pairs]
    kk = [kkf[p] / jnp.maximum(jnp.sqrt(n2[p]), 1e-12) for p in pairs]
    cum = [_dot_f32(ltri, lw[p]) for p in pairs]
    mid = [cum[p][RW_C // 2:RW_C // 2 + 1] for p in pairs]
    cumc = [cum[p] - mid[p] for p in pairs]
    pmid = [jnp.exp(mid[p]) for p in pairs]
    pend = [jnp.exp(jnp.sum(lw[p], axis=0, keepdims=True) - mid[p]) for p in pairs]
    pinv = [jnp.exp(-cumc[p]) for p in pairs]
    ks = [stack(kk[p] * jnp.exp(cumc[p] - lw[p])) for p in pairs]
    rs = [stack(r[p] * jnp.exp(cumc[p])) for p in pairs]
    khs = [stack(k[p] * (1.0 + (a[p] - 1.0) * kap_ref[:, lss[p]]) * pinv[p]) for p in pairs]
    bhs = [stack(kk[p] * a[p] * pinv[p]) for p in pairs]
    vs = [stack(v_ref[:, ls]) for ls in lss]
    big = [_bdot_nt(jnp.concatenate([ks[p], rs[p]], axis=0), jnp.concatenate([bhs[p], khs[p]], axis=0)) for p in pairs]
    a_b = [jnp.where(strict, big[p][:two_c, :two_c], 0.0) for p in pairs]
    a_k = [jnp.where(strict, big[p][:two_c, two_c:], 0.0) for p in pairs]
    l_b = [jnp.where(incl, big[p][two_c:, :two_c], 0.0) for p in pairs]
    l_k = [jnp.where(incl, big[p][two_c:, two_c:], 0.0) for p in pairs]
    npow = [-a_b[p] for p in pairs]
    tinv = [eye + npow[p] for p in pairs]
    for _ in range(5):
        npow = [_bdot(npow[p], npow[p]) for p in pairs]
        tinv = [tinv[p] + _bdot(tinv[p], npow[p]) for p in pairs]
    w1 = [_bdot(a_k[p], vs[p]) for p in pairs]
    tku = [_bdot(tinv[p], jnp.concatenate([ks[p], w1[p]], axis=1)) for p in pairs]
    lku = [_bdot(l_b[p], tku[p]) for p in pairs]
    rtil = [rs[p] - lku[p][:, :LANES] for p in pairs]
    yv = [_bdot(l_k[p], vs[p]) - lku[p][:, LANES:] for p in pairs]
    gt = [_bdot(tku[p][:, :LANES].T, bhs[p]) for p in pairs]
    ht = [_bdot(vs[p].T, khs[p]) - _bdot(tku[p][:, LANES:].T, bhs[p]) for p in pairs]
    s = [s_ref[p] * pmid[p] for p in pairs]
    y = [_bdot_nt(rtil[p], s[p]) + yv[p] for p in pairs]
    sg = [_bdot(s[p], gt[p]) for p in pairs]
    for p in pairs:
        s_ref[p] = (s[p] - sg[p] + ht[p]) * pend[p]
        y_ref[:, lss[p]] = y[p][:RW_C] + y[p][RW_C:]

    @pl.when(c == pl.num_programs(3) - 1)
    def _():
        sfin_ref[...] = s_ref[...]


def _pair_states(s):
    zero = jnp.zeros_like(s[:, :, 0::2])
    top = jnp.concatenate([s[:, :, 0::2], zero], axis=-1)
    bot = jnp.concatenate([zero, s[:, :, 1::2]], axis=-1)
    return jnp.concatenate([top, bot], axis=-2)


def _head_states(sp):
    b = sp.shape[0]
    both = jnp.stack([sp[:, :, :, :D_HS, :D_HS], sp[:, :, :, D_HS:, D_HS:]], axis=3)
    return both.reshape(b, 2, D_HEADS, D_HS, D_HS)


def _rwkv_chunked(zc, lw, a, k_k, k_a, s0, b, t):
    nc = t // RW_C
    gw = RW_PP * LANES
    npg = D_WIDTH // gw

    def cmap(d, c):
        return jnp.where(d == 0, c, nc - 1 - c)

    def z_spec(colblk):
        return pl.BlockSpec((RW_C, gw), lambda i, d, g, c: (i * nc + cmap(d, c), colblk * npg + g))

    dir_spec = pl.BlockSpec((None, RW_C, gw), lambda i, d, g, c: (d, i * nc + cmap(d, c), g))
    par_spec = pl.BlockSpec((1, gw), lambda i, d, g, c: (0, g))
    st_spec = pl.BlockSpec((None, None, RW_PP, LANES, LANES), lambda i, d, g, c: (i, d, g, 0, 0))
    return pl.pallas_call(
        _rwkv_chunk_kernel,
        grid=(b, 2, npg, nc),
        in_specs=[z_spec(0), z_spec(1), z_spec(2), dir_spec, dir_spec, par_spec, par_spec, st_spec],
        out_specs=[dir_spec, st_spec],
        out_shape=[jax.ShapeDtypeStruct((2, b * t, D_WIDTH), F32),
                   jax.ShapeDtypeStruct((b, 2, D_WIDTH // LANES, LANES, LANES), F32)],
        scratch_shapes=[pltpu.VMEM((RW_PP, LANES, LANES), F32)],
        compiler_params=_params("parallel", "parallel", "parallel", "arbitrary"),
        name="rwkv_chunk",
    )(zc, zc, zc, lw, a, k_k.reshape(1, D_WIDTH), k_a.reshape(1, D_WIDTH), s0)


def _d_out_kernel(y_ref, r_ref, k_ref, v_ref, a_ref, g_ref, kap_ref, rkp_ref, lng_ref, lnb_ref, o_ref):
    ones_blk = _head_ones()

    def seg_sum(x):
        hi, lo = _split_hi_lo(x)
        return jnp.dot(hi, ones_blk, preferred_element_type=F32) + jnp.dot(lo, ones_blk, preferred_element_type=F32)

    for cb in range(D_WIDTH // LANES):
        ls = slice(cb * LANES, (cb + 1) * LANES)
        y = y_ref[0, :, ls] + y_ref[1, :, ls]
        mu = seg_sum(y) * (1.0 / D_HS)
        yc = y - mu
        var = seg_sum(yc * yc) * (1.0 / D_HS)
        yn = yc * lax.rsqrt(var + D_LN_EPS) * lng_ref[:, ls] + lnb_ref[:, ls]
        asum = a_ref[0, :, ls] + a_ref[1, :, ls]
        bonus = seg_sum(r_ref[:, ls] * k_ref[:, ls] * rkp_ref[:, ls] * (2.0 + (asum - 2.0) * kap_ref[:, ls]))
        o_ref[:, ls] = (yn + bonus * v_ref[:, ls]) * g_ref[:, ls]


def _d_out(y, zc, a, g, k_a, r_k, ln_g, ln_b, tm=512):
    n = g.shape[0]

    def z_spec(colblk):
        return pl.BlockSpec((tm, D_WIDTH), lambda i: (i, colblk))

    dspec = pl.BlockSpec((2, tm, D_WIDTH), lambda i: (0, i, 0))
    pspec = pl.BlockSpec((1, D_WIDTH), lambda i: (0, 0))
    return pl.pallas_call(
        _d_out_kernel,
        grid=(n // tm,),
        in_specs=[dspec, z_spec(0), z_spec(1), z_spec(2), dspec, pl.BlockSpec((tm, D_WIDTH), lambda i: (i, 0)),
                  pspec, pspec, pspec, pspec],
        out_specs=pl.BlockSpec((tm, D_WIDTH), lambda i: (i, 0)),
        out_shape=jax.ShapeDtypeStruct((n, D_WIDTH), F32),
        compiler_params=_params("parallel"),
        name="d_out",
    )(y, zc, zc, zc, a, g, k_a.reshape(1, D_WIDTH), r_k.reshape(1, D_WIDTH), ln_g.reshape(1, D_WIDTH),
      ln_b.reshape(1, D_WIDTH))


TOPK_TM = 256
BIG = float(2 ** 30)


def _extract_topk(s, labels):
    iota16 = lax.broadcasted_iota(jnp.int32, (PEER_TOPK, s.shape[1]), 0)
    rank = jnp.full(s.shape, BIG, F32)
    vals = jnp.zeros((PEER_TOPK, s.shape[1]), F32)
    for r in range(PEER_TOPK):
        m = jnp.max(s, axis=0, keepdims=True)
        idx = jnp.min(jnp.where(s == m, labels, BIG), axis=0, keepdims=True)
        hit = labels == idx
        rank = jnp.where(hit, float(r), rank)
        vals = jnp.where(iota16 == r, m, vals)
        s = jnp.where(hit, -jnp.inf, s)
    return rank, vals


_CAND_PAIRS = ([(0, b) for b in range(16)] + [(1, b) for b in range(8)] + [(2, b) for b in range(8)]
               + [(3, b) for b in range(8)] + [(4, b) for b in range(4)] + [(5, b) for b in range(4)]
               + [(6, b) for b in range(4)] + [(7, b) for b in range(4)] + [(a, 0) for a in range(8, 16)])
N_CAND = len(_CAND_PAIRS)


def _cand_labels():
    lab = np.array([a * PEER_TOPK + b for a, b in _CAND_PAIRS], np.float32)
    return jnp.asarray(np.broadcast_to(lab[:, None], (N_CAND, LANES)).copy())


def _peer_topk_kernel(q_ref, keys_ref, lab_ref, lim_ref, e1_ref, rb_ref, e2_ref):
    iota_k = lax.broadcasted_iota(jnp.int32, (PEER_NKEYS, LANES), 0).astype(F32)
    row8 = lax.broadcasted_iota(jnp.int32, (8, LANES), 0)
    labels = lab_ref[...]
    for c in range(TOPK_TM // LANES):
        cs = slice(c * LANES, (c + 1) * LANES)
        q1 = q_ref[cs, 0:PEER_HALF].astype(BF16)
        q2 = q_ref[cs, PEER_HALF:2 * PEER_HALF].astype(BF16)
        s1 = _dot_nt(keys_ref[0, 0], q1)
        s2 = _dot_nt(keys_ref[0, 1], q2)
        rank1, sv1 = _extract_topk(s1, iota_k)
        rank2, sv2 = _extract_topk(s2, iota_k)
        lo8 = sv2[0:8]
        lo4 = jnp.where(row8 < 4, lo8, pltpu.roll(lo8, 4, 0))
        cand = jnp.concatenate([
            sv1[0:1] + lo8, sv1[0:1] + sv2[8:16], sv1[1:2] + lo8, sv1[2:3] + lo8, sv1[3:4] + lo8,
            jnp.where(row8 < 4, sv1[4:5], sv1[5:6]) + lo4, jnp.where(row8 < 4, sv1[6:7], sv1[7:8]) + lo4,
            sv1[8:16] + sv2[0:1]], axis=0)
        crank, cvals = _extract_topk(cand, labels)
        z = jnp.sum(jnp.exp(cvals - cvals[0:1]), axis=0, keepdims=True)
        sel = jnp.where(crank < BIG, 1.0, 0.0)

        def count(lo, hi):
            return jnp.sum(sel[lo:hi], axis=0, keepdims=True)

        def count_half(lo, first):
            part = jnp.where((row8 < 4) if first else (row8 >= 4), sel[lo:lo + 8], 0.0)
            return jnp.sum(part, axis=0, keepdims=True)

        n_sel = [count(0, 16), count(16, 24), count(24, 32), count(32, 40),
                 count_half(40, True), count_half(40, False), count_half(48, True), count_half(48, False)]
        n_sel += [sel[56 + a:57 + a] for a in range(8)]
        lim = jnp.zeros_like(s1)
        for a in range(PEER_TOPK):
            lim = jnp.where(rank1 == float(a), n_sel[a], lim)
        lim_ref[0, :, cs] = lim
        e1_ref[0, :, cs] = jnp.exp(s1 - sv1[0:1])
        rb_ref[0, :, cs] = rank2
        e2_ref[0, :, cs] = jnp.exp(s2 - sv2[0:1]) / z


def _peer_topk(q, keys):
    n = q.shape[0]
    ospec = pl.BlockSpec((1, PEER_NKEYS, TOPK_TM), lambda i, h: (h, 0, i))
    return pl.pallas_call(
        _peer_topk_kernel,
        grid=(n // TOPK_TM, PEER_HEADS),
        in_specs=[pl.BlockSpec((TOPK_TM, 2 * PEER_HALF), lambda i, h: (i, h)),
                  pl.BlockSpec((1, 2, PEER_NKEYS, PEER_HALF), lambda i, h: (h, 0, 0, 0)),
                  pl.BlockSpec((N_CAND, LANES), lambda i, h: (0, 0))],
        out_specs=[ospec] * 4,
        out_shape=[jax.ShapeDtypeStruct((PEER_HEADS, PEER_NKEYS, n), F32)] * 4,
        compiler_params=_params("parallel", "parallel"),
        name="peer_topk",
    )(q, keys, _cand_labels())


PEER_TM = 512
PEER_TI = 8
PEER_RG = 4
PEER_JT = 16


def _gelu(x):
    return 0.5 * x * (1.0 + lax.erf(x * (2.0 ** -0.5)))


def _peer_dense_kernel(x_ref, u_ref, v_ref, lim_ref, e1_ref, rb_ref, e2_ref, res_ref, gate_ref, o_ref, w_ref):
    e = pl.program_id(1)

    @pl.when(e == 0)
    def _():
        o_ref[...] = jnp.zeros_like(o_ref)

    hid = _dot_nt(u_ref[...], x_ref[...])
    zero = jnp.zeros((), BF16)
    n_jt = PEER_NKEYS // PEER_JT
    for c in range(PEER_TM // LANES):
        cs = slice(c * LANES, (c + 1) * LANES)
        for rg in range(PEER_TI // PEER_RG):
            rows = range(rg * PEER_RG, (rg + 1) * PEER_RG)
            g = [[jnp.zeros((PEER_JT, LANES), BF16) for _ in range(n_jt)] for _ in rows]
            for h in range(PEER_HEADS):
                lim = [jnp.broadcast_to(lim_ref[h, ii:ii + 1, cs], (PEER_JT, LANES)).astype(BF16) for ii in rows]
                e1 = [jnp.broadcast_to(e1_ref[h, ii:ii + 1, cs], (PEER_JT, LANES)).astype(BF16) for ii in rows]
                for jt in range(n_jt):
                    js = slice(jt * PEER_JT, (jt + 1) * PEER_JT)
                    rb = rb_ref[h, js, cs].astype(BF16)
                    e2 = e2_ref[h, js, cs].astype(BF16)
                    for k in range(PEER_RG):
                        g[k][jt] = g[k][jt] + jnp.where(rb < lim[k], e2, zero) * e1[k]
            for k, ii in enumerate(rows):
                for jt in range(n_jt):
                    e0 = ii * PEER_NKEYS + jt * PEER_JT
                    w_ref[e0:e0 + PEER_JT, cs] = g[k][jt] * _gelu(hid[e0:e0 + PEER_JT, cs].astype(BF16))
    o_ref[...] += _dot_tn(w_ref[...], v_ref[...])

    @pl.when(e == pl.num_programs(1) - 1)
    def _():
        o_ref[...] = res_ref[...] + gate_ref[0] * o_ref[...]


def _peer_dense(xm, u, v, layer, lim, e1, rb, e2, res, gate):
    n, d = xm.shape
    n_exp = u.shape[1]
    te = PEER_TI * PEER_NKEYS
    sel_spec = pl.BlockSpec((PEER_HEADS, PEER_NKEYS, PEER_TM), lambda i, e: (0, 0, i))
    row_spec = pl.BlockSpec((PEER_HEADS, PEER_TI, PEER_TM), lambda i, e: (0, e, i))
    return pl.pallas_call(
        _peer_dense_kernel,
        grid=(n // PEER_TM, n_exp // te),
        in_specs=[pl.BlockSpec((PEER_TM, d), lambda i, e: (i, 0)),
                  pl.BlockSpec((None, te, d), lambda i, e: (layer, e, 0)),
                  pl.BlockSpec((None, te, d), lambda i, e: (layer, e, 0)),
                  row_spec, row_spec, sel_spec, sel_spec,
                  pl.BlockSpec((PEER_TM, d), lambda i, e: (i, 0)),
                  pl.BlockSpec((1, 1, d), lambda i, e: (_cond_of_tile(i, PEER_TM), 0, 0))],
        out_specs=pl.BlockSpec((PEER_TM, d), lambda i, e: (i, 0)),
        out_shape=jax.ShapeDtypeStruct((n, d), F32),
        scratch_shapes=[pltpu.VMEM((te, PEER_TM), BF16)],
        compiler_params=_params("parallel", "arbitrary"),
        name="peer_dense",
    )(xm, u, v, lim, e1, rb, e2, res, gate)


def _peer_layer(x, layer, norm_g, shift, scale, gate, wq, keys, u, v):
    xm = _modulate(x, norm_g, shift, scale)
    q = _matmul(xm, wq, layer, out_dtype=BF16)
    lim, e1, rb, e2 = _peer_topk(q, keys)
    return _peer_dense(xm, u, v, layer, lim, e1, rb, e2, x, gate)


EVEN_PAD = 4096
ODD_PAD = 6656
S_ALL = PAST_LEN + DEC_SEQ


def _even_mixer(h, w_in, layer, lam, subln_g, q_norm_g, w_uq, kv_norm_g, w_ukv, lam_init,
                cache_k, cache_v, cache_ckv, cache_kpe, rope_cos, rope_sin):
    z = _matmul(h, w_in, layer)
    ckv_n = _rmsnorm(z, kv_norm_g, col=3584 // B_KV_RANK)
    cq_n = _rmsnorm(z, q_norm_g, out_dtype=BF16, col=3072 // B_Q_RANK)
    wq3 = w_uq.reshape(B_Q_RANK, HEADS, B_NOPE + B_ROPE)
    w_uq_r = jnp.concatenate([wq3[:, :, :B_NOPE].reshape(B_Q_RANK, -1), wq3[:, :, B_NOPE:].reshape(B_Q_RANK, -1)], axis=1)
    qb = _matmul(cq_n, w_uq_r.astype(BF16), tn=512)
    new = (z[:N_CTX, 1024:2048].reshape(BATCH, SEQ, HEADS, 2 * A_QK),
           z[:N_CTX, 2048:3072].reshape(BATCH, SEQ, HEADS, HEAD_DIM),
           ckv_n[:N_CTX].reshape(BATCH, SEQ, B_KV_RANK), z[:N_CTX, 3840:3904].reshape(BATCH, SEQ, B_ROPE))

    qa_s = _rope(z, rope_cos, rope_sin, ATT_W, col=0)
    ka_s = _rope(z, rope_cos, rope_sin, ATT_W, col=1)
    qr_s = _rope(qb, rope_cos, rope_sin, HEADS * B_ROPE, col=2)
    kpe_s = _rope(z, rope_cos, rope_sin, LANES, col=3840 // LANES)[:, :B_ROPE]

    def with_ctx(cache, own, width):
        both = jnp.concatenate([cache.reshape(DEC_BATCH, PAST_LEN, width).astype(own.dtype),
                                own.reshape(DEC_BATCH, DEC_SEQ, width)], axis=1)
        return both.reshape(DEC_BATCH * S_ALL, width)

    ka_all = with_ctx(cache_k, ka_s, ATT_W)
    va_all = with_ctx(cache_v, z[N_CTX:, 2048:3072].astype(BF16), ATT_W)
    ckv_all = with_ctx(cache_ckv, ckv_n[N_CTX:], B_KV_RANK)
    kpe_all = with_ctx(cache_kpe, kpe_s, B_ROPE)

    wkv3 = w_ukv.reshape(B_KV_RANK, HEADS, B_NOPE + HEAD_DIM)
    w_ukv_r = jnp.concatenate([wkv3[:, :, :B_NOPE].reshape(B_KV_RANK, -1), wkv3[:, :, B_NOPE:].reshape(B_KV_RANK, -1)], axis=1)
    n_skv = DEC_BATCH * S_ALL
    kv = _matmul(jnp.concatenate([ckv_all, ckv_n[:N_CTX]], axis=0), w_ukv_r.astype(BF16), out_dtype=BF16)

    oa_c = _diff_attention(BATCH, SEQ, (z, _q_rows(SEQ, ATT_W, 0, 0)), (z, _kv_rows(SEQ, ATT_W, 0, 1)),
                           (z, _kv_rows(SEQ, ATT_W, 0, 2)), lam, subln_g, lam_init)
    oa_s = _diff_attention(DEC_BATCH, DEC_SEQ, (qa_s, _q_rows(DEC_SEQ, ATT_W)), (ka_all, _kv_rows(S_ALL, ATT_W)),
                           (va_all, _kv_rows(S_ALL, ATT_W)), lam, subln_g, lam_init)
    ob_c = _mla_attention(BATCH, SEQ, (qb, _q_rows(SEQ, ATT_W, 0, 0)), (qb, _q_rows(SEQ, HEADS * B_ROPE, 0, 2)),
                          (kv, _kv_rows(SEQ, ATT_W, n_skv, 0)), (z, _kv_rows(SEQ, LANES, 0, 3840 // LANES)),
                          (kv, _kv_rows(SEQ, ATT_W, n_skv, 1)))
    ob_s = _mla_attention(DEC_BATCH, DEC_SEQ, (qb, _q_rows(DEC_SEQ, ATT_W, N_CTX, 0)),
                          (qr_s, _q_rows(DEC_SEQ, HEADS * B_ROPE)), (kv, _kv_rows(S_ALL, ATT_W, 0, 0)),
                          (kpe_all, _kv_rows(S_ALL, B_ROPE)), (kv, _kv_rows(S_ALL, ATT_W, 0, 1)))
    return (oa_c, oa_s, ob_c, ob_s), new


def _odd_mixer(h, w_in, layer, rpb, conv_w, w0, w2, a0, a2, g2, k_k, k_a, r_k, ln_g, ln_b,
               cache_k, cache_v, state):
    z = _matmul(h, w_in, layer)
    new_k = z[:N_CTX, 1024:2048].reshape(BATCH, SEQ, HEADS, HEAD_DIM)
    new_v = z[:N_CTX, 2048:3072].reshape(BATCH, SEQ, HEADS, HEAD_DIM)
    oc_c = _plain_attention(BATCH, SEQ, (z, _q_rows(SEQ, ATT_W, 0, 0)), (z, _kv_rows(SEQ, ATT_W, 0, 1)),
                            (z, _kv_rows(SEQ, ATT_W, 0, 2)))
    oc_s = _na_attention(z, cache_k.reshape(DEC_BATCH * PAST_LEN, ATT_W), cache_v.reshape(DEC_BATCH * PAST_LEN, ATT_W),
                         _na_bias_tables(rpb))

    zc_c = _conv3(z, conv_w, SEQ, 0, BATCH)
    zc_s = _conv3(z, conv_w, DEC_SEQ, N_CTX, DEC_BATCH)
    lw_c, a_c, g_c = _d_prep(zc_c, w0, w2, a0, a2, g2)
    lw_s, a_s, g_s = _d_prep(zc_s, w0, w2, a0, a2, g2)
    rk = r_k.reshape(-1)
    y_c, sfin = _rwkv_chunked(zc_c, lw_c, a_c, k_k, k_a,
                              jnp.zeros((BATCH, 2, D_WIDTH // LANES, LANES, LANES), F32), BATCH, SEQ)
    sfin = _head_states(sfin)
    y_s, _ = _rwkv_chunked(zc_s, lw_s, a_s, k_k, k_a, _pair_states(state), DEC_BATCH, DEC_SEQ)
    od_c = _d_out(y_c, zc_c, a_c, g_c, k_a, rk, ln_g, ln_b)
    od_s = _d_out(y_s, zc_s, a_s, g_s, k_a, rk, ln_g, ln_b)
    return (oc_c, oc_s, od_c, od_s), (new_k, new_v, sfin)


def kernel(x_prompt, x_sample, cache_a_k, cache_a_v, cache_b_ckv, cache_b_kpe, cache_c_k, cache_c_v, state_d, c, c_ctx, ada_w, ada_b, norm1_g, norm2_g, w_out, peer_wq, peer_keys, peer_u, peer_v, final_g, ab_w_in, a_lam, a_subln_g, b_q_norm_g, b_w_uq, b_kv_norm_g, b_w_ukv, cd_w_in, c_rpb, d_conv, d_w0, d_w2, d_a0, d_a2, d_g2, d_k_k, d_k_a, d_r_k, d_ln_g, d_ln_b):
    x = jnp.concatenate([x_prompt.reshape(N_CTX, D_MODEL), x_sample.reshape(N_SMP, D_MODEL)], axis=0)
    cond8 = jnp.pad(jnp.concatenate([c_ctx[None, :], c], axis=0), ((0, 8 - N_COND), (0, 0)))
    ada = _ada_all(cond8, ada_w, ada_b)
    rope_cos, rope_sin = _rope_tables()
    w_even = jnp.pad(ab_w_in, ((0, 0), (0, 0), (0, EVEN_PAD - EVEN_IN))).astype(BF16)
    w_odd = jnp.pad(cd_w_in, ((0, 0), (0, 0), (0, ODD_PAD - ODD_IN))).astype(BF16)
    w_out_b, wq_b, keys_b = w_out.astype(BF16), peer_wq.astype(BF16), peer_keys.astype(BF16)
    u_b, v_b = peer_u.astype(BF16), peer_v.astype(BF16)
    new_ak, new_av, new_bc, new_bp, new_ck, new_cv, new_sd = [], [], [], [], [], [], []
    for l in range(DEPTH):
        i = l // 2
        mods = ada[l, :N_COND].reshape(N_COND, 6, 1, D_MODEL)
        sh1, sc1, g1, sh2, sc2, g2 = (mods[:, m] for m in range(6))
        h = _modulate(x, norm1_g[l], sh1, sc1)
        if l % 2 == 0:
            lam_init = 0.8 - 0.6 * math.exp(-0.3 * l)
            o, (ak, av, bc, bp) = _even_mixer(
                h, w_even, i, a_lam[i], a_subln_g[i], b_q_norm_g[i], b_w_uq[i], b_kv_norm_g[i], b_w_ukv[i],
                lam_init, cache_a_k[:, i], cache_a_v[:, i], cache_b_ckv[:, i], cache_b_kpe[:, i], rope_cos, rope_sin)
            new_ak.append(ak)
            new_av.append(av)
            new_bc.append(bc)
            new_bp.append(bp)
        else:
            o, (ck, cv, sd) = _odd_mixer(
                h, w_odd, i, c_rpb[i], d_conv[i], d_w0[i], d_w2[i], d_a0[i], d_a2[i], d_g2[i],
                d_k_k[i], d_k_a[i], d_r_k[i], d_ln_g[i], d_ln_b[i], cache_c_k[:, i], cache_c_v[:, i], state_d[:, i])
            new_ck.append(ck)
            new_cv.append(cv)
            new_sd.append(sd)
        x = _matmul_residual(*o, w_out_b, l, x, g1)
        x = _peer_layer(x, l, norm2_g[l], sh2, sc2, g2, wq_b, keys_b[l], u_b, v_b)
    y = _rmsnorm(x, final_g)
    y_prompt = y[:N_CTX].reshape(BATCH, SEQ, D_MODEL)
    y_sample = y[N_CTX:].reshape(DEC_BATCH, DEC_SEQ, D_MODEL)
    return (y_prompt, y_sample, jnp.stack(new_ak, axis=1), jnp.stack(new_av, axis=1), jnp.stack(new_bc, axis=1),
            jnp.stack(new_bp, axis=1), jnp.stack(new_ck, axis=1), jnp.stack(new_cv, axis=1), jnp.stack(new_sd, axis=1))
```

```python
import functools
import math

import numpy as np
import jax
import jax.numpy as jnp
from jax import lax
from jax.experimental import pallas as pl
from jax.experimental.pallas import tpu as pltpu

F32 = jnp.float32
BF16 = jnp.bfloat16

D_MODEL = 2048
BATCH = 16
SEQ = 256
DEPTH = 4
DEC_BATCH = 2
DEC_SEQ = 1024
PAST_LEN = 512
GRID_W = 64
GRID_ROWS = DEC_SEQ // GRID_W
ROPE_BASE = 10000.0
RMS_EPS = 1e-6
NEG_INF = -1e30
N_CTX = BATCH * SEQ
N_SMP = DEC_BATCH * DEC_SEQ
N_TOK = N_CTX + N_SMP
N_COND = 1 + DEC_BATCH

HEADS = 8
HEAD_DIM = 128
A_QK = 64
B_Q_RANK = 512
B_KV_RANK = 256
B_NOPE = 128
B_ROPE = 64
NA_KH = 8
NA_KW = 16
NA_WIN = NA_KH * GRID_W
D_HS = 64
D_WIDTH = 1024
D_HEADS = 16
D_LORA = 64
D_GATE_LORA = 128
D_IN = 3 * D_WIDTH + 2 * D_LORA + 2 * D_LORA + D_GATE_LORA
D_LN_EPS = 64e-5
EVEN_IN = 3904
ODD_IN = 6528
PEER_HEADS = 8
PEER_NKEYS = 128
PEER_HALF = 128
PEER_TOPK = 16

LANES = 128
VMEM_LIMIT = 56 * 1024 * 1024


def _params(*sem):
    return pltpu.CompilerParams(dimension_semantics=sem, vmem_limit_bytes=VMEM_LIMIT)


def _dot_nt(a, b):
    return lax.dot_general(a, b, (((1,), (1,)), ((), ())), preferred_element_type=F32)


def _dot_tn(a, b):
    return lax.dot_general(a, b, (((0,), (0,)), ((), ())), preferred_element_type=F32)


def _cond_of_tile(i, tm):
    n_ctx_tiles = N_CTX // tm
    per_batch = DEC_SEQ // tm
    return jnp.where(i < n_ctx_tiles, 0, 1 + (i - n_ctx_tiles) // per_batch)


def _ada_kernel(c_ref, w_ref, b_ref, o_ref):
    c = c_ref[...]
    sc = c * (1.0 / (1.0 + jnp.exp(-c)))
    o_ref[0] = jnp.dot(sc.astype(BF16), w_ref[0].astype(BF16), preferred_element_type=F32) + b_ref[0]


def _ada_all(cond8, ada_w, ada_b):
    tn = 1024
    n_out = ada_w.shape[-1]
    return pl.pallas_call(
        _ada_kernel,
        grid=(DEPTH, n_out // tn),
        in_specs=[pl.BlockSpec((8, D_MODEL), lambda l, j: (0, 0)),
                  pl.BlockSpec((1, D_MODEL, tn), lambda l, j: (l, 0, j)),
                  pl.BlockSpec((1, 1, tn), lambda l, j: (l, 0, j))],
        out_specs=pl.BlockSpec((1, 8, tn), lambda l, j: (l, 0, j)),
        out_shape=jax.ShapeDtypeStruct((DEPTH, 8, n_out), F32),
        compiler_params=_params("parallel", "parallel"),
        name="ada",
    )(cond8, ada_w, ada_b.reshape(DEPTH, 1, n_out))


def _modulate_kernel(x_ref, g_ref, sh_ref, sc_ref, o_ref):
    x = x_ref[...]
    y = x * lax.rsqrt(jnp.mean(x * x, axis=-1, keepdims=True) + RMS_EPS) * g_ref[...]
    o_ref[...] = (y * (1.0 + sc_ref[0]) + sh_ref[0]).astype(o_ref.dtype)


def _modulate(x, g, shift, scale, out_dtype=BF16, tm=512):
    n, d = x.shape
    return pl.pallas_call(
        _modulate_kernel,
        grid=(n // tm,),
        in_specs=[pl.BlockSpec((tm, d), lambda i: (i, 0)),
                  pl.BlockSpec((1, d), lambda i: (0, 0)),
                  pl.BlockSpec((1, 1, d), lambda i: (_cond_of_tile(i, tm), 0, 0)),
                  pl.BlockSpec((1, 1, d), lambda i: (_cond_of_tile(i, tm), 0, 0))],
        out_specs=pl.BlockSpec((tm, d), lambda i: (i, 0)),
        out_shape=jax.ShapeDtypeStruct((n, d), out_dtype),
        compiler_params=_params("parallel"),
        name="modulate",
    )(x, g.reshape(1, d), shift, scale)


def _rmsnorm_kernel(x_ref, g_ref, o_ref):
    x = x_ref[...]
    y = x * lax.rsqrt(jnp.mean(x * x, axis=-1, keepdims=True) + RMS_EPS) * g_ref[...]
    o_ref[...] = y.astype(o_ref.dtype)


def _rmsnorm(x, g, out_dtype=F32, tm=512, col=0):
    n = x.shape[0]
    d = g.shape[-1]
    return pl.pallas_call(
        _rmsnorm_kernel,
        grid=(n // tm,),
        in_specs=[pl.BlockSpec((tm, d), lambda i: (i, col)),
                  pl.BlockSpec((1, d), lambda i: (0, 0))],
        out_specs=pl.BlockSpec((tm, d), lambda i: (i, 0)),
        out_shape=jax.ShapeDtypeStruct((n, d), out_dtype),
        compiler_params=_params("parallel"),
        name="rmsnorm",
    )(x, g.reshape(1, d))


def _mm_kernel(x_ref, w_ref, o_ref):
    o_ref[...] = jnp.dot(x_ref[...].astype(BF16), w_ref[...], preferred_element_type=F32).astype(o_ref.dtype)


def _weight_spec(w, layer, tn):
    if w.ndim == 2:
        return pl.BlockSpec((w.shape[0], tn), lambda i, j: (0, j))
    return pl.BlockSpec((None, w.shape[1], tn), lambda i, j: (layer, 0, j))


MM_TM = 1024
MM_TN = 1024


def _matmul(x, w, layer=None, out_dtype=F32, tm=MM_TM, tn=None, x_col=0, x_width=None):
    m = x.shape[0]
    k = x_width or x.shape[1]
    n = w.shape[-1]
    if tn is None:
        tn = MM_TN if n % MM_TN == 0 else MM_TN // 2
    tn = min(tn, n)
    assert m % tm == 0 and n % tn == 0, (m, n, tm, tn)
    return pl.pallas_call(
        _mm_kernel,
        grid=(m // tm, n // tn),
        in_specs=[pl.BlockSpec((tm, k), lambda i, j: (i, x_col)),
                  _weight_spec(w, layer, tn)],
        out_specs=pl.BlockSpec((tm, tn), lambda i, j: (i, j)),
        out_shape=jax.ShapeDtypeStruct((m, n), out_dtype),
        compiler_params=_params("parallel", "parallel"),
        name="matmul",
    )(x, w)


def _mm_res_kernel(lc_ref, ls_ref, rc_ref, rs_ref, w_ref, res_ref, gate_ref, o_ref, *, n_ctx_tiles):
    half = w_ref.shape[0] // 2

    def emit(l_ref, r_ref):
        acc = jnp.dot(l_ref[...].astype(BF16), w_ref[0:half, :], preferred_element_type=F32)
        acc = acc + jnp.dot(r_ref[...].astype(BF16), w_ref[half:, :], preferred_element_type=F32)
        o_ref[...] = res_ref[...] + gate_ref[0] * acc

    @pl.when(pl.program_id(0) < n_ctx_tiles)
    def _():
        emit(lc_ref, rc_ref)

    @pl.when(pl.program_id(0) >= n_ctx_tiles)
    def _():
        emit(ls_ref, rs_ref)


def _matmul_residual(left_c, left_s, right_c, right_s, w, layer, res, gate, tm=MM_TM, tn=MM_TN // 2):
    m = res.shape[0]
    n = w.shape[-1]
    kh = left_c.shape[1]
    nct = left_c.shape[0] // tm
    ctx_spec = pl.BlockSpec((tm, kh), lambda i, j: (jnp.minimum(i, nct - 1), 0))
    smp_spec = pl.BlockSpec((tm, kh), lambda i, j: (jnp.maximum(i - nct, 0), 0))
    return pl.pallas_call(
        functools.partial(_mm_res_kernel, n_ctx_tiles=nct),
        grid=(m // tm, n // tn),
        in_specs=[ctx_spec, smp_spec, ctx_spec, smp_spec,
                  _weight_spec(w, layer, tn),
                  pl.BlockSpec((tm, tn), lambda i, j: (i, j)),
                  pl.BlockSpec((1, 1, tn), lambda i, j: (_cond_of_tile(i, tm), 0, j))],
        out_specs=pl.BlockSpec((tm, tn), lambda i, j: (i, j)),
        out_shape=jax.ShapeDtypeStruct((m, n), F32),
        compiler_params=_params("parallel", "parallel"),
        name="matmul_residual",
    )(left_c, left_s, right_c, right_s, w, res, gate)


def _rope_tables():
    nf = 16
    inv = ROPE_BASE ** (-np.arange(nf, dtype=np.float64) / nf)
    t = np.arange(DEC_SEQ)
    rows, cols = t // GRID_W, t % GRID_W
    lane = np.arange(64)
    pos = np.where(lane[None, :] < 32, rows[:, None], cols[:, None]).astype(np.float32)
    ang = (pos * inv[lane % nf][None, :].astype(np.float32)).astype(np.float32)
    first = (lane % 32) < nf
    cos = np.cos(ang.astype(np.float64))
    sin = np.sin(ang.astype(np.float64)) * np.where(first, -1.0, 1.0)[None, :]
    cos = np.tile(cos, (1, 2)).astype(np.float32)
    sin = np.tile(sin, (1, 2)).astype(np.float32)
    return jnp.asarray(cos), jnp.asarray(sin)


def _rope_kernel(x_ref, cos_ref, sin_ref, o_ref):
    cos = cos_ref[...]
    sin = sin_ref[...]
    lane = lax.broadcasted_iota(jnp.int32, cos.shape, 1)
    first = (lane % 32) < 16
    for c in range(x_ref.shape[1] // LANES):
        x = x_ref[:, c * LANES:(c + 1) * LANES].astype(F32)
        partner = jnp.where(first, pltpu.roll(x, LANES - 16, 1), pltpu.roll(x, 16, 1))
        o_ref[:, c * LANES:(c + 1) * LANES] = (x * cos + partner * sin).astype(o_ref.dtype)


def _rope(x, cos, sin, w, col=0, out_dtype=BF16, tm=256):
    n = N_SMP
    row0 = (x.shape[0] - N_SMP) // tm
    per = DEC_SEQ // tm
    return pl.pallas_call(
        _rope_kernel,
        grid=(n // tm,),
        in_specs=[pl.BlockSpec((tm, w), lambda i: (row0 + i, col)),
                  pl.BlockSpec((tm, LANES), lambda i: (i % per, 0)),
                  pl.BlockSpec((tm, LANES), lambda i: (i % per, 0))],
        out_specs=pl.BlockSpec((tm, w), lambda i: (i, 0)),
        out_shape=jax.ShapeDtypeStruct((n, w), out_dtype),
        compiler_params=_params("parallel"),
        name="rope",
    )(x, cos, sin)


def _softmax(s):
    p = jnp.exp(s - jnp.max(s, axis=-1, keepdims=True))
    return p / jnp.sum(p, axis=-1, keepdims=True)


def _diff_attn_kernel(lam_ref, g_ref, q_ref, k_ref, v_ref, o_ref, *, lam_init):
    lam = lam_ref[...]
    l1 = jnp.sum(jnp.sum(lam[0:1] * lam[1:2], axis=-1, keepdims=True), axis=0, keepdims=True)
    l2 = jnp.sum(jnp.sum(lam[2:3] * lam[3:4], axis=-1, keepdims=True), axis=0, keepdims=True)
    lam_val = jnp.exp(l1) - jnp.exp(l2) + lam_init
    scale = A_QK ** -0.5
    for h in range(HEADS):
        sl = slice(h * HEAD_DIM, (h + 1) * HEAD_DIM)
        q = q_ref[:, sl].astype(BF16)
        k = k_ref[:, sl].astype(BF16)
        p1 = _softmax(_dot_nt(q[:, :A_QK], k[:, :A_QK]) * scale)
        p2 = _softmax(_dot_nt(q[:, A_QK:], k[:, A_QK:]) * scale)
        a = p1 - lam_val * p2
        o = jnp.dot(a.astype(BF16), v_ref[:, sl].astype(BF16), preferred_element_type=F32)
        o = o * lax.rsqrt(jnp.mean(o * o, axis=-1, keepdims=True) + RMS_EPS) * g_ref[...]
        o_ref[:, sl] = (o * (1.0 - lam_init)).astype(o_ref.dtype)


ATT_TQ = 256
ATT_W = HEADS * HEAD_DIM


def _q_rows(t, width, row0=0, col=0):
    per, off = t // ATT_TQ, row0 // ATT_TQ
    return pl.BlockSpec((ATT_TQ, width), lambda b, j: (off + b * per + j, col))


def _kv_rows(s, width, row0=0, col=0):
    off = row0 // s
    return pl.BlockSpec((s, width), lambda b, j: (off + b, col))


def _attention_call(kernel, name, nb, t, operands, specs):
    return pl.pallas_call(
        kernel,
        grid=(nb, t // ATT_TQ),
        in_specs=specs,
        out_specs=_q_rows(t, ATT_W),
        out_shape=jax.ShapeDtypeStruct((nb * t, ATT_W), BF16),
        compiler_params=_params("parallel", "parallel"),
        name=name,
    )(*operands)


def _diff_attention(nb, t, q, k, v, lam, subln_g, lam_init):
    const = [pl.BlockSpec((4, A_QK), lambda i, j: (0, 0)), pl.BlockSpec((1, HEAD_DIM), lambda i, j: (0, 0))]
    return _attention_call(functools.partial(_diff_attn_kernel, lam_init=lam_init), "diff_attention", nb, t,
                           [lam, subln_g.reshape(1, HEAD_DIM), q[0], k[0], v[0]], const + [q[1], k[1], v[1]])


def _mla_attn_kernel(qn_ref, qr_ref, kn_ref, kr_ref, v_ref, o_ref):
    scale = (B_NOPE + B_ROPE) ** -0.5
    kr = kr_ref[:, 0:B_ROPE].astype(BF16)
    for h in range(HEADS):
        sl = slice(h * HEAD_DIM, (h + 1) * HEAD_DIM)
        s = _dot_nt(qn_ref[:, sl].astype(BF16), kn_ref[:, sl])
        s = s + _dot_nt(qr_ref[:, h * B_ROPE:(h + 1) * B_ROPE].astype(BF16), kr)
        p = _softmax(s * scale)
        o = jnp.dot(p.astype(BF16), v_ref[:, sl], preferred_element_type=F32)
        o_ref[:, sl] = o.astype(o_ref.dtype)


def _mla_attention(nb, t, qn, qr, kn, kr, v):
    ops = [qn, qr, kn, kr, v]
    return _attention_call(_mla_attn_kernel, "mla_attention", nb, t, [o[0] for o in ops], [o[1] for o in ops])


def _plain_attn_kernel(q_ref, k_ref, v_ref, o_ref):
    scale = HEAD_DIM ** -0.5
    for h in range(HEADS):
        sl = slice(h * HEAD_DIM, (h + 1) * HEAD_DIM)
        p = _softmax(_dot_nt(q_ref[:, sl].astype(BF16), k_ref[:, sl].astype(BF16)) * scale)
        o = jnp.dot(p.astype(BF16), v_ref[:, sl].astype(BF16), preferred_element_type=F32)
        o_ref[:, sl] = o.astype(o_ref.dtype)


def _plain_attention(nb, t, q, k, v):
    ops = [q, k, v]
    return _attention_call(_plain_attn_kernel, "plain_attention", nb, t, [o[0] for o in ops], [o[1] for o in ops])


def _na_window_start(r):
    return jnp.clip(r - NA_KH // 2, 0, GRID_ROWS - NA_KH)


def _na_attn_kernel(q_ref, k_ref, v_ref, kc_ref, vc_ref, bias_ref, o_ref):
    scale = HEAD_DIM ** -0.5
    r = pl.program_id(1)
    start = pl.multiple_of(_na_window_start(r) * GRID_W, GRID_W)
    for h in range(HEADS):
        sl = slice(h * HEAD_DIM, (h + 1) * HEAD_DIM)
        q = q_ref[:, sl].astype(BF16)
        kw = k_ref[pl.ds(start, NA_WIN), sl].astype(BF16)
        vw = v_ref[pl.ds(start, NA_WIN), sl].astype(BF16)
        s_win = _dot_nt(q, kw) * scale + bias_ref[0, h]
        s_ctx = _dot_nt(q, kc_ref[:, sl].astype(BF16)) * scale
        m = jnp.maximum(jnp.max(s_win, axis=-1, keepdims=True), jnp.max(s_ctx, axis=-1, keepdims=True))
        p_win = jnp.exp(s_win - m)
        p_ctx = jnp.exp(s_ctx - m)
        den = jnp.sum(p_win, axis=-1, keepdims=True) + jnp.sum(p_ctx, axis=-1, keepdims=True)
        o = (jnp.dot(p_win.astype(BF16), vw, preferred_element_type=F32)
             + jnp.dot(p_ctx.astype(BF16), vc_ref[:, sl].astype(BF16), preferred_element_type=F32))
        o_ref[:, sl] = (o / den).astype(o_ref.dtype)


NA_NDC = 2 * NA_KW


def _na_bias_kernel(r_ref, e_ref, o_ref):
    o_ref[0] = jnp.dot(r_ref[0], e_ref[...], preferred_element_type=F32, precision=lax.Precision.HIGHEST)


def _na_bias_tables(rpb):
    cols = np.arange(GRID_W)
    cs = np.clip(cols - NA_KW // 2, 0, GRID_W - NA_KW)
    col_in = (cols[None, :] >= cs[:, None]) & (cols[None, :] < cs[:, None] + NA_KW)
    dc_idx = np.clip(cols[None, :] - cols[:, None] + NA_KW - 1, 0, 2 * NA_KW - 2)
    mask = np.broadcast_to(col_in[:, None, :], (GRID_W, NA_KH, GRID_W)).reshape(GRID_W, NA_WIN)
    onehot = (np.arange(NA_NDC)[:, None] == dc_idx.reshape(1, -1)).astype(np.float32)
    rp = jnp.pad(rpb, ((0, 0), (0, 0), (0, NA_NDC - rpb.shape[2])))
    rows = jnp.stack([rp[:, NA_KH - 1 - d:2 * NA_KH - 1 - d, :] for d in range(NA_KH)])
    rows = rows.reshape(NA_KH, HEADS * NA_KH, NA_NDC)
    nqk = GRID_W * GRID_W
    b = pl.pallas_call(
        _na_bias_kernel,
        grid=(NA_KH,),
        in_specs=[pl.BlockSpec((1, HEADS * NA_KH, NA_NDC), lambda d: (d, 0, 0)),
                  pl.BlockSpec((NA_NDC, nqk), lambda d: (0, 0))],
        out_specs=pl.BlockSpec((1, HEADS * NA_KH, nqk), lambda d: (d, 0, 0)),
        out_shape=jax.ShapeDtypeStruct((NA_KH, HEADS * NA_KH, nqk), F32),
        compiler_params=_params("parallel"),
        name="na_bias",
    )(rows, jnp.asarray(onehot))
    b = b.reshape(NA_KH, HEADS, NA_KH, GRID_W, GRID_W).transpose(0, 1, 3, 2, 4).reshape(NA_KH, HEADS, GRID_W, NA_WIN)
    return jnp.where(mask[None, None], b, NEG_INF)


def _na_attention(z, kc, vc, bias):
    w = ATT_W
    row0 = z.shape[0] - N_SMP

    def all_rows(n, col, base):
        return pl.BlockSpec((n, w), lambda i, r: (base // n + i, col))

    return pl.pallas_call(
        _na_attn_kernel,
        grid=(DEC_BATCH, GRID_ROWS),
        in_specs=[pl.BlockSpec((GRID_W, w), lambda i, r: (row0 // GRID_W + i * GRID_ROWS + r, 0)),
                  all_rows(DEC_SEQ, 1, row0), all_rows(DEC_SEQ, 2, row0),
                  all_rows(PAST_LEN, 0, 0), all_rows(PAST_LEN, 0, 0),
                  pl.BlockSpec((1, HEADS, GRID_W, NA_WIN), lambda i, r: (r - _na_window_start(r), 0, 0, 0))],
        out_specs=pl.BlockSpec((GRID_W, w), lambda i, r: (i * GRID_ROWS + r, 0)),
        out_shape=jax.ShapeDtypeStruct((N_SMP, w), BF16),
        compiler_params=_params("parallel", "parallel"),
        name="na_attention",
    )(z, z, z, kc, vc, bias)


CONV_CW = 384


CONV_ROWS = 1024


def _conv3_kernel(x_ref, w_ref, o_ref, *, seq):
    x = x_ref[...]
    t = x.shape[0]
    pos = lax.broadcasted_iota(jnp.int32, x.shape, 0) % seq
    prev = jnp.where(pos == 0, 0.0, pltpu.roll(x, 1, 0))
    nxt = jnp.where(pos == seq - 1, 0.0, pltpu.roll(x, t - 1, 0))
    w = w_ref[...]
    o_ref[...] = prev * w[0:1] + x * w[1:2] + nxt * w[2:3]


def _conv3(z, conv_w, seq, row0, n_seq):
    col0 = (3 * D_WIDTH) // CONV_CW
    blk0 = row0 // CONV_ROWS
    n_rows = n_seq * seq
    return pl.pallas_call(
        functools.partial(_conv3_kernel, seq=seq),
        grid=(n_rows // CONV_ROWS, D_IN // CONV_CW),
        in_specs=[pl.BlockSpec((CONV_ROWS, CONV_CW), lambda i, j: (blk0 + i, col0 + j)),
                  pl.BlockSpec((3, CONV_CW), lambda i, j: (0, j))],
        out_specs=pl.BlockSpec((CONV_ROWS, CONV_CW), lambda i, j: (i, j)),
        out_shape=jax.ShapeDtypeStruct((n_rows, D_IN), F32),
        compiler_params=_params("parallel", "parallel"),
        name="conv3",
    )(z, conv_w)


def _dot_f32(a, b):
    return jnp.dot(a, b, preferred_element_type=F32, precision=lax.Precision.HIGHEST)


def _d_prep_kernel(x_ref, w0_ref, w2_ref, a0_ref, a2_ref, g2_ref, dec_ref, a_ref, g_ref):
    x = x_ref[...]
    xw = x[:, 0:2 * D_LORA]
    xa = x[:, 2 * D_LORA:4 * D_LORA]
    xg = x[:, 4 * D_LORA:]
    for d in range(2):
        u = w0_ref[d] + _bdot(jnp.tanh(xw[:, d * D_LORA:(d + 1) * D_LORA]), w2_ref[d])
        nu = -u
        softplus = jnp.maximum(nu, 0.0) + jnp.log1p(jnp.exp(-jnp.abs(nu)))
        wlog = -softplus - 0.5
        dec_ref[d] = -jnp.exp(wlog)
        av = a0_ref[d] + _bdot(xa[:, d * D_LORA:(d + 1) * D_LORA], a2_ref[d])
        a_ref[d] = 1.0 / (1.0 + jnp.exp(-av))
    g_ref[...] = _bdot(1.0 / (1.0 + jnp.exp(-xg)), g2_ref[...])


def _d_prep(zc, w0, w2, a0, a2, g2, tm=512):
    n = zc.shape[0]
    cb = (3 * D_WIDTH) // CONV_CW
    return pl.pallas_call(
        _d_prep_kernel,
        grid=(n // tm,),
        in_specs=[pl.BlockSpec((tm, CONV_CW), lambda i: (i, cb)),
                  pl.BlockSpec((2, 1, D_WIDTH), lambda i: (0, 0, 0)),
                  pl.BlockSpec((2, D_LORA, D_WIDTH), lambda i: (0, 0, 0)),
                  pl.BlockSpec((2, 1, D_WIDTH), lambda i: (0, 0, 0)),
                  pl.BlockSpec((2, D_LORA, D_WIDTH), lambda i: (0, 0, 0)),
                  pl.BlockSpec((D_GATE_LORA, D_WIDTH), lambda i: (0, 0))],
        out_specs=[pl.BlockSpec((2, tm, D_WIDTH), lambda i: (0, i, 0)),
                   pl.BlockSpec((2, tm, D_WIDTH), lambda i: (0, i, 0)),
                   pl.BlockSpec((tm, D_WIDTH), lambda i: (i, 0))],
        out_shape=[jax.ShapeDtypeStruct((2, n, D_WIDTH), F32),
                   jax.ShapeDtypeStruct((2, n, D_WIDTH), F32),
                   jax.ShapeDtypeStruct((n, D_WIDTH), F32)],
        compiler_params=_params("parallel"),
        name="d_prep",
    )(zc, w0.reshape(2, 1, D_WIDTH), w2, a0.reshape(2, 1, D_WIDTH), a2, g2)


RW_C = 64
RW_PP = 8


def _bdot(a, b):
    ah, al = _split_hi_lo(a)
    bh, bl = _split_hi_lo(b)
    dot = functools.partial(jnp.dot, preferred_element_type=F32)
    return dot(ah, bh) + dot(ah, bl) + dot(al, bh)


def _bdot2(a, b):
    ah, al = _split_hi_lo(a)
    dot = functools.partial(jnp.dot, preferred_element_type=F32)
    return dot(ah, b) + dot(al, b)


def _bdot_nt(a, b):
    ah, al = _split_hi_lo(a)
    bh, bl = _split_hi_lo(b)
    return _dot_nt(ah, bh) + _dot_nt(ah, bl) + _dot_nt(al, bh)


def _bdot_tn(a, b):
    return _dot_tn(a.astype(BF16), b.astype(BF16))


def _split_hi_lo(x):
    hi = x.astype(BF16)
    return hi, (x - hi.astype(F32)).astype(BF16)


def _head_ones():
    r = lax.broadcasted_iota(jnp.int32, (LANES, LANES), 0) // D_HS
    c = lax.broadcasted_iota(jnp.int32, (LANES, LANES), 1) // D_HS
    return jnp.where(r == c, 1.0, 0.0).astype(BF16)


def _rwkv_chunk_kernel(r_ref, k_ref, v_ref, lw_ref, a_ref, kkp_ref, kap_ref, s0_ref, y_ref, sfin_ref, s_ref):
    c = pl.program_id(3)
    two_c = 2 * RW_C

    @pl.when(c == 0)
    def _():
        s_ref[...] = s0_ref[...]

    sgn = jnp.where(pl.program_id(1) == 1, -1, 1)
    row = lax.broadcasted_iota(jnp.int32, (two_c, two_c), 0)
    col = lax.broadcasted_iota(jnp.int32, (two_c, two_c), 1)
    same = (row // RW_C) == (col // RW_C)
    tt, ss = row % RW_C, col % RW_C
    before = (ss - tt) * sgn < 0
    strict = same & before
    incl = same & (before | (ss == tt))
    eye = jnp.where(row == col, 1.0, 0.0)
    r64 = lax.broadcasted_iota(jnp.int32, (RW_C, RW_C), 0)
    c64 = lax.broadcasted_iota(jnp.int32, (RW_C, RW_C), 1)
    ltri = jnp.where((c64 - r64) * sgn <= 0, 1.0, 0.0)
    head0 = lax.broadcasted_iota(jnp.int32, (RW_C, LANES), 1) < D_HS
    ones_blk = _head_ones().astype(F32)

    def stack(x):
        return jnp.concatenate([jnp.where(head0, x, 0.0), jnp.where(head0, 0.0, x)], axis=0)

    pairs = range(RW_PP)
    lss = [slice(p * LANES, (p + 1) * LANES) for p in pairs]
    r = [r_ref[:, ls] for ls in lss]
    k = [k_ref[:, ls] for ls in lss]
    lw = [lw_ref[:, ls] for ls in lss]
    a = [a_ref[:, ls] for ls in lss]
    kkf = [k[p] * kkp_ref[:, lss[p]] for p in pairs]
    n2 = [_dot_f32(kkf[p] * kkf[p], ones_blk) for p in pairs]
    kk = [kkf[p] / jnp.maximum(jnp.sqrt(n2[p]), 1e-12) for p in pairs]
    cum = [_dot_f32(ltri, lw[p]) for p in pairs]
    mid = [cum[p][RW_C // 2:RW_C // 2 + 1] for p in pairs]
    cumc = [cum[p] - mid[p] for p in pairs]
    pmid = [jnp.exp(mid[p]) for p in pairs]
    pend = [jnp.exp(jnp.sum(lw[p], axis=0, keepdims=True) - mid[p]) for p in pairs]
    pinv = [jnp.exp(-cumc[p]) for p in pairs]
    ks = [stack(kk[p] * jnp.exp(cumc[p] - lw[p])).astype(BF16) for p in pairs]
    rs = [stack(r[p] * jnp.exp(cumc[p])).astype(BF16) for p in pairs]
    khs = [stack(k[p] * (1.0 + (a[p] - 1.0) * kap_ref[:, lss[p]]) * pinv[p]).astype(BF16) for p in pairs]
    bhs = [stack(kk[p] * a[p] * pinv[p]).astype(BF16) for p in pairs]
    vs = [stack(v_ref[:, ls]).astype(BF16) for ls in lss]
    big = [_dot_nt(jnp.concatenate([ks[p], rs[p]], axis=0), jnp.concatenate([bhs[p], khs[p]], axis=0)) for p in pairs]
    a_b = [jnp.where(strict, big[p][:two_c, :two_c], 0.0) for p in pairs]
    a_k = [jnp.where(strict, big[p][:two_c, two_c:], 0.0) for p in pairs]
    l_b = [jnp.where(incl, big[p][two_c:, :two_c], 0.0) for p in pairs]
    l_k = [jnp.where(incl, big[p][two_c:, two_c:], 0.0) for p in pairs]
    npow = [-a_b[p] for p in pairs]
    tinv = [eye + npow[p] for p in pairs]
    for _ in range(5):
        npow = [_bdot(npow[p], npow[p]) for p in pairs]
        tinv = [tinv[p] + _bdot(tinv[p], npow[p]) for p in pairs]
    w1 = [_bdot2(a_k[p], vs[p]) for p in pairs]
    ktil = [_bdot2(tinv[p], ks[p]) for p in pairs]
    uv = [_bdot(tinv[p], w1[p]) for p in pairs]
    lku = [_bdot(l_b[p], jnp.concatenate([ktil[p], uv[p]], axis=1)) for p in pairs]
    rtil = [rs[p].astype(F32) - lku[p][:, :LANES] for p in pairs]
    yv = [_bdot2(l_k[p], vs[p]) - lku[p][:, LANES:] for p in pairs]
    gt = [_bdot2(ktil[p].T, bhs[p]) for p in pairs]
    ht = [jnp.dot(vs[p].astype(F32).T.astype(BF16), khs[p], preferred_element_type=F32) - _bdot2(uv[p].T, bhs[p])
          for p in pairs]
    s = [s_ref[p] * pmid[p] for p in pairs]
    y = [_bdot_nt(rtil[p], s[p]) + yv[p] for p in pairs]
    sg = [_bdot(s[p], gt[p]) for p in pairs]
    for p in pairs:
        s_ref[p] = (s[p] - sg[p] + ht[p]) * pend[p]
        y_ref[:, lss[p]] = y[p][:RW_C] + y[p][RW_C:]

    @pl.when(c == pl.num_programs(3) - 1)
    def _():
        sfin_ref[...] = s_ref[...]


def _pair_states(s):
    zero = jnp.zeros_like(s[:, :, 0::2])
    top = jnp.concatenate([s[:, :, 0::2], zero], axis=-1)
    bot = jnp.concatenate([zero, s[:, :, 1::2]], axis=-1)
    return jnp.concatenate([top, bot], axis=-2)


def _head_states(sp):
    b = sp.shape[0]
    both = jnp.stack([sp[:, :, :, :D_HS, :D_HS], sp[:, :, :, D_HS:, D_HS:]], axis=3)
    return both.reshape(b, 2, D_HEADS, D_HS, D_HS)


def _rwkv_chunked(zc, lw, a, k_k, k_a, s0, b, t):
    nc = t // RW_C
    gw = RW_PP * LANES
    npg = D_WIDTH // gw

    def cmap(d, c):
        return jnp.where(d == 0, c, nc - 1 - c)

    def z_spec(colblk):
        return pl.BlockSpec((RW_C, gw), lambda i, d, g, c: (i * nc + cmap(d, c), colblk * npg + g))

    dir_spec = pl.BlockSpec((None, RW_C, gw), lambda i, d, g, c: (d, i * nc + cmap(d, c), g))
    par_spec = pl.BlockSpec((1, gw), lambda i, d, g, c: (0, g))
    st_spec = pl.BlockSpec((None, None, RW_PP, LANES, LANES), lambda i, d, g, c: (i, d, g, 0, 0))
    return pl.pallas_call(
        _rwkv_chunk_kernel,
        grid=(b, 2, npg, nc),
        in_specs=[z_spec(0), z_spec(1), z_spec(2), dir_spec, dir_spec, par_spec, par_spec, st_spec],
        out_specs=[dir_spec, st_spec],
        out_shape=[jax.ShapeDtypeStruct((2, b * t, D_WIDTH), F32),
                   jax.ShapeDtypeStruct((b, 2, D_WIDTH // LANES, LANES, LANES), F32)],
        scratch_shapes=[pltpu.VMEM((RW_PP, LANES, LANES), F32)],
        compiler_params=_params("parallel", "parallel", "parallel", "arbitrary"),
        name="rwkv_chunk",
    )(zc, zc, zc, lw, a, k_k.reshape(1, D_WIDTH), k_a.reshape(1, D_WIDTH), s0)


def _d_out_kernel(y_ref, r_ref, k_ref, v_ref, a_ref, g_ref, kap_ref, rkp_ref, lng_ref, lnb_ref, o_ref):
    ones_blk = _head_ones()

    def seg_sum(x):
        hi, lo = _split_hi_lo(x)
        return jnp.dot(hi, ones_blk, preferred_element_type=F32) + jnp.dot(lo, ones_blk, preferred_element_type=F32)

    for cb in range(D_WIDTH // LANES):
        ls = slice(cb * LANES, (cb + 1) * LANES)
        y = y_ref[0, :, ls] + y_ref[1, :, ls]
        mu = seg_sum(y) * (1.0 / D_HS)
        yc = y - mu
        var = seg_sum(yc * yc) * (1.0 / D_HS)
        yn = yc * lax.rsqrt(var + D_LN_EPS) * lng_ref[:, ls] + lnb_ref[:, ls]
        asum = a_ref[0, :, ls] + a_ref[1, :, ls]
        bonus = seg_sum(r_ref[:, ls] * k_ref[:, ls] * rkp_ref[:, ls] * (2.0 + (asum - 2.0) * kap_ref[:, ls]))
        o_ref[:, ls] = (yn + bonus * v_ref[:, ls]) * g_ref[:, ls]


def _d_out(y, zc, a, g, k_a, r_k, ln_g, ln_b, tm=512):
    n = g.shape[0]

    def z_spec(colblk):
        return pl.BlockSpec((tm, D_WIDTH), lambda i: (i, colblk))

    dspec = pl.BlockSpec((2, tm, D_WIDTH), lambda i: (0, i, 0))
    pspec = pl.BlockSpec((1, D_WIDTH), lambda i: (0, 0))
    return pl.pallas_call(
        _d_out_kernel,
        grid=(n // tm,),
        in_specs=[dspec, z_spec(0), z_spec(1), z_spec(2), dspec, pl.BlockSpec((tm, D_WIDTH), lambda i: (i, 0)),
                  pspec, pspec, pspec, pspec],
        out_specs=pl.BlockSpec((tm, D_WIDTH), lambda i: (i, 0)),
        out_shape=jax.ShapeDtypeStruct((n, D_WIDTH), F32),
        compiler_params=_params("parallel"),
        name="d_out",
    )(y, zc, zc, zc, a, g, k_a.reshape(1, D_WIDTH), r_k.reshape(1, D_WIDTH), ln_g.reshape(1, D_WIDTH),
      ln_b.reshape(1, D_WIDTH))


TOPK_TM = 256
BIG = float(2 ** 30)


def _extract_topk(s, labels):
    iota16 = lax.broadcasted_iota(jnp.int32, (PEER_TOPK, s.shape[1]), 0)
    rank = jnp.full(s.shape, BIG, F32)
    vals = jnp.zeros((PEER_TOPK, s.shape[1]), F32)
    for r in range(PEER_TOPK):
        m = jnp.max(s, axis=0, keepdims=True)
        idx = jnp.min(jnp.where(s == m, labels, BIG), axis=0, keepdims=True)
        hit = labels == idx
        rank = jnp.where(hit, float(r), rank)
        vals = jnp.where(iota16 == r, m, vals)
        s = jnp.where(hit, -jnp.inf, s)
    return rank, vals


_CAND_PAIRS = ([(0, b) for b in range(16)] + [(1, b) for b in range(8)] + [(2, b) for b in range(8)]
               + [(3, b) for b in range(8)] + [(4, b) for b in range(4)] + [(5, b) for b in range(4)]
               + [(6, b) for b in range(4)] + [(7, b) for b in range(4)] + [(a, 0) for a in range(8, 16)])
N_CAND = len(_CAND_PAIRS)


def _cand_labels():
    lab = np.array([a * PEER_TOPK + b for a, b in _CAND_PAIRS], np.float32)
    return jnp.asarray(np.broadcast_to(lab[:, None], (N_CAND, LANES)).copy())


def _peer_topk_kernel(q_ref, keys_ref, lab_ref, lim_ref, e1_ref, rb_ref, e2_ref):
    iota_k = lax.broadcasted_iota(jnp.int32, (PEER_NKEYS, LANES), 0).astype(F32)
    row8 = lax.broadcasted_iota(jnp.int32, (8, LANES), 0)
    labels = lab_ref[...]
    for c in range(TOPK_TM // LANES):
        cs = slice(c * LANES, (c + 1) * LANES)
        q1 = q_ref[cs, 0:PEER_HALF].astype(BF16)
        q2 = q_ref[cs, PEER_HALF:2 * PEER_HALF].astype(BF16)
        s1 = _dot_nt(keys_ref[0, 0], q1)
        s2 = _dot_nt(keys_ref[0, 1], q2)
        rank1, sv1 = _extract_topk(s1, iota_k)
        rank2, sv2 = _extract_topk(s2, iota_k)
        lo8 = sv2[0:8]
        lo4 = jnp.where(row8 < 4, lo8, pltpu.roll(lo8, 4, 0))
        cand = jnp.concatenate([
            sv1[0:1] + lo8, sv1[0:1] + sv2[8:16], sv1[1:2] + lo8, sv1[2:3] + lo8, sv1[3:4] + lo8,
            jnp.where(row8 < 4, sv1[4:5], sv1[5:6]) + lo4, jnp.where(row8 < 4, sv1[6:7], sv1[7:8]) + lo4,
            sv1[8:16] + sv2[0:1]], axis=0)
        crank, cvals = _extract_topk(cand, labels)
        z = jnp.sum(jnp.exp(cvals - cvals[0:1]), axis=0, keepdims=True)
        sel = jnp.where(crank < BIG, 1.0, 0.0)

        def count(lo, hi):
            return jnp.sum(sel[lo:hi], axis=0, keepdims=True)

        def count_half(lo, first):
            part = jnp.where((row8 < 4) if first else (row8 >= 4), sel[lo:lo + 8], 0.0)
            return jnp.sum(part, axis=0, keepdims=True)

        n_sel = [count(0, 16), count(16, 24), count(24, 32), count(32, 40),
                 count_half(40, True), count_half(40, False), count_half(48, True), count_half(48, False)]
        n_sel += [sel[56 + a:57 + a] for a in range(8)]
        lim = jnp.zeros_like(s1)
        for a in range(PEER_TOPK):
            lim = jnp.where(rank1 == float(a), n_sel[a], lim)
        lim_ref[0, :, cs] = lim
        e1_ref[0, :, cs] = jnp.exp(s1 - sv1[0:1])
        rb_ref[0, :, cs] = rank2
        e2_ref[0, :, cs] = jnp.exp(s2 - sv2[0:1]) / z


def _peer_topk(q, keys):
    n = q.shape[0]
    ospec = pl.BlockSpec((1, PEER_NKEYS, TOPK_TM), lambda i, h: (h, 0, i))
    return pl.pallas_call(
        _peer_topk_kernel,
        grid=(n // TOPK_TM, PEER_HEADS),
        in_specs=[pl.BlockSpec((TOPK_TM, 2 * PEER_HALF), lambda i, h: (i, h)),
                  pl.BlockSpec((1, 2, PEER_NKEYS, PEER_HALF), lambda i, h: (h, 0, 0, 0)),
                  pl.BlockSpec((N_CAND, LANES), lambda i, h: (0, 0))],
        out_specs=[ospec] * 4,
        out_shape=[jax.ShapeDtypeStruct((PEER_HEADS, PEER_NKEYS, n), F32)] * 4,
        compiler_params=_params("parallel", "parallel"),
        name="peer_topk",
    )(q, keys, _cand_labels())


PEER_TM = 512
PEER_TI = 8
PEER_RG = 4
PEER_JT = 16


def _gelu(x):
    return 0.5 * x * (1.0 + lax.erf(x * (2.0 ** -0.5)))


def _peer_dense_kernel(x_ref, u_ref, v_ref, lim_ref, e1_ref, rb_ref, e2_ref, res_ref, gate_ref, o_ref, w_ref):
    e = pl.program_id(1)

    @pl.when(e == 0)
    def _():
        o_ref[...] = jnp.zeros_like(o_ref)

    hid = _dot_nt(u_ref[...], x_ref[...])
    zero = jnp.zeros((), BF16)
    n_jt = PEER_NKEYS // PEER_JT
    for c in range(PEER_TM // LANES):
        cs = slice(c * LANES, (c + 1) * LANES)
        for rg in range(PEER_TI // PEER_RG):
            rows = range(rg * PEER_RG, (rg + 1) * PEER_RG)
            g = [[jnp.zeros((PEER_JT, LANES), BF16) for _ in range(n_jt)] for _ in rows]
            for h in range(PEER_HEADS):
                lim = [jnp.broadcast_to(lim_ref[h, ii:ii + 1, cs], (PEER_JT, LANES)).astype(BF16) for ii in rows]
                e1 = [jnp.broadcast_to(e1_ref[h, ii:ii + 1, cs], (PEER_JT, LANES)).astype(BF16) for ii in rows]
                for jt in range(n_jt):
                    js = slice(jt * PEER_JT, (jt + 1) * PEER_JT)
                    rb = rb_ref[h, js, cs].astype(BF16)
                    e2 = e2_ref[h, js, cs].astype(BF16)
                    for k in range(PEER_RG):
                        g[k][jt] = g[k][jt] + jnp.where(rb < lim[k], e2, zero) * e1[k]
            for k, ii in enumerate(rows):
                for jt in range(n_jt):
                    e0 = ii * PEER_NKEYS + jt * PEER_JT
                    w_ref[e0:e0 + PEER_JT, cs] = g[k][jt] * _gelu(hid[e0:e0 + PEER_JT, cs].astype(BF16))
    o_ref[...] += _dot_tn(w_ref[...], v_ref[...])

    @pl.when(e == pl.num_programs(1) - 1)
    def _():
        o_ref[...] = res_ref[...] + gate_ref[0] * o_ref[...]


def _peer_dense(xm, u, v, layer, lim, e1, rb, e2, res, gate):
    n, d = xm.shape
    n_exp = u.shape[1]
    te = PEER_TI * PEER_NKEYS
    sel_spec = pl.BlockSpec((PEER_HEADS, PEER_NKEYS, PEER_TM), lambda i, e: (0, 0, i))
    row_spec = pl.BlockSpec((PEER_HEADS, PEER_TI, PEER_TM), lambda i, e: (0, e, i))
    return pl.pallas_call(
        _peer_dense_kernel,
        grid=(n // PEER_TM, n_exp // te),
        in_specs=[pl.BlockSpec((PEER_TM, d), lambda i, e: (i, 0)),
                  pl.BlockSpec((None, te, d), lambda i, e: (layer, e, 0)),
                  pl.BlockSpec((None, te, d), lambda i, e: (layer, e, 0)),
                  row_spec, row_spec, sel_spec, sel_spec,
                  pl.BlockSpec((PEER_TM, d), lambda i, e: (i, 0)),
                  pl.BlockSpec((1, 1, d), lambda i, e: (_cond_of_tile(i, PEER_TM), 0, 0))],
        out_specs=pl.BlockSpec((PEER_TM, d), lambda i, e: (i, 0)),
        out_shape=jax.ShapeDtypeStruct((n, d), F32),
        scratch_shapes=[pltpu.VMEM((te, PEER_TM), BF16)],
        compiler_params=_params("parallel", "arbitrary"),
        name="peer_dense",
    )(xm, u, v, lim, e1, rb, e2, res, gate)


def _peer_layer(x, layer, norm_g, shift, scale, gate, wq, keys, u, v):
    xm = _modulate(x, norm_g, shift, scale)
    q = _matmul(xm, wq, layer, out_dtype=BF16)
    lim, e1, rb, e2 = _peer_topk(q, keys)
    return _peer_dense(xm, u, v, layer, lim, e1, rb, e2, x, gate)


EVEN_PAD = 4096
ODD_PAD = 6656
S_ALL = PAST_LEN + DEC_SEQ


def _even_mixer(h, w_in, layer, lam, subln_g, q_norm_g, w_uq, kv_norm_g, w_ukv, lam_init,
                cache_k, cache_v, cache_ckv, cache_kpe, rope_cos, rope_sin):
    z = _matmul(h, w_in, layer)
    ckv_n = _rmsnorm(z, kv_norm_g, col=3584 // B_KV_RANK)
    cq_n = _rmsnorm(z, q_norm_g, out_dtype=BF16, col=3072 // B_Q_RANK)
    wq3 = w_uq.reshape(B_Q_RANK, HEADS, B_NOPE + B_ROPE)
    w_uq_r = jnp.concatenate([wq3[:, :, :B_NOPE].reshape(B_Q_RANK, -1), wq3[:, :, B_NOPE:].reshape(B_Q_RANK, -1)], axis=1)
    qb = _matmul(cq_n, w_uq_r.astype(BF16), tn=512)
    new = (z[:N_CTX, 1024:2048].reshape(BATCH, SEQ, HEADS, 2 * A_QK),
           z[:N_CTX, 2048:3072].reshape(BATCH, SEQ, HEADS, HEAD_DIM),
           ckv_n[:N_CTX].reshape(BATCH, SEQ, B_KV_RANK), z[:N_CTX, 3840:3904].reshape(BATCH, SEQ, B_ROPE))

    qa_s = _rope(z, rope_cos, rope_sin, ATT_W, col=0)
    ka_s = _rope(z, rope_cos, rope_sin, ATT_W, col=1)
    qr_s = _rope(qb, rope_cos, rope_sin, HEADS * B_ROPE, col=2)
    kpe_s = _rope(z, rope_cos, rope_sin, LANES, col=3840 // LANES)[:, :B_ROPE]

    def with_ctx(cache, own, width):
        both = jnp.concatenate([cache.reshape(DEC_BATCH, PAST_LEN, width).astype(own.dtype),
                                own.reshape(DEC_BATCH, DEC_SEQ, width)], axis=1)
        return both.reshape(DEC_BATCH * S_ALL, width)

    ka_all = with_ctx(cache_k, ka_s, ATT_W)
    va_all = with_ctx(cache_v, z[N_CTX:, 2048:3072].astype(BF16), ATT_W)
    ckv_all = with_ctx(cache_ckv, ckv_n[N_CTX:], B_KV_RANK)
    kpe_all = with_ctx(cache_kpe, kpe_s, B_ROPE)

    wkv3 = w_ukv.reshape(B_KV_RANK, HEADS, B_NOPE + HEAD_DIM)
    w_ukv_r = jnp.concatenate([wkv3[:, :, :B_NOPE].reshape(B_KV_RANK, -1), wkv3[:, :, B_NOPE:].reshape(B_KV_RANK, -1)], axis=1)
    n_skv = DEC_BATCH * S_ALL
    kv = _matmul(jnp.concatenate([ckv_all, ckv_n[:N_CTX]], axis=0), w_ukv_r.astype(BF16), out_dtype=BF16)

    oa_c = _diff_attention(BATCH, SEQ, (z, _q_rows(SEQ, ATT_W, 0, 0)), (z, _kv_rows(SEQ, ATT_W, 0, 1)),
                           (z, _kv_rows(SEQ, ATT_W, 0, 2)), lam, subln_g, lam_init)
    oa_s = _diff_attention(DEC_BATCH, DEC_SEQ, (qa_s, _q_rows(DEC_SEQ, ATT_W)), (ka_all, _kv_rows(S_ALL, ATT_W)),
                           (va_all, _kv_rows(S_ALL, ATT_W)), lam, subln_g, lam_init)
    ob_c = _mla_attention(BATCH, SEQ, (qb, _q_rows(SEQ, ATT_W, 0, 0)), (qb, _q_rows(SEQ, HEADS * B_ROPE, 0, 2)),
                          (kv, _kv_rows(SEQ, ATT_W, n_skv, 0)), (z, _kv_rows(SEQ, LANES, 0, 3840 // LANES)),
                          (kv, _kv_rows(SEQ, ATT_W, n_skv, 1)))
    ob_s = _mla_attention(DEC_BATCH, DEC_SEQ, (qb, _q_rows(DEC_SEQ, ATT_W, N_CTX, 0)),
                          (qr_s, _q_rows(DEC_SEQ, HEADS * B_ROPE)), (kv, _kv_rows(S_ALL, ATT_W, 0, 0)),
                          (kpe_all, _kv_rows(S_ALL, B_ROPE)), (kv, _kv_rows(S_ALL, ATT_W, 0, 1)))
    return (oa_c, oa_s, ob_c, ob_s), new


def _odd_mixer(h, w_in, layer, rpb, conv_w, w0, w2, a0, a2, g2, k_k, k_a, r_k, ln_g, ln_b,
               cache_k, cache_v, state):
    z = _matmul(h, w_in, layer)
    new_k = z[:N_CTX, 1024:2048].reshape(BATCH, SEQ, HEADS, HEAD_DIM)
    new_v = z[:N_CTX, 2048:3072].reshape(BATCH, SEQ, HEADS, HEAD_DIM)
    oc_c = _plain_attention(BATCH, SEQ, (z, _q_rows(SEQ, ATT_W, 0, 0)), (z, _kv_rows(SEQ, ATT_W, 0, 1)),
                            (z, _kv_rows(SEQ, ATT_W, 0, 2)))
    oc_s = _na_attention(z, cache_k.reshape(DEC_BATCH * PAST_LEN, ATT_W), cache_v.reshape(DEC_BATCH * PAST_LEN, ATT_W),
                         _na_bias_tables(rpb))

    zc_c = _conv3(z, conv_w, SEQ, 0, BATCH)
    zc_s = _conv3(z, conv_w, DEC_SEQ, N_CTX, DEC_BATCH)
    lw_c, a_c, g_c = _d_prep(zc_c, w0, w2, a0, a2, g2)
    lw_s, a_s, g_s = _d_prep(zc_s, w0, w2, a0, a2, g2)
    rk = r_k.reshape(-1)
    y_c, sfin = _rwkv_chunked(zc_c, lw_c, a_c, k_k, k_a,
                              jnp.zeros((BATCH, 2, D_WIDTH // LANES, LANES, LANES), F32), BATCH, SEQ)
    sfin = _head_states(sfin)
    y_s, _ = _rwkv_chunked(zc_s, lw_s, a_s, k_k, k_a, _pair_states(state), DEC_BATCH, DEC_SEQ)
    od_c = _d_out(y_c, zc_c, a_c, g_c, k_a, rk, ln_g, ln_b)
    od_s = _d_out(y_s, zc_s, a_s, g_s, k_a, rk, ln_g, ln_b)
    return (oc_c, oc_s, od_c, od_s), (new_k, new_v, sfin)


def kernel(x_prompt, x_sample, cache_a_k, cache_a_v, cache_b_ckv, cache_b_kpe, cache_c_k, cache_c_v, state_d, c, c_ctx, ada_w, ada_b, norm1_g, norm2_g, w_out, peer_wq, peer_keys, peer_u, peer_v, final_g, ab_w_in, a_lam, a_subln_g, b_q_norm_g, b_w_uq, b_kv_norm_g, b_w_ukv, cd_w_in, c_rpb, d_conv, d_w0, d_w2, d_a0, d_a2, d_g2, d_k_k, d_k_a, d_r_k, d_ln_g, d_ln_b):
    x = jnp.concatenate([x_prompt.reshape(N_CTX, D_MODEL), x_sample.reshape(N_SMP, D_MODEL)], axis=0)
    cond8 = jnp.pad(jnp.concatenate([c_ctx[None, :], c], axis=0), ((0, 8 - N_COND), (0, 0)))
    ada = _ada_all(cond8, ada_w, ada_b)
    rope_cos, rope_sin = _rope_tables()
    w_even = jnp.pad(ab_w_in, ((0, 0), (0, 0), (0, EVEN_PAD - EVEN_IN))).astype(BF16)
    w_odd = jnp.pad(cd_w_in, ((0, 0), (0, 0), (0, ODD_PAD - ODD_IN))).astype(BF16)
    w_out_b, wq_b, keys_b = w_out.astype(BF16), peer_wq.astype(BF16), peer_keys.astype(BF16)
    u_b, v_b = peer_u.astype(BF16), peer_v.astype(BF16)
    new_ak, new_av, new_bc, new_bp, new_ck, new_cv, new_sd = [], [], [], [], [], [], []
    for l in range(DEPTH):
        i = l // 2
        mods = ada[l, :N_COND].reshape(N_COND, 6, 1, D_MODEL)
        sh1, sc1, g1, sh2, sc2, g2 = (mods[:, m] for m in range(6))
        h = _modulate(x, norm1_g[l], sh1, sc1)
        if l % 2 == 0:
            lam_init = 0.8 - 0.6 * math.exp(-0.3 * l)
            o, (ak, av, bc, bp) = _even_mixer(
                h, w_even, i, a_lam[i], a_subln_g[i], b_q_norm_g[i], b_w_uq[i], b_kv_norm_g[i], b_w_ukv[i],
                lam_init, cache_a_k[:, i], cache_a_v[:, i], cache_b_ckv[:, i], cache_b_kpe[:, i], rope_cos, rope_sin)
            new_ak.append(ak)
            new_av.append(av)
            new_bc.append(bc)
            new_bp.append(bp)
        else:
            o, (ck, cv, sd) = _odd_mixer(
                h, w_odd, i, c_rpb[i], d_conv[i], d_w0[i], d_w2[i], d_a0[i], d_a2[i], d_g2[i],
                d_k_k[i], d_k_a[i], d_r_k[i], d_ln_g[i], d_ln_b[i], cache_c_k[:, i], cache_c_v[:, i], state_d[:, i])
            new_ck.append(ck)
            new_cv.append(cv)
            new_sd.append(sd)
        x = _matmul_residual(*o, w_out_b, l, x, g1)
        x = _peer_layer(x, l, norm2_g[l], sh2, sc2, g2, wq_b, keys_b[l], u_b, v_b)
    y = _rmsnorm(x, final_g)
    y_prompt = y[:N_CTX].reshape(BATCH, SEQ, D_MODEL)
    y_sample = y[N_CTX:].reshape(DEC_BATCH, DEC_SEQ, D_MODEL)
    return (y_prompt, y_sample, jnp.stack(new_ak, axis=1), jnp.stack(new_av, axis=1), jnp.stack(new_bc, axis=1),
            jnp.stack(new_bp, axis=1), jnp.stack(new_ck, axis=1), jnp.stack(new_cv, axis=1), jnp.stack(new_sd, axis=1))
```

```python
import functools
import math

import numpy as np
import jax
import jax.numpy as jnp
from jax import lax
from jax.experimental import pallas as pl
from jax.experimental.pallas import tpu as pltpu

F32 = jnp.float32
BF16 = jnp.bfloat16

D_MODEL = 2048
BATCH = 16
SEQ = 256
DEPTH = 4
DEC_BATCH = 2
DEC_SEQ = 1024
PAST_LEN = 512
GRID_W = 64
GRID_ROWS = DEC_SEQ // GRID_W
ROPE_BASE = 10000.0
RMS_EPS = 1e-6
NEG_INF = -1e30
N_CTX = BATCH * SEQ
N_SMP = DEC_BATCH * DEC_SEQ
N_TOK = N_CTX + N_SMP
N_COND = 1 + DEC_BATCH

HEADS = 8
HEAD_DIM = 128
A_QK = 64
B_Q_RANK = 512
B_KV_RANK = 256
B_NOPE = 128
B_ROPE = 64
NA_KH = 8
NA_KW = 16
NA_WIN = NA_KH * GRID_W
D_HS = 64
D_WIDTH = 1024
D_HEADS = 16
D_LORA = 64
D_GATE_LORA = 128
D_IN = 3 * D_WIDTH + 2 * D_LORA + 2 * D_LORA + D_GATE_LORA
D_LN_EPS = 64e-5
EVEN_IN = 3904
ODD_IN = 6528
PEER_HEADS = 8
PEER_NKEYS = 128
PEER_HALF = 128
PEER_TOPK = 16

LANES = 128
VMEM_LIMIT = 56 * 1024 * 1024


def _params(*sem):
    return pltpu.CompilerParams(dimension_semantics=sem, vmem_limit_bytes=VMEM_LIMIT)


def _dot_nt(a, b):
    return lax.dot_general(a, b, (((1,), (1,)), ((), ())), preferred_element_type=F32)


def _dot_tn(a, b):
    return lax.dot_general(a, b, (((0,), (0,)), ((), ())), preferred_element_type=F32)


def _cond_of_tile(i, tm):
    n_ctx_tiles = N_CTX // tm
    per_batch = DEC_SEQ // tm
    return jnp.where(i < n_ctx_tiles, 0, 1 + (i - n_ctx_tiles) // per_batch)


def _ada_kernel(c_ref, w_ref, b_ref, o_ref):
    c = c_ref[...]
    sc = c * (1.0 / (1.0 + jnp.exp(-c)))
    o_ref[0] = jnp.dot(sc.astype(BF16), w_ref[0].astype(BF16), preferred_element_type=F32) + b_ref[0]


def _ada_all(cond8, ada_w, ada_b):
    tn = 1024
    n_out = ada_w.shape[-1]
    return pl.pallas_call(
        _ada_kernel,
        grid=(DEPTH, n_out // tn),
        in_specs=[pl.BlockSpec((8, D_MODEL), lambda l, j: (0, 0)),
                  pl.BlockSpec((1, D_MODEL, tn), lambda l, j: (l, 0, j)),
                  pl.BlockSpec((1, 1, tn), lambda l, j: (l, 0, j))],
        out_specs=pl.BlockSpec((1, 8, tn), lambda l, j: (l, 0, j)),
        out_shape=jax.ShapeDtypeStruct((DEPTH, 8, n_out), F32),
        compiler_params=_params("parallel", "parallel"),
        name="ada",
    )(cond8, ada_w, ada_b.reshape(DEPTH, 1, n_out))


def _modulate_kernel(x_ref, g_ref, sh_ref, sc_ref, o_ref):
    x = x_ref[...]
    y = x * lax.rsqrt(jnp.mean(x * x, axis=-1, keepdims=True) + RMS_EPS) * g_ref[...]
    o_ref[...] = (y * (1.0 + sc_ref[0]) + sh_ref[0]).astype(o_ref.dtype)


def _modulate(x, g, shift, scale, out_dtype=BF16, tm=512):
    n, d = x.shape
    return pl.pallas_call(
        _modulate_kernel,
        grid=(n // tm,),
        in_specs=[pl.BlockSpec((tm, d), lambda i: (i, 0)),
                  pl.BlockSpec((1, d), lambda i: (0, 0)),
                  pl.BlockSpec((1, 1, d), lambda i: (_cond_of_tile(i, tm), 0, 0)),
                  pl.BlockSpec((1, 1, d), lambda i: (_cond_of_tile(i, tm), 0, 0))],
        out_specs=pl.BlockSpec((tm, d), lambda i: (i, 0)),
        out_shape=jax.ShapeDtypeStruct((n, d), out_dtype),
        compiler_params=_params("parallel"),
        name="modulate",
    )(x, g.reshape(1, d), shift, scale)


def _rmsnorm_kernel(x_ref, g_ref, o_ref):
    x = x_ref[...]
    y = x * lax.rsqrt(jnp.mean(x * x, axis=-1, keepdims=True) + RMS_EPS) * g_ref[...]
    o_ref[...] = y.astype(o_ref.dtype)


def _rmsnorm(x, g, out_dtype=F32, tm=512, col=0):
    n = x.shape[0]
    d = g.shape[-1]
    return pl.pallas_call(
        _rmsnorm_kernel,
        grid=(n // tm,),
        in_specs=[pl.BlockSpec((tm, d), lambda i: (i, col)),
                  pl.BlockSpec((1, d), lambda i: (0, 0))],
        out_specs=pl.BlockSpec((tm, d), lambda i: (i, 0)),
        out_shape=jax.ShapeDtypeStruct((n, d), out_dtype),
        compiler_params=_params("parallel"),
        name="rmsnorm",
    )(x, g.reshape(1, d))


def _mm_kernel(x_ref, w_ref, o_ref):
    o_ref[...] = jnp.dot(x_ref[...].astype(BF16), w_ref[...], preferred_element_type=F32).astype(o_ref.dtype)


def _weight_spec(w, layer, tn):
    if w.ndim == 2:
        return pl.BlockSpec((w.shape[0], tn), lambda i, j: (0, j))
    return pl.BlockSpec((None, w.shape[1], tn), lambda i, j: (layer, 0, j))


MM_TM = 1024
MM_TN = 1024


def _matmul(x, w, layer=None, out_dtype=F32, tm=MM_TM, tn=None, x_col=0, x_width=None):
    m = x.shape[0]
    k = x_width or x.shape[1]
    n = w.shape[-1]
    if tn is None:
        tn = MM_TN if n % MM_TN == 0 else MM_TN // 2
    tn = min(tn, n)
    assert m % tm == 0 and n % tn == 0, (m, n, tm, tn)
    return pl.pallas_call(
        _mm_kernel,
        grid=(m // tm, n // tn),
        in_specs=[pl.BlockSpec((tm, k), lambda i, j: (i, x_col)),
                  _weight_spec(w, layer, tn)],
        out_specs=pl.BlockSpec((tm, tn), lambda i, j: (i, j)),
        out_shape=jax.ShapeDtypeStruct((m, n), out_dtype),
        compiler_params=_params("parallel", "parallel"),
        name="matmul",
    )(x, w)


def _mm_res_kernel(lc_ref, ls_ref, rc_ref, rs_ref, w_ref, res_ref, gate_ref, o_ref, *, n_ctx_tiles):
    half = w_ref.shape[0] // 2

    def emit(l_ref, r_ref):
        acc = jnp.dot(l_ref[...].astype(BF16), w_ref[0:half, :], preferred_element_type=F32)
        acc = acc + jnp.dot(r_ref[...].astype(BF16), w_ref[half:, :], preferred_element_type=F32)
        o_ref[...] = res_ref[...] + gate_ref[0] * acc

    @pl.when(pl.program_id(0) < n_ctx_tiles)
    def _():
        emit(lc_ref, rc_ref)

    @pl.when(pl.program_id(0) >= n_ctx_tiles)
    def _():
        emit(ls_ref, rs_ref)


def _matmul_residual(left_c, left_s, right_c, right_s, w, layer, res, gate, tm=MM_TM, tn=MM_TN // 2):
    m = res.shape[0]
    n = w.shape[-1]
    kh = left_c.shape[1]
    nct = left_c.shape[0] // tm
    ctx_spec = pl.BlockSpec((tm, kh), lambda i, j: (jnp.minimum(i, nct - 1), 0))
    smp_spec = pl.BlockSpec((tm, kh), lambda i, j: (jnp.maximum(i - nct, 0), 0))
    return pl.pallas_call(
        functools.partial(_mm_res_kernel, n_ctx_tiles=nct),
        grid=(m // tm, n // tn),
        in_specs=[ctx_spec, smp_spec, ctx_spec, smp_spec,
                  _weight_spec(w, layer, tn),
                  pl.BlockSpec((tm, tn), lambda i, j: (i, j)),
                  pl.BlockSpec((1, 1, tn), lambda i, j: (_cond_of_tile(i, tm), 0, j))],
        out_specs=pl.BlockSpec((tm, tn), lambda i, j: (i, j)),
        out_shape=jax.ShapeDtypeStruct((m, n), F32),
        compiler_params=_params("parallel", "parallel"),
        name="matmul_residual",
    )(left_c, left_s, right_c, right_s, w, res, gate)


def _rope_tables():
    nf = 16
    inv = ROPE_BASE ** (-np.arange(nf, dtype=np.float64) / nf)
    t = np.arange(DEC_SEQ)
    rows, cols = t // GRID_W, t % GRID_W
    lane = np.arange(64)
    pos = np.where(lane[None, :] < 32, rows[:, None], cols[:, None]).astype(np.float32)
    ang = (pos * inv[lane % nf][None, :].astype(np.float32)).astype(np.float32)
    first = (lane % 32) < nf
    cos = np.cos(ang.astype(np.float64))
    sin = np.sin(ang.astype(np.float64)) * np.where(first, -1.0, 1.0)[None, :]
    cos = np.tile(cos, (1, 2)).astype(np.float32)
    sin = np.tile(sin, (1, 2)).astype(np.float32)
    return jnp.asarray(cos), jnp.asarray(sin)


def _rope_kernel(x_ref, cos_ref, sin_ref, o_ref):
    cos = cos_ref[...]
    sin = sin_ref[...]
    lane = lax.broadcasted_iota(jnp.int32, cos.shape, 1)
    first = (lane % 32) < 16
    for c in range(x_ref.shape[1] // LANES):
        x = x_ref[:, c * LANES:(c + 1) * LANES].astype(F32)
        partner = jnp.where(first, pltpu.roll(x, LANES - 16, 1), pltpu.roll(x, 16, 1))
        o_ref[:, c * LANES:(c + 1) * LANES] = (x * cos + partner * sin).astype(o_ref.dtype)


def _rope(x, cos, sin, w, col=0, out_dtype=BF16, tm=256):
    n = N_SMP
    row0 = (x.shape[0] - N_SMP) // tm
    per = DEC_SEQ // tm
    return pl.pallas_call(
        _rope_kernel,
        grid=(n // tm,),
        in_specs=[pl.BlockSpec((tm, w), lambda i: (row0 + i, col)),
                  pl.BlockSpec((tm, LANES), lambda i: (i % per, 0)),
                  pl.BlockSpec((tm, LANES), lambda i: (i % per, 0))],
        out_specs=pl.BlockSpec((tm, w), lambda i: (i, 0)),
        out_shape=jax.ShapeDtypeStruct((n, w), out_dtype),
        compiler_params=_params("parallel"),
        name="rope",
    )(x, cos, sin)


def _softmax(s):
    p = jnp.exp(s - jnp.max(s, axis=-1, keepdims=True))
    return p / jnp.sum(p, axis=-1, keepdims=True)


def _diff_attn_kernel(lam_ref, g_ref, q_ref, k_ref, v_ref, o_ref, *, lam_init):
    lam = lam_ref[...]
    l1 = jnp.sum(jnp.sum(lam[0:1] * lam[1:2], axis=-1, keepdims=True), axis=0, keepdims=True)
    l2 = jnp.sum(jnp.sum(lam[2:3] * lam[3:4], axis=-1, keepdims=True), axis=0, keepdims=True)
    lam_val = jnp.exp(l1) - jnp.exp(l2) + lam_init
    scale = A_QK ** -0.5
    for h in range(HEADS):
        sl = slice(h * HEAD_DIM, (h + 1) * HEAD_DIM)
        q = q_ref[:, sl].astype(BF16)
        k = k_ref[:, sl].astype(BF16)
        p1 = _softmax(_dot_nt(q[:, :A_QK], k[:, :A_QK]) * scale)
        p2 = _softmax(_dot_nt(q[:, A_QK:], k[:, A_QK:]) * scale)
        a = p1 - lam_val * p2
        o = jnp.dot(a.astype(BF16), v_ref[:, sl].astype(BF16), preferred_element_type=F32)
        o = o * lax.rsqrt(jnp.mean(o * o, axis=-1, keepdims=True) + RMS_EPS) * g_ref[...]
        o_ref[:, sl] = (o * (1.0 - lam_init)).astype(o_ref.dtype)


ATT_TQ = 256
ATT_W = HEADS * HEAD_DIM


def _q_rows(t, width, row0=0, col=0):
    per, off = t // ATT_TQ, row0 // ATT_TQ
    return pl.BlockSpec((ATT_TQ, width), lambda b, j: (off + b * per + j, col))


def _kv_rows(s, width, row0=0, col=0):
    off = row0 // s
    return pl.BlockSpec((s, width), lambda b, j: (off + b, col))


def _attention_call(kernel, name, nb, t, operands, specs):
    return pl.pallas_call(
        kernel,
        grid=(nb, t // ATT_TQ),
        in_specs=specs,
        out_specs=_q_rows(t, ATT_W),
        out_shape=jax.ShapeDtypeStruct((nb * t, ATT_W), BF16),
        compiler_params=_params("parallel", "parallel"),
        name=name,
    )(*operands)


def _diff_attention(nb, t, q, k, v, lam, subln_g, lam_init):
    const = [pl.BlockSpec((4, A_QK), lambda i, j: (0, 0)), pl.BlockSpec((1, HEAD_DIM), lambda i, j: (0, 0))]
    return _attention_call(functools.partial(_diff_attn_kernel, lam_init=lam_init), "diff_attention", nb, t,
                           [lam, subln_g.reshape(1, HEAD_DIM), q[0], k[0], v[0]], const + [q[1], k[1], v[1]])


def _mla_attn_kernel(qn_ref, qr_ref, kn_ref, kr_ref, v_ref, o_ref):
    scale = (B_NOPE + B_ROPE) ** -0.5
    kr = kr_ref[:, 0:B_ROPE].astype(BF16)
    for h in range(HEADS):
        sl = slice(h * HEAD_DIM, (h + 1) * HEAD_DIM)
        s = _dot_nt(qn_ref[:, sl].astype(BF16), kn_ref[:, sl])
        s = s + _dot_nt(qr_ref[:, h * B_ROPE:(h + 1) * B_ROPE].astype(BF16), kr)
        p = _softmax(s * scale)
        o = jnp.dot(p.astype(BF16), v_ref[:, sl], preferred_element_type=F32)
        o_ref[:, sl] = o.astype(o_ref.dtype)


def _mla_attention(nb, t, qn, qr, kn, kr, v):
    ops = [qn, qr, kn, kr, v]
    return _attention_call(_mla_attn_kernel, "mla_attention", nb, t, [o[0] for o in ops], [o[1] for o in ops])


def _plain_attn_kernel(q_ref, k_ref, v_ref, o_ref):
    scale = HEAD_DIM ** -0.5
    for h in range(HEADS):
        sl = slice(h * HEAD_DIM, (h + 1) * HEAD_DIM)
        p = _softmax(_dot_nt(q_ref[:, sl].astype(BF16), k_ref[:, sl].astype(BF16)) * scale)
        o = jnp.dot(p.astype(BF16), v_ref[:, sl].astype(BF16), preferred_element_type=F32)
        o_ref[:, sl] = o.astype(o_ref.dtype)


def _plain_attention(nb, t, q, k, v):
    ops = [q, k, v]
    return _attention_call(_plain_attn_kernel, "plain_attention", nb, t, [o[0] for o in ops], [o[1] for o in ops])


def _na_window_start(r):
    return jnp.clip(r - NA_KH // 2, 0, GRID_ROWS - NA_KH)


def _na_attn_kernel(q_ref, k_ref, v_ref, kc_ref, vc_ref, bias_ref, o_ref):
    scale = HEAD_DIM ** -0.5
    r = pl.program_id(1)
    start = pl.multiple_of(_na_window_start(r) * GRID_W, GRID_W)
    for h in range(HEADS):
        sl = slice(h * HEAD_DIM, (h + 1) * HEAD_DIM)
        q = q_ref[:, sl].astype(BF16)
        kw = k_ref[pl.ds(start, NA_WIN), sl].astype(BF16)
        vw = v_ref[pl.ds(start, NA_WIN), sl].astype(BF16)
        s_win = _dot_nt(q, kw) * scale + bias_ref[0, h]
        s_ctx = _dot_nt(q, kc_ref[:, sl].astype(BF16)) * scale
        m = jnp.maximum(jnp.max(s_win, axis=-1, keepdims=True), jnp.max(s_ctx, axis=-1, keepdims=True))
        p_win = jnp.exp(s_win - m)
        p_ctx = jnp.exp(s_ctx - m)
        den = jnp.sum(p_win, axis=-1, keepdims=True) + jnp.sum(p_ctx, axis=-1, keepdims=True)
        o = (jnp.dot(p_win.astype(BF16), vw, preferred_element_type=F32)
             + jnp.dot(p_ctx.astype(BF16), vc_ref[:, sl].astype(BF16), preferred_element_type=F32))
        o_ref[:, sl] = (o / den).astype(o_ref.dtype)


NA_NDC = 2 * NA_KW


def _na_bias_kernel(r_ref, e_ref, o_ref):
    o_ref[0] = jnp.dot(r_ref[0], e_ref[...], preferred_element_type=F32, precision=lax.Precision.HIGHEST)


def _na_bias_tables(rpb):
    cols = np.arange(GRID_W)
    cs = np.clip(cols - NA_KW // 2, 0, GRID_W - NA_KW)
    col_in = (cols[None, :] >= cs[:, None]) & (cols[None, :] < cs[:, None] + NA_KW)
    dc_idx = np.clip(cols[None, :] - cols[:, None] + NA_KW - 1, 0, 2 * NA_KW - 2)
    mask = np.broadcast_to(col_in[:, None, :], (GRID_W, NA_KH, GRID_W)).reshape(GRID_W, NA_WIN)
    onehot = (np.arange(NA_NDC)[:, None] == dc_idx.reshape(1, -1)).astype(np.float32)
    rp = jnp.pad(rpb, ((0, 0), (0, 0), (0, NA_NDC - rpb.shape[2])))
    rows = jnp.stack([rp[:, NA_KH - 1 - d:2 * NA_KH - 1 - d, :] for d in range(NA_KH)])
    rows = rows.reshape(NA_KH, HEADS * NA_KH, NA_NDC)
    nqk = GRID_W * GRID_W
    b = pl.pallas_call(
        _na_bias_kernel,
        grid=(NA_KH,),
        in_specs=[pl.BlockSpec((1, HEADS * NA_KH, NA_NDC), lambda d: (d, 0, 0)),
                  pl.BlockSpec((NA_NDC, nqk), lambda d: (0, 0))],
        out_specs=pl.BlockSpec((1, HEADS * NA_KH, nqk), lambda d: (d, 0, 0)),
        out_shape=jax.ShapeDtypeStruct((NA_KH, HEADS * NA_KH, nqk), F32),
        compiler_params=_params("parallel"),
        name="na_bias",
    )(rows, jnp.asarray(onehot))
    b = b.reshape(NA_KH, HEADS, NA_KH, GRID_W, GRID_W).transpose(0, 1, 3, 2, 4).reshape(NA_KH, HEADS, GRID_W, NA_WIN)
    return jnp.where(mask[None, None], b, NEG_INF)


def _na_attention(z, kc, vc, bias):
    w = ATT_W
    row0 = z.shape[0] - N_SMP

    def all_rows(n, col, base):
        return pl.BlockSpec((n, w), lambda i, r: (base // n + i, col))

    return pl.pallas_call(
        _na_attn_kernel,
        grid=(DEC_BATCH, GRID_ROWS),
        in_specs=[pl.BlockSpec((GRID_W, w), lambda i, r: (row0 // GRID_W + i * GRID_ROWS + r, 0)),
                  all_rows(DEC_SEQ, 1, row0), all_rows(DEC_SEQ, 2, row0),
                  all_rows(PAST_LEN, 0, 0), all_rows(PAST_LEN, 0, 0),
                  pl.BlockSpec((1, HEADS, GRID_W, NA_WIN), lambda i, r: (r - _na_window_start(r), 0, 0, 0))],
        out_specs=pl.BlockSpec((GRID_W, w), lambda i, r: (i * GRID_ROWS + r, 0)),
        out_shape=jax.ShapeDtypeStruct((N_SMP, w), BF16),
        compiler_params=_params("parallel", "parallel"),
        name="na_attention",
    )(z, z, z, kc, vc, bias)


CONV_CW = 384


CONV_ROWS = 1024


def _conv3_kernel(x_ref, w_ref, o_ref, *, seq):
    x = x_ref[...]
    t = x.shape[0]
    pos = lax.broadcasted_iota(jnp.int32, x.shape, 0) % seq
    prev = jnp.where(pos == 0, 0.0, pltpu.roll(x, 1, 0))
    nxt = jnp.where(pos == seq - 1, 0.0, pltpu.roll(x, t - 1, 0))
    w = w_ref[...]
    o_ref[...] = prev * w[0:1] + x * w[1:2] + nxt * w[2:3]


def _conv3(z, conv_w, seq, row0, n_seq):
    col0 = (3 * D_WIDTH) // CONV_CW
    blk0 = row0 // CONV_ROWS
    n_rows = n_seq * seq
    return pl.pallas_call(
        functools.partial(_conv3_kernel, seq=seq),
        grid=(n_rows // CONV_ROWS, D_IN // CONV_CW),
        in_specs=[pl.BlockSpec((CONV_ROWS, CONV_CW), lambda i, j: (blk0 + i, col0 + j)),
                  pl.BlockSpec((3, CONV_CW), lambda i, j: (0, j))],
        out_specs=pl.BlockSpec((CONV_ROWS, CONV_CW), lambda i, j: (i, j)),
        out_shape=jax.ShapeDtypeStruct((n_rows, D_IN), F32),
        compiler_params=_params("parallel", "parallel"),
        name="conv3",
    )(z, conv_w)


def _dot_f32(a, b):
    return jnp.dot(a, b, preferred_element_type=F32, precision=lax.Precision.HIGHEST)


def _d_prep_kernel(x_ref, w0_ref, w2_ref, a0_ref, a2_ref, g2_ref, dec_ref, a_ref, g_ref):
    x = x_ref[...]
    xw = x[:, 0:2 * D_LORA]
    xa = x[:, 2 * D_LORA:4 * D_LORA]
    xg = x[:, 4 * D_LORA:]
    for d in range(2):
        u = w0_ref[d] + _bdot(jnp.tanh(xw[:, d * D_LORA:(d + 1) * D_LORA]), w2_ref[d])
        nu = -u
        softplus = jnp.maximum(nu, 0.0) + jnp.log1p(jnp.exp(-jnp.abs(nu)))
        wlog = -softplus - 0.5
        dec_ref[d] = -jnp.exp(wlog)
        av = a0_ref[d] + _bdot(xa[:, d * D_LORA:(d + 1) * D_LORA], a2_ref[d])
        a_ref[d] = 1.0 / (1.0 + jnp.exp(-av))
    g_ref[...] = _bdot(1.0 / (1.0 + jnp.exp(-xg)), g2_ref[...])


def _d_prep(zc, w0, w2, a0, a2, g2, tm=512):
    n = zc.shape[0]
    cb = (3 * D_WIDTH) // CONV_CW
    return pl.pallas_call(
        _d_prep_kernel,
        grid=(n // tm,),
        in_specs=[pl.BlockSpec((tm, CONV_CW), lambda i: (i, cb)),
                  pl.BlockSpec((2, 1, D_WIDTH), lambda i: (0, 0, 0)),
                  pl.BlockSpec((2, D_LORA, D_WIDTH), lambda i: (0, 0, 0)),
                  pl.BlockSpec((2, 1, D_WIDTH), lambda i: (0, 0, 0)),
                  pl.BlockSpec((2, D_LORA, D_WIDTH), lambda i: (0, 0, 0)),
                  pl.BlockSpec((D_GATE_LORA, D_WIDTH), lambda i: (0, 0))],
        out_specs=[pl.BlockSpec((2, tm, D_WIDTH), lambda i: (0, i, 0)),
                   pl.BlockSpec((2, tm, D_WIDTH), lambda i: (0, i, 0)),
                   pl.BlockSpec((tm, D_WIDTH), lambda i: (i, 0))],
        out_shape=[jax.ShapeDtypeStruct((2, n, D_WIDTH), F32),
                   jax.ShapeDtypeStruct((2, n, D_WIDTH), F32),
                   jax.ShapeDtypeStruct((n, D_WIDTH), F32)],
        compiler_params=_params("parallel"),
        name="d_prep",
    )(zc, w0.reshape(2, 1, D_WIDTH), w2, a0.reshape(2, 1, D_WIDTH), a2, g2)


RW_C = 64
RW_PP = 8


def _bdot(a, b):
    ah, al = _split_hi_lo(a)
    bh, bl = _split_hi_lo(b)
    dot = functools.partial(jnp.dot, preferred_element_type=F32)
    return dot(ah, bh) + dot(ah, bl) + dot(al, bh)


def _bdot2(a, b):
    ah, al = _split_hi_lo(a)
    dot = functools.partial(jnp.dot, preferred_element_type=F32)
    return dot(ah, b) + dot(al, b)


def _bdot_nt(a, b):
    ah, al = _split_hi_lo(a)
    bh, bl = _split_hi_lo(b)
    return _dot_nt(ah, bh) + _dot_nt(ah, bl) + _dot_nt(al, bh)


def _bdot_tn(a, b):
    return _dot_tn(a.astype(BF16), b.astype(BF16))


def _split_hi_lo(x):
    hi = x.astype(BF16)
    return hi, (x - hi.astype(F32)).astype(BF16)


def _head_ones():
    r = lax.broadcasted_iota(jnp.int32, (LANES, LANES), 0) // D_HS
    c = lax.broadcasted_iota(jnp.int32, (LANES, LANES), 1) // D_HS
    return jnp.where(r == c, 1.0, 0.0).astype(BF16)


def _rwkv_chunk_kernel(r_ref, k_ref, v_ref, lw_ref, a_ref, kkp_ref, kap_ref, s0_ref, y_ref, sfin_ref, s_ref):
    c = pl.program_id(3)
    two_c = 2 * RW_C

    @pl.when(c == 0)
    def _():
        s_ref[...] = s0_ref[...]

    sgn = jnp.where(pl.program_id(1) == 1, -1, 1)
    row = lax.broadcasted_iota(jnp.int32, (two_c, two_c), 0)
    col = lax.broadcasted_iota(jnp.int32, (two_c, two_c), 1)
    same = (row // RW_C) == (col // RW_C)
    tt, ss = row % RW_C, col % RW_C
    before = (ss - tt) * sgn < 0
    strict = same & before
    incl = same & (before | (ss == tt))
    eye = jnp.where(row == col, 1.0, 0.0)
    r64 = lax.broadcasted_iota(jnp.int32, (RW_C, RW_C), 0)
    c64 = lax.broadcasted_iota(jnp.int32, (RW_C, RW_C), 1)
    ltri = jnp.where((c64 - r64) * sgn <= 0, 1.0, 0.0)
    head0 = lax.broadcasted_iota(jnp.int32, (RW_C, LANES), 1) < D_HS
    ones_blk = _head_ones().astype(F32)

    def stack(x):
        return jnp.concatenate([jnp.where(head0, x, 0.0), jnp.where(head0, 0.0, x)], axis=0)

    pairs = range(RW_PP)
    lss = [slice(p * LANES, (p + 1) * LANES) for p in pairs]
    r = [r_ref[:, ls] for ls in lss]
    k = [k_ref[:, ls] for ls in lss]
    lw = [lw_ref[:, ls] for ls in lss]
    a = [a_ref[:, ls] for ls in lss]
    kkf = [k[p] * kkp_ref[:, lss[p]] for p in pairs]
    n2 = [_dot_f32(kkf[p] * kkf[p], ones_blk) for p in pairs]
    kk = [kkf[p] / jnp.maximum(jnp.sqrt(n2[p]), 1e-12) for p in pairs]
    cum = [_dot_f32(ltri, lw[p]) for p in pairs]
    mid = [cum[p][RW_C // 2:RW_C // 2 + 1] for p in pairs]
    cumc = [cum[p] - mid[p] for p in pairs]
    pmid = [jnp.exp(mid[p]) for p in pairs]
    pend = [jnp.exp(jnp.sum(lw[p], axis=0, keepdims=True) - mid[p]) for p in pairs]
    pinv = [jnp.exp(-cumc[p]) for p in pairs]
    ks = [stack(kk[p] * jnp.exp(cumc[p] - lw[p])).astype(BF16) for p in pairs]
    rs = [stack(r[p] * jnp.exp(cumc[p])).astype(BF16) for p in pairs]
    khs = [stack(k[p] * (1.0 + (a[p] - 1.0) * kap_ref[:, lss[p]]) * pinv[p]).astype(BF16) for p in pairs]
    bhs = [stack(kk[p] * a[p] * pinv[p]).astype(BF16) for p in pairs]
    vs = [stack(v_ref[:, ls]).astype(BF16) for ls in lss]
    big = [_dot_nt(jnp.concatenate([ks[p], rs[p]], axis=0), jnp.concatenate([bhs[p], khs[p]], axis=0)) for p in pairs]
    a_b = [jnp.where(strict, big[p][:two_c, :two_c], 0.0) for p in pairs]
    a_k = [jnp.where(strict, big[p][:two_c, two_c:], 0.0) for p in pairs]
    l_b = [jnp.where(incl, big[p][two_c:, :two_c], 0.0) for p in pairs]
    l_k = [jnp.where(incl, big[p][two_c:, two_c:], 0.0) for p in pairs]
    npow = [-a_b[p] for p in pairs]
    tinv = [eye + npow[p] for p in pairs]
    for _ in range(5):
        npow = [_bdot(npow[p], npow[p]) for p in pairs]
        tinv = [tinv[p] + _bdot(tinv[p], npow[p]) for p in pairs]
    w1 = [_bdot2(a_k[p], vs[p]) for p in pairs]
    ktil = [_bdot2(tinv[p], ks[p]) for p in pairs]
    uv = [_bdot(tinv[p], w1[p]) for p in pairs]
    lku = [_bdot(l_b[p], jnp.concatenate([ktil[p], uv[p]], axis=1)) for p in pairs]
    rtil = [rs[p].astype(F32) - lku[p][:, :LANES] for p in pairs]
    yv = [_bdot2(l_k[p], vs[p]) - lku[p][:, LANES:] for p in pairs]
    gt = [_bdot2(ktil[p].T, bhs[p]) for p in pairs]
    ht = [jnp.dot(vs[p].astype(F32).T.astype(BF16), khs[p], preferred_element_type=F32) - _bdot2(uv[p].T, bhs[p])
          for p in pairs]
    s = [s_ref[p] * pmid[p] for p in pairs]
    y = [_bdot_nt(rtil[p], s[p]) + yv[p] for p in pairs]
    sg = [_bdot(s[p], gt[p]) for p in pairs]
    for p in pairs:
        s_ref[p] = (s[p] - sg[p] + ht[p]) * pend[p]
        y_ref[:, lss[p]] = y[p][:RW_C] + y[p][RW_C:]

    @pl.when(c == pl.num_programs(3) - 1)
    def _():
        sfin_ref[...] = s_ref[...]


def _pair_states(s):
    zero = jnp.zeros_like(s[:, :, 0::2])
    top = jnp.concatenate([s[:, :, 0::2], zero], axis=-1)
    bot = jnp.concatenate([zero, s[:, :, 1::2]], axis=-1)
    return jnp.concatenate([top, bot], axis=-2)


def _head_states(sp):
    b = sp.shape[0]
    both = jnp.stack([sp[:, :, :, :D_HS, :D_HS], sp[:, :, :, D_HS:, D_HS:]], axis=3)
    return both.reshape(b, 2, D_HEADS, D_HS, D_HS)


def _rwkv_chunked(zc, lw, a, k_k, k_a, s0, b, t):
    nc = t // RW_C
    gw = RW_PP * LANES
    npg = D_WIDTH // gw

    def cmap(d, c):
        return jnp.where(d == 0, c, nc - 1 - c)

    def z_spec(colblk):
        return pl.BlockSpec((RW_C, gw), lambda i, d, g, c: (i * nc + cmap(d, c), colblk * npg + g))

    dir_spec = pl.BlockSpec((None, RW_C, gw), lambda i, d, g, c: (d, i * nc + cmap(d, c), g))
    par_spec = pl.BlockSpec((1, gw), lambda i, d, g, c: (0, g))
    st_spec = pl.BlockSpec((None, None, RW_PP, LANES, LANES), lambda i, d, g, c: (i, d, g, 0, 0))
    return pl.pallas_call(
        _rwkv_chunk_kernel,
        grid=(b, 2, npg, nc),
        in_specs=[z_spec(0), z_spec(1), z_spec(2), dir_spec, dir_spec, par_spec, par_spec, st_spec],
        out_specs=[dir_spec, st_spec],
        out_shape=[jax.ShapeDtypeStruct((2, b * t, D_WIDTH), F32),
                   jax.ShapeDtypeStruct((b, 2, D_WIDTH // LANES, LANES, LANES), F32)],
        scratch_shapes=[pltpu.VMEM((RW_PP, LANES, LANES), F32)],
        compiler_params=_params("parallel", "parallel", "parallel", "arbitrary"),
        name="rwkv_chunk",
    )(zc, zc, zc, lw, a, k_k.reshape(1, D_WIDTH), k_a.reshape(1, D_WIDTH), s0)


def _d_out_kernel(y_ref, r_ref, k_ref, v_ref, a_ref, g_ref, kap_ref, rkp_ref, lng_ref, lnb_ref, o_ref):
    ones_blk = _head_ones()

    def seg_sum(x):
        hi, lo = _split_hi_lo(x)
        return jnp.dot(hi, ones_blk, preferred_element_type=F32) + jnp.dot(lo, ones_blk, preferred_element_type=F32)

    for cb in range(D_WIDTH // LANES):
        ls = slice(cb * LANES, (cb + 1) * LANES)
        y = y_ref[0, :, ls] + y_ref[1, :, ls]
        mu = seg_sum(y) * (1.0 / D_HS)
        yc = y - mu
        var = seg_sum(yc * yc) * (1.0 / D_HS)
        yn = yc * lax.rsqrt(var + D_LN_EPS) * lng_ref[:, ls] + lnb_ref[:, ls]
        asum = a_ref[0, :, ls] + a_ref[1, :, ls]
        bonus = seg_sum(r_ref[:, ls] * k_ref[:, ls] * rkp_ref[:, ls] * (2.0 + (asum - 2.0) * kap_ref[:, ls]))
        o_ref[:, ls] = (yn + bonus * v_ref[:, ls]) * g_ref[:, ls]


def _d_out(y, zc, a, g, k_a, r_k, ln_g, ln_b, tm=512):
    n = g.shape[0]

    def z_spec(colblk):
        return pl.BlockSpec((tm, D_WIDTH), lambda i: (i, colblk))

    dspec = pl.BlockSpec((2, tm, D_WIDTH), lambda i: (0, i, 0))
    pspec = pl.BlockSpec((1, D_WIDTH), lambda i: (0, 0))
    return pl.pallas_call(
        _d_out_kernel,
        grid=(n // tm,),
        in_specs=[dspec, z_spec(0), z_spec(1), z_spec(2), dspec, pl.BlockSpec((tm, D_WIDTH), lambda i: (i, 0)),
                  pspec, pspec, pspec, pspec],
        out_specs=pl.BlockSpec((tm, D_WIDTH), lambda i: (i, 0)),
        out_shape=jax.ShapeDtypeStruct((n, D_WIDTH), F32),
        compiler_params=_params("parallel"),
        name="d_out",
    )(y, zc, zc, zc, a, g, k_a.reshape(1, D_WIDTH), r_k.reshape(1, D_WIDTH), ln_g.reshape(1, D_WIDTH),
      ln_b.reshape(1, D_WIDTH))


TOPK_TM = 256
BIG = float(2 ** 30)


def _extract_topk(s, labels):
    iota16 = lax.broadcasted_iota(jnp.int32, (PEER_TOPK, s.shape[1]), 0)
    rank = jnp.full(s.shape, BIG, F32)
    vals = jnp.zeros((PEER_TOPK, s.shape[1]), F32)
    for r in range(PEER_TOPK):
        m = jnp.max(s, axis=0, keepdims=True)
        idx = jnp.min(jnp.where(s == m, labels, BIG), axis=0, keepdims=True)
        hit = labels == idx
        rank = jnp.where(hit, float(r), rank)
        vals = jnp.where(iota16 == r, m, vals)
        s = jnp.where(hit, -jnp.inf, s)
    return rank, vals


_CAND_PAIRS = ([(0, b) for b in range(16)] + [(1, b) for b in range(8)] + [(2, b) for b in range(8)]
               + [(3, b) for b in range(8)] + [(4, b) for b in range(4)] + [(5, b) for b in range(4)]
               + [(6, b) for b in range(4)] + [(7, b) for b in range(4)] + [(a, 0) for a in range(8, 16)])
N_CAND = len(_CAND_PAIRS)


def _cand_labels():
    lab = np.array([a * PEER_TOPK + b for a, b in _CAND_PAIRS], np.float32)
    return jnp.asarray(np.broadcast_to(lab[:, None], (N_CAND, LANES)).copy())


def _peer_topk_kernel(q_ref, keys_ref, lab_ref, lim_ref, e1_ref, rb_ref, e2_ref):
    iota_k = lax.broadcasted_iota(jnp.int32, (PEER_NKEYS, LANES), 0).astype(F32)
    row8 = lax.broadcasted_iota(jnp.int32, (8, LANES), 0)
    labels = lab_ref[...]
    for c in range(TOPK_TM // LANES):
        cs = slice(c * LANES, (c + 1) * LANES)
        q1 = q_ref[cs, 0:PEER_HALF].astype(BF16)
        q2 = q_ref[cs, PEER_HALF:2 * PEER_HALF].astype(BF16)
        s1 = _dot_nt(keys_ref[0, 0], q1)
        s2 = _dot_nt(keys_ref[0, 1], q2)
        rank1, sv1 = _extract_topk(s1, iota_k)
        rank2, sv2 = _extract_topk(s2, iota_k)
        lo8 = sv2[0:8]
        lo4 = jnp.where(row8 < 4, lo8, pltpu.roll(lo8, 4, 0))
        cand = jnp.concatenate([
            sv1[0:1] + lo8, sv1[0:1] + sv2[8:16], sv1[1:2] + lo8, sv1[2:3] + lo8, sv1[3:4] + lo8,
            jnp.where(row8 < 4, sv1[4:5], sv1[5:6]) + lo4, jnp.where(row8 < 4, sv1[6:7], sv1[7:8]) + lo4,
            sv1[8:16] + sv2[0:1]], axis=0)
        crank, cvals = _extract_topk(cand, labels)
        z = jnp.sum(jnp.exp(cvals - cvals[0:1]), axis=0, keepdims=True)
        sel = jnp.where(crank < BIG, 1.0, 0.0)

        def count(lo, hi):
            return jnp.sum(sel[lo:hi], axis=0, keepdims=True)

        def count_half(lo, first):
            part = jnp.where((row8 < 4) if first else (row8 >= 4), sel[lo:lo + 8], 0.0)
            return jnp.sum(part, axis=0, keepdims=True)

        n_sel = [count(0, 16), count(16, 24), count(24, 32), count(32, 40),
                 count_half(40, True), count_half(40, False), count_half(48, True), count_half(48, False)]
        n_sel += [sel[56 + a:57 + a] for a in range(8)]
        lim = jnp.zeros_like(s1)
        for a in range(PEER_TOPK):
            lim = jnp.where(rank1 == float(a), n_sel[a], lim)
        lim_ref[0, :, cs] = lim
        e1_ref[0, :, cs] = jnp.exp(s1 - sv1[0:1])
        rb_ref[0, :, cs] = rank2
        e2_ref[0, :, cs] = jnp.exp(s2 - sv2[0:1]) / z


def _peer_topk(q, keys):
    n = q.shape[0]
    ospec = pl.BlockSpec((1, PEER_NKEYS, TOPK_TM), lambda i, h: (h, 0, i))
    return pl.pallas_call(
        _peer_topk_kernel,
        grid=(n // TOPK_TM, PEER_HEADS),
        in_specs=[pl.BlockSpec((TOPK_TM, 2 * PEER_HALF), lambda i, h: (i, h)),
                  pl.BlockSpec((1, 2, PEER_NKEYS, PEER_HALF), lambda i, h: (h, 0, 0, 0)),
                  pl.BlockSpec((N_CAND, LANES), lambda i, h: (0, 0))],
        out_specs=[ospec] * 4,
        out_shape=[jax.ShapeDtypeStruct((PEER_HEADS, PEER_NKEYS, n), F32)] * 4,
        compiler_params=_params("parallel", "parallel"),
        name="peer_topk",
    )(q, keys, _cand_labels())


PEER_TM = 512
PEER_TI = 8
PEER_RG = 4
PEER_JT = 32


def _gelu(x):
    return 0.5 * x * (1.0 + lax.erf(x * (2.0 ** -0.5)))


def _peer_dense_kernel(x_ref, u_ref, v_ref, lim_ref, e1_ref, rb_ref, e2_ref, res_ref, gate_ref, o_ref, w_ref):
    e = pl.program_id(1)

    @pl.when(e == 0)
    def _():
        o_ref[...] = jnp.zeros_like(o_ref)

    hid = _dot_nt(u_ref[...], x_ref[...])
    for c in range(PEER_TM // LANES):
        cs = slice(c * LANES, (c + 1) * LANES)
        for jt in range(PEER_NKEYS // PEER_JT):
            js = slice(jt * PEER_JT, (jt + 1) * PEER_JT)
            for rg in range(PEER_TI // PEER_RG):
                rows = range(rg * PEER_RG, (rg + 1) * PEER_RG)
                g = [jnp.zeros((PEER_JT, LANES), F32) for _ in rows]
                for h in range(PEER_HEADS):
                    rb = rb_ref[h, js, cs]
                    e2 = e2_ref[h, js, cs]
                    for k, ii in enumerate(rows):
                        g[k] = g[k] + jnp.where(rb < lim_ref[h, ii:ii + 1, cs], e2, 0.0) * e1_ref[h, ii:ii + 1, cs]
                for k, ii in enumerate(rows):
                    es = slice(ii * PEER_NKEYS + jt * PEER_JT, ii * PEER_NKEYS + (jt + 1) * PEER_JT)
                    w_ref[es, cs] = (g[k] * _gelu(hid[es, cs])).astype(BF16)
    o_ref[...] += _dot_tn(w_ref[...], v_ref[...])

    @pl.when(e == pl.num_programs(1) - 1)
    def _():
        o_ref[...] = res_ref[...] + gate_ref[0] * o_ref[...]


def _peer_dense(xm, u, v, layer, lim, e1, rb, e2, res, gate):
    n, d = xm.shape
    n_exp = u.shape[1]
    te = PEER_TI * PEER_NKEYS
    sel_spec = pl.BlockSpec((PEER_HEADS, PEER_NKEYS, PEER_TM), lambda i, e: (0, 0, i))
    row_spec = pl.BlockSpec((PEER_HEADS, PEER_TI, PEER_TM), lambda i, e: (0, e, i))
    return pl.pallas_call(
        _peer_dense_kernel,
        grid=(n // PEER_TM, n_exp // te),
        in_specs=[pl.BlockSpec((PEER_TM, d), lambda i, e: (i, 0)),
                  pl.BlockSpec((None, te, d), lambda i, e: (layer, e, 0)),
                  pl.BlockSpec((None, te, d), lambda i, e: (layer, e, 0)),
                  row_spec, row_spec, sel_spec, sel_spec,
                  pl.BlockSpec((PEER_TM, d), lambda i, e: (i, 0)),
                  pl.BlockSpec((1, 1, d), lambda i, e: (_cond_of_tile(i, PEER_TM), 0, 0))],
        out_specs=pl.BlockSpec((PEER_TM, d), lambda i, e: (i, 0)),
        out_shape=jax.ShapeDtypeStruct((n, d), F32),
        scratch_shapes=[pltpu.VMEM((te, PEER_TM), BF16)],
        compiler_params=_params("parallel", "arbitrary"),
        name="peer_dense",
    )(xm, u, v, lim, e1, rb, e2, res, gate)


def _peer_layer(x, layer, norm_g, shift, scale, gate, wq, keys, u, v):
    xm = _modulate(x, norm_g, shift, scale)
    q = _matmul(xm, wq, layer, out_dtype=BF16)
    lim, e1, rb, e2 = _peer_topk(q, keys)
    return _peer_dense(xm, u, v, layer, lim, e1, rb, e2, x, gate)


EVEN_PAD = 4096
ODD_PAD = 6656
S_ALL = PAST_LEN + DEC_SEQ


def _even_mixer(h, w_in, layer, lam, subln_g, q_norm_g, w_uq, kv_norm_g, w_ukv, lam_init,
                cache_k, cache_v, cache_ckv, cache_kpe, rope_cos, rope_sin):
    z = _matmul(h, w_in, layer)
    ckv_n = _rmsnorm(z, kv_norm_g, col=3584 // B_KV_RANK)
    cq_n = _rmsnorm(z, q_norm_g, out_dtype=BF16, col=3072 // B_Q_RANK)
    wq3 = w_uq.reshape(B_Q_RANK, HEADS, B_NOPE + B_ROPE)
    w_uq_r = jnp.concatenate([wq3[:, :, :B_NOPE].reshape(B_Q_RANK, -1), wq3[:, :, B_NOPE:].reshape(B_Q_RANK, -1)], axis=1)
    qb = _matmul(cq_n, w_uq_r.astype(BF16), tn=512)
    new = (z[:N_CTX, 1024:2048].reshape(BATCH, SEQ, HEADS, 2 * A_QK),
           z[:N_CTX, 2048:3072].reshape(BATCH, SEQ, HEADS, HEAD_DIM),
           ckv_n[:N_CTX].reshape(BATCH, SEQ, B_KV_RANK), z[:N_CTX, 3840:3904].reshape(BATCH, SEQ, B_ROPE))

    qa_s = _rope(z, rope_cos, rope_sin, ATT_W, col=0)
    ka_s = _rope(z, rope_cos, rope_sin, ATT_W, col=1)
    qr_s = _rope(qb, rope_cos, rope_sin, HEADS * B_ROPE, col=2)
    kpe_s = _rope(z, rope_cos, rope_sin, LANES, col=3840 // LANES)[:, :B_ROPE]

    def with_ctx(cache, own, width):
        both = jnp.concatenate([cache.reshape(DEC_BATCH, PAST_LEN, width).astype(own.dtype),
                                own.reshape(DEC_BATCH, DEC_SEQ, width)], axis=1)
        return both.reshape(DEC_BATCH * S_ALL, width)

    ka_all = with_ctx(cache_k, ka_s, ATT_W)
    va_all = with_ctx(cache_v, z[N_CTX:, 2048:3072].astype(BF16), ATT_W)
    ckv_all = with_ctx(cache_ckv, ckv_n[N_CTX:], B_KV_RANK)
    kpe_all = with_ctx(cache_kpe, kpe_s, B_ROPE)

    wkv3 = w_ukv.reshape(B_KV_RANK, HEADS, B_NOPE + HEAD_DIM)
    w_ukv_r = jnp.concatenate([wkv3[:, :, :B_NOPE].reshape(B_KV_RANK, -1), wkv3[:, :, B_NOPE:].reshape(B_KV_RANK, -1)], axis=1)
    n_skv = DEC_BATCH * S_ALL
    kv = _matmul(jnp.concatenate([ckv_all, ckv_n[:N_CTX]], axis=0), w_ukv_r.astype(BF16), out_dtype=BF16)

    oa_c = _diff_attention(BATCH, SEQ, (z, _q_rows(SEQ, ATT_W, 0, 0)), (z, _kv_rows(SEQ, ATT_W, 0, 1)),
                           (z, _kv_rows(SEQ, ATT_W, 0, 2)), lam, subln_g, lam_init)
    oa_s = _diff_attention(DEC_BATCH, DEC_SEQ, (qa_s, _q_rows(DEC_SEQ, ATT_W)), (ka_all, _kv_rows(S_ALL, ATT_W)),
                           (va_all, _kv_rows(S_ALL, ATT_W)), lam, subln_g, lam_init)
    ob_c = _mla_attention(BATCH, SEQ, (qb, _q_rows(SEQ, ATT_W, 0, 0)), (qb, _q_rows(SEQ, HEADS * B_ROPE, 0, 2)),
                          (kv, _kv_rows(SEQ, ATT_W, n_skv, 0)), (z, _kv_rows(SEQ, LANES, 0, 3840 // LANES)),
                          (kv, _kv_rows(SEQ, ATT_W, n_skv, 1)))
    ob_s = _mla_attention(DEC_BATCH, DEC_SEQ, (qb, _q_rows(DEC_SEQ, ATT_W, N_CTX, 0)),
                          (qr_s, _q_rows(DEC_SEQ, HEADS * B_ROPE)), (kv, _kv_rows(S_ALL, ATT_W, 0, 0)),
                          (kpe_all, _kv_rows(S_ALL, B_ROPE)), (kv, _kv_rows(S_ALL, ATT_W, 0, 1)))
    return (oa_c, oa_s, ob_c, ob_s), new


def _odd_mixer(h, w_in, layer, rpb, conv_w, w0, w2, a0, a2, g2, k_k, k_a, r_k, ln_g, ln_b,
               cache_k, cache_v, state):
    z = _matmul(h, w_in, layer)
    new_k = z[:N_CTX, 1024:2048].reshape(BATCH, SEQ, HEADS, HEAD_DIM)
    new_v = z[:N_CTX, 2048:3072].reshape(BATCH, SEQ, HEADS, HEAD_DIM)
    oc_c = _plain_attention(BATCH, SEQ, (z, _q_rows(SEQ, ATT_W, 0, 0)), (z, _kv_rows(SEQ, ATT_W, 0, 1)),
                            (z, _kv_rows(SEQ, ATT_W, 0, 2)))
    oc_s = _na_attention(z, cache_k.reshape(DEC_BATCH * PAST_LEN, ATT_W), cache_v.reshape(DEC_BATCH * PAST_LEN, ATT_W),
                         _na_bias_tables(rpb))

    zc_c = _conv3(z, conv_w, SEQ, 0, BATCH)
    zc_s = _conv3(z, conv_w, DEC_SEQ, N_CTX, DEC_BATCH)
    lw_c, a_c, g_c = _d_prep(zc_c, w0, w2, a0, a2, g2)
    lw_s, a_s, g_s = _d_prep(zc_s, w0, w2, a0, a2, g2)
    rk = r_k.reshape(-1)
    y_c, sfin = _rwkv_chunked(zc_c, lw_c, a_c, k_k, k_a,
                              jnp.zeros((BATCH, 2, D_WIDTH // LANES, LANES, LANES), F32), BATCH, SEQ)
    sfin = _head_states(sfin)
    y_s, _ = _rwkv_chunked(zc_s, lw_s, a_s, k_k, k_a, _pair_states(state), DEC_BATCH, DEC_SEQ)
    od_c = _d_out(y_c, zc_c, a_c, g_c, k_a, rk, ln_g, ln_b)
    od_s = _d_out(y_s, zc_s, a_s, g_s, k_a, rk, ln_g, ln_b)
    return (oc_c, oc_s, od_c, od_s), (new_k, new_v, sfin)


def kernel(x_prompt, x_sample, cache_a_k, cache_a_v, cache_b_ckv, cache_b_kpe, cache_c_k, cache_c_v, state_d, c, c_ctx, ada_w, ada_b, norm1_g, norm2_g, w_out, peer_wq, peer_keys, peer_u, peer_v, final_g, ab_w_in, a_lam, a_subln_g, b_q_norm_g, b_w_uq, b_kv_norm_g, b_w_ukv, cd_w_in, c_rpb, d_conv, d_w0, d_w2, d_a0, d_a2, d_g2, d_k_k, d_k_a, d_r_k, d_ln_g, d_ln_b):
    x = jnp.concatenate([x_prompt.reshape(N_CTX, D_MODEL), x_sample.reshape(N_SMP, D_MODEL)], axis=0)
    cond8 = jnp.pad(jnp.concatenate([c_ctx[None, :], c], axis=0), ((0, 8 - N_COND), (0, 0)))
    ada = _ada_all(cond8, ada_w, ada_b)
    rope_cos, rope_sin = _rope_tables()
    w_even = jnp.pad(ab_w_in, ((0, 0), (0, 0), (0, EVEN_PAD - EVEN_IN))).astype(BF16)
    w_odd = jnp.pad(cd_w_in, ((0, 0), (0, 0), (0, ODD_PAD - ODD_IN))).astype(BF16)
    w_out_b, wq_b, keys_b = w_out.astype(BF16), peer_wq.astype(BF16), peer_keys.astype(BF16)
    u_b, v_b = peer_u.astype(BF16), peer_v.astype(BF16)
    new_ak, new_av, new_bc, new_bp, new_ck, new_cv, new_sd = [], [], [], [], [], [], []
    for l in range(DEPTH):
        i = l // 2
        mods = ada[l, :N_COND].reshape(N_COND, 6, 1, D_MODEL)
        sh1, sc1, g1, sh2, sc2, g2 = (mods[:, m] for m in range(6))
        h = _modulate(x, norm1_g[l], sh1, sc1)
        if l % 2 == 0:
            lam_init = 0.8 - 0.6 * math.exp(-0.3 * l)
            o, (ak, av, bc, bp) = _even_mixer(
                h, w_even, i, a_lam[i], a_subln_g[i], b_q_norm_g[i], b_w_uq[i], b_kv_norm_g[i], b_w_ukv[i],
                lam_init, cache_a_k[:, i], cache_a_v[:, i], cache_b_ckv[:, i], cache_b_kpe[:, i], rope_cos, rope_sin)
            new_ak.append(ak)
            new_av.append(av)
            new_bc.append(bc)
            new_bp.append(bp)
        else:
            o, (ck, cv, sd) = _odd_mixer(
                h, w_odd, i, c_rpb[i], d_conv[i], d_w0[i], d_w2[i], d_a0[i], d_a2[i], d_g2[i],
                d_k_k[i], d_k_a[i], d_r_k[i], d_ln_g[i], d_ln_b[i], cache_c_k[:, i], cache_c_v[:, i], state_d[:, i])
            new_ck.append(ck)
            new_cv.append(cv)
            new_sd.append(sd)
        x = _matmul_residual(*o, w_out_b, l, x, g1)
        x = _peer_layer(x, l, norm2_g[l], sh2, sc2, g2, wq_b, keys_b[l], u_b, v_b)
    y = _rmsnorm(x, final_g)
    y_prompt = y[:N_CTX].reshape(BATCH, SEQ, D_MODEL)
    y_sample = y[N_CTX:].reshape(DEC_BATCH, DEC_SEQ, D_MODEL)
    return (y_prompt, y_sample, jnp.stack(new_ak, axis=1), jnp.stack(new_av, axis=1), jnp.stack(new_bc, axis=1),
            jnp.stack(new_bp, axis=1), jnp.stack(new_ck, axis=1), jnp.stack(new_cv, axis=1), jnp.stack(new_sd, axis=1))
```

```python
import functools
import math

import numpy as np
import jax
import jax.numpy as jnp
from jax import lax
from jax.experimental import pallas as pl
from jax.experimental.pallas import tpu as pltpu

F32 = jnp.float32
BF16 = jnp.bfloat16

D_MODEL = 2048
BATCH = 16
SEQ = 256
DEPTH = 4
DEC_BATCH = 2
DEC_SEQ = 1024
PAST_LEN = 512
GRID_W = 64
GRID_ROWS = DEC_SEQ // GRID_W
ROPE_BASE = 10000.0
RMS_EPS = 1e-6
NEG_INF = -1e30
N_CTX = BATCH * SEQ
N_SMP = DEC_BATCH * DEC_SEQ
N_TOK = N_CTX + N_SMP
N_COND = 1 + DEC_BATCH

HEADS = 8
HEAD_DIM = 128
A_QK = 64
B_Q_RANK = 512
B_KV_RANK = 256
B_NOPE = 128
B_ROPE = 64
NA_KH = 8
NA_KW = 16
NA_WIN = NA_KH * GRID_W
D_HS = 64
D_WIDTH = 1024
D_HEADS = 16
D_LORA = 64
D_GATE_LORA = 128
D_IN = 3 * D_WIDTH + 2 * D_LORA + 2 * D_LORA + D_GATE_LORA
D_LN_EPS = 64e-5
EVEN_IN = 3904
ODD_IN = 6528
PEER_HEADS = 8
PEER_NKEYS = 128
PEER_HALF = 128
PEER_TOPK = 16

LANES = 128
VMEM_LIMIT = 56 * 1024 * 1024


def _params(*sem):
    return pltpu.CompilerParams(dimension_semantics=sem, vmem_limit_bytes=VMEM_LIMIT)


def _dot_nt(a, b):
    return lax.dot_general(a, b, (((1,), (1,)), ((), ())), preferred_element_type=F32)


def _dot_tn(a, b):
    return lax.dot_general(a, b, (((0,), (0,)), ((), ())), preferred_element_type=F32)


def _cond_of_tile(i, tm):
    n_ctx_tiles = N_CTX // tm
    per_batch = DEC_SEQ // tm
    return jnp.where(i < n_ctx_tiles, 0, 1 + (i - n_ctx_tiles) // per_batch)


def _ada_kernel(c_ref, w_ref, b_ref, o_ref):
    c = c_ref[...]
    sc = c * (1.0 / (1.0 + jnp.exp(-c)))
    o_ref[0] = jnp.dot(sc.astype(BF16), w_ref[0].astype(BF16), preferred_element_type=F32) + b_ref[0]


def _ada_all(cond8, ada_w, ada_b):
    tn = 1024
    n_out = ada_w.shape[-1]
    return pl.pallas_call(
        _ada_kernel,
        grid=(DEPTH, n_out // tn),
        in_specs=[pl.BlockSpec((8, D_MODEL), lambda l, j: (0, 0)),
                  pl.BlockSpec((1, D_MODEL, tn), lambda l, j: (l, 0, j)),
                  pl.BlockSpec((1, 1, tn), lambda l, j: (l, 0, j))],
        out_specs=pl.BlockSpec((1, 8, tn), lambda l, j: (l, 0, j)),
        out_shape=jax.ShapeDtypeStruct((DEPTH, 8, n_out), F32),
        compiler_params=_params("parallel", "parallel"),
        name="ada",
    )(cond8, ada_w, ada_b.reshape(DEPTH, 1, n_out))


def _modulate_kernel(x_ref, g_ref, sh_ref, sc_ref, o_ref):
    x = x_ref[...]
    y = x * lax.rsqrt(jnp.mean(x * x, axis=-1, keepdims=True) + RMS_EPS) * g_ref[...]
    o_ref[...] = (y * (1.0 + sc_ref[0]) + sh_ref[0]).astype(o_ref.dtype)


def _modulate(x, g, shift, scale, out_dtype=BF16, tm=512):
    n, d = x.shape
    return pl.pallas_call(
        _modulate_kernel,
        grid=(n // tm,),
        in_specs=[pl.BlockSpec((tm, d), lambda i: (i, 0)),
                  pl.BlockSpec((1, d), lambda i: (0, 0)),
                  pl.BlockSpec((1, 1, d), lambda i: (_cond_of_tile(i, tm), 0, 0)),
                  pl.BlockSpec((1, 1, d), lambda i: (_cond_of_tile(i, tm), 0, 0))],
        out_specs=pl.BlockSpec((tm, d), lambda i: (i, 0)),
        out_shape=jax.ShapeDtypeStruct((n, d), out_dtype),
        compiler_params=_params("parallel"),
        name="modulate",
    )(x, g.reshape(1, d), shift, scale)


def _rmsnorm_kernel(x_ref, g_ref, o_ref):
    x = x_ref[...]
    y = x * lax.rsqrt(jnp.mean(x * x, axis=-1, keepdims=True) + RMS_EPS) * g_ref[...]
    o_ref[...] = y.astype(o_ref.dtype)


def _rmsnorm(x, g, out_dtype=F32, tm=512, col=0):
    n = x.shape[0]
    d = g.shape[-1]
    return pl.pallas_call(
        _rmsnorm_kernel,
        grid=(n // tm,),
        in_specs=[pl.BlockSpec((tm, d), lambda i: (i, col)),
                  pl.BlockSpec((1, d), lambda i: (0, 0))],
        out_specs=pl.BlockSpec((tm, d), lambda i: (i, 0)),
        out_shape=jax.ShapeDtypeStruct((n, d), out_dtype),
        compiler_params=_params("parallel"),
        name="rmsnorm",
    )(x, g.reshape(1, d))


def _mm_kernel(x_ref, w_ref, o_ref):
    o_ref[...] = jnp.dot(x_ref[...].astype(BF16), w_ref[...], preferred_element_type=F32).astype(o_ref.dtype)


def _weight_spec(w, layer, tn):
    if w.ndim == 2:
        return pl.BlockSpec((w.shape[0], tn), lambda i, j: (0, j))
    return pl.BlockSpec((None, w.shape[1], tn), lambda i, j: (layer, 0, j))


MM_TM = 1024
MM_TN = 1024


def _matmul(x, w, layer=None, out_dtype=F32, tm=MM_TM, tn=None, x_col=0, x_width=None):
    m = x.shape[0]
    k = x_width or x.shape[1]
    n = w.shape[-1]
    if tn is None:
        tn = MM_TN if n % MM_TN == 0 else MM_TN // 2
    tn = min(tn, n)
    assert m % tm == 0 and n % tn == 0, (m, n, tm, tn)
    return pl.pallas_call(
        _mm_kernel,
        grid=(m // tm, n // tn),
        in_specs=[pl.BlockSpec((tm, k), lambda i, j: (i, x_col)),
                  _weight_spec(w, layer, tn)],
        out_specs=pl.BlockSpec((tm, tn), lambda i, j: (i, j)),
        out_shape=jax.ShapeDtypeStruct((m, n), out_dtype),
        compiler_params=_params("parallel", "parallel"),
        name="matmul",
    )(x, w)


def _mm_res_kernel(lc_ref, ls_ref, rc_ref, rs_ref, w_ref, res_ref, gate_ref, o_ref, *, n_ctx_tiles):
    half = w_ref.shape[0] // 2

    def emit(l_ref, r_ref):
        acc = jnp.dot(l_ref[...].astype(BF16), w_ref[0:half, :], preferred_element_type=F32)
        acc = acc + jnp.dot(r_ref[...].astype(BF16), w_ref[half:, :], preferred_element_type=F32)
        o_ref[...] = res_ref[...] + gate_ref[0] * acc

    @pl.when(pl.program_id(0) < n_ctx_tiles)
    def _():
        emit(lc_ref, rc_ref)

    @pl.when(pl.program_id(0) >= n_ctx_tiles)
    def _():
        emit(ls_ref, rs_ref)


def _matmul_residual(left_c, left_s, right_c, right_s, w, layer, res, gate, tm=MM_TM, tn=MM_TN // 2):
    m = res.shape[0]
    n = w.shape[-1]
    kh = left_c.shape[1]
    nct = left_c.shape[0] // tm
    ctx_spec = pl.BlockSpec((tm, kh), lambda i, j: (jnp.minimum(i, nct - 1), 0))
    smp_spec = pl.BlockSpec((tm, kh), lambda i, j: (jnp.maximum(i - nct, 0), 0))
    return pl.pallas_call(
        functools.partial(_mm_res_kernel, n_ctx_tiles=nct),
        grid=(m // tm, n // tn),
        in_specs=[ctx_spec, smp_spec, ctx_spec, smp_spec,
                  _weight_spec(w, layer, tn),
                  pl.BlockSpec((tm, tn), lambda i, j: (i, j)),
                  pl.BlockSpec((1, 1, tn), lambda i, j: (_cond_of_tile(i, tm), 0, j))],
        out_specs=pl.BlockSpec((tm, tn), lambda i, j: (i, j)),
        out_shape=jax.ShapeDtypeStruct((m, n), F32),
        compiler_params=_params("parallel", "parallel"),
        name="matmul_residual",
    )(left_c, left_s, right_c, right_s, w, res, gate)


def _rope_tables():
    nf = 16
    inv = ROPE_BASE ** (-np.arange(nf, dtype=np.float64) / nf)
    t = np.arange(DEC_SEQ)
    rows, cols = t // GRID_W, t % GRID_W
    lane = np.arange(64)
    pos = np.where(lane[None, :] < 32, rows[:, None], cols[:, None]).astype(np.float32)
    ang = (pos * inv[lane % nf][None, :].astype(np.float32)).astype(np.float32)
    first = (lane % 32) < nf
    cos = np.cos(ang.astype(np.float64))
    sin = np.sin(ang.astype(np.float64)) * np.where(first, -1.0, 1.0)[None, :]
    cos = np.tile(cos, (1, 2)).astype(np.float32)
    sin = np.tile(sin, (1, 2)).astype(np.float32)
    return jnp.asarray(cos), jnp.asarray(sin)


def _rope_kernel(x_ref, cos_ref, sin_ref, o_ref):
    cos = cos_ref[...]
    sin = sin_ref[...]
    lane = lax.broadcasted_iota(jnp.int32, cos.shape, 1)
    first = (lane % 32) < 16
    for c in range(x_ref.shape[1] // LANES):
        x = x_ref[:, c * LANES:(c + 1) * LANES].astype(F32)
        partner = jnp.where(first, pltpu.roll(x, LANES - 16, 1), pltpu.roll(x, 16, 1))
        o_ref[:, c * LANES:(c + 1) * LANES] = (x * cos + partner * sin).astype(o_ref.dtype)


def _rope(x, cos, sin, w, col=0, out_dtype=BF16, tm=256):
    n = N_SMP
    row0 = (x.shape[0] - N_SMP) // tm
    per = DEC_SEQ // tm
    return pl.pallas_call(
        _rope_kernel,
        grid=(n // tm,),
        in_specs=[pl.BlockSpec((tm, w), lambda i: (row0 + i, col)),
                  pl.BlockSpec((tm, LANES), lambda i: (i % per, 0)),
                  pl.BlockSpec((tm, LANES), lambda i: (i % per, 0))],
        out_specs=pl.BlockSpec((tm, w), lambda i: (i, 0)),
        out_shape=jax.ShapeDtypeStruct((n, w), out_dtype),
        compiler_params=_params("parallel"),
        name="rope",
    )(x, cos, sin)


def _softmax(s):
    p = jnp.exp(s - jnp.max(s, axis=-1, keepdims=True))
    return p / jnp.sum(p, axis=-1, keepdims=True)


def _diff_attn_kernel(lam_ref, g_ref, q_ref, k_ref, v_ref, o_ref, *, lam_init):
    lam = lam_ref[...]
    l1 = jnp.sum(jnp.sum(lam[0:1] * lam[1:2], axis=-1, keepdims=True), axis=0, keepdims=True)
    l2 = jnp.sum(jnp.sum(lam[2:3] * lam[3:4], axis=-1, keepdims=True), axis=0, keepdims=True)
    lam_val = jnp.exp(l1) - jnp.exp(l2) + lam_init
    scale = A_QK ** -0.5
    for h in range(HEADS):
        sl = slice(h * HEAD_DIM, (h + 1) * HEAD_DIM)
        q = q_ref[:, sl].astype(BF16)
        k = k_ref[:, sl].astype(BF16)
        p1 = _softmax(_dot_nt(q[:, :A_QK], k[:, :A_QK]) * scale)
        p2 = _softmax(_dot_nt(q[:, A_QK:], k[:, A_QK:]) * scale)
        a = p1 - lam_val * p2
        o = jnp.dot(a.astype(BF16), v_ref[:, sl].astype(BF16), preferred_element_type=F32)
        o = o * lax.rsqrt(jnp.mean(o * o, axis=-1, keepdims=True) + RMS_EPS) * g_ref[...]
        o_ref[:, sl] = (o * (1.0 - lam_init)).astype(o_ref.dtype)


ATT_TQ = 256
ATT_W = HEADS * HEAD_DIM


def _q_rows(t, width, row0=0, col=0):
    per, off = t // ATT_TQ, row0 // ATT_TQ
    return pl.BlockSpec((ATT_TQ, width), lambda b, j: (off + b * per + j, col))


def _kv_rows(s, width, row0=0, col=0):
    off = row0 // s
    return pl.BlockSpec((s, width), lambda b, j: (off + b, col))


def _attention_call(kernel, name, nb, t, operands, specs):
    return pl.pallas_call(
        kernel,
        grid=(nb, t // ATT_TQ),
        in_specs=specs,
        out_specs=_q_rows(t, ATT_W),
        out_shape=jax.ShapeDtypeStruct((nb * t, ATT_W), BF16),
        compiler_params=_params("parallel", "parallel"),
        name=name,
    )(*operands)


def _diff_attention(nb, t, q, k, v, lam, subln_g, lam_init):
    const = [pl.BlockSpec((4, A_QK), lambda i, j: (0, 0)), pl.BlockSpec((1, HEAD_DIM), lambda i, j: (0, 0))]
    return _attention_call(functools.partial(_diff_attn_kernel, lam_init=lam_init), "diff_attention", nb, t,
                           [lam, subln_g.reshape(1, HEAD_DIM), q[0], k[0], v[0]], const + [q[1], k[1], v[1]])


def _mla_attn_kernel(qn_ref, qr_ref, kn_ref, kr_ref, v_ref, o_ref):
    scale = (B_NOPE + B_ROPE) ** -0.5
    kr = kr_ref[:, 0:B_ROPE].astype(BF16)
    for h in range(HEADS):
        sl = slice(h * HEAD_DIM, (h + 1) * HEAD_DIM)
        s = _dot_nt(qn_ref[:, sl].astype(BF16), kn_ref[:, sl])
        s = s + _dot_nt(qr_ref[:, h * B_ROPE:(h + 1) * B_ROPE].astype(BF16), kr)
        p = _softmax(s * scale)
        o = jnp.dot(p.astype(BF16), v_ref[:, sl], preferred_element_type=F32)
        o_ref[:, sl] = o.astype(o_ref.dtype)


def _mla_attention(nb, t, qn, qr, kn, kr, v):
    ops = [qn, qr, kn, kr, v]
    return _attention_call(_mla_attn_kernel, "mla_attention", nb, t, [o[0] for o in ops], [o[1] for o in ops])


def _plain_attn_kernel(q_ref, k_ref, v_ref, o_ref):
    scale = HEAD_DIM ** -0.5
    for h in range(HEADS):
        sl = slice(h * HEAD_DIM, (h + 1) * HEAD_DIM)
        p = _softmax(_dot_nt(q_ref[:, sl].astype(BF16), k_ref[:, sl].astype(BF16)) * scale)
        o = jnp.dot(p.astype(BF16), v_ref[:, sl].astype(BF16), preferred_element_type=F32)
        o_ref[:, sl] = o.astype(o_ref.dtype)


def _plain_attention(nb, t, q, k, v):
    ops = [q, k, v]
    return _attention_call(_plain_attn_kernel, "plain_attention", nb, t, [o[0] for o in ops], [o[1] for o in ops])


def _na_window_start(r):
    return jnp.clip(r - NA_KH // 2, 0, GRID_ROWS - NA_KH)


def _na_attn_kernel(q_ref, k_ref, v_ref, kc_ref, vc_ref, bias_ref, o_ref):
    scale = HEAD_DIM ** -0.5
    r = pl.program_id(1)
    start = pl.multiple_of(_na_window_start(r) * GRID_W, GRID_W)
    for h in range(HEADS):
        sl = slice(h * HEAD_DIM, (h + 1) * HEAD_DIM)
        q = q_ref[:, sl].astype(BF16)
        kw = k_ref[pl.ds(start, NA_WIN), sl].astype(BF16)
        vw = v_ref[pl.ds(start, NA_WIN), sl].astype(BF16)
        s_win = _dot_nt(q, kw) * scale + bias_ref[0, h]
        s_ctx = _dot_nt(q, kc_ref[:, sl].astype(BF16)) * scale
        m = jnp.maximum(jnp.max(s_win, axis=-1, keepdims=True), jnp.max(s_ctx, axis=-1, keepdims=True))
        p_win = jnp.exp(s_win - m)
        p_ctx = jnp.exp(s_ctx - m)
        den = jnp.sum(p_win, axis=-1, keepdims=True) + jnp.sum(p_ctx, axis=-1, keepdims=True)
        o = (jnp.dot(p_win.astype(BF16), vw, preferred_element_type=F32)
             + jnp.dot(p_ctx.astype(BF16), vc_ref[:, sl].astype(BF16), preferred_element_type=F32))
        o_ref[:, sl] = (o / den).astype(o_ref.dtype)


NA_NDC = 2 * NA_KW


def _na_bias_kernel(r_ref, e_ref, o_ref):
    o_ref[0] = jnp.dot(r_ref[0], e_ref[...], preferred_element_type=F32, precision=lax.Precision.HIGHEST)


def _na_bias_tables(rpb):
    cols = np.arange(GRID_W)
    cs = np.clip(cols - NA_KW // 2, 0, GRID_W - NA_KW)
    col_in = (cols[None, :] >= cs[:, None]) & (cols[None, :] < cs[:, None] + NA_KW)
    dc_idx = np.clip(cols[None, :] - cols[:, None] + NA_KW - 1, 0, 2 * NA_KW - 2)
    mask = np.broadcast_to(col_in[:, None, :], (GRID_W, NA_KH, GRID_W)).reshape(GRID_W, NA_WIN)
    onehot = (np.arange(NA_NDC)[:, None] == dc_idx.reshape(1, -1)).astype(np.float32)
    rp = jnp.pad(rpb, ((0, 0), (0, 0), (0, NA_NDC - rpb.shape[2])))
    rows = jnp.stack([rp[:, NA_KH - 1 - d:2 * NA_KH - 1 - d, :] for d in range(NA_KH)])
    rows = rows.reshape(NA_KH, HEADS * NA_KH, NA_NDC)
    nqk = GRID_W * GRID_W
    b = pl.pallas_call(
        _na_bias_kernel,
        grid=(NA_KH,),
        in_specs=[pl.BlockSpec((1, HEADS * NA_KH, NA_NDC), lambda d: (d, 0, 0)),
                  pl.BlockSpec((NA_NDC, nqk), lambda d: (0, 0))],
        out_specs=pl.BlockSpec((1, HEADS * NA_KH, nqk), lambda d: (d, 0, 0)),
        out_shape=jax.ShapeDtypeStruct((NA_KH, HEADS * NA_KH, nqk), F32),
        compiler_params=_params("parallel"),
        name="na_bias",
    )(rows, jnp.asarray(onehot))
    b = b.reshape(NA_KH, HEADS, NA_KH, GRID_W, GRID_W).transpose(0, 1, 3, 2, 4).reshape(NA_KH, HEADS, GRID_W, NA_WIN)
    return jnp.where(mask[None, None], b, NEG_INF)


def _na_attention(z, kc, vc, bias):
    w = ATT_W
    row0 = z.shape[0] - N_SMP

    def all_rows(n, col, base):
        return pl.BlockSpec((n, w), lambda i, r: (base // n + i, col))

    return pl.pallas_call(
        _na_attn_kernel,
        grid=(DEC_BATCH, GRID_ROWS),
        in_specs=[pl.BlockSpec((GRID_W, w), lambda i, r: (row0 // GRID_W + i * GRID_ROWS + r, 0)),
                  all_rows(DEC_SEQ, 1, row0), all_rows(DEC_SEQ, 2, row0),
                  all_rows(PAST_LEN, 0, 0), all_rows(PAST_LEN, 0, 0),
                  pl.BlockSpec((1, HEADS, GRID_W, NA_WIN), lambda i, r: (r - _na_window_start(r), 0, 0, 0))],
        out_specs=pl.BlockSpec((GRID_W, w), lambda i, r: (i * GRID_ROWS + r, 0)),
        out_shape=jax.ShapeDtypeStruct((N_SMP, w), BF16),
        compiler_params=_params("parallel", "parallel"),
        name="na_attention",
    )(z, z, z, kc, vc, bias)


CONV_CW = 384


CONV_ROWS = 1024


def _conv3_kernel(x_ref, w_ref, o_ref, *, seq):
    x = x_ref[...]
    t = x.shape[0]
    pos = lax.broadcasted_iota(jnp.int32, x.shape, 0) % seq
    prev = jnp.where(pos == 0, 0.0, pltpu.roll(x, 1, 0))
    nxt = jnp.where(pos == seq - 1, 0.0, pltpu.roll(x, t - 1, 0))
    w = w_ref[...]
    o_ref[...] = prev * w[0:1] + x * w[1:2] + nxt * w[2:3]


def _conv3(z, conv_w, seq, row0, n_seq):
    col0 = (3 * D_WIDTH) // CONV_CW
    blk0 = row0 // CONV_ROWS
    n_rows = n_seq * seq
    return pl.pallas_call(
        functools.partial(_conv3_kernel, seq=seq),
        grid=(n_rows // CONV_ROWS, D_IN // CONV_CW),
        in_specs=[pl.BlockSpec((CONV_ROWS, CONV_CW), lambda i, j: (blk0 + i, col0 + j)),
                  pl.BlockSpec((3, CONV_CW), lambda i, j: (0, j))],
        out_specs=pl.BlockSpec((CONV_ROWS, CONV_CW), lambda i, j: (i, j)),
        out_shape=jax.ShapeDtypeStruct((n_rows, D_IN), F32),
        compiler_params=_params("parallel", "parallel"),
        name="conv3",
    )(z, conv_w)


def _dot_f32(a, b):
    return jnp.dot(a, b, preferred_element_type=F32, precision=lax.Precision.HIGHEST)


def _d_prep_kernel(x_ref, w0_ref, w2_ref, a0_ref, a2_ref, g2_ref, dec_ref, a_ref, g_ref):
    x = x_ref[...]
    xw = x[:, 0:2 * D_LORA]
    xa = x[:, 2 * D_LORA:4 * D_LORA]
    xg = x[:, 4 * D_LORA:]
    for d in range(2):
        u = w0_ref[d] + _bdot(jnp.tanh(xw[:, d * D_LORA:(d + 1) * D_LORA]), w2_ref[d])
        nu = -u
        softplus = jnp.maximum(nu, 0.0) + jnp.log1p(jnp.exp(-jnp.abs(nu)))
        wlog = -softplus - 0.5
        dec_ref[d] = -jnp.exp(wlog)
        av = a0_ref[d] + _bdot(xa[:, d * D_LORA:(d + 1) * D_LORA], a2_ref[d])
        a_ref[d] = 1.0 / (1.0 + jnp.exp(-av))
    g_ref[...] = _bdot(1.0 / (1.0 + jnp.exp(-xg)), g2_ref[...])


def _d_prep(zc, w0, w2, a0, a2, g2, tm=512):
    n = zc.shape[0]
    cb = (3 * D_WIDTH) // CONV_CW
    return pl.pallas_call(
        _d_prep_kernel,
        grid=(n // tm,),
        in_specs=[pl.BlockSpec((tm, CONV_CW), lambda i: (i, cb)),
                  pl.BlockSpec((2, 1, D_WIDTH), lambda i: (0, 0, 0)),
                  pl.BlockSpec((2, D_LORA, D_WIDTH), lambda i: (0, 0, 0)),
                  pl.BlockSpec((2, 1, D_WIDTH), lambda i: (0, 0, 0)),
                  pl.BlockSpec((2, D_LORA, D_WIDTH), lambda i: (0, 0, 0)),
                  pl.BlockSpec((D_GATE_LORA, D_WIDTH), lambda i: (0, 0))],
        out_specs=[pl.BlockSpec((2, tm, D_WIDTH), lambda i: (0, i, 0)),
                   pl.BlockSpec((2, tm, D_WIDTH), lambda i: (0, i, 0)),
                   pl.BlockSpec((tm, D_WIDTH), lambda i: (i, 0))],
        out_shape=[jax.ShapeDtypeStruct((2, n, D_WIDTH), F32),
                   jax.ShapeDtypeStruct((2, n, D_WIDTH), F32),
                   jax.ShapeDtypeStruct((n, D_WIDTH), F32)],
        compiler_params=_params("parallel"),
        name="d_prep",
    )(zc, w0.reshape(2, 1, D_WIDTH), w2, a0.reshape(2, 1, D_WIDTH), a2, g2)


RW_C = 64
RW_PP = 8


def _bdot(a, b):
    ah, al = _split_hi_lo(a)
    bh, bl = _split_hi_lo(b)
    dot = functools.partial(jnp.dot, preferred_element_type=F32)
    return dot(ah, bh) + dot(ah, bl) + dot(al, bh)


def _bdot2(a, b):
    ah, al = _split_hi_lo(a)
    dot = functools.partial(jnp.dot, preferred_element_type=F32)
    return dot(ah, b) + dot(al, b)


def _bdot_nt(a, b):
    ah, al = _split_hi_lo(a)
    bh, bl = _split_hi_lo(b)
    return _dot_nt(ah, bh) + _dot_nt(ah, bl) + _dot_nt(al, bh)


def _split_hi_lo(x):
    hi = x.astype(BF16)
    return hi, (x - hi.astype(F32)).astype(BF16)


def _head_ones():
    r = lax.broadcasted_iota(jnp.int32, (LANES, LANES), 0) // D_HS
    c = lax.broadcasted_iota(jnp.int32, (LANES, LANES), 1) // D_HS
    return jnp.where(r == c, 1.0, 0.0).astype(BF16)


def _rwkv_chunk_kernel(r_ref, k_ref, v_ref, lw_ref, a_ref, kkp_ref, kap_ref, s0_ref, y_ref, sfin_ref, s_ref):
    c = pl.program_id(3)
    two_c = 2 * RW_C

    @pl.when(c == 0)
    def _():
        s_ref[...] = s0_ref[...]

    sgn = jnp.where(pl.program_id(1) == 1, -1, 1)
    row = lax.broadcasted_iota(jnp.int32, (two_c, two_c), 0)
    col = lax.broadcasted_iota(jnp.int32, (two_c, two_c), 1)
    same = (row // RW_C) == (col // RW_C)
    tt, ss = row % RW_C, col % RW_C
    before = (ss - tt) * sgn < 0
    strict = same & before
    incl = same & (before | (ss == tt))
    eye = jnp.where(row == col, 1.0, 0.0)
    r64 = lax.broadcasted_iota(jnp.int32, (RW_C, RW_C), 0)
    c64 = lax.broadcasted_iota(jnp.int32, (RW_C, RW_C), 1)
    ltri = jnp.where((c64 - r64) * sgn <= 0, 1.0, 0.0)
    head0 = lax.broadcasted_iota(jnp.int32, (RW_C, LANES), 1) < D_HS
    ones_blk = _head_ones().astype(F32)

    def stack(x):
        return jnp.concatenate([jnp.where(head0, x, 0.0), jnp.where(head0, 0.0, x)], axis=0)

    pairs = range(RW_PP)
    lss = [slice(p * LANES, (p + 1) * LANES) for p in pairs]
    r = [r_ref[:, ls] for ls in lss]
    k = [k_ref[:, ls] for ls in lss]
    lw = [lw_ref[:, ls] for ls in lss]
    a = [a_ref[:, ls] for ls in lss]
    kkf = [k[p] * kkp_ref[:, lss[p]] for p in pairs]
    n2 = [_dot_f32(kkf[p] * kkf[p], ones_blk) for p in pairs]
    kk = [kkf[p] / jnp.maximum(jnp.sqrt(n2[p]), 1e-12) for p in pairs]
    cum = [_dot_f32(ltri, lw[p]) for p in pairs]
    mid = [cum[p][RW_C // 2:RW_C // 2 + 1] for p in pairs]
    cumc = [cum[p] - mid[p] for p in pairs]
    pmid = [jnp.exp(mid[p]) for p in pairs]
    pend = [jnp.exp(jnp.sum(lw[p], axis=0, keepdims=True) - mid[p]) for p in pairs]
    pinv = [jnp.exp(-cumc[p]) for p in pairs]
    ks = [stack(kk[p] * jnp.exp(cumc[p] - lw[p])).astype(BF16) for p in pairs]
    rs = [stack(r[p] * jnp.exp(cumc[p])).astype(BF16) for p in pairs]
    khs = [stack(k[p] * (1.0 + (a[p] - 1.0) * kap_ref[:, lss[p]]) * pinv[p]).astype(BF16) for p in pairs]
    bhs = [stack(kk[p] * a[p] * pinv[p]).astype(BF16) for p in pairs]
    vs = [stack(v_ref[:, ls]).astype(BF16) for ls in lss]
    big = [_dot_nt(jnp.concatenate([ks[p], rs[p]], axis=0), jnp.concatenate([bhs[p], khs[p]], axis=0)) for p in pairs]
    a_b = [jnp.where(strict, big[p][:two_c, :two_c], 0.0) for p in pairs]
    a_k = [jnp.where(strict, big[p][:two_c, two_c:], 0.0) for p in pairs]
    l_b = [jnp.where(incl, big[p][two_c:, :two_c], 0.0) for p in pairs]
    l_k = [jnp.where(incl, big[p][two_c:, two_c:], 0.0) for p in pairs]
    npow = [-a_b[p] for p in pairs]
    tinv = [eye + npow[p] for p in pairs]
    for _ in range(5):
        npow = [_bdot(npow[p], npow[p]) for p in pairs]
        tinv = [tinv[p] + _bdot(tinv[p], npow[p]) for p in pairs]
    w1 = [_bdot2(a_k[p], vs[p]) for p in pairs]
    ktil = [_bdot2(tinv[p], ks[p]) for p in pairs]
    uv = [_bdot(tinv[p], w1[p]) for p in pairs]
    lku = [_bdot(l_b[p], jnp.concatenate([ktil[p], uv[p]], axis=1)) for p in pairs]
    rtil = [rs[p].astype(F32) - lku[p][:, :LANES] for p in pairs]
    yv = [_bdot2(l_k[p], vs[p]) - lku[p][:, LANES:] for p in pairs]
    gt = [_bdot2(ktil[p].T, bhs[p]) for p in pairs]
    ht = [jnp.dot(vs[p].astype(F32).T.astype(BF16), khs[p], preferred_element_type=F32) - _bdot2(uv[p].T, bhs[p])
          for p in pairs]
    s = [s_ref[p] * pmid[p] for p in pairs]
    y = [_bdot_nt(rtil[p], s[p]) + yv[p] for p in pairs]
    sg = [_bdot(s[p], gt[p]) for p in pairs]
    for p in pairs:
        s_ref[p] = (s[p] - sg[p] + ht[p]) * pend[p]
        y_ref[:, lss[p]] = y[p][:RW_C] + y[p][RW_C:]

    @pl.when(c == pl.num_programs(3) - 1)
    def _():
        sfin_ref[...] = s_ref[...]


def _pair_states(s):
    zero = jnp.zeros_like(s[:, :, 0::2])
    top = jnp.concatenate([s[:, :, 0::2], zero], axis=-1)
    bot = jnp.concatenate([zero, s[:, :, 1::2]], axis=-1)
    return jnp.concatenate([top, bot], axis=-2)


def _head_states(sp):
    b = sp.shape[0]
    both = jnp.stack([sp[:, :, :, :D_HS, :D_HS], sp[:, :, :, D_HS:, D_HS:]], axis=3)
    return both.reshape(b, 2, D_HEADS, D_HS, D_HS)


def _rwkv_chunked(zc, lw, a, k_k, k_a, s0, b, t):
    nc = t // RW_C
    gw = RW_PP * LANES
    npg = D_WIDTH // gw

    def cmap(d, c):
        return jnp.where(d == 0, c, nc - 1 - c)

    def z_spec(colblk):
        return pl.BlockSpec((RW_C, gw), lambda i, d, g, c: (i * nc + cmap(d, c), colblk * npg + g))

    dir_spec = pl.BlockSpec((None, RW_C, gw), lambda i, d, g, c: (d, i * nc + cmap(d, c), g))
    par_spec = pl.BlockSpec((1, gw), lambda i, d, g, c: (0, g))
    st_spec = pl.BlockSpec((None, None, RW_PP, LANES, LANES), lambda i, d, g, c: (i, d, g, 0, 0))
    return pl.pallas_call(
        _rwkv_chunk_kernel,
        grid=(b, 2, npg, nc),
        in_specs=[z_spec(0), z_spec(1), z_spec(2), dir_spec, dir_spec, par_spec, par_spec, st_spec],
        out_specs=[dir_spec, st_spec],
        out_shape=[jax.ShapeDtypeStruct((2, b * t, D_WIDTH), F32),
                   jax.ShapeDtypeStruct((b, 2, D_WIDTH // LANES, LANES, LANES), F32)],
        scratch_shapes=[pltpu.VMEM((RW_PP, LANES, LANES), F32)],
        compiler_params=_params("parallel", "parallel", "parallel", "arbitrary"),
        name="rwkv_chunk",
    )(zc, zc, zc, lw, a, k_k.reshape(1, D_WIDTH), k_a.reshape(1, D_WIDTH), s0)


def _d_out_kernel(y_ref, r_ref, k_ref, v_ref, a_ref, g_ref, kap_ref, rkp_ref, lng_ref, lnb_ref, o_ref):
    ones_blk = _head_ones()

    def seg_sum(x):
        hi, lo = _split_hi_lo(x)
        return jnp.dot(hi, ones_blk, preferred_element_type=F32) + jnp.dot(lo, ones_blk, preferred_element_type=F32)

    for cb in range(D_WIDTH // LANES):
        ls = slice(cb * LANES, (cb + 1) * LANES)
        y = y_ref[0, :, ls] + y_ref[1, :, ls]
        mu = seg_sum(y) * (1.0 / D_HS)
        yc = y - mu
        var = seg_sum(yc * yc) * (1.0 / D_HS)
        yn = yc * lax.rsqrt(var + D_LN_EPS) * lng_ref[:, ls] + lnb_ref[:, ls]
        asum = a_ref[0, :, ls] + a_ref[1, :, ls]
        bonus = seg_sum(r_ref[:, ls] * k_ref[:, ls] * rkp_ref[:, ls] * (2.0 + (asum - 2.0) * kap_ref[:, ls]))
        o_ref[:, ls] = (yn + bonus * v_ref[:, ls]) * g_ref[:, ls]


def _d_out(y, zc, a, g, k_a, r_k, ln_g, ln_b, tm=512):
    n = g.shape[0]

    def z_spec(colblk):
        return pl.BlockSpec((tm, D_WIDTH), lambda i: (i, colblk))

    dspec = pl.BlockSpec((2, tm, D_WIDTH), lambda i: (0, i, 0))
    pspec = pl.BlockSpec((1, D_WIDTH), lambda i: (0, 0))
    return pl.pallas_call(
        _d_out_kernel,
        grid=(n // tm,),
        in_specs=[dspec, z_spec(0), z_spec(1), z_spec(2), dspec, pl.BlockSpec((tm, D_WIDTH), lambda i: (i, 0)),
                  pspec, pspec, pspec, pspec],
        out_specs=pl.BlockSpec((tm, D_WIDTH), lambda i: (i, 0)),
        out_shape=jax.ShapeDtypeStruct((n, D_WIDTH), F32),
        compiler_params=_params("parallel"),
        name="d_out",
    )(y, zc, zc, zc, a, g, k_a.reshape(1, D_WIDTH), r_k.reshape(1, D_WIDTH), ln_g.reshape(1, D_WIDTH),
      ln_b.reshape(1, D_WIDTH))


TOPK_TM = 256
BIG = float(2 ** 30)


def _extract_topk(s, labels):
    iota16 = lax.broadcasted_iota(jnp.int32, (PEER_TOPK, s.shape[1]), 0)
    rank = jnp.full(s.shape, BIG, F32)
    vals = jnp.zeros((PEER_TOPK, s.shape[1]), F32)
    for r in range(PEER_TOPK):
        m = jnp.max(s, axis=0, keepdims=True)
        idx = jnp.min(jnp.where(s == m, labels, BIG), axis=0, keepdims=True)
        hit = labels == idx
        rank = jnp.where(hit, float(r), rank)
        vals = jnp.where(iota16 == r, m, vals)
        s = jnp.where(hit, -jnp.inf, s)
    return rank, vals


_CAND_PAIRS = ([(0, b) for b in range(16)] + [(1, b) for b in range(8)] + [(2, b) for b in range(8)]
               + [(3, b) for b in range(8)] + [(4, b) for b in range(4)] + [(5, b) for b in range(4)]
               + [(6, b) for b in range(4)] + [(7, b) for b in range(4)] + [(a, 0) for a in range(8, 16)])
N_CAND = len(_CAND_PAIRS)


def _cand_labels():
    lab = np.array([a * PEER_TOPK + b for a, b in _CAND_PAIRS], np.float32)
    return jnp.asarray(np.broadcast_to(lab[:, None], (N_CAND, LANES)).copy())


def _peer_topk_kernel(q_ref, keys_ref, lab_ref, lim_ref, e1_ref, rb_ref, e2_ref):
    iota_k = lax.broadcasted_iota(jnp.int32, (PEER_NKEYS, LANES), 0).astype(F32)
    row8 = lax.broadcasted_iota(jnp.int32, (8, LANES), 0)
    labels = lab_ref[...]
    for c in range(TOPK_TM // LANES):
        cs = slice(c * LANES, (c + 1) * LANES)
        q1 = q_ref[cs, 0:PEER_HALF].astype(BF16)
        q2 = q_ref[cs, PEER_HALF:2 * PEER_HALF].astype(BF16)
        s1 = _dot_nt(keys_ref[0, 0], q1)
        s2 = _dot_nt(keys_ref[0, 1], q2)
        rank1, sv1 = _extract_topk(s1, iota_k)
        rank2, sv2 = _extract_topk(s2, iota_k)
        lo8 = sv2[0:8]
        lo4 = jnp.where(row8 < 4, lo8, pltpu.roll(lo8, 4, 0))
        cand = jnp.concatenate([
            sv1[0:1] + lo8, sv1[0:1] + sv2[8:16], sv1[1:2] + lo8, sv1[2:3] + lo8, sv1[3:4] + lo8,
            jnp.where(row8 < 4, sv1[4:5], sv1[5:6]) + lo4, jnp.where(row8 < 4, sv1[6:7], sv1[7:8]) + lo4,
            sv1[8:16] + sv2[0:1]], axis=0)
        crank, cvals = _extract_topk(cand, labels)
        z = jnp.sum(jnp.exp(cvals - cvals[0:1]), axis=0, keepdims=True)
        sel = jnp.where(crank < BIG, 1.0, 0.0)

        def count(lo, hi):
            return jnp.sum(sel[lo:hi], axis=0, keepdims=True)

        def count_half(lo, first):
            part = jnp.where((row8 < 4) if first else (row8 >= 4), sel[lo:lo + 8], 0.0)
            return jnp.sum(part, axis=0, keepdims=True)

        n_sel = [count(0, 16), count(16, 24), count(24, 32), count(32, 40),
                 count_half(40, True), count_half(40, False), count_half(48, True), count_half(48, False)]
        n_sel += [sel[56 + a:57 + a] for a in range(8)]
        lim = jnp.zeros_like(s1)
        for a in range(PEER_TOPK):
            lim = jnp.where(rank1 == float(a), n_sel[a], lim)
        lim_ref[0, :, cs] = lim
        e1_ref[0, :, cs] = jnp.exp(s1 - sv1[0:1])
        rb_ref[0, :, cs] = rank2
        e2_ref[0, :, cs] = jnp.exp(s2 - sv2[0:1]) / z


def _peer_topk(q, keys):
    n = q.shape[0]
    ospec = pl.BlockSpec((1, PEER_NKEYS, TOPK_TM), lambda i, h: (h, 0, i))
    return pl.pallas_call(
        _peer_topk_kernel,
        grid=(n // TOPK_TM, PEER_HEADS),
        in_specs=[pl.BlockSpec((TOPK_TM, 2 * PEER_HALF), lambda i, h: (i, h)),
                  pl.BlockSpec((1, 2, PEER_NKEYS, PEER_HALF), lambda i, h: (h, 0, 0, 0)),
                  pl.BlockSpec((N_CAND, LANES), lambda i, h: (0, 0))],
        out_specs=[ospec] * 4,
        out_shape=[jax.ShapeDtypeStruct((PEER_HEADS, PEER_NKEYS, n), F32)] * 4,
        compiler_params=_params("parallel", "parallel"),
        name="peer_topk",
    )(q, keys, _cand_labels())


PEER_TM = 512
PEER_TI = 8
PEER_RG = 4
PEER_JT = 32


def _gelu(x):
    return 0.5 * x * (1.0 + lax.erf(x * (2.0 ** -0.5)))


def _peer_dense_kernel(x_ref, u_ref, v_ref, lim_ref, e1_ref, rb_ref, e2_ref, res_ref, gate_ref, o_ref, w_ref):
    e = pl.program_id(1)

    @pl.when(e == 0)
    def _():
        o_ref[...] = jnp.zeros_like(o_ref)

    hid = _dot_nt(u_ref[...], x_ref[...])
    for c in range(PEER_TM // LANES):
        cs = slice(c * LANES, (c + 1) * LANES)
        for jt in range(PEER_NKEYS // PEER_JT):
            js = slice(jt * PEER_JT, (jt + 1) * PEER_JT)
            for rg in range(PEER_TI // PEER_RG):
                rows = range(rg * PEER_RG, (rg + 1) * PEER_RG)
                g = [jnp.zeros((PEER_JT, LANES), F32) for _ in rows]
                for h in range(PEER_HEADS):
                    rb = rb_ref[h, js, cs]
                    e2 = e2_ref[h, js, cs]
                    for k, ii in enumerate(rows):
                        g[k] = g[k] + jnp.where(rb < lim_ref[h, ii:ii + 1, cs], e2, 0.0) * e1_ref[h, ii:ii + 1, cs]
                for k, ii in enumerate(rows):
                    es = slice(ii * PEER_NKEYS + jt * PEER_JT, ii * PEER_NKEYS + (jt + 1) * PEER_JT)
                    w_ref[es, cs] = (g[k] * _gelu(hid[es, cs])).astype(BF16)
    o_ref[...] += _dot_tn(w_ref[...], v_ref[...])

    @pl.when(e == pl.num_programs(1) - 1)
    def _():
        o_ref[...] = res_ref[...] + gate_ref[0] * o_ref[...]


def _peer_dense(xm, u, v, layer, lim, e1, rb, e2, res, gate):
    n, d = xm.shape
    n_exp = u.shape[1]
    te = PEER_TI * PEER_NKEYS
    sel_spec = pl.BlockSpec((PEER_HEADS, PEER_NKEYS, PEER_TM), lambda i, e: (0, 0, i))
    row_spec = pl.BlockSpec((PEER_HEADS, PEER_TI, PEER_TM), lambda i, e: (0, e, i))
    return pl.pallas_call(
        _peer_dense_kernel,
        grid=(n // PEER_TM, n_exp // te),
        in_specs=[pl.BlockSpec((PEER_TM, d), lambda i, e: (i, 0)),
                  pl.BlockSpec((None, te, d), lambda i, e: (layer, e, 0)),
                  pl.BlockSpec((None, te, d), lambda i, e: (layer, e, 0)),
                  row_spec, row_spec, sel_spec, sel_spec,
                  pl.BlockSpec((PEER_TM, d), lambda i, e: (i, 0)),
                  pl.BlockSpec((1, 1, d), lambda i, e: (_cond_of_tile(i, PEER_TM), 0, 0))],
        out_specs=pl.BlockSpec((PEER_TM, d), lambda i, e: (i, 0)),
        out_shape=jax.ShapeDtypeStruct((n, d), F32),
        scratch_shapes=[pltpu.VMEM((te, PEER_TM), BF16)],
        compiler_params=_params("parallel", "arbitrary"),
        name="peer_dense",
    )(xm, u, v, lim, e1, rb, e2, res, gate)


def _peer_layer(x, layer, norm_g, shift, scale, gate, wq, keys, u, v):
    xm = _modulate(x, norm_g, shift, scale)
    q = _matmul(xm, wq, layer, out_dtype=BF16)
    lim, e1, rb, e2 = _peer_topk(q, keys)
    return _peer_dense(xm, u, v, layer, lim, e1, rb, e2, x, gate)


EVEN_PAD = 4096
ODD_PAD = 6656
S_ALL = PAST_LEN + DEC_SEQ


def _even_mixer(h, w_in, layer, lam, subln_g, q_norm_g, w_uq, kv_norm_g, w_ukv, lam_init,
                cache_k, cache_v, cache_ckv, cache_kpe, rope_cos, rope_sin):
    z = _matmul(h, w_in, layer)
    ckv_n = _rmsnorm(z, kv_norm_g, col=3584 // B_KV_RANK)
    cq_n = _rmsnorm(z, q_norm_g, out_dtype=BF16, col=3072 // B_Q_RANK)
    wq3 = w_uq.reshape(B_Q_RANK, HEADS, B_NOPE + B_ROPE)
    w_uq_r = jnp.concatenate([wq3[:, :, :B_NOPE].reshape(B_Q_RANK, -1), wq3[:, :, B_NOPE:].reshape(B_Q_RANK, -1)], axis=1)
    qb = _matmul(cq_n, w_uq_r.astype(BF16), tn=512)
    new = (z[:N_CTX, 1024:2048].reshape(BATCH, SEQ, HEADS, 2 * A_QK),
           z[:N_CTX, 2048:3072].reshape(BATCH, SEQ, HEADS, HEAD_DIM),
           ckv_n[:N_CTX].reshape(BATCH, SEQ, B_KV_RANK), z[:N_CTX, 3840:3904].reshape(BATCH, SEQ, B_ROPE))

    qa_s = _rope(z, rope_cos, rope_sin, ATT_W, col=0)
    ka_s = _rope(z, rope_cos, rope_sin, ATT_W, col=1)
    qr_s = _rope(qb, rope_cos, rope_sin, HEADS * B_ROPE, col=2)
    kpe_s = _rope(z, rope_cos, rope_sin, LANES, col=3840 // LANES)[:, :B_ROPE]

    def with_ctx(cache, own, width):
        both = jnp.concatenate([cache.reshape(DEC_BATCH, PAST_LEN, width).astype(own.dtype),
                                own.reshape(DEC_BATCH, DEC_SEQ, width)], axis=1)
        return both.reshape(DEC_BATCH * S_ALL, width)

    ka_all = with_ctx(cache_k, ka_s, ATT_W)
    va_all = with_ctx(cache_v, z[N_CTX:, 2048:3072].astype(BF16), ATT_W)
    ckv_all = with_ctx(cache_ckv, ckv_n[N_CTX:], B_KV_RANK)
    kpe_all = with_ctx(cache_kpe, kpe_s, B_ROPE)

    wkv3 = w_ukv.reshape(B_KV_RANK, HEADS, B_NOPE + HEAD_DIM)
    w_ukv_r = jnp.concatenate([wkv3[:, :, :B_NOPE].reshape(B_KV_RANK, -1), wkv3[:, :, B_NOPE:].reshape(B_KV_RANK, -1)], axis=1)
    n_skv = DEC_BATCH * S_ALL
    kv = _matmul(jnp.concatenate([ckv_all, ckv_n[:N_CTX]], axis=0), w_ukv_r.astype(BF16), out_dtype=BF16)

    oa_c = _diff_attention(BATCH, SEQ, (z, _q_rows(SEQ, ATT_W, 0, 0)), (z, _kv_rows(SEQ, ATT_W, 0, 1)),
                           (z, _kv_rows(SEQ, ATT_W, 0, 2)), lam, subln_g, lam_init)
    oa_s = _diff_attention(DEC_BATCH, DEC_SEQ, (qa_s, _q_rows(DEC_SEQ, ATT_W)), (ka_all, _kv_rows(S_ALL, ATT_W)),
                           (va_all, _kv_rows(S_ALL, ATT_W)), lam, subln_g, lam_init)
    ob_c = _mla_attention(BATCH, SEQ, (qb, _q_rows(SEQ, ATT_W, 0, 0)), (qb, _q_rows(SEQ, HEADS * B_ROPE, 0, 2)),
                          (kv, _kv_rows(SEQ, ATT_W, n_skv, 0)), (z, _kv_rows(SEQ, LANES, 0, 3840 // LANES)),
                          (kv, _kv_rows(SEQ, ATT_W, n_skv, 1)))
    ob_s = _mla_attention(DEC_BATCH, DEC_SEQ, (qb, _q_rows(DEC_SEQ, ATT_W, N_CTX, 0)),
                          (qr_s, _q_rows(DEC_SEQ, HEADS * B_ROPE)), (kv, _kv_rows(S_ALL, ATT_W, 0, 0)),
                          (kpe_all, _kv_rows(S_ALL, B_ROPE)), (kv, _kv_rows(S_ALL, ATT_W, 0, 1)))
    return (oa_c, oa_s, ob_c, ob_s), new


def _odd_mixer(h, w_in, layer, rpb, conv_w, w0, w2, a0, a2, g2, k_k, k_a, r_k, ln_g, ln_b,
               cache_k, cache_v, state):
    z = _matmul(h, w_in, layer)
    new_k = z[:N_CTX, 1024:2048].reshape(BATCH, SEQ, HEADS, HEAD_DIM)
    new_v = z[:N_CTX, 2048:3072].reshape(BATCH, SEQ, HEADS, HEAD_DIM)
    oc_c = _plain_attention(BATCH, SEQ, (z, _q_rows(SEQ, ATT_W, 0, 0)), (z, _kv_rows(SEQ, ATT_W, 0, 1)),
                            (z, _kv_rows(SEQ, ATT_W, 0, 2)))
    oc_s = _na_attention(z, cache_k.reshape(DEC_BATCH * PAST_LEN, ATT_W), cache_v.reshape(DEC_BATCH * PAST_LEN, ATT_W),
                         _na_bias_tables(rpb))

    zc_c = _conv3(z, conv_w, SEQ, 0, BATCH)
    zc_s = _conv3(z, conv_w, DEC_SEQ, N_CTX, DEC_BATCH)
    lw_c, a_c, g_c = _d_prep(zc_c, w0, w2, a0, a2, g2)
    lw_s, a_s, g_s = _d_prep(zc_s, w0, w2, a0, a2, g2)
    rk = r_k.reshape(-1)
    y_c, sfin = _rwkv_chunked(zc_c, lw_c, a_c, k_k, k_a,
                              jnp.zeros((BATCH, 2, D_WIDTH // LANES, LANES, LANES), F32), BATCH, SEQ)
    sfin = _head_states(sfin)
    y_s, _ = _rwkv_chunked(zc_s, lw_s, a_s, k_k, k_a, _pair_states(state), DEC_BATCH, DEC_SEQ)
    od_c = _d_out(y_c, zc_c, a_c, g_c, k_a, rk, ln_g, ln_b)
    od_s = _d_out(y_s, zc_s, a_s, g_s, k_a, rk, ln_g, ln_b)
    return (oc_c, oc_s, od_c, od_s), (new_k, new_v, sfin)


def kernel(x_prompt, x_sample, cache_a_k, cache_a_v, cache_b_ckv, cache_b_kpe, cache_c_k, cache_c_v, state_d, c, c_ctx, ada_w, ada_b, norm1_g, norm2_g, w_out, peer_wq, peer_keys, peer_u, peer_v, final_g, ab_w_in, a_lam, a_subln_g, b_q_norm_g, b_w_uq, b_kv_norm_g, b_w_ukv, cd_w_in, c_rpb, d_conv, d_w0, d_w2, d_a0, d_a2, d_g2, d_k_k, d_k_a, d_r_k, d_ln_g, d_ln_b):
    x = jnp.concatenate([x_prompt.reshape(N_CTX, D_MODEL), x_sample.reshape(N_SMP, D_MODEL)], axis=0)
    cond8 = jnp.pad(jnp.concatenate([c_ctx[None, :], c], axis=0), ((0, 8 - N_COND), (0, 0)))
    ada = _ada_all(cond8, ada_w, ada_b)
    rope_cos, rope_sin = _rope_tables()
    w_even = jnp.pad(ab_w_in, ((0, 0), (0, 0), (0, EVEN_PAD - EVEN_IN))).astype(BF16)
    w_odd = jnp.pad(cd_w_in, ((0, 0), (0, 0), (0, ODD_PAD - ODD_IN))).astype(BF16)
    w_out_b, wq_b, keys_b = w_out.astype(BF16), peer_wq.astype(BF16), peer_keys.astype(BF16)
    u_b, v_b = peer_u.astype(BF16), peer_v.astype(BF16)
    new_ak, new_av, new_bc, new_bp, new_ck, new_cv, new_sd = [], [], [], [], [], [], []
    for l in range(DEPTH):
        i = l // 2
        mods = ada[l, :N_COND].reshape(N_COND, 6, 1, D_MODEL)
        sh1, sc1, g1, sh2, sc2, g2 = (mods[:, m] for m in range(6))
        h = _modulate(x, norm1_g[l], sh1, sc1)
        if l % 2 == 0:
            lam_init = 0.8 - 0.6 * math.exp(-0.3 * l)
            o, (ak, av, bc, bp) = _even_mixer(
                h, w_even, i, a_lam[i], a_subln_g[i], b_q_norm_g[i], b_w_uq[i], b_kv_norm_g[i], b_w_ukv[i],
                lam_init, cache_a_k[:, i], cache_a_v[:, i], cache_b_ckv[:, i], cache_b_kpe[:, i], rope_cos, rope_sin)
            new_ak.append(ak)
            new_av.append(av)
            new_bc.append(bc)
            new_bp.append(bp)
        else:
            o, (ck, cv, sd) = _odd_mixer(
                h, w_odd, i, c_rpb[i], d_conv[i], d_w0[i], d_w2[i], d_a0[i], d_a2[i], d_g2[i],
                d_k_k[i], d_k_a[i], d_r_k[i], d_ln_g[i], d_ln_b[i], cache_c_k[:, i], cache_c_v[:, i], state_d[:, i])
            new_ck.append(ck)
            new_cv.append(cv)
            new_sd.append(sd)
        x = _matmul_residual(*o, w_out_b, l, x, g1)
        x = _peer_layer(x, l, norm2_g[l], sh2, sc2, g2, wq_b, keys_b[l], u_b, v_b)
    y = _rmsnorm(x, final_g)
    y_prompt = y[:N_CTX].reshape(BATCH, SEQ, D_MODEL)
    y_sample = y[N_CTX:].reshape(DEC_BATCH, DEC_SEQ, D_MODEL)
    return (y_prompt, y_sample, jnp.stack(new_ak, axis=1), jnp.stack(new_av, axis=1), jnp.stack(new_bc, axis=1),
            jnp.stack(new_bp, axis=1), jnp.stack(new_ck, axis=1), jnp.stack(new_cv, axis=1), jnp.stack(new_sd, axis=1))
```

```python
import functools
import math

import numpy as np
import jax
import jax.numpy as jnp
from jax import lax
from jax.experimental import pallas as pl
from jax.experimental.pallas import tpu as pltpu

F32 = jnp.float32
BF16 = jnp.bfloat16

D_MODEL = 2048
BATCH = 16
SEQ = 256
DEPTH = 4
DEC_BATCH = 2
DEC_SEQ = 1024
PAST_LEN = 512
GRID_W = 64
GRID_ROWS = DEC_SEQ // GRID_W
ROPE_BASE = 10000.0
RMS_EPS = 1e-6
NEG_INF = -1e30
N_CTX = BATCH * SEQ
N_SMP = DEC_BATCH * DEC_SEQ
N_TOK = N_CTX + N_SMP
N_COND = 1 + DEC_BATCH

HEADS = 8
HEAD_DIM = 128
A_QK = 64
B_Q_RANK = 512
B_KV_RANK = 256
B_NOPE = 128
B_ROPE = 64
NA_KH = 8
NA_KW = 16
NA_WIN = NA_KH * GRID_W
D_HS = 64
D_WIDTH = 1024
D_HEADS = 16
D_LORA = 64
D_GATE_LORA = 128
D_IN = 3 * D_WIDTH + 2 * D_LORA + 2 * D_LORA + D_GATE_LORA
D_LN_EPS = 64e-5
EVEN_IN = 3904
ODD_IN = 6528
PEER_HEADS = 8
PEER_NKEYS = 128
PEER_HALF = 128
PEER_TOPK = 16

LANES = 128
VMEM_LIMIT = 56 * 1024 * 1024


def _params(*sem):
    return pltpu.CompilerParams(dimension_semantics=sem, vmem_limit_bytes=VMEM_LIMIT)


def _dot_nt(a, b):
    return lax.dot_general(a, b, (((1,), (1,)), ((), ())), preferred_element_type=F32)


def _dot_tn(a, b):
    return lax.dot_general(a, b, (((0,), (0,)), ((), ())), preferred_element_type=F32)


def _cond_of_tile(i, tm):
    n_ctx_tiles = N_CTX // tm
    per_batch = DEC_SEQ // tm
    return jnp.where(i < n_ctx_tiles, 0, 1 + (i - n_ctx_tiles) // per_batch)


def _ada_kernel(c_ref, w_ref, b_ref, o_ref):
    c = c_ref[...]
    sc = c * (1.0 / (1.0 + jnp.exp(-c)))
    o_ref[0] = jnp.dot(sc.astype(BF16), w_ref[0].astype(BF16), preferred_element_type=F32) + b_ref[0]


def _ada_all(cond8, ada_w, ada_b):
    tn = 1024
    n_out = ada_w.shape[-1]
    return pl.pallas_call(
        _ada_kernel,
        grid=(DEPTH, n_out // tn),
        in_specs=[pl.BlockSpec((8, D_MODEL), lambda l, j: (0, 0)),
                  pl.BlockSpec((1, D_MODEL, tn), lambda l, j: (l, 0, j)),
                  pl.BlockSpec((1, 1, tn), lambda l, j: (l, 0, j))],
        out_specs=pl.BlockSpec((1, 8, tn), lambda l, j: (l, 0, j)),
        out_shape=jax.ShapeDtypeStruct((DEPTH, 8, n_out), F32),
        compiler_params=_params("parallel", "parallel"),
        name="ada",
    )(cond8, ada_w, ada_b.reshape(DEPTH, 1, n_out))


def _modulate_kernel(x_ref, g_ref, sh_ref, sc_ref, o_ref):
    x = x_ref[...]
    y = x * lax.rsqrt(jnp.mean(x * x, axis=-1, keepdims=True) + RMS_EPS) * g_ref[...]
    o_ref[...] = (y * (1.0 + sc_ref[0]) + sh_ref[0]).astype(o_ref.dtype)


def _modulate(x, g, shift, scale, out_dtype=BF16, tm=512):
    n, d = x.shape
    return pl.pallas_call(
        _modulate_kernel,
        grid=(n // tm,),
        in_specs=[pl.BlockSpec((tm, d), lambda i: (i, 0)),
                  pl.BlockSpec((1, d), lambda i: (0, 0)),
                  pl.BlockSpec((1, 1, d), lambda i: (_cond_of_tile(i, tm), 0, 0)),
                  pl.BlockSpec((1, 1, d), lambda i: (_cond_of_tile(i, tm), 0, 0))],
        out_specs=pl.BlockSpec((tm, d), lambda i: (i, 0)),
        out_shape=jax.ShapeDtypeStruct((n, d), out_dtype),
        compiler_params=_params("parallel"),
        name="modulate",
    )(x, g.reshape(1, d), shift, scale)


def _rmsnorm_kernel(x_ref, g_ref, o_ref):
    x = x_ref[...]
    y = x * lax.rsqrt(jnp.mean(x * x, axis=-1, keepdims=True) + RMS_EPS) * g_ref[...]
    o_ref[...] = y.astype(o_ref.dtype)


def _rmsnorm(x, g, out_dtype=F32, tm=512, col=0):
    n = x.shape[0]
    d = g.shape[-1]
    return pl.pallas_call(
        _rmsnorm_kernel,
        grid=(n // tm,),
        in_specs=[pl.BlockSpec((tm, d), lambda i: (i, col)),
                  pl.BlockSpec((1, d), lambda i: (0, 0))],
        out_specs=pl.BlockSpec((tm, d), lambda i: (i, 0)),
        out_shape=jax.ShapeDtypeStruct((n, d), out_dtype),
        compiler_params=_params("parallel"),
        name="rmsnorm",
    )(x, g.reshape(1, d))


def _mm_kernel(x_ref, w_ref, o_ref):
    o_ref[...] = jnp.dot(x_ref[...].astype(BF16), w_ref[...], preferred_element_type=F32).astype(o_ref.dtype)


def _weight_spec(w, layer, tn):
    if w.ndim == 2:
        return pl.BlockSpec((w.shape[0], tn), lambda i, j: (0, j))
    return pl.BlockSpec((None, w.shape[1], tn), lambda i, j: (layer, 0, j))


MM_TM = 1024
MM_TN = 1024


def _matmul(x, w, layer=None, out_dtype=F32, tm=MM_TM, tn=None, x_col=0, x_width=None):
    m = x.shape[0]
    k = x_width or x.shape[1]
    n = w.shape[-1]
    if tn is None:
        tn = MM_TN if n % MM_TN == 0 else MM_TN // 2
    tn = min(tn, n)
    assert m % tm == 0 and n % tn == 0, (m, n, tm, tn)
    return pl.pallas_call(
        _mm_kernel,
        grid=(m // tm, n // tn),
        in_specs=[pl.BlockSpec((tm, k), lambda i, j: (i, x_col)),
                  _weight_spec(w, layer, tn)],
        out_specs=pl.BlockSpec((tm, tn), lambda i, j: (i, j)),
        out_shape=jax.ShapeDtypeStruct((m, n), out_dtype),
        compiler_params=_params("parallel", "parallel"),
        name="matmul",
    )(x, w)


def _mm_res_kernel(lc_ref, ls_ref, rc_ref, rs_ref, w_ref, res_ref, gate_ref, o_ref, *, n_ctx_tiles):
    half = w_ref.shape[0] // 2

    def emit(l_ref, r_ref):
        acc = jnp.dot(l_ref[...].astype(BF16), w_ref[0:half, :], preferred_element_type=F32)
        acc = acc + jnp.dot(r_ref[...].astype(BF16), w_ref[half:, :], preferred_element_type=F32)
        o_ref[...] = res_ref[...] + gate_ref[0] * acc

    @pl.when(pl.program_id(0) < n_ctx_tiles)
    def _():
        emit(lc_ref, rc_ref)

    @pl.when(pl.program_id(0) >= n_ctx_tiles)
    def _():
        emit(ls_ref, rs_ref)


def _matmul_residual(left_c, left_s, right_c, right_s, w, layer, res, gate, tm=MM_TM, tn=MM_TN // 2):
    m = res.shape[0]
    n = w.shape[-1]
    kh = left_c.shape[1]
    nct = left_c.shape[0] // tm
    ctx_spec = pl.BlockSpec((tm, kh), lambda i, j: (jnp.minimum(i, nct - 1), 0))
    smp_spec = pl.BlockSpec((tm, kh), lambda i, j: (jnp.maximum(i - nct, 0), 0))
    return pl.pallas_call(
        functools.partial(_mm_res_kernel, n_ctx_tiles=nct),
        grid=(m // tm, n // tn),
        in_specs=[ctx_spec, smp_spec, ctx_spec, smp_spec,
                  _weight_spec(w, layer, tn),
                  pl.BlockSpec((tm, tn), lambda i, j: (i, j)),
                  pl.BlockSpec((1, 1, tn), lambda i, j: (_cond_of_tile(i, tm), 0, j))],
        out_specs=pl.BlockSpec((tm, tn), lambda i, j: (i, j)),
        out_shape=jax.ShapeDtypeStruct((m, n), F32),
        compiler_params=_params("parallel", "parallel"),
        name="matmul_residual",
    )(left_c, left_s, right_c, right_s, w, res, gate)


def _rope_tables():
    nf = 16
    inv = ROPE_BASE ** (-np.arange(nf, dtype=np.float64) / nf)
    t = np.arange(DEC_SEQ)
    rows, cols = t // GRID_W, t % GRID_W
    lane = np.arange(64)
    pos = np.where(lane[None, :] < 32, rows[:, None], cols[:, None]).astype(np.float32)
    ang = (pos * inv[lane % nf][None, :].astype(np.float32)).astype(np.float32)
    first = (lane % 32) < nf
    cos = np.cos(ang.astype(np.float64))
    sin = np.sin(ang.astype(np.float64)) * np.where(first, -1.0, 1.0)[None, :]
    cos = np.tile(cos, (1, 2)).astype(np.float32)
    sin = np.tile(sin, (1, 2)).astype(np.float32)
    return jnp.asarray(cos), jnp.asarray(sin)


def _rope_kernel(x_ref, cos_ref, sin_ref, o_ref):
    cos = cos_ref[...]
    sin = sin_ref[...]
    lane = lax.broadcasted_iota(jnp.int32, cos.shape, 1)
    first = (lane % 32) < 16
    for c in range(x_ref.shape[1] // LANES):
        x = x_ref[:, c * LANES:(c + 1) * LANES].astype(F32)
        partner = jnp.where(first, pltpu.roll(x, LANES - 16, 1), pltpu.roll(x, 16, 1))
        o_ref[:, c * LANES:(c + 1) * LANES] = (x * cos + partner * sin).astype(o_ref.dtype)


def _rope(x, cos, sin, w, col=0, out_dtype=BF16, tm=256):
    n = N_SMP
    row0 = (x.shape[0] - N_SMP) // tm
    per = DEC_SEQ // tm
    return pl.pallas_call(
        _rope_kernel,
        grid=(n // tm,),
        in_specs=[pl.BlockSpec((tm, w), lambda i: (row0 + i, col)),
                  pl.BlockSpec((tm, LANES), lambda i: (i % per, 0)),
                  pl.BlockSpec((tm, LANES), lambda i: (i % per, 0))],
        out_specs=pl.BlockSpec((tm, w), lambda i: (i, 0)),
        out_shape=jax.ShapeDtypeStruct((n, w), out_dtype),
        compiler_params=_params("parallel"),
        name="rope",
    )(x, cos, sin)


def _softmax(s):
    p = jnp.exp(s - jnp.max(s, axis=-1, keepdims=True))
    return p / jnp.sum(p, axis=-1, keepdims=True)


def _diff_attn_kernel(lam_ref, g_ref, q_ref, k_ref, v_ref, o_ref, *, lam_init):
    lam = lam_ref[...]
    l1 = jnp.sum(jnp.sum(lam[0:1] * lam[1:2], axis=-1, keepdims=True), axis=0, keepdims=True)
    l2 = jnp.sum(jnp.sum(lam[2:3] * lam[3:4], axis=-1, keepdims=True), axis=0, keepdims=True)
    lam_val = jnp.exp(l1) - jnp.exp(l2) + lam_init
    scale = A_QK ** -0.5
    for h in range(HEADS):
        sl = slice(h * HEAD_DIM, (h + 1) * HEAD_DIM)
        q = q_ref[:, sl].astype(BF16)
        k = k_ref[:, sl].astype(BF16)
        p1 = _softmax(_dot_nt(q[:, :A_QK], k[:, :A_QK]) * scale)
        p2 = _softmax(_dot_nt(q[:, A_QK:], k[:, A_QK:]) * scale)
        a = p1 - lam_val * p2
        o = jnp.dot(a.astype(BF16), v_ref[:, sl].astype(BF16), preferred_element_type=F32)
        o = o * lax.rsqrt(jnp.mean(o * o, axis=-1, keepdims=True) + RMS_EPS) * g_ref[...]
        o_ref[:, sl] = (o * (1.0 - lam_init)).astype(o_ref.dtype)


ATT_TQ = 256
ATT_W = HEADS * HEAD_DIM


def _q_rows(t, width, row0=0, col=0):
    per, off = t // ATT_TQ, row0 // ATT_TQ
    return pl.BlockSpec((ATT_TQ, width), lambda b, j: (off + b * per + j, col))


def _kv_rows(s, width, row0=0, col=0):
    off = row0 // s
    return pl.BlockSpec((s, width), lambda b, j: (off + b, col))


def _attention_call(kernel, name, nb, t, operands, specs):
    return pl.pallas_call(
        kernel,
        grid=(nb, t // ATT_TQ),
        in_specs=specs,
        out_specs=_q_rows(t, ATT_W),
        out_shape=jax.ShapeDtypeStruct((nb * t, ATT_W), BF16),
        compiler_params=_params("parallel", "parallel"),
        name=name,
    )(*operands)


def _diff_attention(nb, t, q, k, v, lam, subln_g, lam_init):
    const = [pl.BlockSpec((4, A_QK), lambda i, j: (0, 0)), pl.BlockSpec((1, HEAD_DIM), lambda i, j: (0, 0))]
    return _attention_call(functools.partial(_diff_attn_kernel, lam_init=lam_init), "diff_attention", nb, t,
                           [lam, subln_g.reshape(1, HEAD_DIM), q[0], k[0], v[0]], const + [q[1], k[1], v[1]])


def _mla_attn_kernel(qn_ref, qr_ref, kn_ref, kr_ref, v_ref, o_ref):
    scale = (B_NOPE + B_ROPE) ** -0.5
    kr = kr_ref[:, 0:B_ROPE].astype(BF16)
    for h in range(HEADS):
        sl = slice(h * HEAD_DIM, (h + 1) * HEAD_DIM)
        s = _dot_nt(qn_ref[:, sl].astype(BF16), kn_ref[:, sl])
        s = s + _dot_nt(qr_ref[:, h * B_ROPE:(h + 1) * B_ROPE].astype(BF16), kr)
        p = _softmax(s * scale)
        o = jnp.dot(p.astype(BF16), v_ref[:, sl], preferred_element_type=F32)
        o_ref[:, sl] = o.astype(o_ref.dtype)


def _mla_attention(nb, t, qn, qr, kn, kr, v):
    ops = [qn, qr, kn, kr, v]
    return _attention_call(_mla_attn_kernel, "mla_attention", nb, t, [o[0] for o in ops], [o[1] for o in ops])


def _plain_attn_kernel(q_ref, k_ref, v_ref, o_ref):
    scale = HEAD_DIM ** -0.5
    for h in range(HEADS):
        sl = slice(h * HEAD_DIM, (h + 1) * HEAD_DIM)
        p = _softmax(_dot_nt(q_ref[:, sl].astype(BF16), k_ref[:, sl].astype(BF16)) * scale)
        o = jnp.dot(p.astype(BF16), v_ref[:, sl].astype(BF16), preferred_element_type=F32)
        o_ref[:, sl] = o.astype(o_ref.dtype)


def _plain_attention(nb, t, q, k, v):
    ops = [q, k, v]
    return _attention_call(_plain_attn_kernel, "plain_attention", nb, t, [o[0] for o in ops], [o[1] for o in ops])


def _na_window_start(r):
    return jnp.clip(r - NA_KH // 2, 0, GRID_ROWS - NA_KH)


def _na_attn_kernel(q_ref, k_ref, v_ref, kc_ref, vc_ref, bias_ref, o_ref):
    scale = HEAD_DIM ** -0.5
    r = pl.program_id(1)
    start = pl.multiple_of(_na_window_start(r) * GRID_W, GRID_W)
    for h in range(HEADS):
        sl = slice(h * HEAD_DIM, (h + 1) * HEAD_DIM)
        q = q_ref[:, sl].astype(BF16)
        kw = k_ref[pl.ds(start, NA_WIN), sl].astype(BF16)
        vw = v_ref[pl.ds(start, NA_WIN), sl].astype(BF16)
        s_win = _dot_nt(q, kw) * scale + bias_ref[0, h]
        s_ctx = _dot_nt(q, kc_ref[:, sl].astype(BF16)) * scale
        m = jnp.maximum(jnp.max(s_win, axis=-1, keepdims=True), jnp.max(s_ctx, axis=-1, keepdims=True))
        p_win = jnp.exp(s_win - m)
        p_ctx = jnp.exp(s_ctx - m)
        den = jnp.sum(p_win, axis=-1, keepdims=True) + jnp.sum(p_ctx, axis=-1, keepdims=True)
        o = (jnp.dot(p_win.astype(BF16), vw, preferred_element_type=F32)
             + jnp.dot(p_ctx.astype(BF16), vc_ref[:, sl].astype(BF16), preferred_element_type=F32))
        o_ref[:, sl] = (o / den).astype(o_ref.dtype)


NA_NDC = 2 * NA_KW


def _na_bias_kernel(r_ref, e_ref, o_ref):
    o_ref[0] = jnp.dot(r_ref[0], e_ref[...], preferred_element_type=F32, precision=lax.Precision.HIGHEST)


def _na_bias_tables(rpb):
    cols = np.arange(GRID_W)
    cs = np.clip(cols - NA_KW // 2, 0, GRID_W - NA_KW)
    col_in = (cols[None, :] >= cs[:, None]) & (cols[None, :] < cs[:, None] + NA_KW)
    dc_idx = np.clip(cols[None, :] - cols[:, None] + NA_KW - 1, 0, 2 * NA_KW - 2)
    mask = np.broadcast_to(col_in[:, None, :], (GRID_W, NA_KH, GRID_W)).reshape(GRID_W, NA_WIN)
    onehot = (np.arange(NA_NDC)[:, None] == dc_idx.reshape(1, -1)).astype(np.float32)
    rp = jnp.pad(rpb, ((0, 0), (0, 0), (0, NA_NDC - rpb.shape[2])))
    rows = jnp.stack([rp[:, NA_KH - 1 - d:2 * NA_KH - 1 - d, :] for d in range(NA_KH)])
    rows = rows.reshape(NA_KH, HEADS * NA_KH, NA_NDC)
    nqk = GRID_W * GRID_W
    b = pl.pallas_call(
        _na_bias_kernel,
        grid=(NA_KH,),
        in_specs=[pl.BlockSpec((1, HEADS * NA_KH, NA_NDC), lambda d: (d, 0, 0)),
                  pl.BlockSpec((NA_NDC, nqk), lambda d: (0, 0))],
        out_specs=pl.BlockSpec((1, HEADS * NA_KH, nqk), lambda d: (d, 0, 0)),
        out_shape=jax.ShapeDtypeStruct((NA_KH, HEADS * NA_KH, nqk), F32),
        compiler_params=_params("parallel"),
        name="na_bias",
    )(rows, jnp.asarray(onehot))
    b = b.reshape(NA_KH, HEADS, NA_KH, GRID_W, GRID_W).transpose(0, 1, 3, 2, 4).reshape(NA_KH, HEADS, GRID_W, NA_WIN)
    return jnp.where(mask[None, None], b, NEG_INF)


def _na_attention(z, kc, vc, bias):
    w = ATT_W
    row0 = z.shape[0] - N_SMP

    def all_rows(n, col, base):
        return pl.BlockSpec((n, w), lambda i, r: (base // n + i, col))

    return pl.pallas_call(
        _na_attn_kernel,
        grid=(DEC_BATCH, GRID_ROWS),
        in_specs=[pl.BlockSpec((GRID_W, w), lambda i, r: (row0 // GRID_W + i * GRID_ROWS + r, 0)),
                  all_rows(DEC_SEQ, 1, row0), all_rows(DEC_SEQ, 2, row0),
                  all_rows(PAST_LEN, 0, 0), all_rows(PAST_LEN, 0, 0),
                  pl.BlockSpec((1, HEADS, GRID_W, NA_WIN), lambda i, r: (r - _na_window_start(r), 0, 0, 0))],
        out_specs=pl.BlockSpec((GRID_W, w), lambda i, r: (i * GRID_ROWS + r, 0)),
        out_shape=jax.ShapeDtypeStruct((N_SMP, w), BF16),
        compiler_params=_params("parallel", "parallel"),
        name="na_attention",
    )(z, z, z, kc, vc, bias)


CONV_CW = 384


CONV_ROWS = 1024


def _conv3_kernel(x_ref, w_ref, o_ref, *, seq):
    x = x_ref[...]
    t = x.shape[0]
    pos = lax.broadcasted_iota(jnp.int32, x.shape, 0) % seq
    prev = jnp.where(pos == 0, 0.0, pltpu.roll(x, 1, 0))
    nxt = jnp.where(pos == seq - 1, 0.0, pltpu.roll(x, t - 1, 0))
    w = w_ref[...]
    o_ref[...] = prev * w[0:1] + x * w[1:2] + nxt * w[2:3]


def _conv3(z, conv_w, seq, row0, n_seq):
    col0 = (3 * D_WIDTH) // CONV_CW
    blk0 = row0 // CONV_ROWS
    n_rows = n_seq * seq
    return pl.pallas_call(
        functools.partial(_conv3_kernel, seq=seq),
        grid=(n_rows // CONV_ROWS, D_IN // CONV_CW),
        in_specs=[pl.BlockSpec((CONV_ROWS, CONV_CW), lambda i, j: (blk0 + i, col0 + j)),
                  pl.BlockSpec((3, CONV_CW), lambda i, j: (0, j))],
        out_specs=pl.BlockSpec((CONV_ROWS, CONV_CW), lambda i, j: (i, j)),
        out_shape=jax.ShapeDtypeStruct((n_rows, D_IN), F32),
        compiler_params=_params("parallel", "parallel"),
        name="conv3",
    )(z, conv_w)


def _d_prep_kernel(x_ref, w0_ref, w2_ref, a0_ref, a2_ref, g2_ref, dec_ref, a_ref, g_ref):
    x = x_ref[...]
    xw = x[:, 0:2 * D_LORA]
    xa = x[:, 2 * D_LORA:4 * D_LORA]
    xg = x[:, 4 * D_LORA:]
    for d in range(2):
        u = w0_ref[d] + _bdot(jnp.tanh(xw[:, d * D_LORA:(d + 1) * D_LORA]), w2_ref[d])
        nu = -u
        softplus = jnp.maximum(nu, 0.0) + jnp.log1p(jnp.exp(-jnp.abs(nu)))
        wlog = -softplus - 0.5
        dec_ref[d] = -jnp.exp(wlog)
        av = a0_ref[d] + _bdot(xa[:, d * D_LORA:(d + 1) * D_LORA], a2_ref[d])
        a_ref[d] = 1.0 / (1.0 + jnp.exp(-av))
    g_ref[...] = _bdot(1.0 / (1.0 + jnp.exp(-xg)), g2_ref[...])


def _d_prep(zc, w0, w2, a0, a2, g2, tm=512):
    n = zc.shape[0]
    cb = (3 * D_WIDTH) // CONV_CW
    return pl.pallas_call(
        _d_prep_kernel,
        grid=(n // tm,),
        in_specs=[pl.BlockSpec((tm, CONV_CW), lambda i: (i, cb)),
                  pl.BlockSpec((2, 1, D_WIDTH), lambda i: (0, 0, 0)),
                  pl.BlockSpec((2, D_LORA, D_WIDTH), lambda i: (0, 0, 0)),
                  pl.BlockSpec((2, 1, D_WIDTH), lambda i: (0, 0, 0)),
                  pl.BlockSpec((2, D_LORA, D_WIDTH), lambda i: (0, 0, 0)),
                  pl.BlockSpec((D_GATE_LORA, D_WIDTH), lambda i: (0, 0))],
        out_specs=[pl.BlockSpec((2, tm, D_WIDTH), lambda i: (0, i, 0)),
                   pl.BlockSpec((2, tm, D_WIDTH), lambda i: (0, i, 0)),
                   pl.BlockSpec((tm, D_WIDTH), lambda i: (i, 0))],
        out_shape=[jax.ShapeDtypeStruct((2, n, D_WIDTH), F32),
                   jax.ShapeDtypeStruct((2, n, D_WIDTH), F32),
                   jax.ShapeDtypeStruct((n, D_WIDTH), F32)],
        compiler_params=_params("parallel"),
        name="d_prep",
    )(zc, w0.reshape(2, 1, D_WIDTH), w2, a0.reshape(2, 1, D_WIDTH), a2, g2)


RW_C = 64
RW_PP = 8


def _bdot(a, b):
    ah, al = _split_hi_lo(a)
    bh, bl = _split_hi_lo(b)
    dot = functools.partial(jnp.dot, preferred_element_type=F32)
    return dot(ah, bh) + dot(ah, bl) + dot(al, bh)


def _bdot2(a, b):
    ah, al = _split_hi_lo(a)
    dot = functools.partial(jnp.dot, preferred_element_type=F32)
    return dot(ah, b) + dot(al, b)


def _bdot_nt(a, b):
    ah, al = _split_hi_lo(a)
    bh, bl = _split_hi_lo(b)
    return _dot_nt(ah, bh) + _dot_nt(ah, bl) + _dot_nt(al, bh)


def _split_hi_lo(x):
    hi = x.astype(BF16)
    return hi, (x - hi.astype(F32)).astype(BF16)


def _head_ones():
    r = lax.broadcasted_iota(jnp.int32, (LANES, LANES), 0) // D_HS
    c = lax.broadcasted_iota(jnp.int32, (LANES, LANES), 1) // D_HS
    return jnp.where(r == c, 1.0, 0.0).astype(BF16)


def _rwkv_chunk_kernel(r_ref, k_ref, v_ref, lw_ref, a_ref, kkp_ref, kap_ref, s0_ref, y_ref, sfin_ref, s_ref):
    c = pl.program_id(3)
    two_c = 2 * RW_C

    @pl.when(c == 0)
    def _():
        s_ref[...] = s0_ref[...]

    sgn = jnp.where(pl.program_id(1) == 1, -1, 1)
    row = lax.broadcasted_iota(jnp.int32, (two_c, two_c), 0)
    col = lax.broadcasted_iota(jnp.int32, (two_c, two_c), 1)
    same = (row // RW_C) == (col // RW_C)
    tt, ss = row % RW_C, col % RW_C
    before = (ss - tt) * sgn < 0
    strict = same & before
    incl = same & (before | (ss == tt))
    eye = jnp.where(row == col, 1.0, 0.0)
    r64 = lax.broadcasted_iota(jnp.int32, (RW_C, RW_C), 0)
    c64 = lax.broadcasted_iota(jnp.int32, (RW_C, RW_C), 1)
    ltri = jnp.where((c64 - r64) * sgn <= 0, 1.0, 0.0).astype(BF16)
    head0 = lax.broadcasted_iota(jnp.int32, (RW_C, LANES), 1) < D_HS
    ones_blk = _head_ones()

    def stack(x):
        return jnp.concatenate([jnp.where(head0, x, 0.0), jnp.where(head0, 0.0, x)], axis=0)

    pairs = range(RW_PP)
    lss = [slice(p * LANES, (p + 1) * LANES) for p in pairs]
    r = [r_ref[:, ls] for ls in lss]
    k = [k_ref[:, ls] for ls in lss]
    lw = [lw_ref[:, ls] for ls in lss]
    a = [a_ref[:, ls] for ls in lss]
    kkf = [k[p] * kkp_ref[:, lss[p]] for p in pairs]
    n2 = [_bdot2(kkf[p] * kkf[p], ones_blk) for p in pairs]
    kk = [kkf[p] / jnp.maximum(jnp.sqrt(n2[p]), 1e-12) for p in pairs]
    lws = [_split_hi_lo(lw[p]) for p in pairs]
    cum = [jnp.dot(ltri, lws[p][0], preferred_element_type=F32) + jnp.dot(ltri, lws[p][1], preferred_element_type=F32)
           for p in pairs]
    mid = [cum[p][RW_C // 2:RW_C // 2 + 1] for p in pairs]
    cumc = [cum[p] - mid[p] for p in pairs]
    pmid = [jnp.exp(mid[p]) for p in pairs]
    pend = [jnp.exp(jnp.sum(lw[p], axis=0, keepdims=True) - mid[p]) for p in pairs]
    pinv = [jnp.exp(-cumc[p]) for p in pairs]
    ks = [stack(kk[p] * jnp.exp(cumc[p] - lw[p])).astype(BF16) for p in pairs]
    rs = [stack(r[p] * jnp.exp(cumc[p])).astype(BF16) for p in pairs]
    khs = [stack(k[p] * (1.0 + (a[p] - 1.0) * kap_ref[:, lss[p]]) * pinv[p]).astype(BF16) for p in pairs]
    bhs = [stack(kk[p] * a[p] * pinv[p]).astype(BF16) for p in pairs]
    vs = [stack(v_ref[:, ls]).astype(BF16) for ls in lss]
    big = [_dot_nt(jnp.concatenate([ks[p], rs[p]], axis=0), jnp.concatenate([bhs[p], khs[p]], axis=0)) for p in pairs]
    a_b = [jnp.where(strict, big[p][:two_c, :two_c], 0.0) for p in pairs]
    a_k = [jnp.where(strict, big[p][:two_c, two_c:], 0.0) for p in pairs]
    l_b = [jnp.where(incl, big[p][two_c:, :two_c], 0.0) for p in pairs]
    l_k = [jnp.where(incl, big[p][two_c:, two_c:], 0.0) for p in pairs]
    npow = [-a_b[p] for p in pairs]
    tinv = [eye + npow[p] for p in pairs]
    for _ in range(5):
        npow = [_bdot(npow[p], npow[p]) for p in pairs]
        tinv = [tinv[p] + _bdot(tinv[p], npow[p]) for p in pairs]
    w1 = [_bdot2(a_k[p], vs[p]) for p in pairs]
    ktil = [_bdot2(tinv[p], ks[p]) for p in pairs]
    uv = [_bdot(tinv[p], w1[p]) for p in pairs]
    lku = [_bdot(l_b[p], jnp.concatenate([ktil[p], uv[p]], axis=1)) for p in pairs]
    rtil = [rs[p].astype(F32) - lku[p][:, :LANES] for p in pairs]
    yv = [_bdot2(l_k[p], vs[p]) - lku[p][:, LANES:] for p in pairs]
    gt = [_bdot2(ktil[p].T, bhs[p]) for p in pairs]
    ht = [jnp.dot(vs[p].astype(F32).T.astype(BF16), khs[p], preferred_element_type=F32) - _bdot2(uv[p].T, bhs[p])
          for p in pairs]
    s = [s_ref[p] * pmid[p] for p in pairs]
    y = [_bdot_nt(rtil[p], s[p]) + yv[p] for p in pairs]
    sg = [_bdot(s[p], gt[p]) for p in pairs]
    for p in pairs:
        s_ref[p] = (s[p] - sg[p] + ht[p]) * pend[p]
        y_ref[:, lss[p]] = y[p][:RW_C] + y[p][RW_C:]

    @pl.when(c == pl.num_programs(3) - 1)
    def _():
        sfin_ref[...] = s_ref[...]


def _pair_states(s):
    zero = jnp.zeros_like(s[:, :, 0::2])
    top = jnp.concatenate([s[:, :, 0::2], zero], axis=-1)
    bot = jnp.concatenate([zero, s[:, :, 1::2]], axis=-1)
    return jnp.concatenate([top, bot], axis=-2)


def _head_states(sp):
    b = sp.shape[0]
    both = jnp.stack([sp[:, :, :, :D_HS, :D_HS], sp[:, :, :, D_HS:, D_HS:]], axis=3)
    return both.reshape(b, 2, D_HEADS, D_HS, D_HS)


def _rwkv_chunked(zc, lw, a, k_k, k_a, s0, b, t):
    nc = t // RW_C
    gw = RW_PP * LANES
    npg = D_WIDTH // gw

    def cmap(d, c):
        return jnp.where(d == 0, c, nc - 1 - c)

    def z_spec(colblk):
        return pl.BlockSpec((RW_C, gw), lambda i, d, g, c: (i * nc + cmap(d, c), colblk * npg + g))

    dir_spec = pl.BlockSpec((None, RW_C, gw), lambda i, d, g, c: (d, i * nc + cmap(d, c), g))
    par_spec = pl.BlockSpec((1, gw), lambda i, d, g, c: (0, g))
    st_spec = pl.BlockSpec((None, None, RW_PP, LANES, LANES), lambda i, d, g, c: (i, d, g, 0, 0))
    return pl.pallas_call(
        _rwkv_chunk_kernel,
        grid=(b, 2, npg, nc),
        in_specs=[z_spec(0), z_spec(1), z_spec(2), dir_spec, dir_spec, par_spec, par_spec, st_spec],
        out_specs=[dir_spec, st_spec],
        out_shape=[jax.ShapeDtypeStruct((2, b * t, D_WIDTH), F32),
                   jax.ShapeDtypeStruct((b, 2, D_WIDTH // LANES, LANES, LANES), F32)],
        scratch_shapes=[pltpu.VMEM((RW_PP, LANES, LANES), F32)],
        compiler_params=_params("parallel", "parallel", "parallel", "arbitrary"),
        name="rwkv_chunk",
    )(zc, zc, zc, lw, a, k_k.reshape(1, D_WIDTH), k_a.reshape(1, D_WIDTH), s0)


def _d_out_kernel(y_ref, r_ref, k_ref, v_ref, a_ref, g_ref, kap_ref, rkp_ref, lng_ref, lnb_ref, o_ref):
    ones_blk = _head_ones()

    def seg_sum(x):
        hi, lo = _split_hi_lo(x)
        return jnp.dot(hi, ones_blk, preferred_element_type=F32) + jnp.dot(lo, ones_blk, preferred_element_type=F32)

    for cb in range(D_WIDTH // LANES):
        ls = slice(cb * LANES, (cb + 1) * LANES)
        y = y_ref[0, :, ls] + y_ref[1, :, ls]
        mu = seg_sum(y) * (1.0 / D_HS)
        yc = y - mu
        var = seg_sum(yc * yc) * (1.0 / D_HS)
        yn = yc * lax.rsqrt(var + D_LN_EPS) * lng_ref[:, ls] + lnb_ref[:, ls]
        asum = a_ref[0, :, ls] + a_ref[1, :, ls]
        bonus = seg_sum(r_ref[:, ls] * k_ref[:, ls] * rkp_ref[:, ls] * (2.0 + (asum - 2.0) * kap_ref[:, ls]))
        o_ref[:, ls] = (yn + bonus * v_ref[:, ls]) * g_ref[:, ls]


def _d_out(y, zc, a, g, k_a, r_k, ln_g, ln_b, tm=512):
    n = g.shape[0]

    def z_spec(colblk):
        return pl.BlockSpec((tm, D_WIDTH), lambda i: (i, colblk))

    dspec = pl.BlockSpec((2, tm, D_WIDTH), lambda i: (0, i, 0))
    pspec = pl.BlockSpec((1, D_WIDTH), lambda i: (0, 0))
    return pl.pallas_call(
        _d_out_kernel,
        grid=(n // tm,),
        in_specs=[dspec, z_spec(0), z_spec(1), z_spec(2), dspec, pl.BlockSpec((tm, D_WIDTH), lambda i: (i, 0)),
                  pspec, pspec, pspec, pspec],
        out_specs=pl.BlockSpec((tm, D_WIDTH), lambda i: (i, 0)),
        out_shape=jax.ShapeDtypeStruct((n, D_WIDTH), F32),
        compiler_params=_params("parallel"),
        name="d_out",
    )(y, zc, zc, zc, a, g, k_a.reshape(1, D_WIDTH), r_k.reshape(1, D_WIDTH), ln_g.reshape(1, D_WIDTH),
      ln_b.reshape(1, D_WIDTH))


TOPK_TM = 256
BIG = float(2 ** 30)


def _extract_topk(s, labels):
    iota16 = lax.broadcasted_iota(jnp.int32, (PEER_TOPK, s.shape[1]), 0)
    rank = jnp.full(s.shape, BIG, F32)
    vals = jnp.zeros((PEER_TOPK, s.shape[1]), F32)
    for r in range(PEER_TOPK):
        m = jnp.max(s, axis=0, keepdims=True)
        idx = jnp.min(jnp.where(s == m, labels, BIG), axis=0, keepdims=True)
        hit = labels == idx
        rank = jnp.where(hit, float(r), rank)
        vals = jnp.where(iota16 == r, m, vals)
        s = jnp.where(hit, -jnp.inf, s)
    return rank, vals


_CAND_PAIRS = ([(0, b) for b in range(16)] + [(1, b) for b in range(8)] + [(2, b) for b in range(8)]
               + [(3, b) for b in range(8)] + [(4, b) for b in range(4)] + [(5, b) for b in range(4)]
               + [(6, b) for b in range(4)] + [(7, b) for b in range(4)] + [(a, 0) for a in range(8, 16)])
N_CAND = len(_CAND_PAIRS)


def _cand_labels():
    lab = np.array([a * PEER_TOPK + b for a, b in _CAND_PAIRS], np.float32)
    return jnp.asarray(np.broadcast_to(lab[:, None], (N_CAND, LANES)).copy())


def _peer_topk_kernel(q_ref, keys_ref, lab_ref, lim_ref, e1_ref, rb_ref, e2_ref):
    iota_k = lax.broadcasted_iota(jnp.int32, (PEER_NKEYS, LANES), 0).astype(F32)
    row8 = lax.broadcasted_iota(jnp.int32, (8, LANES), 0)
    labels = lab_ref[...]
    for c in range(TOPK_TM // LANES):
        cs = slice(c * LANES, (c + 1) * LANES)
        q1 = q_ref[cs, 0:PEER_HALF].astype(BF16)
        q2 = q_ref[cs, PEER_HALF:2 * PEER_HALF].astype(BF16)
        s1 = _dot_nt(keys_ref[0, 0], q1)
        s2 = _dot_nt(keys_ref[0, 1], q2)
        rank1, sv1 = _extract_topk(s1, iota_k)
        rank2, sv2 = _extract_topk(s2, iota_k)
        lo8 = sv2[0:8]
        lo4 = jnp.where(row8 < 4, lo8, pltpu.roll(lo8, 4, 0))
        cand = jnp.concatenate([
            sv1[0:1] + lo8, sv1[0:1] + sv2[8:16], sv1[1:2] + lo8, sv1[2:3] + lo8, sv1[3:4] + lo8,
            jnp.where(row8 < 4, sv1[4:5], sv1[5:6]) + lo4, jnp.where(row8 < 4, sv1[6:7], sv1[7:8]) + lo4,
            sv1[8:16] + sv2[0:1]], axis=0)
        crank, cvals = _extract_topk(cand, labels)
        z = jnp.sum(jnp.exp(cvals - cvals[0:1]), axis=0, keepdims=True)
        sel = jnp.where(crank < BIG, 1.0, 0.0)

        def count(lo, hi):
            return jnp.sum(sel[lo:hi], axis=0, keepdims=True)

        def count_half(lo, first):
            part = jnp.where((row8 < 4) if first else (row8 >= 4), sel[lo:lo + 8], 0.0)
            return jnp.sum(part, axis=0, keepdims=True)

        n_sel = [count(0, 16), count(16, 24), count(24, 32), count(32, 40),
                 count_half(40, True), count_half(40, False), count_half(48, True), count_half(48, False)]
        n_sel += [sel[56 + a:57 + a] for a in range(8)]
        lim = jnp.zeros_like(s1)
        for a in range(PEER_TOPK):
            lim = jnp.where(rank1 == float(a), n_sel[a], lim)
        lim_ref[0, :, cs] = lim
        e1_ref[0, :, cs] = jnp.exp(s1 - sv1[0:1])
        rb_ref[0, :, cs] = rank2
        e2_ref[0, :, cs] = jnp.exp(s2 - sv2[0:1]) / z


def _peer_topk(q, keys):
    n = q.shape[0]
    ospec = pl.BlockSpec((1, PEER_NKEYS, TOPK_TM), lambda i, h: (h, 0, i))
    return pl.pallas_call(
        _peer_topk_kernel,
        grid=(n // TOPK_TM, PEER_HEADS),
        in_specs=[pl.BlockSpec((TOPK_TM, 2 * PEER_HALF), lambda i, h: (i, h)),
                  pl.BlockSpec((1, 2, PEER_NKEYS, PEER_HALF), lambda i, h: (h, 0, 0, 0)),
                  pl.BlockSpec((N_CAND, LANES), lambda i, h: (0, 0))],
        out_specs=[ospec] * 4,
        out_shape=[jax.ShapeDtypeStruct((PEER_HEADS, PEER_NKEYS, n), F32)] * 4,
        compiler_params=_params("parallel", "parallel"),
        name="peer_topk",
    )(q, keys, _cand_labels())


PEER_TM = 512
PEER_TI = 8
PEER_RG = 4
PEER_JT = 32


def _gelu(x):
    return 0.5 * x * (1.0 + lax.erf(x * (2.0 ** -0.5)))


def _peer_dense_kernel(x_ref, u_ref, v_ref, lim_ref, e1_ref, rb_ref, e2_ref, res_ref, gate_ref, o_ref, w_ref):
    e = pl.program_id(1)

    @pl.when(e == 0)
    def _():
        o_ref[...] = jnp.zeros_like(o_ref)

    hid = _dot_nt(u_ref[...], x_ref[...])
    for c in range(PEER_TM // LANES):
        cs = slice(c * LANES, (c + 1) * LANES)
        for jt in range(PEER_NKEYS // PEER_JT):
            js = slice(jt * PEER_JT, (jt + 1) * PEER_JT)
            for rg in range(PEER_TI // PEER_RG):
                rows = range(rg * PEER_RG, (rg + 1) * PEER_RG)
                g = [jnp.zeros((PEER_JT, LANES), F32) for _ in rows]
                for h in range(PEER_HEADS):
                    rb = rb_ref[h, js, cs]
                    e2 = e2_ref[h, js, cs]
                    for k, ii in enumerate(rows):
                        g[k] = g[k] + jnp.where(rb < lim_ref[h, ii:ii + 1, cs], e2, 0.0) * e1_ref[h, ii:ii + 1, cs]
                for k, ii in enumerate(rows):
                    es = slice(ii * PEER_NKEYS + jt * PEER_JT, ii * PEER_NKEYS + (jt + 1) * PEER_JT)
                    w_ref[es, cs] = (g[k] * _gelu(hid[es, cs])).astype(BF16)
    o_ref[...] += _dot_tn(w_ref[...], v_ref[...])

    @pl.when(e == pl.num_programs(1) - 1)
    def _():
        o_ref[...] = res_ref[...] + gate_ref[0] * o_ref[...]


def _peer_dense(xm, u, v, layer, lim, e1, rb, e2, res, gate):
    n, d = xm.shape
    n_exp = u.shape[1]
    te = PEER_TI * PEER_NKEYS
    sel_spec = pl.BlockSpec((PEER_HEADS, PEER_NKEYS, PEER_TM), lambda i, e: (0, 0, i))
    row_spec = pl.BlockSpec((PEER_HEADS, PEER_TI, PEER_TM), lambda i, e: (0, e, i))
    return pl.pallas_call(
        _peer_dense_kernel,
        grid=(n // PEER_TM, n_exp // te),
        in_specs=[pl.BlockSpec((PEER_TM, d), lambda i, e: (i, 0)),
                  pl.BlockSpec((None, te, d), lambda i, e: (layer, e, 0)),
                  pl.BlockSpec((None, te, d), lambda i, e: (layer, e, 0)),
                  row_spec, row_spec, sel_spec, sel_spec,
                  pl.BlockSpec((PEER_TM, d), lambda i, e: (i, 0)),
                  pl.BlockSpec((1, 1, d), lambda i, e: (_cond_of_tile(i, PEER_TM), 0, 0))],
        out_specs=pl.BlockSpec((PEER_TM, d), lambda i, e: (i, 0)),
        out_shape=jax.ShapeDtypeStruct((n, d), F32),
        scratch_shapes=[pltpu.VMEM((te, PEER_TM), BF16)],
        compiler_params=_params("parallel", "arbitrary"),
        name="peer_dense",
    )(xm, u, v, lim, e1, rb, e2, res, gate)


def _peer_layer(x, layer, norm_g, shift, scale, gate, wq, keys, u, v):
    xm = _modulate(x, norm_g, shift, scale)
    q = _matmul(xm, wq, layer, out_dtype=BF16)
    lim, e1, rb, e2 = _peer_topk(q, keys)
    return _peer_dense(xm, u, v, layer, lim, e1, rb, e2, x, gate)


EVEN_PAD = 4096
ODD_PAD = 6656
S_ALL = PAST_LEN + DEC_SEQ


def _even_mixer(h, w_in, layer, lam, subln_g, q_norm_g, w_uq, kv_norm_g, w_ukv, lam_init,
                cache_k, cache_v, cache_ckv, cache_kpe, rope_cos, rope_sin):
    z = _matmul(h, w_in, layer)
    ckv_n = _rmsnorm(z, kv_norm_g, col=3584 // B_KV_RANK)
    cq_n = _rmsnorm(z, q_norm_g, out_dtype=BF16, col=3072 // B_Q_RANK)
    wq3 = w_uq.reshape(B_Q_RANK, HEADS, B_NOPE + B_ROPE)
    w_uq_r = jnp.concatenate([wq3[:, :, :B_NOPE].reshape(B_Q_RANK, -1), wq3[:, :, B_NOPE:].reshape(B_Q_RANK, -1)], axis=1)
    qb = _matmul(cq_n, w_uq_r.astype(BF16), tn=512)
    new = (z[:N_CTX, 1024:2048].reshape(BATCH, SEQ, HEADS, 2 * A_QK),
           z[:N_CTX, 2048:3072].reshape(BATCH, SEQ, HEADS, HEAD_DIM),
           ckv_n[:N_CTX].reshape(BATCH, SEQ, B_KV_RANK), z[:N_CTX, 3840:3904].reshape(BATCH, SEQ, B_ROPE))

    qa_s = _rope(z, rope_cos, rope_sin, ATT_W, col=0)
    ka_s = _rope(z, rope_cos, rope_sin, ATT_W, col=1)
    qr_s = _rope(qb, rope_cos, rope_sin, HEADS * B_ROPE, col=2)
    kpe_s = _rope(z, rope_cos, rope_sin, LANES, col=3840 // LANES)[:, :B_ROPE]

    def with_ctx(cache, own, width):
        both = jnp.concatenate([cache.reshape(DEC_BATCH, PAST_LEN, width).astype(own.dtype),
                                own.reshape(DEC_BATCH, DEC_SEQ, width)], axis=1)
        return both.reshape(DEC_BATCH * S_ALL, width)

    ka_all = with_ctx(cache_k, ka_s, ATT_W)
    va_all = with_ctx(cache_v, z[N_CTX:, 2048:3072].astype(BF16), ATT_W)
    ckv_all = with_ctx(cache_ckv, ckv_n[N_CTX:], B_KV_RANK)
    kpe_all = with_ctx(cache_kpe, kpe_s, B_ROPE)

    wkv3 = w_ukv.reshape(B_KV_RANK, HEADS, B_NOPE + HEAD_DIM)
    w_ukv_r = jnp.concatenate([wkv3[:, :, :B_NOPE].reshape(B_KV_RANK, -1), wkv3[:, :, B_NOPE:].reshape(B_KV_RANK, -1)], axis=1)
    n_skv = DEC_BATCH * S_ALL
    kv = _matmul(jnp.concatenate([ckv_all, ckv_n[:N_CTX]], axis=0), w_ukv_r.astype(BF16), out_dtype=BF16)

    oa_c = _diff_attention(BATCH, SEQ, (z, _q_rows(SEQ, ATT_W, 0, 0)), (z, _kv_rows(SEQ, ATT_W, 0, 1)),
                           (z, _kv_rows(SEQ, ATT_W, 0, 2)), lam, subln_g, lam_init)
    oa_s = _diff_attention(DEC_BATCH, DEC_SEQ, (qa_s, _q_rows(DEC_SEQ, ATT_W)), (ka_all, _kv_rows(S_ALL, ATT_W)),
                           (va_all, _kv_rows(S_ALL, ATT_W)), lam, subln_g, lam_init)
    ob_c = _mla_attention(BATCH, SEQ, (qb, _q_rows(SEQ, ATT_W, 0, 0)), (qb, _q_rows(SEQ, HEADS * B_ROPE, 0, 2)),
                          (kv, _kv_rows(SEQ, ATT_W, n_skv, 0)), (z, _kv_rows(SEQ, LANES, 0, 3840 // LANES)),
                          (kv, _kv_rows(SEQ, ATT_W, n_skv, 1)))
    ob_s = _mla_attention(DEC_BATCH, DEC_SEQ, (qb, _q_rows(DEC_SEQ, ATT_W, N_CTX, 0)),
                          (qr_s, _q_rows(DEC_SEQ, HEADS * B_ROPE)), (kv, _kv_rows(S_ALL, ATT_W, 0, 0)),
                          (kpe_all, _kv_rows(S_ALL, B_ROPE)), (kv, _kv_rows(S_ALL, ATT_W, 0, 1)))
    return (oa_c, oa_s, ob_c, ob_s), new


def _odd_mixer(h, w_in, layer, rpb, conv_w, w0, w2, a0, a2, g2, k_k, k_a, r_k, ln_g, ln_b,
               cache_k, cache_v, state):
    z = _matmul(h, w_in, layer)
    new_k = z[:N_CTX, 1024:2048].reshape(BATCH, SEQ, HEADS, HEAD_DIM)
    new_v = z[:N_CTX, 2048:3072].reshape(BATCH, SEQ, HEADS, HEAD_DIM)
    oc_c = _plain_attention(BATCH, SEQ, (z, _q_rows(SEQ, ATT_W, 0, 0)), (z, _kv_rows(SEQ, ATT_W, 0, 1)),
                            (z, _kv_rows(SEQ, ATT_W, 0, 2)))
    oc_s = _na_attention(z, cache_k.reshape(DEC_BATCH * PAST_LEN, ATT_W), cache_v.reshape(DEC_BATCH * PAST_LEN, ATT_W),
                         _na_bias_tables(rpb))

    zc_c = _conv3(z, conv_w, SEQ, 0, BATCH)
    zc_s = _conv3(z, conv_w, DEC_SEQ, N_CTX, DEC_BATCH)
    lw_c, a_c, g_c = _d_prep(zc_c, w0, w2, a0, a2, g2)
    lw_s, a_s, g_s = _d_prep(zc_s, w0, w2, a0, a2, g2)
    rk = r_k.reshape(-1)
    y_c, sfin = _rwkv_chunked(zc_c, lw_c, a_c, k_k, k_a,
                              jnp.zeros((BATCH, 2, D_WIDTH // LANES, LANES, LANES), F32), BATCH, SEQ)
    sfin = _head_states(sfin)
    y_s, _ = _rwkv_chunked(zc_s, lw_s, a_s, k_k, k_a, _pair_states(state), DEC_BATCH, DEC_SEQ)
    od_c = _d_out(y_c, zc_c, a_c, g_c, k_a, rk, ln_g, ln_b)
    od_s = _d_out(y_s, zc_s, a_s, g_s, k_a, rk, ln_g, ln_b)
    return (oc_c, oc_s, od_c, od_s), (new_k, new_v, sfin)


def kernel(x_prompt, x_sample, cache_a_k, cache_a_v, cache_b_ckv, cache_b_kpe, cache_c_k, cache_c_v, state_d, c, c_ctx, ada_w, ada_b, norm1_g, norm2_g, w_out, peer_wq, peer_keys, peer_u, peer_v, final_g, ab_w_in, a_lam, a_subln_g, b_q_norm_g, b_w_uq, b_kv_norm_g, b_w_ukv, cd_w_in, c_rpb, d_conv, d_w0, d_w2, d_a0, d_a2, d_g2, d_k_k, d_k_a, d_r_k, d_ln_g, d_ln_b):
    x = jnp.concatenate([x_prompt.reshape(N_CTX, D_MODEL), x_sample.reshape(N_SMP, D_MODEL)], axis=0)
    cond8 = jnp.pad(jnp.concatenate([c_ctx[None, :], c], axis=0), ((0, 8 - N_COND), (0, 0)))
    ada = _ada_all(cond8, ada_w, ada_b)
    rope_cos, rope_sin = _rope_tables()
    w_even = jnp.pad(ab_w_in, ((0, 0), (0, 0), (0, EVEN_PAD - EVEN_IN))).astype(BF16)
    w_odd = jnp.pad(cd_w_in, ((0, 0), (0, 0), (0, ODD_PAD - ODD_IN))).astype(BF16)
    w_out_b, wq_b, keys_b = w_out.astype(BF16), peer_wq.astype(BF16), peer_keys.astype(BF16)
    u_b, v_b = peer_u.astype(BF16), peer_v.astype(BF16)
    new_ak, new_av, new_bc, new_bp, new_ck, new_cv, new_sd = [], [], [], [], [], [], []
    for l in range(DEPTH):
        i = l // 2
        mods = ada[l, :N_COND].reshape(N_COND, 6, 1, D_MODEL)
        sh1, sc1, g1, sh2, sc2, g2 = (mods[:, m] for m in range(6))
        h = _modulate(x, norm1_g[l], sh1, sc1)
        if l % 2 == 0:
            lam_init = 0.8 - 0.6 * math.exp(-0.3 * l)
            o, (ak, av, bc, bp) = _even_mixer(
                h, w_even, i, a_lam[i], a_subln_g[i], b_q_norm_g[i], b_w_uq[i], b_kv_norm_g[i], b_w_ukv[i],
                lam_init, cache_a_k[:, i], cache_a_v[:, i], cache_b_ckv[:, i], cache_b_kpe[:, i], rope_cos, rope_sin)
            new_ak.append(ak)
            new_av.append(av)
            new_bc.append(bc)
            new_bp.append(bp)
        else:
            o, (ck, cv, sd) = _odd_mixer(
                h, w_odd, i, c_rpb[i], d_conv[i], d_w0[i], d_w2[i], d_a0[i], d_a2[i], d_g2[i],
                d_k_k[i], d_k_a[i], d_r_k[i], d_ln_g[i], d_ln_b[i], cache_c_k[:, i], cache_c_v[:, i], state_d[:, i])
            new_ck.append(ck)
            new_cv.append(cv)
            new_sd.append(sd)
        x = _matmul_residual(*o, w_out_b, l, x, g1)
        x = _peer_layer(x, l, norm2_g[l], sh2, sc2, g2, wq_b, keys_b[l], u_b, v_b)
    y = _rmsnorm(x, final_g)
    y_prompt = y[:N_CTX].reshape(BATCH, SEQ, D_MODEL)
    y_sample = y[N_CTX:].reshape(DEC_BATCH, DEC_SEQ, D_MODEL)
    return (y_prompt, y_sample, jnp.stack(new_ak, axis=1), jnp.stack(new_av, axis=1), jnp.stack(new_bc, axis=1),
            jnp.stack(new_bp, axis=1), jnp.stack(new_ck, axis=1), jnp.stack(new_cv, axis=1), jnp.stack(new_sd, axis=1))
```

```python
import functools
import math

import numpy as np
import jax
import jax.numpy as jnp
from jax import lax
from jax.experimental import pallas as pl
from jax.experimental.pallas import tpu as pltpu

F32 = jnp.float32
BF16 = jnp.bfloat16

D_MODEL = 2048
BATCH = 16
SEQ = 256
DEPTH = 4
DEC_BATCH = 2
DEC_SEQ = 1024
PAST_LEN = 512
GRID_W = 64
GRID_ROWS = DEC_SEQ // GRID_W
ROPE_BASE = 10000.0
RMS_EPS = 1e-6
NEG_INF = -1e30
N_CTX = BATCH * SEQ
N_SMP = DEC_BATCH * DEC_SEQ
N_TOK = N_CTX + N_SMP
N_COND = 1 + DEC_BATCH

HEADS = 8
HEAD_DIM = 128
A_QK = 64
B_Q_RANK = 512
B_KV_RANK = 256
B_NOPE = 128
B_ROPE = 64
NA_KH = 8
NA_KW = 16
NA_WIN = NA_KH * GRID_W
D_HS = 64
D_WIDTH = 1024
D_HEADS = 16
D_LORA = 64
D_GATE_LORA = 128
D_IN = 3 * D_WIDTH + 2 * D_LORA + 2 * D_LORA + D_GATE_LORA
D_LN_EPS = 64e-5
EVEN_IN = 3904
ODD_IN = 6528
PEER_HEADS = 8
PEER_NKEYS = 128
PEER_HALF = 128
PEER_TOPK = 16

LANES = 128
VMEM_LIMIT = 56 * 1024 * 1024


def _params(*sem):
    return pltpu.CompilerParams(dimension_semantics=sem, vmem_limit_bytes=VMEM_LIMIT)


def _dot_nt(a, b):
    return lax.dot_general(a, b, (((1,), (1,)), ((), ())), preferred_element_type=F32)


def _dot_tn(a, b):
    return lax.dot_general(a, b, (((0,), (0,)), ((), ())), preferred_element_type=F32)


def _cond_of_tile(i, tm):
    n_ctx_tiles = N_CTX // tm
    per_batch = DEC_SEQ // tm
    return jnp.where(i < n_ctx_tiles, 0, 1 + (i - n_ctx_tiles) // per_batch)


def _ada_kernel(c_ref, w_ref, b_ref, o_ref):
    c = c_ref[...]
    sc = c * (1.0 / (1.0 + jnp.exp(-c)))
    o_ref[0] = jnp.dot(sc.astype(BF16), w_ref[0].astype(BF16), preferred_element_type=F32) + b_ref[0]


def _ada_all(cond8, ada_w, ada_b):
    tn = 1024
    n_out = ada_w.shape[-1]
    return pl.pallas_call(
        _ada_kernel,
        grid=(DEPTH, n_out // tn),
        in_specs=[pl.BlockSpec((8, D_MODEL), lambda l, j: (0, 0)),
                  pl.BlockSpec((1, D_MODEL, tn), lambda l, j: (l, 0, j)),
                  pl.BlockSpec((1, 1, tn), lambda l, j: (l, 0, j))],
        out_specs=pl.BlockSpec((1, 8, tn), lambda l, j: (l, 0, j)),
        out_shape=jax.ShapeDtypeStruct((DEPTH, 8, n_out), F32),
        compiler_params=_params("parallel", "parallel"),
        name="ada",
    )(cond8, ada_w, ada_b.reshape(DEPTH, 1, n_out))


def _modulate_kernel(x_ref, g_ref, sh_ref, sc_ref, o_ref):
    x = x_ref[...]
    y = x * lax.rsqrt(jnp.mean(x * x, axis=-1, keepdims=True) + RMS_EPS) * g_ref[...]
    o_ref[...] = (y * (1.0 + sc_ref[0]) + sh_ref[0]).astype(o_ref.dtype)


def _modulate(x, g, shift, scale, out_dtype=BF16, tm=512):
    n, d = x.shape
    return pl.pallas_call(
        _modulate_kernel,
        grid=(n // tm,),
        in_specs=[pl.BlockSpec((tm, d), lambda i: (i, 0)),
                  pl.BlockSpec((1, d), lambda i: (0, 0)),
                  pl.BlockSpec((1, 1, d), lambda i: (_cond_of_tile(i, tm), 0, 0)),
                  pl.BlockSpec((1, 1, d), lambda i: (_cond_of_tile(i, tm), 0, 0))],
        out_specs=pl.BlockSpec((tm, d), lambda i: (i, 0)),
        out_shape=jax.ShapeDtypeStruct((n, d), out_dtype),
        compiler_params=_params("parallel"),
        name="modulate",
    )(x, g.reshape(1, d), shift, scale)


def _rmsnorm_kernel(x_ref, g_ref, o_ref):
    x = x_ref[...]
    y = x * lax.rsqrt(jnp.mean(x * x, axis=-1, keepdims=True) + RMS_EPS) * g_ref[...]
    o_ref[...] = y.astype(o_ref.dtype)


def _rmsnorm(x, g, out_dtype=F32, tm=512, col=0):
    n = x.shape[0]
    d = g.shape[-1]
    return pl.pallas_call(
        _rmsnorm_kernel,
        grid=(n // tm,),
        in_specs=[pl.BlockSpec((tm, d), lambda i: (i, col)),
                  pl.BlockSpec((1, d), lambda i: (0, 0))],
        out_specs=pl.BlockSpec((tm, d), lambda i: (i, 0)),
        out_shape=jax.ShapeDtypeStruct((n, d), out_dtype),
        compiler_params=_params("parallel"),
        name="rmsnorm",
    )(x, g.reshape(1, d))


def _mm_kernel(x_ref, w_ref, o_ref):
    o_ref[...] = jnp.dot(x_ref[...].astype(BF16), w_ref[...], preferred_element_type=F32).astype(o_ref.dtype)


def _weight_spec(w, layer, tn):
    if w.ndim == 2:
        return pl.BlockSpec((w.shape[0], tn), lambda i, j: (0, j))
    return pl.BlockSpec((None, w.shape[1], tn), lambda i, j: (layer, 0, j))


MM_TM = 1024
MM_TN = 1024


def _matmul(x, w, layer=None, out_dtype=F32, tm=MM_TM, tn=None, x_col=0, x_width=None):
    m = x.shape[0]
    k = x_width or x.shape[1]
    n = w.shape[-1]
    if tn is None:
        tn = MM_TN if n % MM_TN == 0 else MM_TN // 2
    tn = min(tn, n)
    assert m % tm == 0 and n % tn == 0, (m, n, tm, tn)
    return pl.pallas_call(
        _mm_kernel,
        grid=(m // tm, n // tn),
        in_specs=[pl.BlockSpec((tm, k), lambda i, j: (i, x_col)),
                  _weight_spec(w, layer, tn)],
        out_specs=pl.BlockSpec((tm, tn), lambda i, j: (i, j)),
        out_shape=jax.ShapeDtypeStruct((m, n), out_dtype),
        compiler_params=_params("parallel", "parallel"),
        name="matmul",
    )(x, w)


def _mm_res_kernel(lc_ref, ls_ref, rc_ref, rs_ref, w_ref, res_ref, gate_ref, o_ref, *, n_ctx_tiles):
    half = w_ref.shape[0] // 2

    def emit(l_ref, r_ref):
        acc = jnp.dot(l_ref[...].astype(BF16), w_ref[0:half, :], preferred_element_type=F32)
        acc = acc + jnp.dot(r_ref[...].astype(BF16), w_ref[half:, :], preferred_element_type=F32)
        o_ref[...] = res_ref[...] + gate_ref[0] * acc

    @pl.when(pl.program_id(0) < n_ctx_tiles)
    def _():
        emit(lc_ref, rc_ref)

    @pl.when(pl.program_id(0) >= n_ctx_tiles)
    def _():
        emit(ls_ref, rs_ref)


def _matmul_residual(left_c, left_s, right_c, right_s, w, layer, res, gate, tm=MM_TM, tn=MM_TN // 2):
    m = res.shape[0]
    n = w.shape[-1]
    kh = left_c.shape[1]
    nct = left_c.shape[0] // tm
    ctx_spec = pl.BlockSpec((tm, kh), lambda i, j: (jnp.minimum(i, nct - 1), 0))
    smp_spec = pl.BlockSpec((tm, kh), lambda i, j: (jnp.maximum(i - nct, 0), 0))
    return pl.pallas_call(
        functools.partial(_mm_res_kernel, n_ctx_tiles=nct),
        grid=(m // tm, n // tn),
        in_specs=[ctx_spec, smp_spec, ctx_spec, smp_spec,
                  _weight_spec(w, layer, tn),
                  pl.BlockSpec((tm, tn), lambda i, j: (i, j)),
                  pl.BlockSpec((1, 1, tn), lambda i, j: (_cond_of_tile(i, tm), 0, j))],
        out_specs=pl.BlockSpec((tm, tn), lambda i, j: (i, j)),
        out_shape=jax.ShapeDtypeStruct((m, n), F32),
        compiler_params=_params("parallel", "parallel"),
        name="matmul_residual",
    )(left_c, left_s, right_c, right_s, w, res, gate)


def _rope_tables():
    nf = 16
    inv = ROPE_BASE ** (-np.arange(nf, dtype=np.float64) / nf)
    t = np.arange(DEC_SEQ)
    rows, cols = t // GRID_W, t % GRID_W
    lane = np.arange(64)
    pos = np.where(lane[None, :] < 32, rows[:, None], cols[:, None]).astype(np.float32)
    ang = (pos * inv[lane % nf][None, :].astype(np.float32)).astype(np.float32)
    first = (lane % 32) < nf
    cos = np.cos(ang.astype(np.float64))
    sin = np.sin(ang.astype(np.float64)) * np.where(first, -1.0, 1.0)[None, :]
    cos = np.tile(cos, (1, 2)).astype(np.float32)
    sin = np.tile(sin, (1, 2)).astype(np.float32)
    return jnp.asarray(cos), jnp.asarray(sin)


def _rope_kernel(x_ref, cos_ref, sin_ref, o_ref):
    cos = cos_ref[...]
    sin = sin_ref[...]
    lane = lax.broadcasted_iota(jnp.int32, cos.shape, 1)
    first = (lane % 32) < 16
    for c in range(x_ref.shape[1] // LANES):
        x = x_ref[:, c * LANES:(c + 1) * LANES].astype(F32)
        partner = jnp.where(first, pltpu.roll(x, LANES - 16, 1), pltpu.roll(x, 16, 1))
        o_ref[:, c * LANES:(c + 1) * LANES] = (x * cos + partner * sin).astype(o_ref.dtype)


def _rope(x, cos, sin, w, col=0, out_dtype=BF16, tm=256):
    n = N_SMP
    row0 = (x.shape[0] - N_SMP) // tm
    per = DEC_SEQ // tm
    return pl.pallas_call(
        _rope_kernel,
        grid=(n // tm,),
        in_specs=[pl.BlockSpec((tm, w), lambda i: (row0 + i, col)),
                  pl.BlockSpec((tm, LANES), lambda i: (i % per, 0)),
                  pl.BlockSpec((tm, LANES), lambda i: (i % per, 0))],
        out_specs=pl.BlockSpec((tm, w), lambda i: (i, 0)),
        out_shape=jax.ShapeDtypeStruct((n, w), out_dtype),
        compiler_params=_params("parallel"),
        name="rope",
    )(x, cos, sin)


def _softmax(s):
    p = jnp.exp(s - jnp.max(s, axis=-1, keepdims=True))
    return p / jnp.sum(p, axis=-1, keepdims=True)


def _diff_attn_kernel(lam_ref, g_ref, q_ref, k_ref, v_ref, o_ref, *, lam_init):
    lam = lam_ref[...]
    l1 = jnp.sum(jnp.sum(lam[0:1] * lam[1:2], axis=-1, keepdims=True), axis=0, keepdims=True)
    l2 = jnp.sum(jnp.sum(lam[2:3] * lam[3:4], axis=-1, keepdims=True), axis=0, keepdims=True)
    lam_val = jnp.exp(l1) - jnp.exp(l2) + lam_init
    scale = A_QK ** -0.5
    for h in range(HEADS):
        sl = slice(h * HEAD_DIM, (h + 1) * HEAD_DIM)
        q = q_ref[:, sl].astype(BF16)
        k = k_ref[:, sl].astype(BF16)
        p1 = _softmax(_dot_nt(q[:, :A_QK], k[:, :A_QK]) * scale)
        p2 = _softmax(_dot_nt(q[:, A_QK:], k[:, A_QK:]) * scale)
        a = p1 - lam_val * p2
        o = jnp.dot(a.astype(BF16), v_ref[:, sl].astype(BF16), preferred_element_type=F32)
        o = o * lax.rsqrt(jnp.mean(o * o, axis=-1, keepdims=True) + RMS_EPS) * g_ref[...]
        o_ref[:, sl] = (o * (1.0 - lam_init)).astype(o_ref.dtype)


ATT_TQ = 256
ATT_W = HEADS * HEAD_DIM


def _q_rows(t, width, row0=0, col=0):
    per, off = t // ATT_TQ, row0 // ATT_TQ
    return pl.BlockSpec((ATT_TQ, width), lambda b, j: (off + b * per + j, col))


def _kv_rows(s, width, row0=0, col=0):
    off = row0 // s
    return pl.BlockSpec((s, width), lambda b, j: (off + b, col))


def _attention_call(kernel, name, nb, t, operands, specs):
    return pl.pallas_call(
        kernel,
        grid=(nb, t // ATT_TQ),
        in_specs=specs,
        out_specs=_q_rows(t, ATT_W),
        out_shape=jax.ShapeDtypeStruct((nb * t, ATT_W), BF16),
        compiler_params=_params("parallel", "parallel"),
        name=name,
    )(*operands)


def _diff_attention(nb, t, q, k, v, lam, subln_g, lam_init):
    const = [pl.BlockSpec((4, A_QK), lambda i, j: (0, 0)), pl.BlockSpec((1, HEAD_DIM), lambda i, j: (0, 0))]
    return _attention_call(functools.partial(_diff_attn_kernel, lam_init=lam_init), "diff_attention", nb, t,
                           [lam, subln_g.reshape(1, HEAD_DIM), q[0], k[0], v[0]], const + [q[1], k[1], v[1]])


def _mla_attn_kernel(qn_ref, qr_ref, kn_ref, kr_ref, v_ref, o_ref):
    scale = (B_NOPE + B_ROPE) ** -0.5
    kr = kr_ref[:, 0:B_ROPE].astype(BF16)
    for h in range(HEADS):
        sl = slice(h * HEAD_DIM, (h + 1) * HEAD_DIM)
        s = _dot_nt(qn_ref[:, sl].astype(BF16), kn_ref[:, sl])
        s = s + _dot_nt(qr_ref[:, h * B_ROPE:(h + 1) * B_ROPE].astype(BF16), kr)
        p = _softmax(s * scale)
        o = jnp.dot(p.astype(BF16), v_ref[:, sl], preferred_element_type=F32)
        o_ref[:, sl] = o.astype(o_ref.dtype)


def _mla_attention(nb, t, qn, qr, kn, kr, v):
    ops = [qn, qr, kn, kr, v]
    return _attention_call(_mla_attn_kernel, "mla_attention", nb, t, [o[0] for o in ops], [o[1] for o in ops])


def _plain_attn_kernel(q_ref, k_ref, v_ref, o_ref):
    scale = HEAD_DIM ** -0.5
    for h in range(HEADS):
        sl = slice(h * HEAD_DIM, (h + 1) * HEAD_DIM)
        p = _softmax(_dot_nt(q_ref[:, sl].astype(BF16), k_ref[:, sl].astype(BF16)) * scale)
        o = jnp.dot(p.astype(BF16), v_ref[:, sl].astype(BF16), preferred_element_type=F32)
        o_ref[:, sl] = o.astype(o_ref.dtype)


def _plain_attention(nb, t, q, k, v):
    ops = [q, k, v]
    return _attention_call(_plain_attn_kernel, "plain_attention", nb, t, [o[0] for o in ops], [o[1] for o in ops])


def _na_window_start(r):
    return jnp.clip(r - NA_KH // 2, 0, GRID_ROWS - NA_KH)


def _na_attn_kernel(q_ref, k_ref, v_ref, kc_ref, vc_ref, bias_ref, o_ref):
    scale = HEAD_DIM ** -0.5
    r = pl.program_id(1)
    start = pl.multiple_of(_na_window_start(r) * GRID_W, GRID_W)
    for h in range(HEADS):
        sl = slice(h * HEAD_DIM, (h + 1) * HEAD_DIM)
        q = q_ref[:, sl].astype(BF16)
        kw = k_ref[pl.ds(start, NA_WIN), sl].astype(BF16)
        vw = v_ref[pl.ds(start, NA_WIN), sl].astype(BF16)
        s_win = _dot_nt(q, kw) * scale + bias_ref[0, h]
        s_ctx = _dot_nt(q, kc_ref[:, sl].astype(BF16)) * scale
        m = jnp.maximum(jnp.max(s_win, axis=-1, keepdims=True), jnp.max(s_ctx, axis=-1, keepdims=True))
        p_win = jnp.exp(s_win - m)
        p_ctx = jnp.exp(s_ctx - m)
        den = jnp.sum(p_win, axis=-1, keepdims=True) + jnp.sum(p_ctx, axis=-1, keepdims=True)
        o = (jnp.dot(p_win.astype(BF16), vw, preferred_element_type=F32)
             + jnp.dot(p_ctx.astype(BF16), vc_ref[:, sl].astype(BF16), preferred_element_type=F32))
        o_ref[:, sl] = (o / den).astype(o_ref.dtype)


NA_NDC = 2 * NA_KW


def _na_bias_kernel(r_ref, e_ref, o_ref):
    o_ref[0] = jnp.dot(r_ref[0], e_ref[...], preferred_element_type=F32, precision=lax.Precision.HIGHEST)


def _na_bias_tables(rpb):
    cols = np.arange(GRID_W)
    cs = np.clip(cols - NA_KW // 2, 0, GRID_W - NA_KW)
    col_in = (cols[None, :] >= cs[:, None]) & (cols[None, :] < cs[:, None] + NA_KW)
    dc_idx = np.clip(cols[None, :] - cols[:, None] + NA_KW - 1, 0, 2 * NA_KW - 2)
    mask = np.broadcast_to(col_in[:, None, :], (GRID_W, NA_KH, GRID_W)).reshape(GRID_W, NA_WIN)
    onehot = (np.arange(NA_NDC)[:, None] == dc_idx.reshape(1, -1)).astype(np.float32)
    rp = jnp.pad(rpb, ((0, 0), (0, 0), (0, NA_NDC - rpb.shape[2])))
    rows = jnp.stack([rp[:, NA_KH - 1 - d:2 * NA_KH - 1 - d, :] for d in range(NA_KH)])
    rows = rows.reshape(NA_KH, HEADS * NA_KH, NA_NDC)
    nqk = GRID_W * GRID_W
    b = pl.pallas_call(
        _na_bias_kernel,
        grid=(NA_KH,),
        in_specs=[pl.BlockSpec((1, HEADS * NA_KH, NA_NDC), lambda d: (d, 0, 0)),
                  pl.BlockSpec((NA_NDC, nqk), lambda d: (0, 0))],
        out_specs=pl.BlockSpec((1, HEADS * NA_KH, nqk), lambda d: (d, 0, 0)),
        out_shape=jax.ShapeDtypeStruct((NA_KH, HEADS * NA_KH, nqk), F32),
        compiler_params=_params("parallel"),
        name="na_bias",
    )(rows, jnp.asarray(onehot))
    b = b.reshape(NA_KH, HEADS, NA_KH, GRID_W, GRID_W).transpose(0, 1, 3, 2, 4).reshape(NA_KH, HEADS, GRID_W, NA_WIN)
    return jnp.where(mask[None, None], b, NEG_INF)


def _na_attention(z, kc, vc, bias):
    w = ATT_W
    row0 = z.shape[0] - N_SMP

    def all_rows(n, col, base):
        return pl.BlockSpec((n, w), lambda i, r: (base // n + i, col))

    return pl.pallas_call(
        _na_attn_kernel,
        grid=(DEC_BATCH, GRID_ROWS),
        in_specs=[pl.BlockSpec((GRID_W, w), lambda i, r: (row0 // GRID_W + i * GRID_ROWS + r, 0)),
                  all_rows(DEC_SEQ, 1, row0), all_rows(DEC_SEQ, 2, row0),
                  all_rows(PAST_LEN, 0, 0), all_rows(PAST_LEN, 0, 0),
                  pl.BlockSpec((1, HEADS, GRID_W, NA_WIN), lambda i, r: (r - _na_window_start(r), 0, 0, 0))],
        out_specs=pl.BlockSpec((GRID_W, w), lambda i, r: (i * GRID_ROWS + r, 0)),
        out_shape=jax.ShapeDtypeStruct((N_SMP, w), BF16),
        compiler_params=_params("parallel", "parallel"),
        name="na_attention",
    )(z, z, z, kc, vc, bias)


CONV_CW = 384


CONV_ROWS = 1024


def _conv3_kernel(x_ref, w_ref, o_ref, *, seq):
    x = x_ref[...]
    t = x.shape[0]
    pos = lax.broadcasted_iota(jnp.int32, x.shape, 0) % seq
    prev = jnp.where(pos == 0, 0.0, pltpu.roll(x, 1, 0))
    nxt = jnp.where(pos == seq - 1, 0.0, pltpu.roll(x, t - 1, 0))
    w = w_ref[...]
    o_ref[...] = prev * w[0:1] + x * w[1:2] + nxt * w[2:3]


def _conv3(z, conv_w, seq, row0, n_seq):
    col0 = (3 * D_WIDTH) // CONV_CW
    blk0 = row0 // CONV_ROWS
    n_rows = n_seq * seq
    return pl.pallas_call(
        functools.partial(_conv3_kernel, seq=seq),
        grid=(n_rows // CONV_ROWS, D_IN // CONV_CW),
        in_specs=[pl.BlockSpec((CONV_ROWS, CONV_CW), lambda i, j: (blk0 + i, col0 + j)),
                  pl.BlockSpec((3, CONV_CW), lambda i, j: (0, j))],
        out_specs=pl.BlockSpec((CONV_ROWS, CONV_CW), lambda i, j: (i, j)),
        out_shape=jax.ShapeDtypeStruct((n_rows, D_IN), F32),
        compiler_params=_params("parallel", "parallel"),
        name="conv3",
    )(z, conv_w)


def _d_prep_kernel(x_ref, w0_ref, w2_ref, a0_ref, a2_ref, g2_ref, dec_ref, a_ref, g_ref):
    x = x_ref[...]
    xw = x[:, 0:2 * D_LORA]
    xa = x[:, 2 * D_LORA:4 * D_LORA]
    xg = x[:, 4 * D_LORA:]
    for d in range(2):
        u = w0_ref[d] + _bdot(jnp.tanh(xw[:, d * D_LORA:(d + 1) * D_LORA]), w2_ref[d])
        nu = -u
        softplus = jnp.maximum(nu, 0.0) + jnp.log1p(jnp.exp(-jnp.abs(nu)))
        wlog = -softplus - 0.5
        dec_ref[d] = -jnp.exp(wlog)
        av = a0_ref[d] + _bdot(xa[:, d * D_LORA:(d + 1) * D_LORA], a2_ref[d])
        a_ref[d] = 1.0 / (1.0 + jnp.exp(-av))
    g_ref[...] = _bdot(1.0 / (1.0 + jnp.exp(-xg)), g2_ref[...])


def _d_prep(zc, w0, w2, a0, a2, g2, tm=512):
    n = zc.shape[0]
    cb = (3 * D_WIDTH) // CONV_CW
    return pl.pallas_call(
        _d_prep_kernel,
        grid=(n // tm,),
        in_specs=[pl.BlockSpec((tm, CONV_CW), lambda i: (i, cb)),
                  pl.BlockSpec((2, 1, D_WIDTH), lambda i: (0, 0, 0)),
                  pl.BlockSpec((2, D_LORA, D_WIDTH), lambda i: (0, 0, 0)),
                  pl.BlockSpec((2, 1, D_WIDTH), lambda i: (0, 0, 0)),
                  pl.BlockSpec((2, D_LORA, D_WIDTH), lambda i: (0, 0, 0)),
                  pl.BlockSpec((D_GATE_LORA, D_WIDTH), lambda i: (0, 0))],
        out_specs=[pl.BlockSpec((2, tm, D_WIDTH), lambda i: (0, i, 0)),
                   pl.BlockSpec((2, tm, D_WIDTH), lambda i: (0, i, 0)),
                   pl.BlockSpec((tm, D_WIDTH), lambda i: (i, 0))],
        out_shape=[jax.ShapeDtypeStruct((2, n, D_WIDTH), F32),
                   jax.ShapeDtypeStruct((2, n, D_WIDTH), F32),
                   jax.ShapeDtypeStruct((n, D_WIDTH), F32)],
        compiler_params=_params("parallel"),
        name="d_prep",
    )(zc, w0.reshape(2, 1, D_WIDTH), w2, a0.reshape(2, 1, D_WIDTH), a2, g2)


RW_C = 64
RW_PP = 8


def _bdot(a, b):
    ah, al = _split_hi_lo(a)
    bh, bl = _split_hi_lo(b)
    dot = functools.partial(jnp.dot, preferred_element_type=F32)
    return dot(ah, bh) + dot(ah, bl) + dot(al, bh)


def _bdot2(a, b):
    ah, al = _split_hi_lo(a)
    dot = functools.partial(jnp.dot, preferred_element_type=F32)
    return dot(ah, b) + dot(al, b)


def _bdot_nt(a, b):
    ah, al = _split_hi_lo(a)
    bh, bl = _split_hi_lo(b)
    return _dot_nt(ah, bh) + _dot_nt(ah, bl) + _dot_nt(al, bh)


def _split_hi_lo(x):
    hi = x.astype(BF16)
    return hi, (x - hi.astype(F32)).astype(BF16)


def _head_ones():
    r = lax.broadcasted_iota(jnp.int32, (LANES, LANES), 0) // D_HS
    c = lax.broadcasted_iota(jnp.int32, (LANES, LANES), 1) // D_HS
    return jnp.where(r == c, 1.0, 0.0).astype(BF16)


def _rwkv_chunk_kernel(r_ref, k_ref, v_ref, lw_ref, a_ref, kkp_ref, kap_ref, s0_ref, y_ref, sfin_ref, s_ref):
    c = pl.program_id(3)
    two_c = 2 * RW_C

    @pl.when(c == 0)
    def _():
        s_ref[...] = s0_ref[...]

    sgn = jnp.where(pl.program_id(1) == 1, -1, 1)
    row = lax.broadcasted_iota(jnp.int32, (two_c, two_c), 0)
    col = lax.broadcasted_iota(jnp.int32, (two_c, two_c), 1)
    same = (row // RW_C) == (col // RW_C)
    tt, ss = row % RW_C, col % RW_C
    before = (ss - tt) * sgn < 0
    strict = same & before
    incl = same & (before | (ss == tt))
    eye = jnp.where(row == col, 1.0, 0.0)
    r64 = lax.broadcasted_iota(jnp.int32, (RW_C, RW_C), 0)
    c64 = lax.broadcasted_iota(jnp.int32, (RW_C, RW_C), 1)
    ltri = jnp.where((c64 - r64) * sgn <= 0, 1.0, 0.0).astype(BF16)
    head0 = lax.broadcasted_iota(jnp.int32, (RW_C, LANES), 1) < D_HS
    ones_blk = _head_ones()

    def stack(x):
        return jnp.concatenate([jnp.where(head0, x, 0.0), jnp.where(head0, 0.0, x)], axis=0)

    pairs = range(RW_PP)
    lss = [slice(p * LANES, (p + 1) * LANES) for p in pairs]
    r = [r_ref[:, ls] for ls in lss]
    k = [k_ref[:, ls] for ls in lss]
    lw = [lw_ref[:, ls] for ls in lss]
    a = [a_ref[:, ls] for ls in lss]
    kkf = [k[p] * kkp_ref[:, lss[p]] for p in pairs]
    n2 = [_bdot2(kkf[p] * kkf[p], ones_blk) for p in pairs]
    kk = [kkf[p] / jnp.maximum(jnp.sqrt(n2[p]), 1e-12) for p in pairs]
    lws = [_split_hi_lo(lw[p]) for p in pairs]
    cum = [jnp.dot(ltri, lws[p][0], preferred_element_type=F32) + jnp.dot(ltri, lws[p][1], preferred_element_type=F32)
           for p in pairs]
    mid = [cum[p][RW_C // 2:RW_C // 2 + 1] for p in pairs]
    cumc = [cum[p] - mid[p] for p in pairs]
    pmid = [jnp.exp(mid[p]) for p in pairs]
    pend = [jnp.exp(jnp.sum(lw[p], axis=0, keepdims=True) - mid[p]) for p in pairs]
    pinv = [jnp.exp(-cumc[p]) for p in pairs]
    ks = [stack(kk[p] * jnp.exp(cumc[p] - lw[p])).astype(BF16) for p in pairs]
    rs = [stack(r[p] * jnp.exp(cumc[p])).astype(BF16) for p in pairs]
    khs = [stack(k[p] * (1.0 + (a[p] - 1.0) * kap_ref[:, lss[p]]) * pinv[p]).astype(BF16) for p in pairs]
    bhs = [stack(kk[p] * a[p] * pinv[p]).astype(BF16) for p in pairs]
    vs = [stack(v_ref[:, ls]).astype(BF16) for ls in lss]
    big = [_dot_nt(jnp.concatenate([ks[p], rs[p]], axis=0), jnp.concatenate([bhs[p], khs[p]], axis=0)) for p in pairs]
    a_b = [jnp.where(strict, big[p][:two_c, :two_c], 0.0) for p in pairs]
    a_k = [jnp.where(strict, big[p][:two_c, two_c:], 0.0) for p in pairs]
    l_b = [jnp.where(incl, big[p][two_c:, :two_c], 0.0) for p in pairs]
    l_k = [jnp.where(incl, big[p][two_c:, two_c:], 0.0) for p in pairs]
    npow = [-a_b[p] for p in pairs]
    tinv = [eye + npow[p] for p in pairs]
    for _ in range(5):
        npow = [_bdot(npow[p], npow[p]) for p in pairs]
        tinv = [tinv[p] + _bdot(tinv[p], npow[p]) for p in pairs]
    w1 = [_bdot2(a_k[p], vs[p]) for p in pairs]
    ktil = [_bdot2(tinv[p], ks[p]) for p in pairs]
    uv = [_bdot(tinv[p], w1[p]) for p in pairs]
    lku = [_bdot(l_b[p], jnp.concatenate([ktil[p], uv[p]], axis=1)) for p in pairs]
    rtil = [rs[p].astype(F32) - lku[p][:, :LANES] for p in pairs]
    yv = [_bdot2(l_k[p], vs[p]) - lku[p][:, LANES:] for p in pairs]
    gt = [_bdot2(ktil[p].T, bhs[p]) for p in pairs]
    ht = [jnp.dot(vs[p].astype(F32).T.astype(BF16), khs[p], preferred_element_type=F32) - _bdot2(uv[p].T, bhs[p])
          for p in pairs]
    s = [s_ref[p] * pmid[p] for p in pairs]
    y = [_bdot_nt(rtil[p], s[p]) + yv[p] for p in pairs]
    sg = [_bdot(s[p], gt[p]) for p in pairs]
    for p in pairs:
        s_ref[p] = (s[p] - sg[p] + ht[p]) * pend[p]
        y_ref[:, lss[p]] = y[p][:RW_C] + y[p][RW_C:]

    @pl.when(c == pl.num_programs(3) - 1)
    def _():
        sfin_ref[...] = s_ref[...]


def _pair_states(s):
    zero = jnp.zeros_like(s[:, :, 0::2])
    top = jnp.concatenate([s[:, :, 0::2], zero], axis=-1)
    bot = jnp.concatenate([zero, s[:, :, 1::2]], axis=-1)
    return jnp.concatenate([top, bot], axis=-2)


def _head_states(sp):
    b = sp.shape[0]
    both = jnp.stack([sp[:, :, :, :D_HS, :D_HS], sp[:, :, :, D_HS:, D_HS:]], axis=3)
    return both.reshape(b, 2, D_HEADS, D_HS, D_HS)


def _rwkv_chunked(zc, lw, a, k_k, k_a, s0, b, t):
    nc = t // RW_C
    gw = RW_PP * LANES
    npg = D_WIDTH // gw

    def cmap(d, c):
        return jnp.where(d == 0, c, nc - 1 - c)

    def z_spec(colblk):
        return pl.BlockSpec((RW_C, gw), lambda i, d, g, c: (i * nc + cmap(d, c), colblk * npg + g))

    dir_spec = pl.BlockSpec((None, RW_C, gw), lambda i, d, g, c: (d, i * nc + cmap(d, c), g))
    par_spec = pl.BlockSpec((1, gw), lambda i, d, g, c: (0, g))
    st_spec = pl.BlockSpec((None, None, RW_PP, LANES, LANES), lambda i, d, g, c: (i, d, g, 0, 0))
    return pl.pallas_call(
        _rwkv_chunk_kernel,
        grid=(b, 2, npg, nc),
        in_specs=[z_spec(0), z_spec(1), z_spec(2), dir_spec, dir_spec, par_spec, par_spec, st_spec],
        out_specs=[dir_spec, st_spec],
        out_shape=[jax.ShapeDtypeStruct((2, b * t, D_WIDTH), F32),
                   jax.ShapeDtypeStruct((b, 2, D_WIDTH // LANES, LANES, LANES), F32)],
        scratch_shapes=[pltpu.VMEM((RW_PP, LANES, LANES), F32)],
        compiler_params=_params("parallel", "parallel", "parallel", "arbitrary"),
        name="rwkv_chunk",
    )(zc, zc, zc, lw, a, k_k.reshape(1, D_WIDTH), k_a.reshape(1, D_WIDTH), s0)


def _d_out_kernel(y_ref, r_ref, k_ref, v_ref, a_ref, g_ref, kap_ref, rkp_ref, lng_ref, lnb_ref, o_ref):
    ones_blk = _head_ones()

    def seg_sum(x):
        hi, lo = _split_hi_lo(x)
        return jnp.dot(hi, ones_blk, preferred_element_type=F32) + jnp.dot(lo, ones_blk, preferred_element_type=F32)

    for cb in range(D_WIDTH // LANES):
        ls = slice(cb * LANES, (cb + 1) * LANES)
        y = y_ref[0, :, ls] + y_ref[1, :, ls]
        mu = seg_sum(y) * (1.0 / D_HS)
        yc = y - mu
        var = seg_sum(yc * yc) * (1.0 / D_HS)
        yn = yc * lax.rsqrt(var + D_LN_EPS) * lng_ref[:, ls] + lnb_ref[:, ls]
        asum = a_ref[0, :, ls] + a_ref[1, :, ls]
        bonus = seg_sum(r_ref[:, ls] * k_ref[:, ls] * rkp_ref[:, ls] * (2.0 + (asum - 2.0) * kap_ref[:, ls]))
        o_ref[:, ls] = (yn + bonus * v_ref[:, ls]) * g_ref[:, ls]


def _d_out(y, zc, a, g, k_a, r_k, ln_g, ln_b, tm=512):
    n = g.shape[0]

    def z_spec(colblk):
        return pl.BlockSpec((tm, D_WIDTH), lambda i: (i, colblk))

    dspec = pl.BlockSpec((2, tm, D_WIDTH), lambda i: (0, i, 0))
    pspec = pl.BlockSpec((1, D_WIDTH), lambda i: (0, 0))
    return pl.pallas_call(
        _d_out_kernel,
        grid=(n // tm,),
        in_specs=[dspec, z_spec(0), z_spec(1), z_spec(2), dspec, pl.BlockSpec((tm, D_WIDTH), lambda i: (i, 0)),
                  pspec, pspec, pspec, pspec],
        out_specs=pl.BlockSpec((tm, D_WIDTH), lambda i: (i, 0)),
        out_shape=jax.ShapeDtypeStruct((n, D_WIDTH), F32),
        compiler_params=_params("parallel"),
        name="d_out",
    )(y, zc, zc, zc, a, g, k_a.reshape(1, D_WIDTH), r_k.reshape(1, D_WIDTH), ln_g.reshape(1, D_WIDTH),
      ln_b.reshape(1, D_WIDTH))


TOPK_TM = 512
BIG = float(2 ** 30)


def _extract_topk(s, labels):
    iota16 = lax.broadcasted_iota(jnp.int32, (PEER_TOPK, s.shape[1]), 0)
    rank = jnp.full(s.shape, BIG, F32)
    vals = jnp.zeros((PEER_TOPK, s.shape[1]), F32)
    for r in range(PEER_TOPK):
        m = jnp.max(s, axis=0, keepdims=True)
        idx = jnp.min(jnp.where(s == m, labels, BIG), axis=0, keepdims=True)
        hit = labels == idx
        rank = jnp.where(hit, float(r), rank)
        vals = jnp.where(iota16 == r, m, vals)
        s = jnp.where(hit, -jnp.inf, s)
    return rank, vals


_CAND_PAIRS = ([(0, b) for b in range(16)] + [(1, b) for b in range(8)] + [(2, b) for b in range(8)]
               + [(3, b) for b in range(8)] + [(4, b) for b in range(4)] + [(5, b) for b in range(4)]
               + [(6, b) for b in range(4)] + [(7, b) for b in range(4)] + [(a, 0) for a in range(8, 16)])
N_CAND = len(_CAND_PAIRS)


def _cand_labels():
    lab = np.array([a * PEER_TOPK + b for a, b in _CAND_PAIRS], np.float32)
    return jnp.asarray(np.broadcast_to(lab[:, None], (N_CAND, LANES)).copy())


def _peer_topk_kernel(q_ref, keys_ref, lab_ref, lim_ref, e1_ref, rb_ref, e2_ref):
    iota_k = lax.broadcasted_iota(jnp.int32, (PEER_NKEYS, LANES), 0).astype(F32)
    row8 = lax.broadcasted_iota(jnp.int32, (8, LANES), 0)
    labels = lab_ref[...]
    for c in range(TOPK_TM // LANES):
        cs = slice(c * LANES, (c + 1) * LANES)
        q1 = q_ref[cs, 0:PEER_HALF].astype(BF16)
        q2 = q_ref[cs, PEER_HALF:2 * PEER_HALF].astype(BF16)
        s1 = _dot_nt(keys_ref[0, 0], q1)
        s2 = _dot_nt(keys_ref[0, 1], q2)
        rank1, sv1 = _extract_topk(s1, iota_k)
        rank2, sv2 = _extract_topk(s2, iota_k)
        lo8 = sv2[0:8]
        lo4 = jnp.where(row8 < 4, lo8, pltpu.roll(lo8, 4, 0))
        cand = jnp.concatenate([
            sv1[0:1] + lo8, sv1[0:1] + sv2[8:16], sv1[1:2] + lo8, sv1[2:3] + lo8, sv1[3:4] + lo8,
            jnp.where(row8 < 4, sv1[4:5], sv1[5:6]) + lo4, jnp.where(row8 < 4, sv1[6:7], sv1[7:8]) + lo4,
            sv1[8:16] + sv2[0:1]], axis=0)
        crank, cvals = _extract_topk(cand, labels)
        z = jnp.sum(jnp.exp(cvals - cvals[0:1]), axis=0, keepdims=True)
        sel = jnp.where(crank < BIG, 1.0, 0.0)

        def count(lo, hi):
            return jnp.sum(sel[lo:hi], axis=0, keepdims=True)

        def count_half(lo, first):
            part = jnp.where((row8 < 4) if first else (row8 >= 4), sel[lo:lo + 8], 0.0)
            return jnp.sum(part, axis=0, keepdims=True)

        n_sel = [count(0, 16), count(16, 24), count(24, 32), count(32, 40),
                 count_half(40, True), count_half(40, False), count_half(48, True), count_half(48, False)]
        n_sel += [sel[56 + a:57 + a] for a in range(8)]
        lim = jnp.zeros_like(s1)
        for a in range(PEER_TOPK):
            lim = jnp.where(rank1 == float(a), n_sel[a], lim)
        lim_ref[0, :, cs] = lim
        e1_ref[0, :, cs] = jnp.exp(s1 - sv1[0:1])
        rb_ref[0, :, cs] = rank2
        e2_ref[0, :, cs] = jnp.exp(s2 - sv2[0:1]) / z


def _peer_topk(q, keys):
    n = q.shape[0]
    ospec = pl.BlockSpec((1, PEER_NKEYS, TOPK_TM), lambda i, h: (h, 0, i))
    return pl.pallas_call(
        _peer_topk_kernel,
        grid=(n // TOPK_TM, PEER_HEADS),
        in_specs=[pl.BlockSpec((TOPK_TM, 2 * PEER_HALF), lambda i, h: (i, h)),
                  pl.BlockSpec((1, 2, PEER_NKEYS, PEER_HALF), lambda i, h: (h, 0, 0, 0)),
                  pl.BlockSpec((N_CAND, LANES), lambda i, h: (0, 0))],
        out_specs=[ospec] * 4,
        out_shape=[jax.ShapeDtypeStruct((PEER_HEADS, PEER_NKEYS, n), F32)] * 4,
        compiler_params=_params("parallel", "parallel"),
        name="peer_topk",
    )(q, keys, _cand_labels())


PEER_TM = 512
PEER_TI = 8
PEER_RG = 4
PEER_JT = 32


def _gelu(x):
    return 0.5 * x * (1.0 + lax.erf(x * (2.0 ** -0.5)))


def _peer_dense_kernel(x_ref, u_ref, v_ref, lim_ref, e1_ref, rb_ref, e2_ref, res_ref, gate_ref, o_ref, w_ref):
    e = pl.program_id(1)

    @pl.when(e == 0)
    def _():
        o_ref[...] = jnp.zeros_like(o_ref)

    hid = _dot_nt(u_ref[...], x_ref[...])
    for c in range(PEER_TM // LANES):
        cs = slice(c * LANES, (c + 1) * LANES)
        for jt in range(PEER_NKEYS // PEER_JT):
            js = slice(jt * PEER_JT, (jt + 1) * PEER_JT)
            for rg in range(PEER_TI // PEER_RG):
                rows = range(rg * PEER_RG, (rg + 1) * PEER_RG)
                g = [jnp.zeros((PEER_JT, LANES), F32) for _ in rows]
                for h in range(PEER_HEADS):
                    rb = rb_ref[h, js, cs]
                    e2 = e2_ref[h, js, cs]
                    for k, ii in enumerate(rows):
                        g[k] = g[k] + jnp.where(rb < lim_ref[h, ii:ii + 1, cs], e2, 0.0) * e1_ref[h, ii:ii + 1, cs]
                for k, ii in enumerate(rows):
                    es = slice(ii * PEER_NKEYS + jt * PEER_JT, ii * PEER_NKEYS + (jt + 1) * PEER_JT)
                    w_ref[es, cs] = (g[k] * _gelu(hid[es, cs])).astype(BF16)
    o_ref[...] += _dot_tn(w_ref[...], v_ref[...])

    @pl.when(e == pl.num_programs(1) - 1)
    def _():
        o_ref[...] = res_ref[...] + gate_ref[0] * o_ref[...]


def _peer_dense(xm, u, v, layer, lim, e1, rb, e2, res, gate):
    n, d = xm.shape
    n_exp = u.shape[1]
    te = PEER_TI * PEER_NKEYS
    sel_spec = pl.BlockSpec((PEER_HEADS, PEER_NKEYS, PEER_TM), lambda i, e: (0, 0, i))
    row_spec = pl.BlockSpec((PEER_HEADS, PEER_TI, PEER_TM), lambda i, e: (0, e, i))
    return pl.pallas_call(
        _peer_dense_kernel,
        grid=(n // PEER_TM, n_exp // te),
        in_specs=[pl.BlockSpec((PEER_TM, d), lambda i, e: (i, 0)),
                  pl.BlockSpec((None, te, d), lambda i, e: (layer, e, 0)),
                  pl.BlockSpec((None, te, d), lambda i, e: (layer, e, 0)),
                  row_spec, row_spec, sel_spec, sel_spec,
                  pl.BlockSpec((PEER_TM, d), lambda i, e: (i, 0)),
                  pl.BlockSpec((1, 1, d), lambda i, e: (_cond_of_tile(i, PEER_TM), 0, 0))],
        out_specs=pl.BlockSpec((PEER_TM, d), lambda i, e: (i, 0)),
        out_shape=jax.ShapeDtypeStruct((n, d), F32),
        scratch_shapes=[pltpu.VMEM((te, PEER_TM), BF16)],
        compiler_params=_params("parallel", "arbitrary"),
        name="peer_dense",
    )(xm, u, v, lim, e1, rb, e2, res, gate)


def _peer_layer(x, layer, norm_g, shift, scale, gate, wq, keys, u, v):
    xm = _modulate(x, norm_g, shift, scale)
    q = _matmul(xm, wq, layer, out_dtype=BF16)
    lim, e1, rb, e2 = _peer_topk(q, keys)
    return _peer_dense(xm, u, v, layer, lim, e1, rb, e2, x, gate)


EVEN_PAD = 4096
ODD_PAD = 6656
S_ALL = PAST_LEN + DEC_SEQ


def _even_mixer(h, w_in, layer, lam, subln_g, q_norm_g, w_uq, kv_norm_g, w_ukv, lam_init,
                cache_k, cache_v, cache_ckv, cache_kpe, rope_cos, rope_sin):
    z = _matmul(h, w_in, layer)
    ckv_n = _rmsnorm(z, kv_norm_g, col=3584 // B_KV_RANK)
    cq_n = _rmsnorm(z, q_norm_g, out_dtype=BF16, col=3072 // B_Q_RANK)
    wq3 = w_uq.reshape(B_Q_RANK, HEADS, B_NOPE + B_ROPE)
    w_uq_r = jnp.concatenate([wq3[:, :, :B_NOPE].reshape(B_Q_RANK, -1), wq3[:, :, B_NOPE:].reshape(B_Q_RANK, -1)], axis=1)
    qb = _matmul(cq_n, w_uq_r.astype(BF16), tn=512)
    new = (z[:N_CTX, 1024:2048].reshape(BATCH, SEQ, HEADS, 2 * A_QK),
           z[:N_CTX, 2048:3072].reshape(BATCH, SEQ, HEADS, HEAD_DIM),
           ckv_n[:N_CTX].reshape(BATCH, SEQ, B_KV_RANK), z[:N_CTX, 3840:3904].reshape(BATCH, SEQ, B_ROPE))

    qa_s = _rope(z, rope_cos, rope_sin, ATT_W, col=0)
    ka_s = _rope(z, rope_cos, rope_sin, ATT_W, col=1)
    qr_s = _rope(qb, rope_cos, rope_sin, HEADS * B_ROPE, col=2)
    kpe_s = _rope(z, rope_cos, rope_sin, LANES, col=3840 // LANES)[:, :B_ROPE]

    def with_ctx(cache, own, width):
        both = jnp.concatenate([cache.reshape(DEC_BATCH, PAST_LEN, width).astype(own.dtype),
                                own.reshape(DEC_BATCH, DEC_SEQ, width)], axis=1)
        return both.reshape(DEC_BATCH * S_ALL, width)

    ka_all = with_ctx(cache_k, ka_s, ATT_W)
    va_all = with_ctx(cache_v, z[N_CTX:, 2048:3072].astype(BF16), ATT_W)
    ckv_all = with_ctx(cache_ckv, ckv_n[N_CTX:], B_KV_RANK)
    kpe_all = with_ctx(cache_kpe, kpe_s, B_ROPE)

    wkv3 = w_ukv.reshape(B_KV_RANK, HEADS, B_NOPE + HEAD_DIM)
    w_ukv_r = jnp.concatenate([wkv3[:, :, :B_NOPE].reshape(B_KV_RANK, -1), wkv3[:, :, B_NOPE:].reshape(B_KV_RANK, -1)], axis=1)
    n_skv = DEC_BATCH * S_ALL
    kv = _matmul(jnp.concatenate([ckv_all, ckv_n[:N_CTX]], axis=0), w_ukv_r.astype(BF16), out_dtype=BF16)

    oa_c = _diff_attention(BATCH, SEQ, (z, _q_rows(SEQ, ATT_W, 0, 0)), (z, _kv_rows(SEQ, ATT_W, 0, 1)),
                           (z, _kv_rows(SEQ, ATT_W, 0, 2)), lam, subln_g, lam_init)
    oa_s = _diff_attention(DEC_BATCH, DEC_SEQ, (qa_s, _q_rows(DEC_SEQ, ATT_W)), (ka_all, _kv_rows(S_ALL, ATT_W)),
                           (va_all, _kv_rows(S_ALL, ATT_W)), lam, subln_g, lam_init)
    ob_c = _mla_attention(BATCH, SEQ, (qb, _q_rows(SEQ, ATT_W, 0, 0)), (qb, _q_rows(SEQ, HEADS * B_ROPE, 0, 2)),
                          (kv, _kv_rows(SEQ, ATT_W, n_skv, 0)), (z, _kv_rows(SEQ, LANES, 0, 3840 // LANES)),
                          (kv, _kv_rows(SEQ, ATT_W, n_skv, 1)))
    ob_s = _mla_attention(DEC_BATCH, DEC_SEQ, (qb, _q_rows(DEC_SEQ, ATT_W, N_CTX, 0)),
                          (qr_s, _q_rows(DEC_SEQ, HEADS * B_ROPE)), (kv, _kv_rows(S_ALL, ATT_W, 0, 0)),
                          (kpe_all, _kv_rows(S_ALL, B_ROPE)), (kv, _kv_rows(S_ALL, ATT_W, 0, 1)))
    return (oa_c, oa_s, ob_c, ob_s), new


def _odd_mixer(h, w_in, layer, rpb, conv_w, w0, w2, a0, a2, g2, k_k, k_a, r_k, ln_g, ln_b,
               cache_k, cache_v, state):
    z = _matmul(h, w_in, layer)
    new_k = z[:N_CTX, 1024:2048].reshape(BATCH, SEQ, HEADS, HEAD_DIM)
    new_v = z[:N_CTX, 2048:3072].reshape(BATCH, SEQ, HEADS, HEAD_DIM)
    oc_c = _plain_attention(BATCH, SEQ, (z, _q_rows(SEQ, ATT_W, 0, 0)), (z, _kv_rows(SEQ, ATT_W, 0, 1)),
                            (z, _kv_rows(SEQ, ATT_W, 0, 2)))
    oc_s = _na_attention(z, cache_k.reshape(DEC_BATCH * PAST_LEN, ATT_W), cache_v.reshape(DEC_BATCH * PAST_LEN, ATT_W),
                         _na_bias_tables(rpb))

    zc_c = _conv3(z, conv_w, SEQ, 0, BATCH)
    zc_s = _conv3(z, conv_w, DEC_SEQ, N_CTX, DEC_BATCH)
    lw_c, a_c, g_c = _d_prep(zc_c, w0, w2, a0, a2, g2)
    lw_s, a_s, g_s = _d_prep(zc_s, w0, w2, a0, a2, g2)
    rk = r_k.reshape(-1)
    y_c, sfin = _rwkv_chunked(zc_c, lw_c, a_c, k_k, k_a,
                              jnp.zeros((BATCH, 2, D_WIDTH // LANES, LANES, LANES), F32), BATCH, SEQ)
    sfin = _head_states(sfin)
    y_s, _ = _rwkv_chunked(zc_s, lw_s, a_s, k_k, k_a, _pair_states(state), DEC_BATCH, DEC_SEQ)
    od_c = _d_out(y_c, zc_c, a_c, g_c, k_a, rk, ln_g, ln_b)
    od_s = _d_out(y_s, zc_s, a_s, g_s, k_a, rk, ln_g, ln_b)
    return (oc_c, oc_s, od_c, od_s), (new_k, new_v, sfin)


def kernel(x_prompt, x_sample, cache_a_k, cache_a_v, cache_b_ckv, cache_b_kpe, cache_c_k, cache_c_v, state_d, c, c_ctx, ada_w, ada_b, norm1_g, norm2_g, w_out, peer_wq, peer_keys, peer_u, peer_v, final_g, ab_w_in, a_lam, a_subln_g, b_q_norm_g, b_w_uq, b_kv_norm_g, b_w_ukv, cd_w_in, c_rpb, d_conv, d_w0, d_w2, d_a0, d_a2, d_g2, d_k_k, d_k_a, d_r_k, d_ln_g, d_ln_b):
    x = jnp.concatenate([x_prompt.reshape(N_CTX, D_MODEL), x_sample.reshape(N_SMP, D_MODEL)], axis=0)
    cond8 = jnp.pad(jnp.concatenate([c_ctx[None, :], c], axis=0), ((0, 8 - N_COND), (0, 0)))
    ada = _ada_all(cond8, ada_w, ada_b)
    rope_cos, rope_sin = _rope_tables()
    w_even = jnp.pad(ab_w_in, ((0, 0), (0, 0), (0, EVEN_PAD - EVEN_IN))).astype(BF16)
    w_odd = jnp.pad(cd_w_in, ((0, 0), (0, 0), (0, ODD_PAD - ODD_IN))).astype(BF16)
    w_out_b, wq_b, keys_b = w_out.astype(BF16), peer_wq.astype(BF16), peer_keys.astype(BF16)
    u_b, v_b = peer_u.astype(BF16), peer_v.astype(BF16)
    new_ak, new_av, new_bc, new_bp, new_ck, new_cv, new_sd = [], [], [], [], [], [], []
    for l in range(DEPTH):
        i = l // 2
        mods = ada[l, :N_COND].reshape(N_COND, 6, 1, D_MODEL)
        sh1, sc1, g1, sh2, sc2, g2 = (mods[:, m] for m in range(6))
        h = _modulate(x, norm1_g[l], sh1, sc1)
        if l % 2 == 0:
            lam_init = 0.8 - 0.6 * math.exp(-0.3 * l)
            o, (ak, av, bc, bp) = _even_mixer(
                h, w_even, i, a_lam[i], a_subln_g[i], b_q_norm_g[i], b_w_uq[i], b_kv_norm_g[i], b_w_ukv[i],
                lam_init, cache_a_k[:, i], cache_a_v[:, i], cache_b_ckv[:, i], cache_b_kpe[:, i], rope_cos, rope_sin)
            new_ak.append(ak)
            new_av.append(av)
            new_bc.append(bc)
            new_bp.append(bp)
        else:
            o, (ck, cv, sd) = _odd_mixer(
                h, w_odd, i, c_rpb[i], d_conv[i], d_w0[i], d_w2[i], d_a0[i], d_a2[i], d_g2[i],
                d_k_k[i], d_k_a[i], d_r_k[i], d_ln_g[i], d_ln_b[i], cache_c_k[:, i], cache_c_v[:, i], state_d[:, i])
            new_ck.append(ck)
            new_cv.append(cv)
            new_sd.append(sd)
        x = _matmul_residual(*o, w_out_b, l, x, g1)
        x = _peer_layer(x, l, norm2_g[l], sh2, sc2, g2, wq_b, keys_b[l], u_b, v_b)
    y = _rmsnorm(x, final_g)
    y_prompt = y[:N_CTX].reshape(BATCH, SEQ, D_MODEL)
    y_sample = y[N_CTX:].reshape(DEC_BATCH, DEC_SEQ, D_MODEL)
    return (y_prompt, y_sample, jnp.stack(new_ak, axis=1), jnp.stack(new_av, axis=1), jnp.stack(new_bc, axis=1),
            jnp.stack(new_bp, axis=1), jnp.stack(new_ck, axis=1), jnp.stack(new_cv, axis=1), jnp.stack(new_sd, axis=1))
```

```python
import functools
import math

import numpy as np
import jax
import jax.numpy as jnp
from jax import lax
from jax.experimental import pallas as pl
from jax.experimental.pallas import tpu as pltpu

F32 = jnp.float32
BF16 = jnp.bfloat16

D_MODEL = 2048
BATCH = 16
SEQ = 256
DEPTH = 4
DEC_BATCH = 2
DEC_SEQ = 1024
PAST_LEN = 512
GRID_W = 64
GRID_ROWS = DEC_SEQ // GRID_W
ROPE_BASE = 10000.0
RMS_EPS = 1e-6
NEG_INF = -1e30
N_CTX = BATCH * SEQ
N_SMP = DEC_BATCH * DEC_SEQ
N_TOK = N_CTX + N_SMP
N_COND = 1 + DEC_BATCH

HEADS = 8
HEAD_DIM = 128
A_QK = 64
B_Q_RANK = 512
B_KV_RANK = 256
B_NOPE = 128
B_ROPE = 64
NA_KH = 8
NA_KW = 16
NA_WIN = NA_KH * GRID_W
D_HS = 64
D_WIDTH = 1024
D_HEADS = 16
D_LORA = 64
D_GATE_LORA = 128
D_IN = 3 * D_WIDTH + 2 * D_LORA + 2 * D_LORA + D_GATE_LORA
D_LN_EPS = 64e-5
EVEN_IN = 3904
ODD_IN = 6528
PEER_HEADS = 8
PEER_NKEYS = 128
PEER_HALF = 128
PEER_TOPK = 16

LANES = 128
VMEM_LIMIT = 56 * 1024 * 1024


def _params(*sem):
    return pltpu.CompilerParams(dimension_semantics=sem, vmem_limit_bytes=VMEM_LIMIT)


def _dot_nt(a, b):
    return lax.dot_general(a, b, (((1,), (1,)), ((), ())), preferred_element_type=F32)


def _dot_tn(a, b):
    return lax.dot_general(a, b, (((0,), (0,)), ((), ())), preferred_element_type=F32)


def _cond_of_tile(i, tm):
    n_ctx_tiles = N_CTX // tm
    per_batch = DEC_SEQ // tm
    return jnp.where(i < n_ctx_tiles, 0, 1 + (i - n_ctx_tiles) // per_batch)


def _ada_kernel(c_ref, w_ref, b_ref, o_ref):
    c = c_ref[...]
    sc = c * (1.0 / (1.0 + jnp.exp(-c)))
    o_ref[0] = jnp.dot(sc.astype(BF16), w_ref[0].astype(BF16), preferred_element_type=F32) + b_ref[0]


def _ada_all(cond8, ada_w, ada_b):
    tn = 1024
    n_out = ada_w.shape[-1]
    return pl.pallas_call(
        _ada_kernel,
        grid=(DEPTH, n_out // tn),
        in_specs=[pl.BlockSpec((8, D_MODEL), lambda l, j: (0, 0)),
                  pl.BlockSpec((1, D_MODEL, tn), lambda l, j: (l, 0, j)),
                  pl.BlockSpec((1, 1, tn), lambda l, j: (l, 0, j))],
        out_specs=pl.BlockSpec((1, 8, tn), lambda l, j: (l, 0, j)),
        out_shape=jax.ShapeDtypeStruct((DEPTH, 8, n_out), F32),
        compiler_params=_params("parallel", "parallel"),
        name="ada",
    )(cond8, ada_w, ada_b.reshape(DEPTH, 1, n_out))


def _modulate_kernel(x_ref, g_ref, sh_ref, sc_ref, o_ref):
    x = x_ref[...]
    y = x * lax.rsqrt(jnp.mean(x * x, axis=-1, keepdims=True) + RMS_EPS) * g_ref[...]
    o_ref[...] = (y * (1.0 + sc_ref[0]) + sh_ref[0]).astype(o_ref.dtype)


def _modulate(x, g, shift, scale, out_dtype=BF16, tm=512):
    n, d = x.shape
    return pl.pallas_call(
        _modulate_kernel,
        grid=(n // tm,),
        in_specs=[pl.BlockSpec((tm, d), lambda i: (i, 0)),
                  pl.BlockSpec((1, d), lambda i: (0, 0)),
                  pl.BlockSpec((1, 1, d), lambda i: (_cond_of_tile(i, tm), 0, 0)),
                  pl.BlockSpec((1, 1, d), lambda i: (_cond_of_tile(i, tm), 0, 0))],
        out_specs=pl.BlockSpec((tm, d), lambda i: (i, 0)),
        out_shape=jax.ShapeDtypeStruct((n, d), out_dtype),
        compiler_params=_params("parallel"),
        name="modulate",
    )(x, g.reshape(1, d), shift, scale)


def _rmsnorm_kernel(x_ref, g_ref, o_ref):
    x = x_ref[...]
    y = x * lax.rsqrt(jnp.mean(x * x, axis=-1, keepdims=True) + RMS_EPS) * g_ref[...]
    o_ref[...] = y.astype(o_ref.dtype)


def _rmsnorm(x, g, out_dtype=F32, tm=512, col=0):
    n = x.shape[0]
    d = g.shape[-1]
    return pl.pallas_call(
        _rmsnorm_kernel,
        grid=(n // tm,),
        in_specs=[pl.BlockSpec((tm, d), lambda i: (i, col)),
                  pl.BlockSpec((1, d), lambda i: (0, 0))],
        out_specs=pl.BlockSpec((tm, d), lambda i: (i, 0)),
        out_shape=jax.ShapeDtypeStruct((n, d), out_dtype),
        compiler_params=_params("parallel"),
        name="rmsnorm",
    )(x, g.reshape(1, d))


def _mm_kernel(x_ref, w_ref, o_ref):
    o_ref[...] = jnp.dot(x_ref[...].astype(BF16), w_ref[...], preferred_element_type=F32).astype(o_ref.dtype)


def _weight_spec(w, layer, tn):
    if w.ndim == 2:
        return pl.BlockSpec((w.shape[0], tn), lambda i, j: (0, j))
    return pl.BlockSpec((None, w.shape[1], tn), lambda i, j: (layer, 0, j))


MM_TM = 1024
MM_TN = 1024


def _matmul(x, w, layer=None, out_dtype=F32, tm=MM_TM, tn=None, x_col=0, x_width=None):
    m = x.shape[0]
    k = x_width or x.shape[1]
    n = w.shape[-1]
    if tn is None:
        tn = MM_TN if n % MM_TN == 0 else MM_TN // 2
    tn = min(tn, n)
    assert m % tm == 0 and n % tn == 0, (m, n, tm, tn)
    return pl.pallas_call(
        _mm_kernel,
        grid=(m // tm, n // tn),
        in_specs=[pl.BlockSpec((tm, k), lambda i, j: (i, x_col)),
                  _weight_spec(w, layer, tn)],
        out_specs=pl.BlockSpec((tm, tn), lambda i, j: (i, j)),
        out_shape=jax.ShapeDtypeStruct((m, n), out_dtype),
        compiler_params=_params("parallel", "parallel"),
        name="matmul",
    )(x, w)


def _mm_res_kernel(lc_ref, ls_ref, rc_ref, rs_ref, w_ref, res_ref, gate_ref, o_ref, *, n_ctx_tiles):
    half = w_ref.shape[0] // 2

    def emit(l_ref, r_ref):
        acc = jnp.dot(l_ref[...].astype(BF16), w_ref[0:half, :], preferred_element_type=F32)
        acc = acc + jnp.dot(r_ref[...].astype(BF16), w_ref[half:, :], preferred_element_type=F32)
        o_ref[...] = res_ref[...] + gate_ref[0] * acc

    @pl.when(pl.program_id(0) < n_ctx_tiles)
    def _():
        emit(lc_ref, rc_ref)

    @pl.when(pl.program_id(0) >= n_ctx_tiles)
    def _():
        emit(ls_ref, rs_ref)


def _matmul_residual(left_c, left_s, right_c, right_s, w, layer, res, gate, tm=MM_TM, tn=MM_TN // 2):
    m = res.shape[0]
    n = w.shape[-1]
    kh = left_c.shape[1]
    nct = left_c.shape[0] // tm
    ctx_spec = pl.BlockSpec((tm, kh), lambda i, j: (jnp.minimum(i, nct - 1), 0))
    smp_spec = pl.BlockSpec((tm, kh), lambda i, j: (jnp.maximum(i - nct, 0), 0))
    return pl.pallas_call(
        functools.partial(_mm_res_kernel, n_ctx_tiles=nct),
        grid=(m // tm, n // tn),
        in_specs=[ctx_spec, smp_spec, ctx_spec, smp_spec,
                  _weight_spec(w, layer, tn),
                  pl.BlockSpec((tm, tn), lambda i, j: (i, j)),
                  pl.BlockSpec((1, 1, tn), lambda i, j: (_cond_of_tile(i, tm), 0, j))],
        out_specs=pl.BlockSpec((tm, tn), lambda i, j: (i, j)),
        out_shape=jax.ShapeDtypeStruct((m, n), F32),
        compiler_params=_params("parallel", "parallel"),
        name="matmul_residual",
    )(left_c, left_s, right_c, right_s, w, res, gate)


def _rope_tables():
    nf = 16
    inv = ROPE_BASE ** (-np.arange(nf, dtype=np.float64) / nf)
    t = np.arange(DEC_SEQ)
    rows, cols = t // GRID_W, t % GRID_W
    lane = np.arange(64)
    pos = np.where(lane[None, :] < 32, rows[:, None], cols[:, None]).astype(np.float32)
    ang = (pos * inv[lane % nf][None, :].astype(np.float32)).astype(np.float32)
    first = (lane % 32) < nf
    cos = np.cos(ang.astype(np.float64))
    sin = np.sin(ang.astype(np.float64)) * np.where(first, -1.0, 1.0)[None, :]
    cos = np.tile(cos, (1, 2)).astype(np.float32)
    sin = np.tile(sin, (1, 2)).astype(np.float32)
    return jnp.asarray(cos), jnp.asarray(sin)


def _rope_kernel(x_ref, cos_ref, sin_ref, o_ref):
    cos = cos_ref[...]
    sin = sin_ref[...]
    lane = lax.broadcasted_iota(jnp.int32, cos.shape, 1)
    first = (lane % 32) < 16
    for c in range(x_ref.shape[1] // LANES):
        x = x_ref[:, c * LANES:(c + 1) * LANES].astype(F32)
        partner = jnp.where(first, pltpu.roll(x, LANES - 16, 1), pltpu.roll(x, 16, 1))
        o_ref[:, c * LANES:(c + 1) * LANES] = (x * cos + partner * sin).astype(o_ref.dtype)


def _rope(x, cos, sin, w, col=0, out_dtype=BF16, tm=256):
    n = N_SMP
    row0 = (x.shape[0] - N_SMP) // tm
    per = DEC_SEQ // tm
    return pl.pallas_call(
        _rope_kernel,
        grid=(n // tm,),
        in_specs=[pl.BlockSpec((tm, w), lambda i: (row0 + i, col)),
                  pl.BlockSpec((tm, LANES), lambda i: (i % per, 0)),
                  pl.BlockSpec((tm, LANES), lambda i: (i % per, 0))],
        out_specs=pl.BlockSpec((tm, w), lambda i: (i, 0)),
        out_shape=jax.ShapeDtypeStruct((n, w), out_dtype),
        compiler_params=_params("parallel"),
        name="rope",
    )(x, cos, sin)


def _softmax(s):
    p = jnp.exp(s - jnp.max(s, axis=-1, keepdims=True))
    return p / jnp.sum(p, axis=-1, keepdims=True)


def _diff_attn_kernel(lam_ref, g_ref, q_ref, k_ref, v_ref, o_ref, *, lam_init):
    lam = lam_ref[...]
    l1 = jnp.sum(jnp.sum(lam[0:1] * lam[1:2], axis=-1, keepdims=True), axis=0, keepdims=True)
    l2 = jnp.sum(jnp.sum(lam[2:3] * lam[3:4], axis=-1, keepdims=True), axis=0, keepdims=True)
    lam_val = jnp.exp(l1) - jnp.exp(l2) + lam_init
    scale = A_QK ** -0.5
    for h in range(HEADS):
        sl = slice(h * HEAD_DIM, (h + 1) * HEAD_DIM)
        q = q_ref[:, sl].astype(BF16)
        k = k_ref[:, sl].astype(BF16)
        p1 = _softmax(_dot_nt(q[:, :A_QK], k[:, :A_QK]) * scale)
        p2 = _softmax(_dot_nt(q[:, A_QK:], k[:, A_QK:]) * scale)
        a = p1 - lam_val * p2
        o = jnp.dot(a.astype(BF16), v_ref[:, sl].astype(BF16), preferred_element_type=F32)
        o = o * lax.rsqrt(jnp.mean(o * o, axis=-1, keepdims=True) + RMS_EPS) * g_ref[...]
        o_ref[:, sl] = (o * (1.0 - lam_init)).astype(o_ref.dtype)


ATT_TQ = 256
ATT_W = HEADS * HEAD_DIM


def _q_rows(t, width, row0=0, col=0):
    per, off = t // ATT_TQ, row0 // ATT_TQ
    return pl.BlockSpec((ATT_TQ, width), lambda b, j: (off + b * per + j, col))


def _kv_rows(s, width, row0=0, col=0):
    off = row0 // s
    return pl.BlockSpec((s, width), lambda b, j: (off + b, col))


def _attention_call(kernel, name, nb, t, operands, specs):
    return pl.pallas_call(
        kernel,
        grid=(nb, t // ATT_TQ),
        in_specs=specs,
        out_specs=_q_rows(t, ATT_W),
        out_shape=jax.ShapeDtypeStruct((nb * t, ATT_W), BF16),
        compiler_params=_params("parallel", "parallel"),
        name=name,
    )(*operands)


def _diff_attention(nb, t, q, k, v, lam, subln_g, lam_init):
    const = [pl.BlockSpec((4, A_QK), lambda i, j: (0, 0)), pl.BlockSpec((1, HEAD_DIM), lambda i, j: (0, 0))]
    return _attention_call(functools.partial(_diff_attn_kernel, lam_init=lam_init), "diff_attention", nb, t,
                           [lam, subln_g.reshape(1, HEAD_DIM), q[0], k[0], v[0]], const + [q[1], k[1], v[1]])


def _mla_attn_kernel(qn_ref, qr_ref, kn_ref, kr_ref, v_ref, o_ref):
    scale = (B_NOPE + B_ROPE) ** -0.5
    kr = kr_ref[:, 0:B_ROPE].astype(BF16)
    for h in range(HEADS):
        sl = slice(h * HEAD_DIM, (h + 1) * HEAD_DIM)
        s = _dot_nt(qn_ref[:, sl].astype(BF16), kn_ref[:, sl])
        s = s + _dot_nt(qr_ref[:, h * B_ROPE:(h + 1) * B_ROPE].astype(BF16), kr)
        p = _softmax(s * scale)
        o = jnp.dot(p.astype(BF16), v_ref[:, sl], preferred_element_type=F32)
        o_ref[:, sl] = o.astype(o_ref.dtype)


def _mla_attention(nb, t, qn, qr, kn, kr, v):
    ops = [qn, qr, kn, kr, v]
    return _attention_call(_mla_attn_kernel, "mla_attention", nb, t, [o[0] for o in ops], [o[1] for o in ops])


def _plain_attn_kernel(q_ref, k_ref, v_ref, o_ref):
    scale = HEAD_DIM ** -0.5
    for h in range(HEADS):
        sl = slice(h * HEAD_DIM, (h + 1) * HEAD_DIM)
        p = _softmax(_dot_nt(q_ref[:, sl].astype(BF16), k_ref[:, sl].astype(BF16)) * scale)
        o = jnp.dot(p.astype(BF16), v_ref[:, sl].astype(BF16), preferred_element_type=F32)
        o_ref[:, sl] = o.astype(o_ref.dtype)


def _plain_attention(nb, t, q, k, v):
    ops = [q, k, v]
    return _attention_call(_plain_attn_kernel, "plain_attention", nb, t, [o[0] for o in ops], [o[1] for o in ops])


def _na_window_start(r):
    return jnp.clip(r - NA_KH // 2, 0, GRID_ROWS - NA_KH)


def _na_attn_kernel(q_ref, k_ref, v_ref, kc_ref, vc_ref, bias_ref, o_ref):
    scale = HEAD_DIM ** -0.5
    r = pl.program_id(1)
    start = pl.multiple_of(_na_window_start(r) * GRID_W, GRID_W)
    for h in range(HEADS):
        sl = slice(h * HEAD_DIM, (h + 1) * HEAD_DIM)
        q = q_ref[:, sl].astype(BF16)
        kw = k_ref[pl.ds(start, NA_WIN), sl].astype(BF16)
        vw = v_ref[pl.ds(start, NA_WIN), sl].astype(BF16)
        s_win = _dot_nt(q, kw) * scale + bias_ref[0, h]
        s_ctx = _dot_nt(q, kc_ref[:, sl].astype(BF16)) * scale
        m = jnp.maximum(jnp.max(s_win, axis=-1, keepdims=True), jnp.max(s_ctx, axis=-1, keepdims=True))
        p_win = jnp.exp(s_win - m)
        p_ctx = jnp.exp(s_ctx - m)
        den = jnp.sum(p_win, axis=-1, keepdims=True) + jnp.sum(p_ctx, axis=-1, keepdims=True)
        o = (jnp.dot(p_win.astype(BF16), vw, preferred_element_type=F32)
             + jnp.dot(p_ctx.astype(BF16), vc_ref[:, sl].astype(BF16), preferred_element_type=F32))
        o_ref[:, sl] = (o / den).astype(o_ref.dtype)


NA_NDC = 2 * NA_KW


def _na_bias_kernel(r_ref, e_ref, o_ref):
    o_ref[0] = jnp.dot(r_ref[0], e_ref[...], preferred_element_type=F32, precision=lax.Precision.HIGHEST)


def _na_bias_tables(rpb):
    cols = np.arange(GRID_W)
    cs = np.clip(cols - NA_KW // 2, 0, GRID_W - NA_KW)
    col_in = (cols[None, :] >= cs[:, None]) & (cols[None, :] < cs[:, None] + NA_KW)
    dc_idx = np.clip(cols[None, :] - cols[:, None] + NA_KW - 1, 0, 2 * NA_KW - 2)
    mask = np.broadcast_to(col_in[:, None, :], (GRID_W, NA_KH, GRID_W)).reshape(GRID_W, NA_WIN)
    onehot = (np.arange(NA_NDC)[:, None] == dc_idx.reshape(1, -1)).astype(np.float32)
    rp = jnp.pad(rpb, ((0, 0), (0, 0), (0, NA_NDC - rpb.shape[2])))
    rows = jnp.stack([rp[:, NA_KH - 1 - d:2 * NA_KH - 1 - d, :] for d in range(NA_KH)])
    rows = rows.reshape(NA_KH, HEADS * NA_KH, NA_NDC)
    nqk = GRID_W * GRID_W
    b = pl.pallas_call(
        _na_bias_kernel,
        grid=(NA_KH,),
        in_specs=[pl.BlockSpec((1, HEADS * NA_KH, NA_NDC), lambda d: (d, 0, 0)),
                  pl.BlockSpec((NA_NDC, nqk), lambda d: (0, 0))],
        out_specs=pl.BlockSpec((1, HEADS * NA_KH, nqk), lambda d: (d, 0, 0)),
        out_shape=jax.ShapeDtypeStruct((NA_KH, HEADS * NA_KH, nqk), F32),
        compiler_params=_params("parallel"),
        name="na_bias",
    )(rows, jnp.asarray(onehot))
    b = b.reshape(NA_KH, HEADS, NA_KH, GRID_W, GRID_W).transpose(0, 1, 3, 2, 4).reshape(NA_KH, HEADS, GRID_W, NA_WIN)
    return jnp.where(mask[None, None], b, NEG_INF)


def _na_attention(z, kc, vc, bias):
    w = ATT_W
    row0 = z.shape[0] - N_SMP

    def all_rows(n, col, base):
        return pl.BlockSpec((n, w), lambda i, r: (base // n + i, col))

    return pl.pallas_call(
        _na_attn_kernel,
        grid=(DEC_BATCH, GRID_ROWS),
        in_specs=[pl.BlockSpec((GRID_W, w), lambda i, r: (row0 // GRID_W + i * GRID_ROWS + r, 0)),
                  all_rows(DEC_SEQ, 1, row0), all_rows(DEC_SEQ, 2, row0),
                  all_rows(PAST_LEN, 0, 0), all_rows(PAST_LEN, 0, 0),
                  pl.BlockSpec((1, HEADS, GRID_W, NA_WIN), lambda i, r: (r - _na_window_start(r), 0, 0, 0))],
        out_specs=pl.BlockSpec((GRID_W, w), lambda i, r: (i * GRID_ROWS + r, 0)),
        out_shape=jax.ShapeDtypeStruct((N_SMP, w), BF16),
        compiler_params=_params("parallel", "parallel"),
        name="na_attention",
    )(z, z, z, kc, vc, bias)


CONV_CW = 384


CONV_ROWS = 1024


def _conv3_kernel(x_ref, w_ref, o_ref, *, seq):
    x = x_ref[...]
    t = x.shape[0]
    pos = lax.broadcasted_iota(jnp.int32, x.shape, 0) % seq
    prev = jnp.where(pos == 0, 0.0, pltpu.roll(x, 1, 0))
    nxt = jnp.where(pos == seq - 1, 0.0, pltpu.roll(x, t - 1, 0))
    w = w_ref[...]
    o_ref[...] = prev * w[0:1] + x * w[1:2] + nxt * w[2:3]


def _conv3(z, conv_w, seq, row0, n_seq):
    col0 = (3 * D_WIDTH) // CONV_CW
    blk0 = row0 // CONV_ROWS
    n_rows = n_seq * seq
    return pl.pallas_call(
        functools.partial(_conv3_kernel, seq=seq),
        grid=(n_rows // CONV_ROWS, D_IN // CONV_CW),
        in_specs=[pl.BlockSpec((CONV_ROWS, CONV_CW), lambda i, j: (blk0 + i, col0 + j)),
                  pl.BlockSpec((3, CONV_CW), lambda i, j: (0, j))],
        out_specs=pl.BlockSpec((CONV_ROWS, CONV_CW), lambda i, j: (i, j)),
        out_shape=jax.ShapeDtypeStruct((n_rows, D_IN), F32),
        compiler_params=_params("parallel", "parallel"),
        name="conv3",
    )(z, conv_w)


def _d_prep_kernel(x_ref, w0_ref, w2_ref, a0_ref, a2_ref, g2_ref, dec_ref, a_ref, g_ref):
    x = x_ref[...]
    xw = x[:, 0:2 * D_LORA]
    xa = x[:, 2 * D_LORA:4 * D_LORA]
    xg = x[:, 4 * D_LORA:]
    for d in range(2):
        u = w0_ref[d] + _bdot(jnp.tanh(xw[:, d * D_LORA:(d + 1) * D_LORA]), w2_ref[d])
        nu = -u
        softplus = jnp.maximum(nu, 0.0) + jnp.log1p(jnp.exp(-jnp.abs(nu)))
        wlog = -softplus - 0.5
        dec_ref[d] = -jnp.exp(wlog)
        av = a0_ref[d] + _bdot(xa[:, d * D_LORA:(d + 1) * D_LORA], a2_ref[d])
        a_ref[d] = 1.0 / (1.0 + jnp.exp(-av))
    g_ref[...] = _bdot(1.0 / (1.0 + jnp.exp(-xg)), g2_ref[...])


def _d_prep(zc, w0, w2, a0, a2, g2, tm=512):
    n = zc.shape[0]
    cb = (3 * D_WIDTH) // CONV_CW
    return pl.pallas_call(
        _d_prep_kernel,
        grid=(n // tm,),
        in_specs=[pl.BlockSpec((tm, CONV_CW), lambda i: (i, cb)),
                  pl.BlockSpec((2, 1, D_WIDTH), lambda i: (0, 0, 0)),
                  pl.BlockSpec((2, D_LORA, D_WIDTH), lambda i: (0, 0, 0)),
                  pl.BlockSpec((2, 1, D_WIDTH), lambda i: (0, 0, 0)),
                  pl.BlockSpec((2, D_LORA, D_WIDTH), lambda i: (0, 0, 0)),
                  pl.BlockSpec((D_GATE_LORA, D_WIDTH), lambda i: (0, 0))],
        out_specs=[pl.BlockSpec((2, tm, D_WIDTH), lambda i: (0, i, 0)),
                   pl.BlockSpec((2, tm, D_WIDTH), lambda i: (0, i, 0)),
                   pl.BlockSpec((tm, D_WIDTH), lambda i: (i, 0))],
        out_shape=[jax.ShapeDtypeStruct((2, n, D_WIDTH), F32),
                   jax.ShapeDtypeStruct((2, n, D_WIDTH), F32),
                   jax.ShapeDtypeStruct((n, D_WIDTH), F32)],
        compiler_params=_params("parallel"),
        name="d_prep",
    )(zc, w0.reshape(2, 1, D_WIDTH), w2, a0.reshape(2, 1, D_WIDTH), a2, g2)


RW_C = 64
RW_PP = 8


def _bdot(a, b):
    ah, al = _split_hi_lo(a)
    bh, bl = _split_hi_lo(b)
    dot = functools.partial(jnp.dot, preferred_element_type=F32)
    return dot(ah, bh) + dot(ah, bl) + dot(al, bh)


def _bdot2(a, b):
    ah, al = _split_hi_lo(a)
    dot = functools.partial(jnp.dot, preferred_element_type=F32)
    return dot(ah, b) + dot(al, b)


def _bdot_nt(a, b):
    ah, al = _split_hi_lo(a)
    bh, bl = _split_hi_lo(b)
    return _dot_nt(ah, bh) + _dot_nt(ah, bl) + _dot_nt(al, bh)


def _split_hi_lo(x):
    hi = x.astype(BF16)
    return hi, (x - hi.astype(F32)).astype(BF16)


def _head_ones():
    r = lax.broadcasted_iota(jnp.int32, (LANES, LANES), 0) // D_HS
    c = lax.broadcasted_iota(jnp.int32, (LANES, LANES), 1) // D_HS
    return jnp.where(r == c, 1.0, 0.0).astype(BF16)


def _rwkv_chunk_kernel(r_ref, k_ref, v_ref, lw_ref, a_ref, kkp_ref, kap_ref, s0_ref, y_ref, sfin_ref, s_ref):
    c = pl.program_id(3)
    two_c = 2 * RW_C

    @pl.when(c == 0)
    def _():
        s_ref[...] = s0_ref[...]

    sgn = jnp.where(pl.program_id(1) == 1, -1, 1)
    row = lax.broadcasted_iota(jnp.int32, (two_c, two_c), 0)
    col = lax.broadcasted_iota(jnp.int32, (two_c, two_c), 1)
    same = (row // RW_C) == (col // RW_C)
    tt, ss = row % RW_C, col % RW_C
    before = (ss - tt) * sgn < 0
    strict = same & before
    incl = same & (before | (ss == tt))
    eye = jnp.where(row == col, 1.0, 0.0)
    r64 = lax.broadcasted_iota(jnp.int32, (RW_C, RW_C), 0)
    c64 = lax.broadcasted_iota(jnp.int32, (RW_C, RW_C), 1)
    ltri = jnp.where((c64 - r64) * sgn <= 0, 1.0, 0.0).astype(BF16)
    head0 = lax.broadcasted_iota(jnp.int32, (RW_C, LANES), 1) < D_HS
    ones_blk = _head_ones()

    def stack(x):
        return jnp.concatenate([jnp.where(head0, x, 0.0), jnp.where(head0, 0.0, x)], axis=0)

    pairs = range(RW_PP)
    lss = [slice(p * LANES, (p + 1) * LANES) for p in pairs]
    r = [r_ref[:, ls] for ls in lss]
    k = [k_ref[:, ls] for ls in lss]
    lw = [lw_ref[:, ls] for ls in lss]
    a = [a_ref[:, ls] for ls in lss]
    kkf = [k[p] * kkp_ref[:, lss[p]] for p in pairs]
    n2 = [_bdot2(kkf[p] * kkf[p], ones_blk) for p in pairs]
    kk = [kkf[p] / jnp.maximum(jnp.sqrt(n2[p]), 1e-12) for p in pairs]
    lws = [_split_hi_lo(lw[p]) for p in pairs]
    cum = [jnp.dot(ltri, lws[p][0], preferred_element_type=F32) + jnp.dot(ltri, lws[p][1], preferred_element_type=F32)
           for p in pairs]
    mid = [cum[p][RW_C // 2:RW_C // 2 + 1] for p in pairs]
    cumc = [cum[p] - mid[p] for p in pairs]
    pmid = [jnp.exp(mid[p]) for p in pairs]
    pend = [jnp.exp(jnp.sum(lw[p], axis=0, keepdims=True) - mid[p]) for p in pairs]
    pinv = [jnp.exp(-cumc[p]) for p in pairs]
    ks = [stack(kk[p] * jnp.exp(cumc[p] - lw[p])).astype(BF16) for p in pairs]
    rs = [stack(r[p] * jnp.exp(cumc[p])).astype(BF16) for p in pairs]
    khs = [stack(k[p] * (1.0 + (a[p] - 1.0) * kap_ref[:, lss[p]]) * pinv[p]).astype(BF16) for p in pairs]
    bhs = [stack(kk[p] * a[p] * pinv[p]).astype(BF16) for p in pairs]
    vs = [stack(v_ref[:, ls]).astype(BF16) for ls in lss]
    big = [_dot_nt(jnp.concatenate([ks[p], rs[p]], axis=0), jnp.concatenate([bhs[p], khs[p]], axis=0)) for p in pairs]
    a_b = [jnp.where(strict, big[p][:two_c, :two_c], 0.0) for p in pairs]
    a_k = [jnp.where(strict, big[p][:two_c, two_c:], 0.0) for p in pairs]
    l_b = [jnp.where(incl, big[p][two_c:, :two_c], 0.0) for p in pairs]
    l_k = [jnp.where(incl, big[p][two_c:, two_c:], 0.0) for p in pairs]
    npow = [-a_b[p] for p in pairs]
    tinv = [eye + npow[p] for p in pairs]
    for _ in range(5):
        npow = [_bdot(npow[p], npow[p]) for p in pairs]
        tinv = [tinv[p] + _bdot(tinv[p], npow[p]) for p in pairs]
    w1 = [_bdot2(a_k[p], vs[p]) for p in pairs]
    ktil = [_bdot2(tinv[p], ks[p]) for p in pairs]
    uv = [_bdot(tinv[p], w1[p]) for p in pairs]
    lku = [_bdot(l_b[p], jnp.concatenate([ktil[p], uv[p]], axis=1)) for p in pairs]
    rtil = [rs[p].astype(F32) - lku[p][:, :LANES] for p in pairs]
    yv = [_bdot2(l_k[p], vs[p]) - lku[p][:, LANES:] for p in pairs]
    gt = [_bdot2(ktil[p].T, bhs[p]) for p in pairs]
    ht = [jnp.dot(vs[p].astype(F32).T.astype(BF16), khs[p], preferred_element_type=F32) - _bdot2(uv[p].T, bhs[p])
          for p in pairs]
    s = [s_ref[p] * pmid[p] for p in pairs]
    y = [_bdot_nt(rtil[p], s[p]) + yv[p] for p in pairs]
    sg = [_bdot(s[p], gt[p]) for p in pairs]
    for p in pairs:
        s_ref[p] = (s[p] - sg[p] + ht[p]) * pend[p]
        y_ref[:, lss[p]] = y[p][:RW_C] + y[p][RW_C:]

    @pl.when(c == pl.num_programs(3) - 1)
    def _():
        sfin_ref[...] = s_ref[...]


def _pair_states(s):
    zero = jnp.zeros_like(s[:, :, 0::2])
    top = jnp.concatenate([s[:, :, 0::2], zero], axis=-1)
    bot = jnp.concatenate([zero, s[:, :, 1::2]], axis=-1)
    return jnp.concatenate([top, bot], axis=-2)


def _head_states(sp):
    b = sp.shape[0]
    both = jnp.stack([sp[:, :, :, :D_HS, :D_HS], sp[:, :, :, D_HS:, D_HS:]], axis=3)
    return both.reshape(b, 2, D_HEADS, D_HS, D_HS)


def _rwkv_chunked(zc, lw, a, k_k, k_a, s0, b, t):
    nc = t // RW_C
    gw = RW_PP * LANES
    npg = D_WIDTH // gw

    def cmap(d, c):
        return jnp.where(d == 0, c, nc - 1 - c)

    def z_spec(colblk):
        return pl.BlockSpec((RW_C, gw), lambda i, d, g, c: (i * nc + cmap(d, c), colblk * npg + g))

    dir_spec = pl.BlockSpec((None, RW_C, gw), lambda i, d, g, c: (d, i * nc + cmap(d, c), g))
    par_spec = pl.BlockSpec((1, gw), lambda i, d, g, c: (0, g))
    st_spec = pl.BlockSpec((None, None, RW_PP, LANES, LANES), lambda i, d, g, c: (i, d, g, 0, 0))
    return pl.pallas_call(
        _rwkv_chunk_kernel,
        grid=(b, 2, npg, nc),
        in_specs=[z_spec(0), z_spec(1), z_spec(2), dir_spec, dir_spec, par_spec, par_spec, st_spec],
        out_specs=[dir_spec, st_spec],
        out_shape=[jax.ShapeDtypeStruct((2, b * t, D_WIDTH), F32),
                   jax.ShapeDtypeStruct((b, 2, D_WIDTH // LANES, LANES, LANES), F32)],
        scratch_shapes=[pltpu.VMEM((RW_PP, LANES, LANES), F32)],
        compiler_params=_params("parallel", "parallel", "parallel", "arbitrary"),
        name="rwkv_chunk",
    )(zc, zc, zc, lw, a, k_k.reshape(1, D_WIDTH), k_a.reshape(1, D_WIDTH), s0)


def _d_out_kernel(y_ref, r_ref, k_ref, v_ref, a_ref, g_ref, kap_ref, rkp_ref, lng_ref, lnb_ref, o_ref):
    ones_blk = _head_ones()

    def seg_sum(x):
        hi, lo = _split_hi_lo(x)
        return jnp.dot(hi, ones_blk, preferred_element_type=F32) + jnp.dot(lo, ones_blk, preferred_element_type=F32)

    for cb in range(D_WIDTH // LANES):
        ls = slice(cb * LANES, (cb + 1) * LANES)
        y = y_ref[0, :, ls] + y_ref[1, :, ls]
        mu = seg_sum(y) * (1.0 / D_HS)
        yc = y - mu
        var = seg_sum(yc * yc) * (1.0 / D_HS)
        yn = yc * lax.rsqrt(var + D_LN_EPS) * lng_ref[:, ls] + lnb_ref[:, ls]
        asum = a_ref[0, :, ls] + a_ref[1, :, ls]
        bonus = seg_sum(r_ref[:, ls] * k_ref[:, ls] * rkp_ref[:, ls] * (2.0 + (asum - 2.0) * kap_ref[:, ls]))
        o_ref[:, ls] = (yn + bonus * v_ref[:, ls]) * g_ref[:, ls]


def _d_out(y, zc, a, g, k_a, r_k, ln_g, ln_b, tm=512):
    n = g.shape[0]

    def z_spec(colblk):
        return pl.BlockSpec((tm, D_WIDTH), lambda i: (i, colblk))

    dspec = pl.BlockSpec((2, tm, D_WIDTH), lambda i: (0, i, 0))
    pspec = pl.BlockSpec((1, D_WIDTH), lambda i: (0, 0))
    return pl.pallas_call(
        _d_out_kernel,
        grid=(n // tm,),
        in_specs=[dspec, z_spec(0), z_spec(1), z_spec(2), dspec, pl.BlockSpec((tm, D_WIDTH), lambda i: (i, 0)),
                  pspec, pspec, pspec, pspec],
        out_specs=pl.BlockSpec((tm, D_WIDTH), lambda i: (i, 0)),
        out_shape=jax.ShapeDtypeStruct((n, D_WIDTH), F32),
        compiler_params=_params("parallel"),
        name="d_out",
    )(y, zc, zc, zc, a, g, k_a.reshape(1, D_WIDTH), r_k.reshape(1, D_WIDTH), ln_g.reshape(1, D_WIDTH),
      ln_b.reshape(1, D_WIDTH))


TOPK_TM = 1024
BIG = float(2 ** 30)


def _extract_topk(s, labels):
    iota16 = lax.broadcasted_iota(jnp.int32, (PEER_TOPK, s.shape[1]), 0)
    rank = jnp.full(s.shape, BIG, F32)
    vals = jnp.zeros((PEER_TOPK, s.shape[1]), F32)
    for r in range(PEER_TOPK):
        m = jnp.max(s, axis=0, keepdims=True)
        idx = jnp.min(jnp.where(s == m, labels, BIG), axis=0, keepdims=True)
        hit = labels == idx
        rank = jnp.where(hit, float(r), rank)
        vals = jnp.where(iota16 == r, m, vals)
        s = jnp.where(hit, -jnp.inf, s)
    return rank, vals


_CAND_PAIRS = ([(0, b) for b in range(16)] + [(1, b) for b in range(8)] + [(2, b) for b in range(8)]
               + [(3, b) for b in range(8)] + [(4, b) for b in range(4)] + [(5, b) for b in range(4)]
               + [(6, b) for b in range(4)] + [(7, b) for b in range(4)] + [(a, 0) for a in range(8, 16)])
N_CAND = len(_CAND_PAIRS)


def _cand_labels():
    lab = np.array([a * PEER_TOPK + b for a, b in _CAND_PAIRS], np.float32)
    return jnp.asarray(np.broadcast_to(lab[:, None], (N_CAND, LANES)).copy())


def _peer_topk_kernel(q_ref, keys_ref, lab_ref, lim_ref, e1_ref, rb_ref, e2_ref):
    iota_k = lax.broadcasted_iota(jnp.int32, (PEER_NKEYS, LANES), 0).astype(F32)
    row8 = lax.broadcasted_iota(jnp.int32, (8, LANES), 0)
    labels = lab_ref[...]
    for c in range(TOPK_TM // LANES):
        cs = slice(c * LANES, (c + 1) * LANES)
        q1 = q_ref[cs, 0:PEER_HALF].astype(BF16)
        q2 = q_ref[cs, PEER_HALF:2 * PEER_HALF].astype(BF16)
        s1 = _dot_nt(keys_ref[0, 0], q1)
        s2 = _dot_nt(keys_ref[0, 1], q2)
        rank1, sv1 = _extract_topk(s1, iota_k)
        rank2, sv2 = _extract_topk(s2, iota_k)
        lo8 = sv2[0:8]
        lo4 = jnp.where(row8 < 4, lo8, pltpu.roll(lo8, 4, 0))
        cand = jnp.concatenate([
            sv1[0:1] + lo8, sv1[0:1] + sv2[8:16], sv1[1:2] + lo8, sv1[2:3] + lo8, sv1[3:4] + lo8,
            jnp.where(row8 < 4, sv1[4:5], sv1[5:6]) + lo4, jnp.where(row8 < 4, sv1[6:7], sv1[7:8]) + lo4,
            sv1[8:16] + sv2[0:1]], axis=0)
        crank, cvals = _extract_topk(cand, labels)
        z = jnp.sum(jnp.exp(cvals - cvals[0:1]), axis=0, keepdims=True)
        sel = jnp.where(crank < BIG, 1.0, 0.0)

        def count(lo, hi):
            return jnp.sum(sel[lo:hi], axis=0, keepdims=True)

        def count_half(lo, first):
            part = jnp.where((row8 < 4) if first else (row8 >= 4), sel[lo:lo + 8], 0.0)
            return jnp.sum(part, axis=0, keepdims=True)

        n_sel = [count(0, 16), count(16, 24), count(24, 32), count(32, 40),
                 count_half(40, True), count_half(40, False), count_half(48, True), count_half(48, False)]
        n_sel += [sel[56 + a:57 + a] for a in range(8)]
        lim = jnp.zeros_like(s1)
        for a in range(PEER_TOPK):
            lim = jnp.where(rank1 == float(a), n_sel[a], lim)
        lim_ref[0, :, cs] = lim
        e1_ref[0, :, cs] = jnp.exp(s1 - sv1[0:1])
        rb_ref[0, :, cs] = rank2
        e2_ref[0, :, cs] = jnp.exp(s2 - sv2[0:1]) / z


def _peer_topk(q, keys):
    n = q.shape[0]
    ospec = pl.BlockSpec((1, PEER_NKEYS, TOPK_TM), lambda i, h: (h, 0, i))
    return pl.pallas_call(
        _peer_topk_kernel,
        grid=(n // TOPK_TM, PEER_HEADS),
        in_specs=[pl.BlockSpec((TOPK_TM, 2 * PEER_HALF), lambda i, h: (i, h)),
                  pl.BlockSpec((1, 2, PEER_NKEYS, PEER_HALF), lambda i, h: (h, 0, 0, 0)),
                  pl.BlockSpec((N_CAND, LANES), lambda i, h: (0, 0))],
        out_specs=[ospec] * 4,
        out_shape=[jax.ShapeDtypeStruct((PEER_HEADS, PEER_NKEYS, n), F32)] * 4,
        compiler_params=_params("parallel", "parallel"),
        name="peer_topk",
    )(q, keys, _cand_labels())


PEER_TM = 512
PEER_TI = 8
PEER_RG = 4
PEER_JT = 32


def _gelu(x):
    return 0.5 * x * (1.0 + lax.erf(x * (2.0 ** -0.5)))


def _peer_dense_kernel(x_ref, u_ref, v_ref, lim_ref, e1_ref, rb_ref, e2_ref, res_ref, gate_ref, o_ref, w_ref):
    e = pl.program_id(1)

    @pl.when(e == 0)
    def _():
        o_ref[...] = jnp.zeros_like(o_ref)

    hid = _dot_nt(u_ref[...], x_ref[...])
    for c in range(PEER_TM // LANES):
        cs = slice(c * LANES, (c + 1) * LANES)
        for jt in range(PEER_NKEYS // PEER_JT):
            js = slice(jt * PEER_JT, (jt + 1) * PEER_JT)
            for rg in range(PEER_TI // PEER_RG):
                rows = range(rg * PEER_RG, (rg + 1) * PEER_RG)
                g = [jnp.zeros((PEER_JT, LANES), F32) for _ in rows]
                for h in range(PEER_HEADS):
                    rb = rb_ref[h, js, cs]
                    e2 = e2_ref[h, js, cs]
                    for k, ii in enumerate(rows):
                        g[k] = g[k] + jnp.where(rb < lim_ref[h, ii:ii + 1, cs], e2, 0.0) * e1_ref[h, ii:ii + 1, cs]
                for k, ii in enumerate(rows):
                    es = slice(ii * PEER_NKEYS + jt * PEER_JT, ii * PEER_NKEYS + (jt + 1) * PEER_JT)
                    w_ref[es, cs] = (g[k] * _gelu(hid[es, cs])).astype(BF16)
    o_ref[...] += _dot_tn(w_ref[...], v_ref[...])

    @pl.when(e == pl.num_programs(1) - 1)
    def _():
        o_ref[...] = res_ref[...] + gate_ref[0] * o_ref[...]


def _peer_dense(xm, u, v, layer, lim, e1, rb, e2, res, gate):
    n, d = xm.shape
    n_exp = u.shape[1]
    te = PEER_TI * PEER_NKEYS
    sel_spec = pl.BlockSpec((PEER_HEADS, PEER_NKEYS, PEER_TM), lambda i, e: (0, 0, i))
    row_spec = pl.BlockSpec((PEER_HEADS, PEER_TI, PEER_TM), lambda i, e: (0, e, i))
    return pl.pallas_call(
        _peer_dense_kernel,
        grid=(n // PEER_TM, n_exp // te),
        in_specs=[pl.BlockSpec((PEER_TM, d), lambda i, e: (i, 0)),
                  pl.BlockSpec((None, te, d), lambda i, e: (layer, e, 0)),
                  pl.BlockSpec((None, te, d), lambda i, e: (layer, e, 0)),
                  row_spec, row_spec, sel_spec, sel_spec,
                  pl.BlockSpec((PEER_TM, d), lambda i, e: (i, 0)),
                  pl.BlockSpec((1, 1, d), lambda i, e: (_cond_of_tile(i, PEER_TM), 0, 0))],
        out_specs=pl.BlockSpec((PEER_TM, d), lambda i, e: (i, 0)),
        out_shape=jax.ShapeDtypeStruct((n, d), F32),
        scratch_shapes=[pltpu.VMEM((te, PEER_TM), BF16)],
        compiler_params=_params("parallel", "arbitrary"),
        name="peer_dense",
    )(xm, u, v, lim, e1, rb, e2, res, gate)


def _peer_layer(x, layer, norm_g, shift, scale, gate, wq, keys, u, v):
    xm = _modulate(x, norm_g, shift, scale)
    q = _matmul(xm, wq, layer, out_dtype=BF16)
    lim, e1, rb, e2 = _peer_topk(q, keys)
    return _peer_dense(xm, u, v, layer, lim, e1, rb, e2, x, gate)


EVEN_PAD = 4096
ODD_PAD = 6656
S_ALL = PAST_LEN + DEC_SEQ


def _even_mixer(h, w_in, layer, lam, subln_g, q_norm_g, w_uq, kv_norm_g, w_ukv, lam_init,
                cache_k, cache_v, cache_ckv, cache_kpe, rope_cos, rope_sin):
    z = _matmul(h, w_in, layer)
    ckv_n = _rmsnorm(z, kv_norm_g, col=3584 // B_KV_RANK)
    cq_n = _rmsnorm(z, q_norm_g, out_dtype=BF16, col=3072 // B_Q_RANK)
    wq3 = w_uq.reshape(B_Q_RANK, HEADS, B_NOPE + B_ROPE)
    w_uq_r = jnp.concatenate([wq3[:, :, :B_NOPE].reshape(B_Q_RANK, -1), wq3[:, :, B_NOPE:].reshape(B_Q_RANK, -1)], axis=1)
    qb = _matmul(cq_n, w_uq_r.astype(BF16), tn=512)
    new = (z[:N_CTX, 1024:2048].reshape(BATCH, SEQ, HEADS, 2 * A_QK),
           z[:N_CTX, 2048:3072].reshape(BATCH, SEQ, HEADS, HEAD_DIM),
           ckv_n[:N_CTX].reshape(BATCH, SEQ, B_KV_RANK), z[:N_CTX, 3840:3904].reshape(BATCH, SEQ, B_ROPE))

    qa_s = _rope(z, rope_cos, rope_sin, ATT_W, col=0)
    ka_s = _rope(z, rope_cos, rope_sin, ATT_W, col=1)
    qr_s = _rope(qb, rope_cos, rope_sin, HEADS * B_ROPE, col=2)
    kpe_s = _rope(z, rope_cos, rope_sin, LANES, col=3840 // LANES)[:, :B_ROPE]

    def with_ctx(cache, own, width):
        both = jnp.concatenate([cache.reshape(DEC_BATCH, PAST_LEN, width).astype(own.dtype),
                                own.reshape(DEC_BATCH, DEC_SEQ, width)], axis=1)
        return both.reshape(DEC_BATCH * S_ALL, width)

    ka_all = with_ctx(cache_k, ka_s, ATT_W)
    va_all = with_ctx(cache_v, z[N_CTX:, 2048:3072].astype(BF16), ATT_W)
    ckv_all = with_ctx(cache_ckv, ckv_n[N_CTX:], B_KV_RANK)
    kpe_all = with_ctx(cache_kpe, kpe_s, B_ROPE)

    wkv3 = w_ukv.reshape(B_KV_RANK, HEADS, B_NOPE + HEAD_DIM)
    w_ukv_r = jnp.concatenate([wkv3[:, :, :B_NOPE].reshape(B_KV_RANK, -1), wkv3[:, :, B_NOPE:].reshape(B_KV_RANK, -1)], axis=1)
    n_skv = DEC_BATCH * S_ALL
    kv = _matmul(jnp.concatenate([ckv_all, ckv_n[:N_CTX]], axis=0), w_ukv_r.astype(BF16), out_dtype=BF16)

    oa_c = _diff_attention(BATCH, SEQ, (z, _q_rows(SEQ, ATT_W, 0, 0)), (z, _kv_rows(SEQ, ATT_W, 0, 1)),
                           (z, _kv_rows(SEQ, ATT_W, 0, 2)), lam, subln_g, lam_init)
    oa_s = _diff_attention(DEC_BATCH, DEC_SEQ, (qa_s, _q_rows(DEC_SEQ, ATT_W)), (ka_all, _kv_rows(S_ALL, ATT_W)),
                           (va_all, _kv_rows(S_ALL, ATT_W)), lam, subln_g, lam_init)
    ob_c = _mla_attention(BATCH, SEQ, (qb, _q_rows(SEQ, ATT_W, 0, 0)), (qb, _q_rows(SEQ, HEADS * B_ROPE, 0, 2)),
                          (kv, _kv_rows(SEQ, ATT_W, n_skv, 0)), (z, _kv_rows(SEQ, LANES, 0, 3840 // LANES)),
                          (kv, _kv_rows(SEQ, ATT_W, n_skv, 1)))
    ob_s = _mla_attention(DEC_BATCH, DEC_SEQ, (qb, _q_rows(DEC_SEQ, ATT_W, N_CTX, 0)),
                          (qr_s, _q_rows(DEC_SEQ, HEADS * B_ROPE)), (kv, _kv_rows(S_ALL, ATT_W, 0, 0)),
                          (kpe_all, _kv_rows(S_ALL, B_ROPE)), (kv, _kv_rows(S_ALL, ATT_W, 0, 1)))
    return (oa_c, oa_s, ob_c, ob_s), new


def _odd_mixer(h, w_in, layer, rpb, conv_w, w0, w2, a0, a2, g2, k_k, k_a, r_k, ln_g, ln_b,
               cache_k, cache_v, state):
    z = _matmul(h, w_in, layer)
    new_k = z[:N_CTX, 1024:2048].reshape(BATCH, SEQ, HEADS, HEAD_DIM)
    new_v = z[:N_CTX, 2048:3072].reshape(BATCH, SEQ, HEADS, HEAD_DIM)
    oc_c = _plain_attention(BATCH, SEQ, (z, _q_rows(SEQ, ATT_W, 0, 0)), (z, _kv_rows(SEQ, ATT_W, 0, 1)),
                            (z, _kv_rows(SEQ, ATT_W, 0, 2)))
    oc_s = _na_attention(z, cache_k.reshape(DEC_BATCH * PAST_LEN, ATT_W), cache_v.reshape(DEC_BATCH * PAST_LEN, ATT_W),
                         _na_bias_tables(rpb))

    zc_c = _conv3(z, conv_w, SEQ, 0, BATCH)
    zc_s = _conv3(z, conv_w, DEC_SEQ, N_CTX, DEC_BATCH)
    lw_c, a_c, g_c = _d_prep(zc_c, w0, w2, a0, a2, g2)
    lw_s, a_s, g_s = _d_prep(zc_s, w0, w2, a0, a2, g2)
    rk = r_k.reshape(-1)
    y_c, sfin = _rwkv_chunked(zc_c, lw_c, a_c, k_k, k_a,
                              jnp.zeros((BATCH, 2, D_WIDTH // LANES, LANES, LANES), F32), BATCH, SEQ)
    sfin = _head_states(sfin)
    y_s, _ = _rwkv_chunked(zc_s, lw_s, a_s, k_k, k_a, _pair_states(state), DEC_BATCH, DEC_SEQ)
    od_c = _d_out(y_c, zc_c, a_c, g_c, k_a, rk, ln_g, ln_b)
    od_s = _d_out(y_s, zc_s, a_s, g_s, k_a, rk, ln_g, ln_b)
    return (oc_c, oc_s, od_c, od_s), (new_k, new_v, sfin)


def kernel(x_prompt, x_sample, cache_a_k, cache_a_v, cache_b_ckv, cache_b_kpe, cache_c_k, cache_c_v, state_d, c, c_ctx, ada_w, ada_b, norm1_g, norm2_g, w_out, peer_wq, peer_keys, peer_u, peer_v, final_g, ab_w_in, a_lam, a_subln_g, b_q_norm_g, b_w_uq, b_kv_norm_g, b_w_ukv, cd_w_in, c_rpb, d_conv, d_w0, d_w2, d_a0, d_a2, d_g2, d_k_k, d_k_a, d_r_k, d_ln_g, d_ln_b):
    x = jnp.concatenate([x_prompt.reshape(N_CTX, D_MODEL), x_sample.reshape(N_SMP, D_MODEL)], axis=0)
    cond8 = jnp.pad(jnp.concatenate([c_ctx[None, :], c], axis=0), ((0, 8 - N_COND), (0, 0)))
    ada = _ada_all(cond8, ada_w, ada_b)
    rope_cos, rope_sin = _rope_tables()
    w_even = jnp.pad(ab_w_in, ((0, 0), (0, 0), (0, EVEN_PAD - EVEN_IN))).astype(BF16)
    w_odd = jnp.pad(cd_w_in, ((0, 0), (0, 0), (0, ODD_PAD - ODD_IN))).astype(BF16)
    w_out_b, wq_b, keys_b = w_out.astype(BF16), peer_wq.astype(BF16), peer_keys.astype(BF16)
    u_b, v_b = peer_u.astype(BF16), peer_v.astype(BF16)
    new_ak, new_av, new_bc, new_bp, new_ck, new_cv, new_sd = [], [], [], [], [], [], []
    for l in range(DEPTH):
        i = l // 2
        mods = ada[l, :N_COND].reshape(N_COND, 6, 1, D_MODEL)
        sh1, sc1, g1, sh2, sc2, g2 = (mods[:, m] for m in range(6))
        h = _modulate(x, norm1_g[l], sh1, sc1)
        if l % 2 == 0:
            lam_init = 0.8 - 0.6 * math.exp(-0.3 * l)
            o, (ak, av, bc, bp) = _even_mixer(
                h, w_even, i, a_lam[i], a_subln_g[i], b_q_norm_g[i], b_w_uq[i], b_kv_norm_g[i], b_w_ukv[i],
                lam_init, cache_a_k[:, i], cache_a_v[:, i], cache_b_ckv[:, i], cache_b_kpe[:, i], rope_cos, rope_sin)
            new_ak.append(ak)
            new_av.append(av)
            new_bc.append(bc)
            new_bp.append(bp)
        else:
            o, (ck, cv, sd) = _odd_mixer(
                h, w_odd, i, c_rpb[i], d_conv[i], d_w0[i], d_w2[i], d_a0[i], d_a2[i], d_g2[i],
                d_k_k[i], d_k_a[i], d_r_k[i], d_ln_g[i], d_ln_b[i], cache_c_k[:, i], cache_c_v[:, i], state_d[:, i])
            new_ck.append(ck)
            new_cv.append(cv)
            new_sd.append(sd)
        x = _matmul_residual(*o, w_out_b, l, x, g1)
        x = _peer_layer(x, l, norm2_g[l], sh2, sc2, g2, wq_b, keys_b[l], u_b, v_b)
    y = _rmsnorm(x, final_g)
    y_prompt = y[:N_CTX].reshape(BATCH, SEQ, D_MODEL)
    y_sample = y[N_CTX:].reshape(DEC_BATCH, DEC_SEQ, D_MODEL)
    return (y_prompt, y_sample, jnp.stack(new_ak, axis=1), jnp.stack(new_av, axis=1), jnp.stack(new_bc, axis=1),
            jnp.stack(new_bp, axis=1), jnp.stack(new_ck, axis=1), jnp.stack(new_cv, axis=1), jnp.stack(new_sd, axis=1))
```
